```python
import jax
import jax.numpy as jnp
from jax import lax
import numpy as np

D_MODEL = 1024
BATCH = 2
SEQ = 8192
DEPTH = 4

N_HEADS = 4
HEAD_DIM = 64
BR_WIDTH = N_HEADS * HEAD_DIM
N_BRANCH = 5
N_MEM = 256
ROPE_THETA = 10000.0
EPS = 1e-6
Q_BLOCK = 128
DSA_TOPK_MAX = 256
IDX_HEADS = 4
IDX_DIM = 32
KV_LATENT = 128
NSA_KV_DIM = 64
CMP_LEN = 32
CMP_STRIDE = 16
CMP_HIDDEN = 128
SLC_BLOCK = 64
SLC_TOPN = 16
WINDOW = 512
FORCE_SCORE = 1e4

IN_LAYOUT = (
    ('dsa_q', BR_WIDTH), ('dsa_ckv', KV_LATENT), ('idx_q', IDX_HEADS * IDX_DIM), ('idx_k', IDX_DIM), ('idx_w', IDX_HEADS), ('dsa_z', BR_WIDTH),
    ('fox_q', BR_WIDTH), ('fox_k', BR_WIDTH), ('fox_v', BR_WIDTH), ('fox_f', N_HEADS), ('fox_z', BR_WIDTH),
    ('sb_q', BR_WIDTH), ('sb_k', BR_WIDTH), ('sb_v', BR_WIDTH), ('sb_z', BR_WIDTH),
    ('nsa_q', BR_WIDTH), ('nsa_kc', NSA_KV_DIM), ('nsa_vc', NSA_KV_DIM), ('nsa_ks', NSA_KV_DIM), ('nsa_vs', NSA_KV_DIM),
    ('nsa_kw', NSA_KV_DIM), ('nsa_vw', NSA_KV_DIM), ('nsa_g', 3 * N_HEADS), ('nsa_z', BR_WIDTH),
    ('mem_q', BR_WIDTH), ('mem_z', BR_WIDTH),
    ('merge', N_BRANCH * D_MODEL),
)
N_IN = sum(n for _, n in IN_LAYOUT)

kernel_name = 'gated_parallel_hybrid_trunk'


def _rmsnorm(x, g):
    xf = x.astype(jnp.float32)
    y = xf * lax.rsqrt(jnp.mean(xf * xf, axis=-1, keepdims=True) + EPS)
    return (y * g.astype(jnp.float32)).astype(x.dtype)


def _rope(x, pos):
    dh = x.shape[-1]
    inv = ROPE_THETA ** (-jnp.arange(0, dh, 2, dtype=jnp.float32) / dh)
    ang = pos.astype(jnp.float32)[..., None] * inv
    ang = ang.reshape(ang.shape[:2] + (1,) * (x.ndim - 3) + ang.shape[-1:])
    cos, sin = jnp.cos(ang), jnp.sin(ang)
    xf = x.astype(jnp.float32)
    x1, x2 = xf[..., : dh // 2], xf[..., dh // 2:]
    return jnp.concatenate([x1 * cos - x2 * sin, x2 * cos + x1 * sin], axis=-1).astype(x.dtype)


def _masked_softmax(logits, mask):
    l = jnp.where(mask, logits, -jnp.inf)
    m = jnp.max(l, axis=-1, keepdims=True)
    m = jnp.where(jnp.isfinite(m), m, 0.0)
    e = jnp.where(mask, jnp.exp(l - m), 0.0)
    return e / jnp.maximum(jnp.sum(e, axis=-1, keepdims=True), 1e-30)


def _to_blocks(a):
    b, s = a.shape[:2]
    return jnp.moveaxis(a.reshape((b, s // Q_BLOCK, Q_BLOCK) + a.shape[2:]), 1, 0)


def _from_blocks(a):
    nq, b, q = a.shape[:3]
    return jnp.moveaxis(a, 0, 1).reshape((b, nq * q) + a.shape[3:])


def _gather_rows(table, idx):
    return jax.vmap(lambda tb, ib: tb[ib])(table, idx)


def _split_columns(p):
    out, off = {}, 0
    for name, n in IN_LAYOUT:
        out[name] = p[..., off:off + n]
        off += n
    return out


def _dsa_attention(q, k, v, qi, ki, wi):
    B, S, H, Dh = q.shape
    k_sel = min(DSA_TOPK_MAX, S // 4)
    s_pos = jnp.arange(S)
    scale = Dh ** -0.5
    idx_scale = (IDX_DIM ** -0.5) * (IDX_HEADS ** -0.5)

    def block(args):
        qb, q_blk, qi_blk, wi_blk = args
        t = qb * Q_BLOCK + jnp.arange(Q_BLOCK)
        rel = jax.nn.relu(jnp.einsum('bqhd,bsd->bqhs', qi_blk, ki).astype(jnp.float32))
        score = jnp.einsum('bqhs,bqh->bqs', rel, wi_blk.astype(jnp.float32)) * idx_scale
        score = jnp.where(s_pos[None, None, :] <= t[None, :, None], score, -jnp.inf)
        _, sel = lax.top_k(score, k_sel)
        kg = _gather_rows(k, sel)
        vg = _gather_rows(v, sel)
        logits = jnp.einsum('bqhd,bqkhd->bhqk', q_blk, kg).astype(jnp.float32) * scale
        mask = (sel <= t[None, :, None])[:, None]
        p = _masked_softmax(logits, mask).astype(v.dtype)
        return jnp.einsum('bhqk,bqkhd->bqhd', p, vg)

    out = lax.map(block, (jnp.arange(S // Q_BLOCK), _to_blocks(q), _to_blocks(qi), _to_blocks(wi)))
    return _from_blocks(out)


def _fox_attention(q, k, v, log_f):
    B, S, H, Dh = q.shape
    s_pos = jnp.arange(S)
    scale = Dh ** -0.5
    cum = jnp.cumsum(log_f.astype(jnp.float32), axis=1)
    cum_s = jnp.moveaxis(cum, 1, 2)

    def block(args):
        qb, q_blk, c_blk = args
        t = qb * Q_BLOCK + jnp.arange(Q_BLOCK)
        logits = (jnp.einsum('bqhd,bshd->bhqs', q_blk, k).astype(jnp.float32) * scale
                  + jnp.moveaxis(c_blk, 1, 2)[..., None] - cum_s[:, :, None, :])
        mask = (s_pos[None, :] <= t[:, None])[None, None]
        p = _masked_softmax(logits, mask).astype(v.dtype)
        return jnp.einsum('bhqs,bshd->bqhd', p, v)

    out = lax.map(block, (jnp.arange(S // Q_BLOCK), _to_blocks(q), _to_blocks(cum)))
    return _from_blocks(out)


def _stick_breaking_attention(q, k, v):
    B, S, H, Dh = q.shape
    s_pos = jnp.arange(S)
    scale = Dh ** -0.5

    def block(args):
        qb, q_blk = args
        t = qb * Q_BLOCK + jnp.arange(Q_BLOCK)
        z = jnp.einsum('bqhd,bshd->bhqs', q_blk, k).astype(jnp.float32) * scale
        strict = (s_pos[None, :] < t[:, None])[None, None]
        log_keep = jnp.where(strict, jax.nn.log_sigmoid(-z), 0.0)
        later = lax.cumsum(log_keep, axis=3, reverse=True) - log_keep
        a = jnp.where(strict, jnp.exp(jax.nn.log_sigmoid(z) + later), 0.0).astype(v.dtype)
        return jnp.einsum('bhqs,bshd->bqhd', a, v)

    out = lax.map(block, (jnp.arange(S // Q_BLOCK), _to_blocks(q)))
    return _from_blocks(out)


def _nsa_attention(q, kc_tok, vc_tok, ks, vs, kw, vw, gates, pe_k, pe_v, wc1_k, wc2_k, wc1_v, wc2_v):
    B, S, H, Dh = q.shape
    scale = Dh ** -0.5
    n_cmp = (S - CMP_LEN) // CMP_STRIDE + 1
    n_blk = S // SLC_BLOCK
    n_sel = min(SLC_TOPN, n_blk)
    per_blk = SLC_BLOCK // CMP_STRIDE
    tok = (jnp.arange(n_cmp) * CMP_STRIDE)[:, None] + jnp.arange(CMP_LEN)[None, :]

    def compress(x_tok, pe, w1, w2):
        blocks = (x_tok[:, tok] + pe).reshape(B, n_cmp, CMP_LEN * x_tok.shape[-1])
        return jax.nn.silu(blocks @ w1) @ w2

    kc = compress(kc_tok, pe_k, wc1_k, wc2_k)
    vc = compress(vc_tok, pe_v, wc1_v, wc2_v)
    cmp_end = jnp.arange(n_cmp) * CMP_STRIDE + CMP_LEN - 1
    ks_blk = ks.reshape(B, n_blk, SLC_BLOCK, -1)
    vs_blk = vs.reshape(B, n_blk, SLC_BLOCK, -1)
    kw_pad = jnp.pad(kw, ((0, 0), (WINDOW, 0), (0, 0)))
    vw_pad = jnp.pad(vw, ((0, 0), (WINDOW, 0), (0, 0)))
    blk_ids = jnp.arange(n_blk)
    in_blk = jnp.arange(SLC_BLOCK)
    win_off = jnp.arange(WINDOW + Q_BLOCK) - WINDOW

    def block(args):
        qb, q_blk, g_blk = args
        t = qb * Q_BLOCK + jnp.arange(Q_BLOCK)
        lc = jnp.einsum('bqhd,bnd->bhqn', q_blk, kc).astype(jnp.float32) * scale
        pc = _masked_softmax(lc, (cmp_end[None, :] <= t[:, None])[None, None])
        o_cmp = jnp.einsum('bhqn,bnd->bqhd', pc.astype(vc.dtype), vc)
        imp = jnp.pad(jnp.sum(pc, axis=1), ((0, 0), (0, 0), (0, n_blk * per_blk - n_cmp)))
        imp = jnp.sum(imp.reshape(B, Q_BLOCK, n_blk, per_blk), axis=-1)
        cur = t // SLC_BLOCK
        forced = (blk_ids[None, :] == 0) | (blk_ids[None, :] == cur[:, None]) | (blk_ids[None, :] == cur[:, None] - 1)
        visible = blk_ids[None, :] <= cur[:, None]
        imp = jnp.where(forced[None], FORCE_SCORE, jnp.where(visible[None], imp, -1.0))
        _, sel = lax.top_k(imp, n_sel)
        ksg = _gather_rows(ks_blk, sel).reshape(B, Q_BLOCK, n_sel * SLC_BLOCK, -1)
        vsg = _gather_rows(vs_blk, sel).reshape(B, Q_BLOCK, n_sel * SLC_BLOCK, -1)
        pos = (sel[..., None] * SLC_BLOCK + in_blk).reshape(B, Q_BLOCK, n_sel * SLC_BLOCK)
        ls = jnp.einsum('bqhd,bqkd->bhqk', q_blk, ksg).astype(jnp.float32) * scale
        ps = _masked_softmax(ls, (pos <= t[None, :, None])[:, None])
        o_slc = jnp.einsum('bhqk,bqkd->bqhd', ps.astype(vs.dtype), vsg)
        kwb = lax.dynamic_slice_in_dim(kw_pad, qb * Q_BLOCK, WINDOW + Q_BLOCK, axis=1)
        vwb = lax.dynamic_slice_in_dim(vw_pad, qb * Q_BLOCK, WINDOW + Q_BLOCK, axis=1)
        s = qb * Q_BLOCK + win_off
        d = t[:, None] - s[None, :]
        mw = ((d >= 0) & (d < WINDOW) & (s[None, :] >= 0))[None, None]
        lw = jnp.einsum('bqhd,bkd->bhqk', q_blk, kwb).astype(jnp.float32) * scale
        pw = _masked_softmax(lw, mw)
        o_win = jnp.einsum('bhqk,bkd->bqhd', pw.astype(vw.dtype), vwb)
        return (g_blk[:, :, 0, :, None] * o_cmp + g_blk[:, :, 1, :, None] * o_slc
                + g_blk[:, :, 2, :, None] * o_win)

    out = lax.map(block, (jnp.arange(S // Q_BLOCK), _to_blocks(q), _to_blocks(gates)))
    return _from_blocks(out)


def _memory_attention(q, mem_k, mem_v):
    logits = jnp.einsum('bshd,bmhd->bhsm', q, mem_k).astype(jnp.float32) * HEAD_DIM ** -0.5
    p = jax.nn.softmax(logits, axis=-1).astype(mem_v.dtype)
    return jnp.einsum('bhsm,bmhd->bshd', p, mem_v)


def _layer(x, mem, positions, norm_g, w_in, kv_norm, w_uk, w_uv, fox_b, pe_k, pe_v,
           wc1_k, wc2_k, wc1_v, wc2_v, mem_norm, w_mem_kv, w_branch, w_out):
    B, S, _ = x.shape
    h = _rmsnorm(x, norm_g)
    p = _split_columns(h @ w_in)
    heads = lambda a: a.reshape(B, S, N_HEADS, HEAD_DIM)

    c_kv = _rmsnorm(p['dsa_ckv'], kv_norm)
    k_a = _rope(heads(c_kv @ w_uk), positions)
    v_a = heads(c_kv @ w_uv)
    q_a = _rope(heads(p['dsa_q']), positions)
    qi = _rope(p['idx_q'].reshape(B, S, IDX_HEADS, IDX_DIM), positions)
    ki = _rope(p['idx_k'], positions)
    y_a = _dsa_attention(q_a, k_a, v_a, qi, ki, p['idx_w'])

    log_f = jax.nn.log_sigmoid(p['fox_f'].astype(jnp.float32) + fox_b.astype(jnp.float32))
    y_b = _fox_attention(heads(p['fox_q']), heads(p['fox_k']), heads(p['fox_v']), log_f)

    y_c = _stick_breaking_attention(heads(p['sb_q']), heads(p['sb_k']), heads(p['sb_v']))

    q_d = _rope(heads(p['nsa_q']), positions)
    gates = jax.nn.sigmoid(p['nsa_g'].reshape(B, S, 3, N_HEADS))
    y_d = _nsa_attention(q_d, _rope(p['nsa_kc'], positions), p['nsa_vc'],
                         _rope(p['nsa_ks'], positions), p['nsa_vs'],
                         _rope(p['nsa_kw'], positions), p['nsa_vw'], gates,
                         pe_k, pe_v, wc1_k, wc2_k, wc1_v, wc2_v)

    mkv = _rmsnorm(mem, mem_norm) @ w_mem_kv
    mem_k = mkv[..., :BR_WIDTH].reshape(B, -1, N_HEADS, HEAD_DIM)
    mem_v = mkv[..., BR_WIDTH:].reshape(B, -1, N_HEADS, HEAD_DIM)
    y_e = _memory_attention(heads(p['mem_q']), mem_k, mem_v)

    ys = jnp.stack([
        y_a.reshape(B, S, BR_WIDTH) * jax.nn.silu(p['dsa_z']),
        y_b.reshape(B, S, BR_WIDTH) * jax.nn.silu(p['fox_z']),
        y_c.reshape(B, S, BR_WIDTH) * jax.nn.silu(p['sb_z']),
        y_d.reshape(B, S, BR_WIDTH) * jax.nn.silu(p['nsa_z']),
        y_e.reshape(B, S, BR_WIDTH) * jax.nn.silu(p['mem_z']),
    ], axis=2)
    merge = jax.nn.sigmoid(p['merge'].reshape(B, S, N_BRANCH, D_MODEL))
    merged = jnp.sum(merge * jnp.einsum('bsnc,ncd->bsnd', ys, w_branch), axis=2)
    return x + merged @ w_out


def setup_inputs(seed: int = 0) -> dict:
    key = jax.random.key(seed)
    ks = jax.random.split(key, 24)
    f32 = jnp.float32

    def nrm(k, shape, fan_in):
        return jax.random.normal(k, shape, f32) * fan_in ** -0.5

    def gain(k, shape):
        return 1.0 + 0.02 * jax.random.normal(k, shape, f32)

    return {
        'x': jax.random.normal(ks[0], (BATCH, SEQ, D_MODEL), f32),
        'mem': jax.random.normal(ks[1], (BATCH, N_MEM, D_MODEL), f32),
        'positions': jnp.broadcast_to(jnp.arange(SEQ, dtype=jnp.int32), (BATCH, SEQ)),
        'norm_g': gain(ks[2], (DEPTH, D_MODEL)),
        'w_in': nrm(ks[3], (DEPTH, D_MODEL, N_IN), D_MODEL),
        'kv_norm': gain(ks[4], (DEPTH, KV_LATENT)),
        'w_uk': nrm(ks[5], (DEPTH, KV_LATENT, BR_WIDTH), KV_LATENT),
        'w_uv': nrm(ks[6], (DEPTH, KV_LATENT, BR_WIDTH), KV_LATENT),
        'fox_bias': jnp.linspace(1.0, 4.0, N_HEADS, dtype=f32)[None, :] + 0.1 * jax.random.normal(ks[7], (DEPTH, N_HEADS), f32),
        'nsa_pe_k': 0.1 * jax.random.normal(ks[8], (DEPTH, CMP_LEN, NSA_KV_DIM), f32),
        'nsa_pe_v': 0.1 * jax.random.normal(ks[9], (DEPTH, CMP_LEN, NSA_KV_DIM), f32),
        'nsa_wc1_k': nrm(ks[10], (DEPTH, CMP_LEN * NSA_KV_DIM, CMP_HIDDEN), CMP_LEN * NSA_KV_DIM),
        'nsa_wc2_k': nrm(ks[11], (DEPTH, CMP_HIDDEN, NSA_KV_DIM), CMP_HIDDEN),
        'nsa_wc1_v': nrm(ks[12], (DEPTH, CMP_LEN * NSA_KV_DIM, CMP_HIDDEN), CMP_LEN * NSA_KV_DIM),
        'nsa_wc2_v': nrm(ks[13], (DEPTH, CMP_HIDDEN, NSA_KV_DIM), CMP_HIDDEN),
        'mem_norm': gain(ks[14], (DEPTH, D_MODEL)),
        'w_mem_kv': nrm(ks[15], (DEPTH, D_MODEL, 2 * BR_WIDTH), D_MODEL),
        'w_branch': nrm(ks[16], (DEPTH, N_BRANCH, BR_WIDTH, D_MODEL), BR_WIDTH),
        'w_out': nrm(ks[17], (DEPTH, D_MODEL, D_MODEL), D_MODEL),
        'final_norm': gain(ks[18], (D_MODEL,)),
    }


def reference(x, mem, positions, norm_g, w_in, kv_norm, w_uk, w_uv, fox_bias, nsa_pe_k, nsa_pe_v,
              nsa_wc1_k, nsa_wc2_k, nsa_wc1_v, nsa_wc2_v, mem_norm, w_mem_kv, w_branch, w_out, final_norm):
    for l in range(DEPTH):
        x = _layer(x, mem, positions, norm_g[l], w_in[l], kv_norm[l], w_uk[l], w_uv[l], fox_bias[l],
                   nsa_pe_k[l], nsa_pe_v[l], nsa_wc1_k[l], nsa_wc2_k[l], nsa_wc1_v[l], nsa_wc2_v[l],
                   mem_norm[l], w_mem_kv[l], w_branch[l], w_out[l])
    return _rmsnorm(x, final_norm)
```

```python
import functools

import jax
import jax.numpy as jnp
from jax import lax
from jax.experimental import pallas as pl
from jax.experimental.pallas import tpu as pltpu

F32 = jnp.float32
BF16 = jnp.bfloat16
I32 = jnp.int32

LANES = 128
N_HEADS = 4
HEAD_DIM = 64
BR_WIDTH = N_HEADS * HEAD_DIM
N_BRANCH = 5
ROPE_THETA = 10000.0
EPS = 1e-6
DSA_TOPK_MAX = 256
IDX_HEADS = 4
IDX_DIM = 32
KV_LATENT = 128
NSA_KV_DIM = 64
CMP_LEN = 32
CMP_STRIDE = 16
CMP_HIDDEN = 128
SLC_BLOCK = 64
SLC_TOPN = 16
WINDOW = 512
FORCE_SCORE = 1e4
Q_SCALE = HEAD_DIM ** -0.5
IDX_SCALE = (IDX_DIM ** -0.5) * (IDX_HEADS ** -0.5)
NEG = -1e30
VMEM_LIMIT = 56 * 1024 * 1024

IN_LAYOUT = (
    ('dsa_q', BR_WIDTH), ('dsa_ckv', KV_LATENT), ('idx_q', IDX_HEADS * IDX_DIM), ('idx_k', IDX_DIM),
    ('idx_w', IDX_HEADS), ('dsa_z', BR_WIDTH),
    ('fox_q', BR_WIDTH), ('fox_k', BR_WIDTH), ('fox_v', BR_WIDTH), ('fox_f', N_HEADS), ('fox_z', BR_WIDTH),
    ('sb_q', BR_WIDTH), ('sb_k', BR_WIDTH), ('sb_v', BR_WIDTH), ('sb_z', BR_WIDTH),
    ('nsa_q', BR_WIDTH), ('nsa_kc', NSA_KV_DIM), ('nsa_vc', NSA_KV_DIM), ('nsa_ks', NSA_KV_DIM),
    ('nsa_vs', NSA_KV_DIM), ('nsa_kw', NSA_KV_DIM), ('nsa_vw', NSA_KV_DIM), ('nsa_g', 3 * N_HEADS),
    ('nsa_z', BR_WIDTH), ('mem_q', BR_WIDTH), ('mem_z', BR_WIDTH), ('merge', N_BRANCH * 1024),
)

A_DQ, A_KA, A_VA, A_FQ, A_FK, A_FV, A_SQ, A_SK, A_SV, A_NQ, A_MQ = (2 * i for i in range(11))
A_QI, A_KI, A_KSW, A_VSW = 22, 23, 24, 25
A_COLS = 26 * LANES
SM_WI, SM_CUM, SM_GATE = 0, 4, 8


def _dot(a, b):
    return jnp.dot(a, b, preferred_element_type=F32)


def _dot_nt(a, b):
    return lax.dot_general(a, b, (((1,), (1,)), ((), ())), preferred_element_type=F32)


def _split3(x):
    hi = x.astype(BF16)
    r1 = x - hi.astype(F32)
    mid = r1.astype(BF16)
    lo = (r1 - mid.astype(F32)).astype(BF16)
    return hi, mid, lo


def _dot3(x, m01):
    hi, mid, lo = _split3(x)
    return _dot(hi, m01) + _dot(mid, m01) + _dot(lo, m01)


def _div_pow2(x, n):
    assert n & (n - 1) == 0
    return x >> (n.bit_length() - 1)


def _sigmoid(x):
    return 1.0 / (1.0 + jnp.exp(-x))


def _log_sigmoid(x):
    return jnp.minimum(x, 0.0) - jnp.log(1.0 + jnp.exp(-jnp.abs(x)))


def _rmsnorm(x, g):
    return x * lax.rsqrt(jnp.mean(x * x, axis=-1, keepdims=True) + EPS) * g


def _rope128(x, cos, sin_signed, half):
    lane = lax.broadcasted_iota(I32, x.shape, 1)
    first = (lane & (2 * half - 1)) < half
    partner = jnp.where(first, pltpu.roll(x, LANES - half, 1), pltpu.roll(x, half, 1))
    return x * cos + partner * sin_signed


def _head_pads(q):
    lane = lax.broadcasted_iota(I32, (q.shape[0], LANES), 1)
    lo = lane < HEAD_DIM
    zero = jnp.zeros((q.shape[0], LANES), q.dtype)
    out = []
    for pair in range(2):
        chunk = q[:, pair * LANES:(pair + 1) * LANES]
        out.append(jnp.where(lo, chunk, zero))
        out.append(jnp.where(lo, zero, chunk))
    return out


def _merge_heads(vals):
    lane = lax.broadcasted_iota(I32, vals[0].shape, 1)
    lo = lane < HEAD_DIM
    return jnp.concatenate([jnp.where(lo, vals[0], vals[1]), jnp.where(lo, vals[2], vals[3])], axis=1)


def _rope_table_kernel(pos_ref, inv_ref, sgn_ref, cos_ref, sin_ref):
    ang = pos_ref[0] * inv_ref[...]
    cos_ref[0] = jnp.cos(ang)
    sin_ref[0] = jnp.sin(ang) * sgn_ref[...]


def _rope_tables(positions, tm):
    B, S = positions.shape
    lane = jnp.arange(LANES)

    def inv_row(dh):
        inv = ROPE_THETA ** (-jnp.arange(0, dh, 2, dtype=F32) / dh)
        return inv[(lane % dh) % (dh // 2)]

    def sgn_row(dh):
        return jnp.where((lane % dh) < dh // 2, -1.0, 1.0).astype(F32)

    inv = jnp.concatenate([inv_row(HEAD_DIM), inv_row(IDX_DIM)])[None, :]
    sgn = jnp.concatenate([sgn_row(HEAD_DIM), sgn_row(IDX_DIM)])[None, :]
    pos = positions.astype(F32)[..., None]
    row = pl.BlockSpec((1, 2 * LANES), lambda b, i: (0, 0))
    tab = pl.BlockSpec((1, tm, 2 * LANES), lambda b, i: (b, i, 0))
    return pl.pallas_call(
        _rope_table_kernel,
        out_shape=(jax.ShapeDtypeStruct((B, S, 2 * LANES), F32),) * 2,
        grid=(B, S // tm),
        in_specs=[pl.BlockSpec((1, tm, 1), lambda b, i: (b, i, 0)), row, row],
        out_specs=(tab, tab),
        name="rope_tables",
    )(pos, inv, sgn)


W_DQ, W_CKV, W_QI, W_KI, W_SM, W_FQ, W_SQ, W_NQ, W_MQ, W_KVC, W_KSW, W_VSW = (
    0, 256, 384, 512, 640, 768, 1536, 2304, 2560, 2816, 2944, 3072)
W_COLS = 3200


def _proj_kernel(x_ref, g_ref, w_ref, kvn_ref, wuk_ref, wuv_ref, fb_ref, cos_ref, sin_ref,
                 a_ref, m_ref, h_scr, carry_scr, *, tm):
    h_scr[...] = _rmsnorm(x_ref[0], g_ref[...]).astype(BF16)
    c64, s64 = cos_ref[0, :, :LANES], sin_ref[0, :, :LANES]
    c32, s32 = cos_ref[0, :, LANES:], sin_ref[0, :, LANES:]

    def proj(c0, n):
        return _dot(h_scr[...], w_ref[:, c0:c0 + n])

    def put(unit, val):
        a_ref[0, :, unit * LANES:unit * LANES + val.shape[1]] = val.astype(BF16)

    def rope64(v, j):
        return _rope128(v[:, j * LANES:(j + 1) * LANES], c64, s64, HEAD_DIM // 2)

    p = proj(W_DQ, BR_WIDTH)
    for j in range(2):
        put(A_DQ + j, rope64(p, j) * Q_SCALE)
    c_kv = _rmsnorm(proj(W_CKV, KV_LATENT), kvn_ref[...]).astype(BF16)
    k_a = _dot(c_kv, wuk_ref[...])
    for j in range(2):
        put(A_KA + j, rope64(k_a, j))
    put(A_VA, _dot(c_kv, wuv_ref[...]))
    put(A_QI, _rope128(proj(W_QI, LANES), c32, s32, IDX_DIM // 2))
    put(A_KI, _rope128(proj(W_KI, LANES), c32, s32, IDX_DIM // 2))
    p = proj(W_SM, LANES)
    lane = lax.broadcasted_iota(I32, p.shape, 1)
    log_f = _log_sigmoid(p + fb_ref[...])
    r_i = lax.broadcasted_iota(I32, (tm, tm), 0)
    c_i = lax.broadcasted_iota(I32, (tm, tm), 1)
    tri = jnp.where(r_i >= c_i, 1.0, 0.0).astype(BF16)

    @pl.when(pl.program_id(1) == 0)
    def _():
        carry_scr[...] = jnp.zeros_like(carry_scr)

    cum = _dot3x(tri, log_f) + carry_scr[...]
    carry_scr[...] = cum[tm - 1:tm, :]
    m_ref[0, :, LANES:] = jnp.where(lane < SM_CUM, p * IDX_SCALE,
                                    jnp.where(lane < SM_GATE, cum, _sigmoid(p)))
    for c0, unit, scale in ((W_FQ, A_FQ, Q_SCALE), (W_FQ + 256, A_FK, 1.0), (W_FQ + 512, A_FV, 1.0),
                            (W_SQ, A_SQ, Q_SCALE), (W_SQ + 256, A_SK, 1.0), (W_SQ + 512, A_SV, 1.0),
                            (W_MQ, A_MQ, Q_SCALE)):
        p = proj(c0, BR_WIDTH)
        put(unit, p * scale if scale != 1.0 else p)
    p = proj(W_NQ, BR_WIDTH)
    for j in range(2):
        put(A_NQ + j, rope64(p, j) * Q_SCALE)
    p = proj(W_KVC, LANES)
    m_ref[0, :, :LANES] = jnp.where(lane < NSA_KV_DIM, rope64(p, 0), p)
    put(A_KSW, rope64(proj(W_KSW, LANES), 0))
    put(A_VSW, proj(W_VSW, LANES))


def _dot3x(m01, x):
    hi, mid, lo = _split3(x)
    return _dot(m01, hi) + _dot(m01, mid) + _dot(m01, lo)


def _proj_weight(w_in):
    off, o = {}, 0
    for name, n in IN_LAYOUT:
        off[name] = (o, n)
        o += n

    def col(name):
        s, n = off[name]
        return w_in[..., s:s + n]

    zeros = lambda n: jnp.zeros(w_in.shape[:-1] + (n,), w_in.dtype)
    groups = [col('dsa_q'), col('dsa_ckv'), col('idx_q'), col('idx_k'), col('idx_k'), col('idx_k'),
              col('idx_k'), col('idx_w'), col('fox_f'), col('nsa_g'), zeros(LANES - 20),
              col('fox_q'), col('fox_k'), col('fox_v'), col('sb_q'), col('sb_k'), col('sb_v'),
              col('nsa_q'), col('mem_q'), col('nsa_kc'), col('nsa_vc'), col('nsa_ks'), col('nsa_kw'),
              col('nsa_vs'), col('nsa_vw')]
    w1 = jnp.concatenate(groups, axis=-1).astype(BF16)
    wz = jnp.stack([col(n) for n in ('dsa_z', 'fox_z', 'sb_z', 'nsa_z', 'mem_z')], axis=-3).astype(BF16)
    s, n = off['merge']
    wm = w_in[..., s:s + n].reshape(w_in.shape[:-1] + (N_BRANCH, n // N_BRANCH))
    wm = jnp.moveaxis(wm, -2, -3).astype(BF16)
    return w1, wz, wm


def _proj_call(x, g, w1, kvn, wuk, wuv, fb, cos, sin, tm):
    B, S, D = x.shape
    full = lambda shape: pl.BlockSpec(shape, lambda b, i: (0,) * len(shape))
    tile = lambda n: pl.BlockSpec((1, tm, n), lambda b, i: (b, i, 0))
    return pl.pallas_call(
        functools.partial(_proj_kernel, tm=tm),
        out_shape=(jax.ShapeDtypeStruct((B, S, A_COLS), BF16),
                   jax.ShapeDtypeStruct((B, S, 2 * LANES), F32)),
        grid=(B, S // tm),
        in_specs=[tile(D), full((1, D)), full((D, W_COLS)), full((1, KV_LATENT)),
                  full((KV_LATENT, BR_WIDTH)), full((KV_LATENT, BR_WIDTH)), full((1, LANES)),
                  tile(2 * LANES), tile(2 * LANES)],
        out_specs=(tile(A_COLS), tile(2 * LANES)),
        scratch_shapes=[pltpu.VMEM((tm, D), BF16), pltpu.VMEM((1, LANES), F32)],
        compiler_params=pltpu.CompilerParams(dimension_semantics=("arbitrary", "arbitrary"),
                                             vmem_limit_bytes=VMEM_LIMIT),
        name="proj",
    )(x, g, w1, kvn, wuk, wuv, fb, cos, sin)


def _tile_counts(q0, tq, tk):
    return (q0 + 1) // tk, (q0 + tq + tk - 1) // tk


def _row_col(q0, s0, tq, tk):
    row = q0 + lax.broadcasted_iota(I32, (tq, tk), 0)
    col = s0 + lax.broadcasted_iota(I32, (tq, tk), 1)
    return row, col


def _softmax_step(s, valid, v_tile, m_scr, l_scr, acc_scr, h):
    if valid is not None:
        s = jnp.where(valid, s, NEG)
    m_prev = m_scr[h]
    m_new = jnp.maximum(m_prev, jnp.max(s, axis=1, keepdims=True))
    alpha = jnp.exp(m_prev - m_new)
    p = jnp.exp(s - m_new)
    if valid is not None:
        p = jnp.where(valid, p, 0.0)
    m_scr[h] = m_new
    l_scr[h] = alpha * l_scr[h] + jnp.sum(p, axis=1, keepdims=True)
    acc_scr[h] = alpha * acc_scr[h] + _dot(p.astype(BF16), v_tile)


def _softmax_init(m_scr, l_scr, acc_scr):
    m_scr[...] = jnp.full(m_scr.shape, NEG, F32)
    l_scr[...] = jnp.zeros(l_scr.shape, F32)
    acc_scr[...] = jnp.zeros(acc_scr.shape, F32)


def _softmax_out(l_scr, acc_scr, h):
    return acc_scr[h] / jnp.maximum(l_scr[h], 1e-30)


def _exact_softmax(s, valid):
    s = jnp.where(valid, s, NEG)
    m = jnp.max(s, axis=1, keepdims=True)
    e = jnp.where(valid, jnp.exp(s - m), 0.0)
    return e / jnp.maximum(jnp.sum(e, axis=1, keepdims=True), 1e-30)


def _attn_scratch(tq):
    return [pltpu.VMEM((N_HEADS, tq, 1), F32), pltpu.VMEM((N_HEADS, tq, 1), F32),
            pltpu.VMEM((N_HEADS, tq, LANES), F32)]


def _attn_params():
    return pltpu.CompilerParams(dimension_semantics=("arbitrary", "arbitrary"),
                                vmem_limit_bytes=VMEM_LIMIT)


def _a_block(rows, width_units, unit):
    w = width_units * LANES
    if rows is None:
        return lambda S: pl.BlockSpec((1, S, w), lambda b, i: (b, 0, unit // width_units))
    return lambda S: pl.BlockSpec((1, rows, w), lambda b, i: (b, i, unit // width_units))


def _fox_kernel(q_ref, k_ref, v_ref, sm_ref, ck_ref, o_ref, m_scr, l_scr, acc_scr, *, tq, tk):
    q0 = pl.program_id(1) * tq
    qp = _head_pads(q_ref[0])
    cq = sm_ref[0]
    _softmax_init(m_scr, l_scr, acc_scr)
    n_full, n_total = _tile_counts(q0, tq, tk)

    def tile(j, masked):
        s0 = pl.multiple_of(j * tk, tk)
        valid = None
        if masked:
            row, col = _row_col(q0, s0, tq, tk)
            valid = col <= row
        for pair in range(2):
            kp = k_ref[0, pl.ds(s0, tk), pair * LANES:(pair + 1) * LANES]
            vp = v_ref[0, pl.ds(s0, tk), pair * LANES:(pair + 1) * LANES]
            for h in (2 * pair, 2 * pair + 1):
                bias = cq[:, SM_CUM + h:SM_CUM + h + 1] - ck_ref[0, h:h + 1, pl.ds(s0, tk)]
                _softmax_step(_dot_nt(qp[h], kp) + bias, valid, vp, m_scr, l_scr, acc_scr, h)

    lax.fori_loop(0, n_full, lambda j, c: (tile(j, False), c)[1], 0)
    lax.fori_loop(n_full, n_total, lambda j, c: (tile(j, True), c)[1], 0)
    o_ref[0] = _merge_heads([_softmax_out(l_scr, acc_scr, h) for h in range(N_HEADS)])


def _fox_call(a, misc, cum_t, tq, tk):
    B, S, _ = a.shape
    return pl.pallas_call(
        functools.partial(_fox_kernel, tq=tq, tk=tk),
        out_shape=jax.ShapeDtypeStruct((B, S, BR_WIDTH), F32),
        grid=(B, S // tq),
        in_specs=[_a_block(tq, 2, A_FQ)(S), _a_block(None, 2, A_FK)(S), _a_block(None, 2, A_FV)(S),
                  pl.BlockSpec((1, tq, LANES), lambda b, i: (b, i, 1)),
                  pl.BlockSpec((1, 8, S), lambda b, i: (b, 0, 0))],
        out_specs=pl.BlockSpec((1, tq, BR_WIDTH), lambda b, i: (b, i, 0)),
        scratch_shapes=_attn_scratch(tq),
        compiler_params=_attn_params(),
        name="fox",
    )(a, a, a, misc, cum_t)


def _sb_kernel(q_ref, k_ref, v_ref, o_ref, r_scr, acc_scr, *, tq, tk):
    q0 = pl.program_id(1) * tq
    qp = _head_pads(q_ref[0])
    r_scr[...] = jnp.zeros(r_scr.shape, F32)
    acc_scr[...] = jnp.zeros(acc_scr.shape, F32)
    n_full, n_total = _tile_counts(q0, tq, tk)
    j_i = lax.broadcasted_iota(I32, (tk, tk), 0)
    s_i = lax.broadcasted_iota(I32, (tk, tk), 1)
    later_keys = jnp.where(j_i > s_i, 1.0, 0.0).astype(BF16)

    def tile(j, masked):
        s0 = pl.multiple_of(j * tk, tk)
        strict = None
        if masked:
            row, col = _row_col(q0, s0, tq, tk)
            strict = col < row
        for pair in range(2):
            kp = k_ref[0, pl.ds(s0, tk), pair * LANES:(pair + 1) * LANES]
            vp = v_ref[0, pl.ds(s0, tk), pair * LANES:(pair + 1) * LANES]
            for h in (2 * pair, 2 * pair + 1):
                z = _dot_nt(qp[h], kp)
                log_take = _log_sigmoid(z)
                log_keep = log_take - z
                if masked:
                    log_keep = jnp.where(strict, log_keep, 0.0)
                r_prev = r_scr[h]
                later = r_prev + _dot3(log_keep, later_keys)
                a = jnp.exp(log_take + later)
                if masked:
                    a = jnp.where(strict, a, 0.0)
                acc_scr[h] = acc_scr[h] + _dot(a.astype(BF16), vp)
                r_scr[h] = r_prev + jnp.sum(log_keep, axis=1, keepdims=True)

    lax.fori_loop(0, n_total - n_full, lambda i, c: (tile(n_total - 1 - i, True), c)[1], 0)
    lax.fori_loop(0, n_full, lambda i, c: (tile(n_full - 1 - i, False), c)[1], 0)
    o_ref[0] = _merge_heads([acc_scr[h] for h in range(N_HEADS)])


def _sb_call(a, tq, tk):
    B, S, _ = a.shape
    return pl.pallas_call(
        functools.partial(_sb_kernel, tq=tq, tk=tk),
        out_shape=jax.ShapeDtypeStruct((B, S, BR_WIDTH), F32),
        grid=(B, S // tq),
        in_specs=[_a_block(tq, 2, A_SQ)(S), _a_block(None, 2, A_SK)(S), _a_block(None, 2, A_SV)(S)],
        out_specs=pl.BlockSpec((1, tq, BR_WIDTH), lambda b, i: (b, i, 0)),
        scratch_shapes=[pltpu.VMEM((N_HEADS, tq, 1), F32), pltpu.VMEM((N_HEADS, tq, LANES), F32)],
        compiler_params=_attn_params(),
        name="stick_breaking",
    )(a, a, a)


INT_MIN = -2 ** 31
KEY_ABOVE_NEG_INF = 0x80800000 - 2 ** 32


def _key_to_float(c):
    bits = c ^ ((c >> 31) & 0x7FFFFFFF)
    return lax.bitcast_convert_type(bits, F32)


def _dsa_kernel(q_ref, k_ref, v_ref, qi_ref, ki_ref, sm_ref, o_ref,
                sc_scr, m_scr, l_scr, acc_scr, *, tq, tk, k_sel):
    q0 = pl.program_id(1) * tq
    n_full, n_total = _tile_counts(q0, tq, tk)
    lane = lax.broadcasted_iota(I32, (tq, LANES), 1)
    qi = qi_ref[0]
    zero = jnp.zeros_like(qi)
    qi_pad = [jnp.where((lane >= IDX_DIM * h) & (lane < IDX_DIM * (h + 1)), qi, zero)
              for h in range(IDX_HEADS)]
    wi = sm_ref[0]

    def score_tile(j, masked):
        s0 = pl.multiple_of(j * tk, tk)
        kt = ki_ref[0, pl.ds(s0, tk), :]
        sc = jnp.zeros((tq, tk), F32)
        for h in range(IDX_HEADS):
            sc = sc + wi[:, SM_WI + h:SM_WI + h + 1] * jnp.maximum(_dot_nt(qi_pad[h], kt), 0.0)
        if masked:
            row, col = _row_col(q0, s0, tq, tk)
            sc = jnp.where(col <= row, sc, -jnp.inf)
        sc_scr[:, pl.ds(s0, tk)] = sc

    lax.fori_loop(0, n_full, lambda j, c: (score_tile(j, False), c)[1], 0)
    lax.fori_loop(n_full, n_total, lambda j, c: (score_tile(j, True), c)[1], 0)

    def count(pred, thr):
        thr_b = jnp.broadcast_to(thr, (tq, LANES))

        def body(j, acc):
            s0 = pl.multiple_of(j * tk, tk)
            for c in range(tk // LANES):
                blk = sc_scr[:, pl.ds(s0 + c * LANES, LANES)]
                acc = acc + jnp.where(pred(blk, thr_b), 1.0, 0.0)
            return acc

        acc = lax.fori_loop(0, n_total, body, jnp.zeros((tq, LANES), F32))
        return jnp.sum(acc, axis=1, keepdims=True)

    ge = lambda a, b: a >= b
    kf = float(k_sel)
    key = jnp.where(count(ge, jnp.zeros((tq, 1), F32)) >= kf, 0, INT_MIN).astype(I32)

    def bisect(i, key):
        cand = key + jnp.left_shift(jnp.int32(1), 30 - i)
        return jnp.where(count(ge, _key_to_float(cand)) >= kf, cand, key)

    key = lax.fori_loop(0, 31, bisect, key)
    thr = _key_to_float(jnp.maximum(key, KEY_ABOVE_NEG_INF))
    n_ge = count(ge, thr)
    need = kf - count(lambda a, b: a > b, thr)

    @pl.when(jnp.max(n_ge) > kf)
    def _():
        j_i = lax.broadcasted_iota(I32, (LANES, LANES), 0)
        s_i = lax.broadcasted_iota(I32, (LANES, LANES), 1)
        earlier = jnp.where(j_i < s_i, 1.0, 0.0).astype(BF16)

        def body(c, seen):
            s0 = pl.multiple_of(c * LANES, LANES)
            blk = sc_scr[:, pl.ds(s0, LANES)]
            eq = blk == thr
            eq_f = jnp.where(eq, 1.0, 0.0)
            rank = seen + _dot(eq_f.astype(BF16), earlier)
            sc_scr[:, pl.ds(s0, LANES)] = jnp.where(eq & (rank >= need), -jnp.inf, blk)
            return seen + jnp.sum(eq_f, axis=1, keepdims=True)

        lax.fori_loop(0, n_total * (tk // LANES), body, jnp.zeros((tq, 1), F32))

    qp = _head_pads(q_ref[0])
    _softmax_init(m_scr, l_scr, acc_scr)

    def attn_tile(j, c):
        s0 = pl.multiple_of(j * tk, tk)
        valid = sc_scr[:, pl.ds(s0, tk)] >= thr
        for pair in range(2):
            kp = k_ref[0, pl.ds(s0, tk), pair * LANES:(pair + 1) * LANES]
            vp = v_ref[0, pl.ds(s0, tk), pair * LANES:(pair + 1) * LANES]
            for h in (2 * pair, 2 * pair + 1):
                _softmax_step(_dot_nt(qp[h], kp), valid, vp, m_scr, l_scr, acc_scr, h)
        return c

    lax.fori_loop(0, n_total, attn_tile, 0)
    o_ref[0] = _merge_heads([_softmax_out(l_scr, acc_scr, h) for h in range(N_HEADS)])


def _dsa_call(a, misc, tq, tk):
    B, S, _ = a.shape
    return pl.pallas_call(
        functools.partial(_dsa_kernel, tq=tq, tk=tk, k_sel=min(DSA_TOPK_MAX, S // 4)),
        out_shape=jax.ShapeDtypeStruct((B, S, BR_WIDTH), F32),
        grid=(B, S // tq),
        in_specs=[_a_block(tq, 2, A_DQ)(S), _a_block(None, 2, A_KA)(S), _a_block(None, 2, A_VA)(S),
                  _a_block(tq, 1, A_QI)(S), _a_block(None, 1, A_KI)(S),
                  pl.BlockSpec((1, tq, LANES), lambda b, i: (b, i, 1))],
        out_specs=pl.BlockSpec((1, tq, BR_WIDTH), lambda b, i: (b, i, 0)),
        scratch_shapes=[pltpu.VMEM((tq, S), F32)] + _attn_scratch(tq),
        compiler_params=_attn_params(),
        name="dsa",
    )(a, a, a, a, a, misc)


def _compress_kernel(xk_ref, xv_ref, pek_ref, pev_ref, w1k_ref, w1v_ref, w2k_ref, w2v_ref, o_ref, *, nc):
    def hidden(x_ref, pe_ref, w1_ref):
        x = x_ref[0]
        first = _dot((x + pe_ref[0:1, :]).astype(BF16), w1_ref[0])
        second = _dot((x + pe_ref[1:2, :]).astype(BF16), w1_ref[1])
        pre = first + pltpu.roll(second, nc - 1, 0)
        return (pre * _sigmoid(pre)).astype(BF16)

    o_ref[0] = (_dot(hidden(xk_ref, pek_ref, w1k_ref), w2k_ref[...])
                + _dot(hidden(xv_ref, pev_ref, w1v_ref), w2v_ref[...])).astype(BF16)


def _compress_call(xk, xv, pek, pev, w1k, w1v, w2k, w2v):
    B, nc, width = xk.shape
    full = lambda shape: pl.BlockSpec(shape, lambda b: (0,) * len(shape))
    tok = pl.BlockSpec((1, nc, width), lambda b: (b, 0, 0))
    return pl.pallas_call(
        functools.partial(_compress_kernel, nc=nc),
        out_shape=jax.ShapeDtypeStruct((B, nc, LANES), BF16),
        grid=(B,),
        in_specs=[tok, tok, full((2, width)), full((2, width)), full((2, width, CMP_HIDDEN)),
                  full((2, width, CMP_HIDDEN)), full((CMP_HIDDEN, LANES)), full((CMP_HIDDEN, LANES))],
        out_specs=pl.BlockSpec((1, nc, LANES), lambda b: (b, 0, 0)),
        compiler_params=pltpu.CompilerParams(dimension_semantics=("arbitrary",),
                                             vmem_limit_bytes=VMEM_LIMIT),
        name="nsa_compress",
    )(xk, xv, pek, pev, w1k, w1v, w2k, w2v)


def _nsa_kernel(q_ref, kvc_ref, ksw_ref, vsw_ref, sm_ref, o_ref,
                imp_scr, m_scr, l_scr, acc_scr, *, tq, tk, n_blk, n_sel, nbp):
    q0 = pl.program_id(1) * tq
    n_full, n_total = _tile_counts(q0, tq, tk)
    ncp = kvc_ref.shape[1]
    lane = lax.broadcasted_iota(I32, (tq, LANES), 1)
    lo = lane < HEAD_DIM
    q = q_ref[0]
    zero = jnp.zeros((tq, LANES), q.dtype)
    q_lo, q_hi = [], []
    for h in range(N_HEADS):
        chunk = q[:, (h // 2) * LANES:(h // 2 + 1) * LANES]
        swapped = pltpu.roll(chunk, HEAD_DIM, 1)
        q_lo.append(jnp.where(lo, chunk if h % 2 == 0 else swapped, zero))
        q_hi.append(jnp.where(lo, zero, swapped if h % 2 == 0 else chunk))
    gates = sm_ref[0]
    gate = lambda br, h: gates[:, SM_GATE + br * N_HEADS + h:SM_GATE + br * N_HEADS + h + 1]
    t_col = q0 + lax.broadcasted_iota(I32, (tq, 1), 0)

    kvc = kvc_ref[0]
    cmp_end = lax.broadcasted_iota(I32, (tq, ncp), 1) * CMP_STRIDE + (CMP_LEN - 1)
    cmp_ok = cmp_end <= t_col
    o_cmp, p_sum = [], jnp.zeros((tq, ncp), F32)
    for h in range(N_HEADS):
        pc = _exact_softmax(_dot_nt(q_lo[h], kvc), cmp_ok)
        o_cmp.append(_dot(pc.astype(BF16), kvc))
        p_sum = p_sum + pc
    per_blk = SLC_BLOCK // CMP_STRIDE
    c_i = lax.broadcasted_iota(I32, (ncp, nbp), 0)
    b_i = lax.broadcasted_iota(I32, (ncp, nbp), 1)
    group = jnp.where(_div_pow2(c_i, per_blk) == b_i, 1.0, 0.0).astype(BF16)
    imp = _dot3(p_sum, group)
    blk = lax.broadcasted_iota(I32, (tq, nbp), 1)
    cur = _div_pow2(t_col, SLC_BLOCK)
    forced = (blk == 0) | (blk == cur) | (blk == cur - 1)
    imp = jnp.where(forced, FORCE_SCORE, jnp.where(blk <= cur, imp, -1.0))
    imp = jnp.where(blk < n_blk, imp, -2.0)
    imp_t = imp.T
    imp_scr[...] = imp_t
    blk_row = lax.broadcasted_iota(I32, (nbp, tq), 0)

    def rank_body(jp, cnt):
        other = imp_scr[pl.ds(jp, 1), :]
        ahead = (other > imp_t) | ((other == imp_t) & (jp < blk_row))
        return cnt + jnp.where(ahead, 1.0, 0.0)

    rank = lax.fori_loop(0, n_blk, rank_body, jnp.zeros((nbp, tq), F32))
    sel = jnp.where(rank < float(n_sel), 1.0, 0.0).T.astype(BF16)

    _softmax_init(m_scr, l_scr, acc_scr)

    def slc_tile(j, masked):
        s0 = pl.multiple_of(j * tk, tk)
        e_blk = lax.broadcasted_iota(I32, (nbp, tk), 0)
        e_pos = s0 + lax.broadcasted_iota(I32, (nbp, tk), 1)
        expand = jnp.where(_div_pow2(e_pos, SLC_BLOCK) == e_blk, 1.0, 0.0).astype(BF16)
        valid = _dot(sel, expand) > 0.5
        if masked:
            row, col = _row_col(q0, s0, tq, tk)
            valid = valid & (col <= row)
        kt = ksw_ref[0, pl.ds(s0, tk), :]
        vt = vsw_ref[0, pl.ds(s0, tk), :]
        for h in range(N_HEADS):
            _softmax_step(_dot_nt(q_lo[h], kt), valid, vt, m_scr, l_scr, acc_scr, h)

    lax.fori_loop(0, n_full, lambda j, c: (slc_tile(j, False), c)[1], 0)
    lax.fori_loop(n_full, n_total, lambda j, c: (slc_tile(j, True), c)[1], 0)

    span = WINDOW + tq
    w0 = pl.multiple_of(jnp.maximum(q0 - WINDOW, 0), tq)
    kw = ksw_ref[0, pl.ds(w0, span), :]
    vw = vsw_ref[0, pl.ds(w0, span), :]
    dist = t_col - (w0 + lax.broadcasted_iota(I32, (tq, span), 1))
    win_ok = (dist >= 0) & (dist < WINDOW)
    outs = []
    for h in range(N_HEADS):
        pw = _exact_softmax(_dot_nt(q_hi[h], kw), win_ok)
        o_win = _dot(pw.astype(BF16), vw)
        o_slc = pltpu.roll(_softmax_out(l_scr, acc_scr, h), HEAD_DIM, 1)
        y = gate(0, h) * o_cmp[h] + gate(1, h) * o_slc + gate(2, h) * o_win
        outs.append(pltpu.roll(y, HEAD_DIM, 1) if h % 2 == 0 else y)
    o_ref[0] = _merge_heads(outs)


def _nsa_call(a, kvc, misc, tq, tk):
    B, S, _ = a.shape
    n_blk = S // SLC_BLOCK
    nbp = max(LANES, n_blk)
    return pl.pallas_call(
        functools.partial(_nsa_kernel, tq=tq, tk=tk, n_blk=n_blk, n_sel=min(SLC_TOPN, n_blk), nbp=nbp),
        out_shape=jax.ShapeDtypeStruct((B, S, BR_WIDTH), F32),
        grid=(B, S // tq),
        in_specs=[_a_block(tq, 2, A_NQ)(S),
                  pl.BlockSpec((1, kvc.shape[1], LANES), lambda b, i: (b, 0, 0)),
                  _a_block(None, 1, A_KSW)(S), _a_block(None, 1, A_VSW)(S),
                  pl.BlockSpec((1, tq, LANES), lambda b, i: (b, i, 1))],
        out_specs=pl.BlockSpec((1, tq, BR_WIDTH), lambda b, i: (b, i, 0)),
        scratch_shapes=[pltpu.VMEM((nbp, tq), F32)] + _attn_scratch(tq),
        compiler_params=_attn_params(),
        name="nsa",
    )(a, kvc, a, a, misc)


def _mem_kv_kernel(mem_ref, g_ref, w_ref, o_ref):
    o_ref[0] = _dot(_rmsnorm(mem_ref[0], g_ref[...]).astype(BF16), w_ref[...]).astype(BF16)


def _mem_kv_call(mem, g, w):
    B, M, D = mem.shape
    return pl.pallas_call(
        _mem_kv_kernel,
        out_shape=jax.ShapeDtypeStruct((B, M, 2 * BR_WIDTH), BF16),
        grid=(B,),
        in_specs=[pl.BlockSpec((1, M, D), lambda b: (b, 0, 0)), pl.BlockSpec((1, D), lambda b: (0, 0)),
                  pl.BlockSpec((D, 2 * BR_WIDTH), lambda b: (0, 0))],
        out_specs=pl.BlockSpec((1, M, 2 * BR_WIDTH), lambda b: (b, 0, 0)),
        name="mem_kv",
    )(mem, g, w)


def _out_kernel(x_ref, g_ref, ya_ref, yb_ref, yc_ref, yd_ref, mq_ref, mkv_ref, wz_ref, wm_ref,
                wb_ref, wo_ref, fg_ref, o_ref, h_scr, *, final):
    x = x_ref[0]
    h_scr[...] = _rmsnorm(x, g_ref[...]).astype(BF16)
    qp = _head_pads(mq_ref[0])
    outs = []
    for h in range(N_HEADS):
        pair = h // 2
        mk = mkv_ref[0, :, pair * LANES:(pair + 1) * LANES]
        mv = mkv_ref[0, :, BR_WIDTH + pair * LANES:BR_WIDTH + (pair + 1) * LANES]
        s = _dot_nt(qp[h], mk)
        e = jnp.exp(s - jnp.max(s, axis=1, keepdims=True))
        p = e / jnp.sum(e, axis=1, keepdims=True)
        outs.append(_dot(p.astype(BF16), mv))
    y_e = _merge_heads(outs)
    merged = jnp.zeros(x.shape, F32)
    for n, y in enumerate((ya_ref[0], yb_ref[0], yc_ref[0], yd_ref[0], y_e)):
        z = _dot(h_scr[...], wz_ref[n])
        ys = (y * (z * _sigmoid(z))).astype(BF16)
        merged = merged + _sigmoid(_dot(h_scr[...], wm_ref[n])) * _dot(ys, wb_ref[n])
    out = x + _dot(merged.astype(BF16), wo_ref[...])
    o_ref[0] = _rmsnorm(out, fg_ref[...]) if final else out


def _out_call(x, g, ys, a, mkv, wz, wm, wb, wo, fg, final, tm):
    B, S, D = x.shape
    M = mkv.shape[1]
    full = lambda shape: pl.BlockSpec(shape, lambda b, i: (0,) * len(shape))
    tile = lambda n: pl.BlockSpec((1, tm, n), lambda b, i: (b, i, 0))
    return pl.pallas_call(
        functools.partial(_out_kernel, final=final),
        out_shape=jax.ShapeDtypeStruct((B, S, D), F32),
        grid=(B, S // tm),
        in_specs=[tile(D), full((1, D))] + [tile(BR_WIDTH)] * 4
                 + [_a_block(tm, 2, A_MQ)(S), pl.BlockSpec((1, M, 2 * BR_WIDTH), lambda b, i: (b, 0, 0)),
                    full((N_BRANCH, D, BR_WIDTH)), full((N_BRANCH, D, D)), full((N_BRANCH, BR_WIDTH, D)),
                    full((D, D)), full((1, D))],
        out_specs=tile(D),
        scratch_shapes=[pltpu.VMEM((tm, D), BF16)],
        compiler_params=pltpu.CompilerParams(dimension_semantics=("arbitrary", "arbitrary"),
                                             vmem_limit_bytes=VMEM_LIMIT),
        name="merge_out",
    )(x, g, *ys, a, mkv, wz, wm, wb, wo, fg)


TM_PROJ = 256
TQ = 128
TK = 256


def kernel(x, mem, positions, norm_g, w_in, kv_norm, w_uk, w_uv, fox_bias, nsa_pe_k, nsa_pe_v,
           nsa_wc1_k, nsa_wc2_k, nsa_wc1_v, nsa_wc2_v, mem_norm, w_mem_kv, w_branch, w_out, final_norm):
    B, S, D = x.shape
    depth = norm_g.shape[0]
    assert S % TM_PROJ == 0 and S % TK == 0 and S >= WINDOW + TQ and D == 1024
    cos, sin = _rope_tables(positions, TM_PROJ)
    w1, wz, wm = _proj_weight(w_in)
    fb = jnp.zeros((depth, 1, LANES), F32).at[:, 0, SM_CUM:SM_CUM + N_HEADS].set(fox_bias)
    nc = S // CMP_STRIDE
    tok_w = CMP_STRIDE * NSA_KV_DIM
    pad_k = jnp.zeros((depth, CMP_HIDDEN, LANES), F32).at[..., :NSA_KV_DIM].set(nsa_wc2_k).astype(BF16)
    pad_v = jnp.zeros((depth, CMP_HIDDEN, LANES), F32).at[..., NSA_KV_DIM:].set(nsa_wc2_v).astype(BF16)
    for l in range(depth):
        a, misc = _proj_call(x, norm_g[l][None], w1[l], kv_norm[l][None], w_uk[l].astype(BF16),
                             w_uv[l].astype(BF16), fb[l], cos, sin, TM_PROJ)
        cum_t = jnp.pad(jnp.swapaxes(misc[:, :, LANES + SM_CUM:LANES + SM_CUM + N_HEADS], 1, 2),
                        ((0, 0), (0, 8 - N_HEADS), (0, 0)))
        kvc = _compress_call(
            misc[:, :, :NSA_KV_DIM].reshape(B, nc, tok_w),
            misc[:, :, NSA_KV_DIM:LANES].reshape(B, nc, tok_w),
            nsa_pe_k[l].reshape(2, tok_w), nsa_pe_v[l].reshape(2, tok_w),
            nsa_wc1_k[l].reshape(2, tok_w, CMP_HIDDEN).astype(BF16),
            nsa_wc1_v[l].reshape(2, tok_w, CMP_HIDDEN).astype(BF16), pad_k[l], pad_v[l])
        ys = (_dsa_call(a, misc, TQ, TK), _fox_call(a, misc, cum_t, TQ, TK), _sb_call(a, TQ, TK),
              _nsa_call(a, kvc, misc, TQ, TK))
        mkv = _mem_kv_call(mem, mem_norm[l][None], w_mem_kv[l].astype(BF16))
        x = _out_call(x, norm_g[l][None], ys, a, mkv, wz[l], wm[l], w_branch[l].astype(BF16),
                      w_out[l].astype(BF16), final_norm[None], l == depth - 1, TM_PROJ)
    return x
```

```python
import functools

import jax
import jax.numpy as jnp
from jax import lax
from jax.experimental import pallas as pl
from jax.experimental.pallas import tpu as pltpu

F32 = jnp.float32
BF16 = jnp.bfloat16
I32 = jnp.int32

LANES = 128
SUBLANES = 8
N_HEADS = 4
HEAD_DIM = 64
BR_WIDTH = N_HEADS * HEAD_DIM
N_BRANCH = 5
ROPE_THETA = 10000.0
EPS = 1e-6
DSA_TOPK_MAX = 256
IDX_HEADS = 4
IDX_DIM = 32
KV_LATENT = 128
NSA_KV_DIM = 64
CMP_LEN = 32
CMP_STRIDE = 16
CMP_HIDDEN = 128
SLC_BLOCK = 64
SLC_TOPN = 16
WINDOW = 512
FORCE_SCORE = 1e4
Q_SCALE = HEAD_DIM ** -0.5
IDX_SCALE = (IDX_DIM ** -0.5) * (IDX_HEADS ** -0.5)
NEG = -1e30
VMEM_LIMIT = 56 * 1024 * 1024
T = 256
ONES_ROWS = 16
HV_ROWS = HEAD_DIM + ONES_ROWS
VSW_ROWS = HV_ROWS + NSA_KV_DIM

IN_LAYOUT = (
    ('dsa_q', BR_WIDTH), ('dsa_ckv', KV_LATENT), ('idx_q', IDX_HEADS * IDX_DIM), ('idx_k', IDX_DIM),
    ('idx_w', IDX_HEADS), ('dsa_z', BR_WIDTH),
    ('fox_q', BR_WIDTH), ('fox_k', BR_WIDTH), ('fox_v', BR_WIDTH), ('fox_f', N_HEADS), ('fox_z', BR_WIDTH),
    ('sb_q', BR_WIDTH), ('sb_k', BR_WIDTH), ('sb_v', BR_WIDTH), ('sb_z', BR_WIDTH),
    ('nsa_q', BR_WIDTH), ('nsa_kc', NSA_KV_DIM), ('nsa_vc', NSA_KV_DIM), ('nsa_ks', NSA_KV_DIM),
    ('nsa_vs', NSA_KV_DIM), ('nsa_kw', NSA_KV_DIM), ('nsa_vw', NSA_KV_DIM), ('nsa_g', 3 * N_HEADS),
    ('nsa_z', BR_WIDTH), ('mem_q', BR_WIDTH), ('mem_z', BR_WIDTH), ('merge', N_BRANCH * 1024),
)

A_FK, A_KA, A_SK, A_KI, A_KSW, A_MQ = 0, 4, 6, 8, 9, 10
A_COLS = 12 * LANES
SM_WI, SM_CUM, SM_GATE = 0, 4, 8
FOX_CK, FOX_ONE, FOX_END = HEAD_DIM, HEAD_DIM + 3, HEAD_DIM + 6


def _dot(a, b):
    return jnp.dot(a, b, preferred_element_type=F32)


def _dot_nt(a, b):
    return lax.dot_general(a, b, (((1,), (1,)), ((), ())), preferred_element_type=F32)


def _split3(x):
    hi = x.astype(BF16)
    r1 = x - hi.astype(F32)
    mid = r1.astype(BF16)
    lo = (r1 - mid.astype(F32)).astype(BF16)
    return hi, mid, lo


def _dot3x(m01, x):
    hi, mid, lo = _split3(x)
    return _dot(m01, hi) + _dot(m01, mid) + _dot(m01, lo)


def _div_pow2(x, n):
    assert n & (n - 1) == 0
    return x >> (n.bit_length() - 1)


def _sigmoid(x):
    return 1.0 / (1.0 + jnp.exp(-x))


def _log_sigmoid(x):
    return jnp.minimum(x, 0.0) - jnp.log(1.0 + jnp.exp(-jnp.abs(x)))


def _rmsnorm(x, g):
    return x * lax.rsqrt(jnp.mean(x * x, axis=-1, keepdims=True) + EPS) * g


def _rope128(x, cos, sin_signed, half):
    lane = lax.broadcasted_iota(I32, x.shape, 1)
    first = (lane & (2 * half - 1)) < half
    partner = jnp.where(first, pltpu.roll(x, LANES - half, 1), pltpu.roll(x, half, 1))
    return x * cos + partner * sin_signed


def _rope_table_kernel(pos_ref, inv_ref, sgn_ref, cos_ref, sin_ref):
    ang = pos_ref[0] * inv_ref[...]
    cos_ref[0] = jnp.cos(ang)
    sin_ref[0] = jnp.sin(ang) * sgn_ref[...]


def _rope_tables(positions):
    B, S = positions.shape
    lane = jnp.arange(LANES)

    def inv_row(dh):
        inv = ROPE_THETA ** (-jnp.arange(0, dh, 2, dtype=F32) / dh)
        return inv[(lane % dh) % (dh // 2)]

    def sgn_row(dh):
        return jnp.where((lane % dh) < dh // 2, -1.0, 1.0).astype(F32)

    inv = jnp.concatenate([inv_row(HEAD_DIM), inv_row(IDX_DIM)])[None, :]
    sgn = jnp.concatenate([sgn_row(HEAD_DIM), sgn_row(IDX_DIM)])[None, :]
    pos = positions.astype(F32)[..., None]
    row = pl.BlockSpec((1, 2 * LANES), lambda b, i: (0, 0))
    tab = pl.BlockSpec((1, T, 2 * LANES), lambda b, i: (b, i, 0))
    return pl.pallas_call(
        _rope_table_kernel,
        out_shape=(jax.ShapeDtypeStruct((B, S, 2 * LANES), F32),) * 2,
        grid=(B, S // T),
        in_specs=[pl.BlockSpec((1, T, 1), lambda b, i: (b, i, 0)), row, row],
        out_specs=(tab, tab),
        name="rope_tables",
    )(pos, inv, sgn)


W_DQ, W_CKV, W_QI, W_KI, W_SM, W_FQ, W_SQ, W_NQ, W_MQ, W_KVC, W_KSW, W_VSW = (
    0, 256, 384, 512, 640, 768, 1536, 2304, 2560, 2816, 2944, 3072)
W_COLS = 3200

PROJ_OUTS = (
    ('keys', A_COLS, BF16), ('kvtok', LANES, F32),
    ('dq_t', BR_WIDTH, BF16), ('fq_t', N_HEADS * LANES, BF16), ('sq_t', BR_WIDTH, BF16),
    ('nq_t', BR_WIDTH, BF16), ('qi_t', LANES, BF16), ('sm_t', LANES, F32),
    ('va_t', N_HEADS * HV_ROWS, BF16), ('fv_t', N_HEADS * HV_ROWS, BF16), ('sv_t', BR_WIDTH, BF16),
    ('vsw_t', VSW_ROWS, BF16),
)
ROW_MAJOR = ('keys', 'kvtok')


def _proj_kernel(x_ref, g_ref, w_ref, kvn_ref, wuk_ref, wuv_ref, fb_ref, cos_ref, sin_ref,
                 *refs):
    out = dict(zip([n for n, _, _ in PROJ_OUTS], refs))
    h_scr, carry_scr = refs[len(PROJ_OUTS):]
    keys = out['keys']
    h_scr[...] = _rmsnorm(x_ref[0], g_ref[...]).astype(BF16)
    c64, s64 = cos_ref[0, :, :LANES], sin_ref[0, :, :LANES]
    c32, s32 = cos_ref[0, :, LANES:], sin_ref[0, :, LANES:]
    lane = lax.broadcasted_iota(I32, (T, LANES), 1)
    ones_rows = jnp.ones((ONES_ROWS, T), BF16)

    def proj(c0, n):
        return _dot(h_scr[...], w_ref[:, c0:c0 + n])

    def put(unit, val):
        keys[0, :, unit * LANES:unit * LANES + val.shape[1]] = val.astype(BF16)

    def rope64(v, j):
        return _rope128(v[:, j * LANES:(j + 1) * LANES], c64, s64, HEAD_DIM // 2)

    def put_t(ref, row0, chunk):
        ref[0, 0, row0:row0 + LANES, :] = chunk.T.astype(ref.dtype)

    def put_values_t(ref, v):
        for j in range(2):
            pair_t = v[:, j * LANES:(j + 1) * LANES].T.astype(BF16)
            for odd in range(2):
                r0 = (2 * j + odd) * HV_ROWS
                ref[0, 0, r0:r0 + HEAD_DIM, :] = pair_t[odd * HEAD_DIM:(odd + 1) * HEAD_DIM, :]
                ref[0, 0, r0 + HEAD_DIM:r0 + HV_ROWS, :] = ones_rows

    p = proj(W_DQ, BR_WIDTH)
    for j in range(2):
        put_t(out['dq_t'], j * LANES, rope64(p, j) * Q_SCALE)
    c_kv = _rmsnorm(proj(W_CKV, KV_LATENT), kvn_ref[...]).astype(BF16)
    k_a = _dot(c_kv, wuk_ref[...])
    for j in range(2):
        put(A_KA + j, rope64(k_a, j))
    put_values_t(out['va_t'], _dot(c_kv, wuv_ref[...]))
    put_t(out['qi_t'], 0, _rope128(proj(W_QI, LANES), c32, s32, IDX_DIM // 2))
    put(A_KI, _rope128(proj(W_KI, LANES), c32, s32, IDX_DIM // 2))
    p = proj(W_SM, LANES)
    log_f = _log_sigmoid(p + fb_ref[...])
    r_i = lax.broadcasted_iota(I32, (T, T), 0)
    c_i = lax.broadcasted_iota(I32, (T, T), 1)
    tri = jnp.where(r_i >= c_i, 1.0, 0.0).astype(BF16)

    @pl.when(pl.program_id(1) == 0)
    def _():
        carry_scr[...] = jnp.zeros_like(carry_scr)

    cum = _dot3x(tri, log_f) + carry_scr[...]
    carry_scr[...] = cum[T - 1:T, :]
    small_t = jnp.where(lane < SM_CUM, p * IDX_SCALE,
                        jnp.where(lane < SM_GATE, cum, _sigmoid(p))).T
    out['sm_t'][0, 0] = small_t
    q_f = proj(W_FQ, BR_WIDTH) * Q_SCALE
    k_f = proj(W_FQ + 256, BR_WIDTH)
    row = lax.broadcasted_iota(I32, (HEAD_DIM, T), 0)
    for h in range(N_HEADS):
        pair, odd = h // 2, h % 2
        k_chunk = k_f[:, pair * LANES:(pair + 1) * LANES]
        k_h = pltpu.roll(k_chunk, HEAD_DIM, 1) if odd else k_chunk
        ck = [c.astype(F32) for c in _split3(cum[:, SM_CUM + h:SM_CUM + h + 1])]
        k_aug = jnp.where(lane < HEAD_DIM, k_h,
                          jnp.where(lane == FOX_CK, ck[0],
                                    jnp.where(lane == FOX_CK + 1, ck[1],
                                              jnp.where(lane == FOX_CK + 2, ck[2],
                                                        jnp.where(lane < FOX_END, 1.0, 0.0)))))
        put(A_FK + h, k_aug)
        q_ht = q_f[:, pair * LANES:(pair + 1) * LANES].T[odd * HEAD_DIM:(odd + 1) * HEAD_DIM, :]
        cq = [c.astype(F32) for c in _split3(small_t[SM_CUM + h:SM_CUM + h + 1, :])]
        aug = jnp.where(row < 3, -1.0,
                        jnp.where(row == 3, cq[0], jnp.where(row == 4, cq[1],
                                                             jnp.where(row == 5, cq[2], 0.0))))
        out['fq_t'][0, 0, h * LANES:(h + 1) * LANES, :] = jnp.concatenate(
            [q_ht, aug], axis=0).astype(BF16)
    put_values_t(out['fv_t'], proj(W_FQ + 512, BR_WIDTH))
    p = proj(W_SQ, BR_WIDTH) * Q_SCALE
    for j in range(2):
        put_t(out['sq_t'], j * LANES, p[:, j * LANES:(j + 1) * LANES])
    put(A_SK, proj(W_SQ + 256, BR_WIDTH))
    p = proj(W_SQ + 512, BR_WIDTH)
    for j in range(2):
        put_t(out['sv_t'], j * LANES, p[:, j * LANES:(j + 1) * LANES])
    put(A_MQ, proj(W_MQ, BR_WIDTH) * Q_SCALE)
    p = proj(W_NQ, BR_WIDTH)
    for j in range(2):
        put_t(out['nq_t'], j * LANES, rope64(p, j) * Q_SCALE)
    p = proj(W_KVC, LANES)
    out['kvtok'][0] = jnp.where(lane < NSA_KV_DIM, rope64(p, 0), p)
    put(A_KSW, rope64(proj(W_KSW, LANES), 0))
    vsw_t = proj(W_VSW, LANES).T.astype(BF16)
    out['vsw_t'][0, 0, :NSA_KV_DIM, :] = vsw_t[:NSA_KV_DIM, :]
    out['vsw_t'][0, 0, NSA_KV_DIM:HV_ROWS, :] = ones_rows
    out['vsw_t'][0, 0, HV_ROWS:, :] = vsw_t[NSA_KV_DIM:, :]


def _proj_weight(w_in):
    off, o = {}, 0
    for name, n in IN_LAYOUT:
        off[name] = (o, n)
        o += n

    def col(name):
        s, n = off[name]
        return w_in[..., s:s + n]

    zeros = lambda n: jnp.zeros(w_in.shape[:-1] + (n,), w_in.dtype)
    groups = [col('dsa_q'), col('dsa_ckv'), col('idx_q'), col('idx_k'), col('idx_k'), col('idx_k'),
              col('idx_k'), col('idx_w'), col('fox_f'), col('nsa_g'), zeros(LANES - 20),
              col('fox_q'), col('fox_k'), col('fox_v'), col('sb_q'), col('sb_k'), col('sb_v'),
              col('nsa_q'), col('mem_q'), col('nsa_kc'), col('nsa_vc'), col('nsa_ks'), col('nsa_kw'),
              col('nsa_vs'), col('nsa_vw')]
    w1 = jnp.concatenate(groups, axis=-1).astype(BF16)
    wz = jnp.stack([col(n) for n in ('dsa_z', 'fox_z', 'sb_z', 'nsa_z', 'mem_z')], axis=-3).astype(BF16)
    s, n = off['merge']
    wm = w_in[..., s:s + n].reshape(w_in.shape[:-1] + (N_BRANCH, n // N_BRANCH))
    wm = jnp.moveaxis(wm, -2, -3).astype(BF16)
    return w1, wz, wm


def _proj_call(x, g, w1, kvn, wuk, wuv, fb, cos, sin):
    B, S, D = x.shape
    full = lambda shape: pl.BlockSpec(shape, lambda b, i: (0,) * len(shape))
    tile = lambda n: pl.BlockSpec((1, T, n), lambda b, i: (b, i, 0))
    tile_t = lambda n: pl.BlockSpec((1, 1, n, T), lambda b, i: (b, i, 0, 0))
    shapes, specs = [], []
    for name, n, dt in PROJ_OUTS:
        if name in ROW_MAJOR:
            shapes.append(jax.ShapeDtypeStruct((B, S, n), dt))
            specs.append(tile(n))
        else:
            shapes.append(jax.ShapeDtypeStruct((B, S // T, n, T), dt))
            specs.append(tile_t(n))
    outs = pl.pallas_call(
        _proj_kernel,
        out_shape=tuple(shapes),
        grid=(B, S // T),
        in_specs=[tile(D), full((1, D)), full((D, W_COLS)), full((1, KV_LATENT)),
                  full((KV_LATENT, BR_WIDTH)), full((KV_LATENT, BR_WIDTH)), full((1, LANES)),
                  tile(2 * LANES), tile(2 * LANES)],
        out_specs=tuple(specs),
        scratch_shapes=[pltpu.VMEM((T, D), BF16), pltpu.VMEM((1, LANES), F32)],
        compiler_params=pltpu.CompilerParams(dimension_semantics=("arbitrary", "arbitrary"),
                                             vmem_limit_bytes=VMEM_LIMIT),
        name="proj",
    )(x, g, w1, kvn, wuk, wuv, fb, cos, sin)
    return dict(zip([n for n, _, _ in PROJ_OUTS], outs))


def _key_query_index(qb, j):
    key = j * T + lax.broadcasted_iota(I32, (T, T), 0)
    qry = qb * T + lax.broadcasted_iota(I32, (T, T), 1)
    return key, qry


def _pair_pads(q_t):
    row = lax.broadcasted_iota(I32, (LANES, T), 0)
    top = row < HEAD_DIM
    zero = jnp.zeros((LANES, T), q_t.dtype)
    out = []
    for pair in range(2):
        chunk = q_t[pair * LANES:(pair + 1) * LANES, :]
        out.append(jnp.where(top, chunk, zero))
        out.append(jnp.where(top, zero, chunk))
    return out


def _softmax_update(scores, valid, v_tiles, m_scr, acc_scr):
    m_prev = [m_scr[h] for h in range(N_HEADS)]
    m_new, updates = [], []
    for h, s in enumerate(scores):
        if valid is not None:
            s = jnp.where(valid, s, NEG)
        m = jnp.maximum(m_prev[h], jnp.max(s, axis=0, keepdims=True))
        p = jnp.exp(s - m)
        if valid is not None:
            p = jnp.where(valid, p, 0.0)
        m_new.append(m)
        updates.append(_dot(v_tiles[h], p.astype(BF16)))
    for h in range(N_HEADS):
        acc_scr[h] = jnp.exp(m_prev[h] - m_new[h]) * acc_scr[h] + updates[h]
        m_scr[h] = m_new[h]


def _softmax_init(m_scr, acc_scr):
    m_scr[...] = jnp.full(m_scr.shape, NEG, F32)
    acc_scr[...] = jnp.zeros(acc_scr.shape, F32)


def _softmax_out(acc_scr, h):
    acc = acc_scr[h]
    return acc[:HEAD_DIM, :] / jnp.maximum(acc[HEAD_DIM:HEAD_DIM + 1, :], 1e-30)


def _exact_softmax(s, valid):
    s = jnp.where(valid, s, NEG)
    m = jnp.max(s, axis=0, keepdims=True)
    e = jnp.where(valid, jnp.exp(s - m), 0.0)
    return e / jnp.maximum(jnp.sum(e, axis=0, keepdims=True), 1e-30)


def _store_heads(o_ref, heads_t):
    for pair in range(2):
        o_ref[0, :, pair * LANES:(pair + 1) * LANES] = jnp.concatenate(
            [t[:HEAD_DIM, :] for t in heads_t[2 * pair:2 * pair + 2]], axis=0).T


def _head_values(v_ref, j):
    return [v_ref[0, j, h * HV_ROWS:(h + 1) * HV_ROWS, :] for h in range(N_HEADS)]


def _attn_scratch(rows):
    return [pltpu.VMEM((N_HEADS, 1, T), F32), pltpu.VMEM((N_HEADS, rows, T), F32)]


def _attn_params():
    return pltpu.CompilerParams(dimension_semantics=("arbitrary", "arbitrary"),
                                vmem_limit_bytes=VMEM_LIMIT)


def _keys_spec(S, units, unit):
    return pl.BlockSpec((1, S, units * LANES), lambda b, i: (b, 0, unit // units))


def _q_spec(rows):
    return pl.BlockSpec((1, 1, rows, T), lambda b, i: (b, i, 0, 0))


def _v_spec(S, rows):
    return pl.BlockSpec((1, S // T, rows, T), lambda b, i: (b, 0, 0, 0))


_OUT_SPEC = pl.BlockSpec((1, T, BR_WIDTH), lambda b, i: (b, i, 0))


def _fox_kernel(q_ref, k_ref, v_ref, o_ref, m_scr, acc_scr):
    qb = pl.program_id(1)
    _softmax_init(m_scr, acc_scr)

    def tile(j, masked):
        s0 = pl.multiple_of(j * T, T)
        valid = None
        if masked:
            key, qry = _key_query_index(qb, j)
            valid = key <= qry
        scores = [_dot(k_ref[0, pl.ds(s0, T), h * LANES:(h + 1) * LANES],
                       q_ref[0, 0, h * LANES:(h + 1) * LANES, :]) for h in range(N_HEADS)]
        _softmax_update(scores, valid, _head_values(v_ref, j), m_scr, acc_scr)

    lax.fori_loop(0, qb, lambda j, c: (tile(j, False), c)[1], 0)
    tile(qb, True)
    _store_heads(o_ref, [_softmax_out(acc_scr, h) for h in range(N_HEADS)])


def _fox_call(pr):
    B, S, _ = pr['keys'].shape
    return pl.pallas_call(
        _fox_kernel,
        out_shape=jax.ShapeDtypeStruct((B, S, BR_WIDTH), F32),
        grid=(B, S // T),
        in_specs=[_q_spec(N_HEADS * LANES), _keys_spec(S, 4, A_FK), _v_spec(S, N_HEADS * HV_ROWS)],
        out_specs=_OUT_SPEC,
        scratch_shapes=_attn_scratch(HV_ROWS),
        compiler_params=_attn_params(),
        name="fox",
    )(pr['fq_t'], pr['keys'], pr['fv_t'])


def _sb_kernel(q_ref, k_ref, v_ref, o_ref, r_scr, acc_scr):
    qb = pl.program_id(1)
    qp = _pair_pads(q_ref[0, 0])
    r_scr[...] = jnp.zeros(r_scr.shape, F32)
    acc_scr[...] = jnp.zeros(acc_scr.shape, F32)
    s_i = lax.broadcasted_iota(I32, (T, T), 0)
    j_i = lax.broadcasted_iota(I32, (T, T), 1)
    later_keys = jnp.where(j_i > s_i, 1.0, 0.0).astype(BF16)

    def tile(j, masked):
        s0 = pl.multiple_of(j * T, T)
        strict = None
        if masked:
            key, qry = _key_query_index(qb, j)
            strict = key < qry
        r_prev = [r_scr[h] for h in range(N_HEADS)]
        r_new, updates = [], []
        for h in range(N_HEADS):
            pair = h // 2
            z = _dot(k_ref[0, pl.ds(s0, T), pair * LANES:(pair + 1) * LANES], qp[h])
            log_take = _log_sigmoid(z)
            log_keep = log_take - z
            if masked:
                log_keep = jnp.where(strict, log_keep, 0.0)
            within = _dot3x(later_keys, log_keep)
            a = jnp.exp(log_take + (r_prev[h] + within))
            if masked:
                a = jnp.where(strict, a, 0.0)
            updates.append(_dot(v_ref[0, j, h * HEAD_DIM:(h + 1) * HEAD_DIM, :], a.astype(BF16)))
            r_new.append(r_prev[h] + within[0:1, :] + log_keep[0:1, :])
        for h in range(N_HEADS):
            acc_scr[h] = acc_scr[h] + updates[h]
            r_scr[h] = r_new[h]

    tile(qb, True)
    lax.fori_loop(0, qb, lambda i, c: (tile(qb - 1 - i, False), c)[1], 0)
    _store_heads(o_ref, [acc_scr[h] for h in range(N_HEADS)])


def _sb_call(pr):
    B, S, _ = pr['keys'].shape
    return pl.pallas_call(
        _sb_kernel,
        out_shape=jax.ShapeDtypeStruct((B, S, BR_WIDTH), F32),
        grid=(B, S // T),
        in_specs=[_q_spec(BR_WIDTH), _keys_spec(S, 2, A_SK), _v_spec(S, BR_WIDTH)],
        out_specs=_OUT_SPEC,
        scratch_shapes=_attn_scratch(HEAD_DIM),
        compiler_params=_attn_params(),
        name="stick_breaking",
    )(pr['sq_t'], pr['keys'], pr['sv_t'])


INT_MIN = -2 ** 31
KEY_ABOVE_NEG_INF = 0x80800000 - 2 ** 32


def _key_to_float(c):
    bits = c ^ ((c >> 31) & 0x7FFFFFFF)
    return lax.bitcast_convert_type(bits, F32)


def _dsa_kernel(q_ref, k_ref, v_ref, qi_ref, ki_ref, sm_ref, o_ref, sc_scr, m_scr, acc_scr, *, k_sel):
    qb = pl.program_id(1)
    row = lax.broadcasted_iota(I32, (LANES, T), 0)
    qi = qi_ref[0, 0]
    zero = jnp.zeros_like(qi)
    qi_pad = [jnp.where((row >= IDX_DIM * h) & (row < IDX_DIM * (h + 1)), qi, zero)
              for h in range(IDX_HEADS)]
    wi = [sm_ref[0, 0, SM_WI + h:SM_WI + h + 1, :] for h in range(IDX_HEADS)]

    def score_tile(j, masked):
        s0 = pl.multiple_of(j * T, T)
        kt = ki_ref[0, pl.ds(s0, T), :]
        sc = jnp.zeros((T, T), F32)
        for h in range(IDX_HEADS):
            sc = sc + wi[h] * jnp.maximum(_dot(kt, qi_pad[h]), 0.0)
        if masked:
            key, qry = _key_query_index(qb, j)
            sc = jnp.where(key <= qry, sc, -jnp.inf)
        sc_scr[pl.ds(s0, T), :] = sc

    lax.fori_loop(0, qb, lambda j, c: (score_tile(j, False), c)[1], 0)
    score_tile(qb, True)

    def count(pred, thr):
        thr_b = jnp.broadcast_to(thr, (SUBLANES, T))

        def body(j, acc):
            s0 = pl.multiple_of(j * T, T)
            for r in range(T // SUBLANES):
                blk = sc_scr[pl.ds(s0 + r * SUBLANES, SUBLANES), :]
                acc = acc + jnp.where(pred(blk, thr_b), 1.0, 0.0)
            return acc

        acc = lax.fori_loop(0, qb + 1, body, jnp.zeros((SUBLANES, T), F32))
        return jnp.sum(acc, axis=0, keepdims=True)

    ge = lambda a, b: a >= b
    kf = float(k_sel)
    key = jnp.where(count(ge, jnp.zeros((1, T), F32)) >= kf, 0, INT_MIN).astype(I32)

    def bisect(i, key):
        cand = key + jnp.left_shift(jnp.int32(1), 30 - i)
        return jnp.where(count(ge, _key_to_float(cand)) >= kf, cand, key)

    key = lax.fori_loop(0, 31, bisect, key)
    thr = _key_to_float(jnp.maximum(key, KEY_ABOVE_NEG_INF))
    n_ge = count(ge, thr)
    need = kf - count(lambda a, b: a > b, thr)

    @pl.when(jnp.max(n_ge) > kf)
    def _():
        s_i = lax.broadcasted_iota(I32, (LANES, LANES), 0)
        j_i = lax.broadcasted_iota(I32, (LANES, LANES), 1)
        earlier = jnp.where(j_i < s_i, 1.0, 0.0).astype(BF16)

        def body(c, seen):
            s0 = pl.multiple_of(c * LANES, LANES)
            blk = sc_scr[pl.ds(s0, LANES), :]
            eq = blk == thr
            eq_f = jnp.where(eq, 1.0, 0.0)
            rank = seen + _dot(earlier, eq_f.astype(BF16))
            sc_scr[pl.ds(s0, LANES), :] = jnp.where(eq & (rank >= need), -jnp.inf, blk)
            return seen + jnp.sum(eq_f, axis=0, keepdims=True)

        lax.fori_loop(0, (qb + 1) * (T // LANES), body, jnp.zeros((1, T), F32))

    qp = _pair_pads(q_ref[0, 0])
    _softmax_init(m_scr, acc_scr)

    def attn_tile(j, c):
        s0 = pl.multiple_of(j * T, T)
        valid = sc_scr[pl.ds(s0, T), :] >= thr
        scores = [_dot(k_ref[0, pl.ds(s0, T), (h // 2) * LANES:(h // 2 + 1) * LANES], qp[h])
                  for h in range(N_HEADS)]
        _softmax_update(scores, valid, _head_values(v_ref, j), m_scr, acc_scr)
        return c

    lax.fori_loop(0, qb + 1, attn_tile, 0)
    _store_heads(o_ref, [_softmax_out(acc_scr, h) for h in range(N_HEADS)])


def _dsa_call(pr):
    B, S, _ = pr['keys'].shape
    return pl.pallas_call(
        functools.partial(_dsa_kernel, k_sel=min(DSA_TOPK_MAX, S // 4)),
        out_shape=jax.ShapeDtypeStruct((B, S, BR_WIDTH), F32),
        grid=(B, S // T),
        in_specs=[_q_spec(BR_WIDTH), _keys_spec(S, 2, A_KA), _v_spec(S, N_HEADS * HV_ROWS),
                  _q_spec(LANES), _keys_spec(S, 1, A_KI), _q_spec(LANES)],
        out_specs=_OUT_SPEC,
        scratch_shapes=[pltpu.VMEM((S, T), F32)] + _attn_scratch(HV_ROWS),
        compiler_params=_attn_params(),
        name="dsa",
    )(pr['dq_t'], pr['keys'], pr['va_t'], pr['qi_t'], pr['keys'], pr['sm_t'])


def _compress_kernel(xk_ref, xv_ref, pek_ref, pev_ref, w1k_ref, w1v_ref, w2k_ref, w2v_ref,
                     o_ref, ot_ref, *, nc):
    def hidden(x_ref, pe_ref, w1_ref):
        x = x_ref[0]
        first = _dot((x + pe_ref[0:1, :]).astype(BF16), w1_ref[0])
        second = _dot((x + pe_ref[1:2, :]).astype(BF16), w1_ref[1])
        pre = first + pltpu.roll(second, nc - 1, 0)
        return (pre * _sigmoid(pre)).astype(BF16)

    kvc = (_dot(hidden(xk_ref, pek_ref, w1k_ref), w2k_ref[...])
           + _dot(hidden(xv_ref, pev_ref, w1v_ref), w2v_ref[...]))
    o_ref[0] = kvc.astype(BF16)
    ot_ref[0] = kvc.T.astype(BF16)


def _compress_call(xk, xv, pek, pev, w1k, w1v, w2k, w2v):
    B, nc, width = xk.shape
    full = lambda shape: pl.BlockSpec(shape, lambda b: (0,) * len(shape))
    tok = pl.BlockSpec((1, nc, width), lambda b: (b, 0, 0))
    return pl.pallas_call(
        functools.partial(_compress_kernel, nc=nc),
        out_shape=(jax.ShapeDtypeStruct((B, nc, LANES), BF16), jax.ShapeDtypeStruct((B, LANES, nc), BF16)),
        grid=(B,),
        in_specs=[tok, tok, full((2, width)), full((2, width)), full((2, width, CMP_HIDDEN)),
                  full((2, width, CMP_HIDDEN)), full((CMP_HIDDEN, LANES)), full((CMP_HIDDEN, LANES))],
        out_specs=(pl.BlockSpec((1, nc, LANES), lambda b: (b, 0, 0)),
                   pl.BlockSpec((1, LANES, nc), lambda b: (b, 0, 0))),
        compiler_params=pltpu.CompilerParams(dimension_semantics=("arbitrary",),
                                             vmem_limit_bytes=VMEM_LIMIT),
        name="nsa_compress",
    )(xk, xv, pek, pev, w1k, w1v, w2k, w2v)


def _nsa_kernel(q_ref, kvc_ref, kvct_ref, ksw_ref, vsw_ref, sm_ref, o_ref,
                imp_scr, sel_scr, m_scr, acc_scr, *, n_blk, n_sel, nbp):
    qb = pl.program_id(1)
    ncp = kvc_ref.shape[1]
    q_t = q_ref[0, 0]
    zero = jnp.zeros((HEAD_DIM, T), q_t.dtype)
    heads = [q_t[h * HEAD_DIM:(h + 1) * HEAD_DIM, :] for h in range(N_HEADS)]
    q_lo = [jnp.concatenate([q, zero], axis=0) for q in heads]
    q_hi = [jnp.concatenate([zero, q], axis=0) for q in heads]
    gate = lambda br, h: sm_ref[0, 0, SM_GATE + br * N_HEADS + h:SM_GATE + br * N_HEADS + h + 1, :]
    t_row = qb * T + lax.broadcasted_iota(I32, (1, T), 1)

    kvc = kvc_ref[0]
    cmp_end = lax.broadcasted_iota(I32, (ncp, T), 0) * CMP_STRIDE + (CMP_LEN - 1)
    cmp_ok = cmp_end <= t_row
    o_cmp, p_sum = [], jnp.zeros((ncp, T), F32)
    for h in range(N_HEADS):
        pc = _exact_softmax(_dot(kvc, q_lo[h]), cmp_ok)
        o_cmp.append(_dot(kvct_ref[0], pc.astype(BF16))[HEAD_DIM:, :])
        p_sum = p_sum + pc
    per_blk = SLC_BLOCK // CMP_STRIDE
    b_i = lax.broadcasted_iota(I32, (nbp, ncp), 0)
    c_i = lax.broadcasted_iota(I32, (nbp, ncp), 1)
    group = jnp.where(_div_pow2(c_i, per_blk) == b_i, 1.0, 0.0).astype(BF16)
    imp = _dot3x(group, p_sum)
    blk = lax.broadcasted_iota(I32, (nbp, T), 0)
    cur = _div_pow2(t_row, SLC_BLOCK)
    forced = (blk == 0) | (blk == cur) | (blk == cur - 1)
    imp = jnp.where(forced, FORCE_SCORE, jnp.where(blk <= cur, imp, -1.0))
    imp = jnp.where(blk < n_blk, imp, -2.0)
    imp_scr[...] = imp

    def rank_body(jp, cnt):
        other = imp_scr[pl.ds(jp, 1), :]
        ahead = (other > imp) | ((other == imp) & (jp < blk))
        return cnt + jnp.where(ahead, 1.0, 0.0)

    rank = lax.fori_loop(0, n_blk, rank_body, jnp.zeros((nbp, T), F32))
    sel_scr[...] = jnp.where(rank < float(n_sel), 1.0, 0.0)

    _softmax_init(m_scr, acc_scr)
    blk_per_tile = T // SLC_BLOCK

    def slc_tile(j, masked):
        s0 = pl.multiple_of(j * T, T)
        valid = jnp.concatenate(
            [jnp.broadcast_to(sel_scr[pl.ds(j * blk_per_tile + b, 1), :], (SLC_BLOCK, T))
             for b in range(blk_per_tile)], axis=0) > 0.5
        if masked:
            key, qry = _key_query_index(qb, j)
            valid = valid & (key <= qry)
        kt = ksw_ref[0, pl.ds(s0, T), :]
        _softmax_update([_dot(kt, q_lo[h]) for h in range(N_HEADS)], valid,
                        [vsw_ref[0, j, :HV_ROWS, :]] * N_HEADS, m_scr, acc_scr)

    lax.fori_loop(0, qb, lambda j, c: (slc_tile(j, False), c)[1], 0)
    slc_tile(qb, True)

    n_win = WINDOW // T + 1
    j0 = jnp.maximum(qb - (n_win - 1), 0)
    outs = []
    for h in range(N_HEADS):
        logits, oks = [], []
        for i in range(n_win):
            s0 = pl.multiple_of((j0 + i) * T, T)
            key, qry = _key_query_index(qb, j0 + i)
            dist = qry - key
            oks.append((dist >= 0) & (dist < WINDOW))
            logits.append(jnp.where(oks[i], _dot(ksw_ref[0, pl.ds(s0, T), :], q_hi[h]), NEG))
        m = functools.reduce(jnp.maximum, [jnp.max(s, axis=0, keepdims=True) for s in logits])
        es = [jnp.where(ok, jnp.exp(s - m), 0.0) for s, ok in zip(logits, oks)]
        den = functools.reduce(jnp.add, [jnp.sum(e, axis=0, keepdims=True) for e in es])
        o_win = functools.reduce(jnp.add, [
            _dot(vsw_ref[0, j0 + i, HV_ROWS:, :], (e / jnp.maximum(den, 1e-30)).astype(BF16))
            for i, e in enumerate(es)])
        outs.append(gate(0, h) * o_cmp[h] + gate(1, h) * _softmax_out(acc_scr, h)
                    + gate(2, h) * o_win)
    _store_heads(o_ref, outs)


def _nsa_call(pr, kvc, kvc_t):
    B, S, _ = pr['keys'].shape
    n_blk = S // SLC_BLOCK
    nbp = max(LANES, n_blk)
    ncp = kvc.shape[1]
    return pl.pallas_call(
        functools.partial(_nsa_kernel, n_blk=n_blk, n_sel=min(SLC_TOPN, n_blk), nbp=nbp),
        out_shape=jax.ShapeDtypeStruct((B, S, BR_WIDTH), F32),
        grid=(B, S // T),
        in_specs=[_q_spec(BR_WIDTH),
                  pl.BlockSpec((1, ncp, LANES), lambda b, i: (b, 0, 0)),
                  pl.BlockSpec((1, LANES, ncp), lambda b, i: (b, 0, 0)),
                  _keys_spec(S, 1, A_KSW), _v_spec(S, VSW_ROWS), _q_spec(LANES)],
        out_specs=_OUT_SPEC,
        scratch_shapes=[pltpu.VMEM((nbp, T), F32), pltpu.VMEM((nbp, T), F32)] + _attn_scratch(HV_ROWS),
        compiler_params=_attn_params(),
        name="nsa",
    )(pr['nq_t'], kvc, kvc_t, pr['keys'], pr['vsw_t'], pr['sm_t'])


def _mem_kv_kernel(mem_ref, g_ref, w_ref, o_ref):
    o_ref[0] = _dot(_rmsnorm(mem_ref[0], g_ref[...]).astype(BF16), w_ref[...]).astype(BF16)


def _mem_kv_call(mem, g, w):
    B, M, D = mem.shape
    return pl.pallas_call(
        _mem_kv_kernel,
        out_shape=jax.ShapeDtypeStruct((B, M, 2 * BR_WIDTH), BF16),
        grid=(B,),
        in_specs=[pl.BlockSpec((1, M, D), lambda b: (b, 0, 0)), pl.BlockSpec((1, D), lambda b: (0, 0)),
                  pl.BlockSpec((D, 2 * BR_WIDTH), lambda b: (0, 0))],
        out_specs=pl.BlockSpec((1, M, 2 * BR_WIDTH), lambda b: (b, 0, 0)),
        name="mem_kv",
    )(mem, g, w)


def _out_kernel(x_ref, g_ref, ya_ref, yb_ref, yc_ref, yd_ref, mq_ref, mkv_ref, wz_ref, wm_ref,
                wb_ref, wo_ref, fg_ref, o_ref, h_scr, *, final):
    x = x_ref[0]
    h_scr[...] = _rmsnorm(x, g_ref[...]).astype(BF16)
    q = mq_ref[0]
    lane = lax.broadcasted_iota(I32, (T, LANES), 1)
    lo = lane < HEAD_DIM
    zero = jnp.zeros((T, LANES), q.dtype)
    pairs = []
    for pair in range(2):
        chunk = q[:, pair * LANES:(pair + 1) * LANES]
        mk = mkv_ref[0, :, pair * LANES:(pair + 1) * LANES]
        mv = mkv_ref[0, :, BR_WIDTH + pair * LANES:BR_WIDTH + (pair + 1) * LANES]
        outs = []
        for qh in (jnp.where(lo, chunk, zero), jnp.where(lo, zero, chunk)):
            s = _dot_nt(qh, mk)
            e = jnp.exp(s - jnp.max(s, axis=1, keepdims=True))
            outs.append(_dot((e / jnp.sum(e, axis=1, keepdims=True)).astype(BF16), mv))
        pairs.append(jnp.where(lo, outs[0], outs[1]))
    y_e = jnp.concatenate(pairs, axis=1)
    merged = jnp.zeros(x.shape, F32)
    for n, y in enumerate((ya_ref[0], yb_ref[0], yc_ref[0], yd_ref[0], y_e)):
        z = _dot(h_scr[...], wz_ref[n])
        ys = (y * (z * _sigmoid(z))).astype(BF16)
        merged = merged + _sigmoid(_dot(h_scr[...], wm_ref[n])) * _dot(ys, wb_ref[n])
    out = x + _dot(merged.astype(BF16), wo_ref[...])
    o_ref[0] = _rmsnorm(out, fg_ref[...]) if final else out


def _out_call(x, g, ys, keys, mkv, wz, wm, wb, wo, fg, final):
    B, S, D = x.shape
    M = mkv.shape[1]
    full = lambda shape: pl.BlockSpec(shape, lambda b, i: (0,) * len(shape))
    tile = lambda n: pl.BlockSpec((1, T, n), lambda b, i: (b, i, 0))
    return pl.pallas_call(
        functools.partial(_out_kernel, final=final),
        out_shape=jax.ShapeDtypeStruct((B, S, D), F32),
        grid=(B, S // T),
        in_specs=[tile(D), full((1, D))] + [tile(BR_WIDTH)] * 4
                 + [pl.BlockSpec((1, T, BR_WIDTH), lambda b, i: (b, i, A_MQ // 2)),
                    pl.BlockSpec((1, M, 2 * BR_WIDTH), lambda b, i: (b, 0, 0)),
                    full((N_BRANCH, D, BR_WIDTH)), full((N_BRANCH, D, D)), full((N_BRANCH, BR_WIDTH, D)),
                    full((D, D)), full((1, D))],
        out_specs=tile(D),
        scratch_shapes=[pltpu.VMEM((T, D), BF16)],
        compiler_params=pltpu.CompilerParams(dimension_semantics=("arbitrary", "arbitrary"),
                                             vmem_limit_bytes=VMEM_LIMIT),
        name="merge_out",
    )(x, g, *ys, keys, mkv, wz, wm, wb, wo, fg)


def kernel(x, mem, positions, norm_g, w_in, kv_norm, w_uk, w_uv, fox_bias, nsa_pe_k, nsa_pe_v,
           nsa_wc1_k, nsa_wc2_k, nsa_wc1_v, nsa_wc2_v, mem_norm, w_mem_kv, w_branch, w_out, final_norm):
    B, S, D = x.shape
    depth = norm_g.shape[0]
    assert S % T == 0 and S >= WINDOW + T and WINDOW % T == 0 and D == 1024
    cos, sin = _rope_tables(positions)
    w1, wz, wm = _proj_weight(w_in)
    fb = jnp.zeros((depth, 1, LANES), F32).at[:, 0, SM_CUM:SM_CUM + N_HEADS].set(fox_bias)
    nc = S // CMP_STRIDE
    tok_w = CMP_STRIDE * NSA_KV_DIM
    pad_k = jnp.zeros((depth, CMP_HIDDEN, LANES), F32).at[..., :NSA_KV_DIM].set(nsa_wc2_k).astype(BF16)
    pad_v = jnp.zeros((depth, CMP_HIDDEN, LANES), F32).at[..., NSA_KV_DIM:].set(nsa_wc2_v).astype(BF16)
    for l in range(depth):
        pr = _proj_call(x, norm_g[l][None], w1[l], kv_norm[l][None], w_uk[l].astype(BF16),
                        w_uv[l].astype(BF16), fb[l], cos, sin)
        kvc, kvc_t = _compress_call(
            pr['kvtok'][:, :, :NSA_KV_DIM].reshape(B, nc, tok_w),
            pr['kvtok'][:, :, NSA_KV_DIM:].reshape(B, nc, tok_w),
            nsa_pe_k[l].reshape(2, tok_w), nsa_pe_v[l].reshape(2, tok_w),
            nsa_wc1_k[l].reshape(2, tok_w, CMP_HIDDEN).astype(BF16),
            nsa_wc1_v[l].reshape(2, tok_w, CMP_HIDDEN).astype(BF16), pad_k[l], pad_v[l])
        ys = (_dsa_call(pr), _fox_call(pr), _sb_call(pr), _nsa_call(pr, kvc, kvc_t))
        mkv = _mem_kv_call(mem, mem_norm[l][None], w_mem_kv[l].astype(BF16))
        x = _out_call(x, norm_g[l][None], ys, pr['keys'], mkv, wz[l], wm[l], w_branch[l].astype(BF16),
                      w_out[l].astype(BF16), final_norm[None], l == depth - 1)
    return x
```

```python
import functools

import jax
import jax.numpy as jnp
from jax import lax
from jax.experimental import pallas as pl
from jax.experimental.pallas import tpu as pltpu

F32 = jnp.float32
BF16 = jnp.bfloat16
I32 = jnp.int32

LANES = 128
SUBLANES = 8
N_HEADS = 4
HEAD_DIM = 64
BR_WIDTH = N_HEADS * HEAD_DIM
N_BRANCH = 5
ROPE_THETA = 10000.0
EPS = 1e-6
DSA_TOPK_MAX = 256
IDX_HEADS = 4
IDX_DIM = 32
KV_LATENT = 128
NSA_KV_DIM = 64
CMP_LEN = 32
CMP_STRIDE = 16
CMP_HIDDEN = 128
SLC_BLOCK = 64
SLC_TOPN = 16
WINDOW = 512
FORCE_SCORE = 1e4
Q_SCALE = HEAD_DIM ** -0.5
IDX_SCALE = (IDX_DIM ** -0.5) * (IDX_HEADS ** -0.5)
NEG = -1e30
SB_CUTOFF = -104.0
VMEM_LIMIT = 56 * 1024 * 1024
T = 256
ONES_ROWS = 16
HV_ROWS = HEAD_DIM + ONES_ROWS
VSW_ROWS = HV_ROWS + NSA_KV_DIM

IN_LAYOUT = (
    ('dsa_q', BR_WIDTH), ('dsa_ckv', KV_LATENT), ('idx_q', IDX_HEADS * IDX_DIM), ('idx_k', IDX_DIM),
    ('idx_w', IDX_HEADS), ('dsa_z', BR_WIDTH),
    ('fox_q', BR_WIDTH), ('fox_k', BR_WIDTH), ('fox_v', BR_WIDTH), ('fox_f', N_HEADS), ('fox_z', BR_WIDTH),
    ('sb_q', BR_WIDTH), ('sb_k', BR_WIDTH), ('sb_v', BR_WIDTH), ('sb_z', BR_WIDTH),
    ('nsa_q', BR_WIDTH), ('nsa_kc', NSA_KV_DIM), ('nsa_vc', NSA_KV_DIM), ('nsa_ks', NSA_KV_DIM),
    ('nsa_vs', NSA_KV_DIM), ('nsa_kw', NSA_KV_DIM), ('nsa_vw', NSA_KV_DIM), ('nsa_g', 3 * N_HEADS),
    ('nsa_z', BR_WIDTH), ('mem_q', BR_WIDTH), ('mem_z', BR_WIDTH), ('merge', N_BRANCH * 1024),
)

A_FK, A_KA, A_SK, A_KI, A_KSW, A_MQ = 0, 4, 6, 8, 9, 10
A_COLS = 12 * LANES
SM_WI, SM_CUM, SM_GATE = 0, 4, 8
FOX_CK, FOX_ONE, FOX_END = HEAD_DIM, HEAD_DIM + 3, HEAD_DIM + 6


def _dot(a, b):
    return jnp.dot(a, b, preferred_element_type=F32)


def _dot_nt(a, b):
    return lax.dot_general(a, b, (((1,), (1,)), ((), ())), preferred_element_type=F32)


def _split3(x):
    hi = x.astype(BF16)
    r1 = x - hi.astype(F32)
    mid = r1.astype(BF16)
    lo = (r1 - mid.astype(F32)).astype(BF16)
    return hi, mid, lo


def _dot3x(m01, x):
    hi, mid, lo = _split3(x)
    return _dot(m01, hi) + _dot(m01, mid) + _dot(m01, lo)


def _div_pow2(x, n):
    assert n & (n - 1) == 0
    return x >> (n.bit_length() - 1)


def _sigmoid(x):
    return 1.0 / (1.0 + jnp.exp(-x))


def _log_sigmoid(x):
    return jnp.minimum(x, 0.0) - jnp.log(1.0 + jnp.exp(-jnp.abs(x)))


def _rmsnorm(x, g):
    return x * lax.rsqrt(jnp.mean(x * x, axis=-1, keepdims=True) + EPS) * g


def _rope128(x, cos, sin_signed, half):
    lane = lax.broadcasted_iota(I32, x.shape, 1)
    first = (lane & (2 * half - 1)) < half
    partner = jnp.where(first, pltpu.roll(x, LANES - half, 1), pltpu.roll(x, half, 1))
    return x * cos + partner * sin_signed


def _rope_table_kernel(pos_ref, inv_ref, sgn_ref, cos_ref, sin_ref):
    ang = pos_ref[0] * inv_ref[...]
    cos_ref[0] = jnp.cos(ang)
    sin_ref[0] = jnp.sin(ang) * sgn_ref[...]


def _rope_tables(positions):
    B, S = positions.shape
    lane = jnp.arange(LANES)

    def inv_row(dh):
        inv = ROPE_THETA ** (-jnp.arange(0, dh, 2, dtype=F32) / dh)
        return inv[(lane % dh) % (dh // 2)]

    def sgn_row(dh):
        return jnp.where((lane % dh) < dh // 2, -1.0, 1.0).astype(F32)

    inv = jnp.concatenate([inv_row(HEAD_DIM), inv_row(IDX_DIM)])[None, :]
    sgn = jnp.concatenate([sgn_row(HEAD_DIM), sgn_row(IDX_DIM)])[None, :]
    pos = positions.astype(F32)[..., None]
    row = pl.BlockSpec((1, 2 * LANES), lambda b, i: (0, 0))
    tab = pl.BlockSpec((1, T, 2 * LANES), lambda b, i: (b, i, 0))
    return pl.pallas_call(
        _rope_table_kernel,
        out_shape=(jax.ShapeDtypeStruct((B, S, 2 * LANES), F32),) * 2,
        grid=(B, S // T),
        in_specs=[pl.BlockSpec((1, T, 1), lambda b, i: (b, i, 0)), row, row],
        out_specs=(tab, tab),
        name="rope_tables",
    )(pos, inv, sgn)


W_DQ, W_CKV, W_QI, W_KI, W_SM, W_FQ, W_SQ, W_NQ, W_MQ, W_KVC, W_KSW, W_VSW = (
    0, 256, 384, 512, 640, 768, 1536, 2304, 2560, 2816, 2944, 3072)
W_COLS = 3200

PROJ_OUTS = (
    ('keys', A_COLS, BF16), ('kvtok', LANES, F32),
    ('dq_t', BR_WIDTH, BF16), ('fq_t', N_HEADS * LANES, BF16), ('sq_t', BR_WIDTH, BF16),
    ('nq_t', BR_WIDTH, BF16), ('qi_t', LANES, BF16), ('sm_t', LANES, F32),
    ('va_t', N_HEADS * HV_ROWS, BF16), ('fv_t', N_HEADS * HV_ROWS, BF16), ('sv_t', BR_WIDTH, BF16),
    ('vsw_t', VSW_ROWS, BF16),
)
ROW_MAJOR = ('keys', 'kvtok')


def _proj_kernel(x_ref, g_ref, w_ref, kvn_ref, wuk_ref, wuv_ref, fb_ref, cos_ref, sin_ref,
                 *refs):
    out = dict(zip([n for n, _, _ in PROJ_OUTS], refs))
    h_scr, carry_scr = refs[len(PROJ_OUTS):]
    keys = out['keys']
    h_scr[...] = _rmsnorm(x_ref[0], g_ref[...]).astype(BF16)
    c64, s64 = cos_ref[0, :, :LANES], sin_ref[0, :, :LANES]
    c32, s32 = cos_ref[0, :, LANES:], sin_ref[0, :, LANES:]
    lane = lax.broadcasted_iota(I32, (T, LANES), 1)
    ones_rows = jnp.ones((ONES_ROWS, T), BF16)

    def proj(c0, n):
        return _dot(h_scr[...], w_ref[:, c0:c0 + n])

    def put(unit, val):
        keys[0, :, unit * LANES:unit * LANES + val.shape[1]] = val.astype(BF16)

    def rope64(v, j):
        return _rope128(v[:, j * LANES:(j + 1) * LANES], c64, s64, HEAD_DIM // 2)

    def put_t(ref, row0, chunk):
        ref[0, 0, row0:row0 + LANES, :] = chunk.T.astype(ref.dtype)

    def put_values_t(ref, v):
        for j in range(2):
            pair_t = v[:, j * LANES:(j + 1) * LANES].T.astype(BF16)
            for odd in range(2):
                r0 = (2 * j + odd) * HV_ROWS
                ref[0, 0, r0:r0 + HEAD_DIM, :] = pair_t[odd * HEAD_DIM:(odd + 1) * HEAD_DIM, :]
                ref[0, 0, r0 + HEAD_DIM:r0 + HV_ROWS, :] = ones_rows

    p = proj(W_DQ, BR_WIDTH)
    for j in range(2):
        put_t(out['dq_t'], j * LANES, rope64(p, j) * Q_SCALE)
    c_kv = _rmsnorm(proj(W_CKV, KV_LATENT), kvn_ref[...]).astype(BF16)
    k_a = _dot(c_kv, wuk_ref[...])
    for j in range(2):
        put(A_KA + j, rope64(k_a, j))
    put_values_t(out['va_t'], _dot(c_kv, wuv_ref[...]))
    put_t(out['qi_t'], 0, _rope128(proj(W_QI, LANES), c32, s32, IDX_DIM // 2))
    put(A_KI, _rope128(proj(W_KI, LANES), c32, s32, IDX_DIM // 2))
    p = proj(W_SM, LANES)
    log_f = _log_sigmoid(p + fb_ref[...])
    r_i = lax.broadcasted_iota(I32, (T, T), 0)
    c_i = lax.broadcasted_iota(I32, (T, T), 1)
    tri = jnp.where(r_i >= c_i, 1.0, 0.0).astype(BF16)

    @pl.when(pl.program_id(1) == 0)
    def _():
        carry_scr[...] = jnp.zeros_like(carry_scr)

    cum = _dot3x(tri, log_f) + carry_scr[...]
    carry_scr[...] = cum[T - 1:T, :]
    small_t = jnp.where(lane < SM_CUM, p * IDX_SCALE,
                        jnp.where(lane < SM_GATE, cum, _sigmoid(p))).T
    out['sm_t'][0, 0] = small_t
    q_f = proj(W_FQ, BR_WIDTH) * Q_SCALE
    k_f = proj(W_FQ + 256, BR_WIDTH)
    row = lax.broadcasted_iota(I32, (HEAD_DIM, T), 0)
    for h in range(N_HEADS):
        pair, odd = h // 2, h % 2
        k_chunk = k_f[:, pair * LANES:(pair + 1) * LANES]
        k_h = pltpu.roll(k_chunk, HEAD_DIM, 1) if odd else k_chunk
        ck = [c.astype(F32) for c in _split3(cum[:, SM_CUM + h:SM_CUM + h + 1])]
        k_aug = jnp.where(lane < HEAD_DIM, k_h,
                          jnp.where(lane == FOX_CK, ck[0],
                                    jnp.where(lane == FOX_CK + 1, ck[1],
                                              jnp.where(lane == FOX_CK + 2, ck[2],
                                                        jnp.where(lane < FOX_END, 1.0, 0.0)))))
        put(A_FK + h, k_aug)
        q_ht = q_f[:, pair * LANES:(pair + 1) * LANES].T[odd * HEAD_DIM:(odd + 1) * HEAD_DIM, :]
        cq = [c.astype(F32) for c in _split3(small_t[SM_CUM + h:SM_CUM + h + 1, :])]
        aug = jnp.where(row < 3, -1.0,
                        jnp.where(row == 3, cq[0], jnp.where(row == 4, cq[1],
                                                             jnp.where(row == 5, cq[2], 0.0))))
        out['fq_t'][0, 0, h * LANES:(h + 1) * LANES, :] = jnp.concatenate(
            [q_ht, aug], axis=0).astype(BF16)
    put_values_t(out['fv_t'], proj(W_FQ + 512, BR_WIDTH))
    p = proj(W_SQ, BR_WIDTH) * Q_SCALE
    for j in range(2):
        put_t(out['sq_t'], j * LANES, p[:, j * LANES:(j + 1) * LANES])
    put(A_SK, proj(W_SQ + 256, BR_WIDTH))
    p = proj(W_SQ + 512, BR_WIDTH)
    for j in range(2):
        put_t(out['sv_t'], j * LANES, p[:, j * LANES:(j + 1) * LANES])
    put(A_MQ, proj(W_MQ, BR_WIDTH) * Q_SCALE)
    p = proj(W_NQ, BR_WIDTH)
    for j in range(2):
        put_t(out['nq_t'], j * LANES, rope64(p, j) * Q_SCALE)
    p = proj(W_KVC, LANES)
    out['kvtok'][0] = jnp.where(lane < NSA_KV_DIM, rope64(p, 0), p)
    put(A_KSW, rope64(proj(W_KSW, LANES), 0))
    vsw_t = proj(W_VSW, LANES).T.astype(BF16)
    out['vsw_t'][0, 0, :NSA_KV_DIM, :] = vsw_t[:NSA_KV_DIM, :]
    out['vsw_t'][0, 0, NSA_KV_DIM:HV_ROWS, :] = ones_rows
    out['vsw_t'][0, 0, HV_ROWS:, :] = vsw_t[NSA_KV_DIM:, :]


def _proj_weight(w_in):
    off, o = {}, 0
    for name, n in IN_LAYOUT:
        off[name] = (o, n)
        o += n

    def col(name):
        s, n = off[name]
        return w_in[..., s:s + n]

    zeros = lambda n: jnp.zeros(w_in.shape[:-1] + (n,), w_in.dtype)
    groups = [col('dsa_q'), col('dsa_ckv'), col('idx_q'), col('idx_k'), col('idx_k'), col('idx_k'),
              col('idx_k'), col('idx_w'), col('fox_f'), col('nsa_g'), zeros(LANES - 20),
              col('fox_q'), col('fox_k'), col('fox_v'), col('sb_q'), col('sb_k'), col('sb_v'),
              col('nsa_q'), col('mem_q'), col('nsa_kc'), col('nsa_vc'), col('nsa_ks'), col('nsa_kw'),
              col('nsa_vs'), col('nsa_vw')]
    w1 = jnp.concatenate(groups, axis=-1).astype(BF16)
    wz = jnp.stack([col(n) for n in ('dsa_z', 'fox_z', 'sb_z', 'nsa_z', 'mem_z')], axis=-3).astype(BF16)
    s, n = off['merge']
    wm = w_in[..., s:s + n].reshape(w_in.shape[:-1] + (N_BRANCH, n // N_BRANCH))
    wm = jnp.moveaxis(wm, -2, -3).astype(BF16)
    return w1, wz, wm


def _proj_call(x, g, w1, kvn, wuk, wuv, fb, cos, sin):
    B, S, D = x.shape
    full = lambda shape: pl.BlockSpec(shape, lambda b, i: (0,) * len(shape))
    tile = lambda n: pl.BlockSpec((1, T, n), lambda b, i: (b, i, 0))
    tile_t = lambda n: pl.BlockSpec((1, 1, n, T), lambda b, i: (b, i, 0, 0))
    shapes, specs = [], []
    for name, n, dt in PROJ_OUTS:
        if name in ROW_MAJOR:
            shapes.append(jax.ShapeDtypeStruct((B, S, n), dt))
            specs.append(tile(n))
        else:
            shapes.append(jax.ShapeDtypeStruct((B, S // T, n, T), dt))
            specs.append(tile_t(n))
    outs = pl.pallas_call(
        _proj_kernel,
        out_shape=tuple(shapes),
        grid=(B, S // T),
        in_specs=[tile(D), full((1, D)), full((D, W_COLS)), full((1, KV_LATENT)),
                  full((KV_LATENT, BR_WIDTH)), full((KV_LATENT, BR_WIDTH)), full((1, LANES)),
                  tile(2 * LANES), tile(2 * LANES)],
        out_specs=tuple(specs),
        scratch_shapes=[pltpu.VMEM((T, D), BF16), pltpu.VMEM((1, LANES), F32)],
        compiler_params=pltpu.CompilerParams(dimension_semantics=("arbitrary", "arbitrary"),
                                             vmem_limit_bytes=VMEM_LIMIT),
        name="proj",
    )(x, g, w1, kvn, wuk, wuv, fb, cos, sin)
    return dict(zip([n for n, _, _ in PROJ_OUTS], outs))


def _key_query_index(qb, j):
    key = j * T + lax.broadcasted_iota(I32, (T, T), 0)
    qry = qb * T + lax.broadcasted_iota(I32, (T, T), 1)
    return key, qry


def _pair_pads(q_t):
    row = lax.broadcasted_iota(I32, (LANES, T), 0)
    top = row < HEAD_DIM
    zero = jnp.zeros((LANES, T), q_t.dtype)
    out = []
    for pair in range(2):
        chunk = q_t[pair * LANES:(pair + 1) * LANES, :]
        out.append(jnp.where(top, chunk, zero))
        out.append(jnp.where(top, zero, chunk))
    return out


def _softmax_update(scores, valids, values, m_scr, acc_scr):
    m_prev = [m_scr[h] for h in range(N_HEADS)]
    m_new, updates = [], []
    for h in range(N_HEADS):
        tiles = [s if v is None else jnp.where(v, s, NEG) for s, v in zip(scores[h], valids)]
        m = functools.reduce(jnp.maximum, [m_prev[h]] + [jnp.max(s, axis=0, keepdims=True) for s in tiles])
        update = None
        for i, (s, v) in enumerate(zip(tiles, valids)):
            p = jnp.exp(s - m)
            if v is not None:
                p = jnp.where(v, p, 0.0)
            u = _dot(values[i][h], p.astype(BF16))
            update = u if update is None else update + u
        m_new.append(m)
        updates.append(update)
    for h in range(N_HEADS):
        acc_scr[h] = jnp.exp(m_prev[h] - m_new[h]) * acc_scr[h] + updates[h]
        m_scr[h] = m_new[h]


def _for_tile_groups(n, group_fn):
    def pair(i, c):
        group_fn([2 * i, 2 * i + 1])
        return c

    lax.fori_loop(0, n // 2, pair, 0)

    @pl.when(n % 2 == 1)
    def _():
        group_fn([n - 1])


INT_MIN = -2 ** 31
KEY_ABOVE_NEG_INF = 0x80800000 - 2 ** 32


def _key_to_float(c):
    bits = c ^ ((c >> 31) & 0x7FFFFFFF)
    return lax.bitcast_convert_type(bits, F32)


def _top_k_threshold(sc_ref, n_steps, step_rows, k, n_valid, early_exit):
    kf = float(k)
    n_chunks = n_steps * (step_rows // LANES)

    def count(pred, thr):
        thr_b = jnp.broadcast_to(thr, (SUBLANES, T))

        def body(c, accs):
            blk = sc_ref[pl.ds(pl.multiple_of(c * step_rows, step_rows), step_rows), :]
            accs = list(accs)
            for r in range(step_rows // SUBLANES):
                hit = jnp.where(pred(blk[r * SUBLANES:(r + 1) * SUBLANES, :], thr_b), 1.0, 0.0)
                accs[r % len(accs)] = accs[r % len(accs)] + hit
            return tuple(accs)

        accs = lax.fori_loop(0, n_steps, body, (jnp.zeros((SUBLANES, T), F32),) * 4)
        return jnp.sum(accs[0] + accs[1] + accs[2] + accs[3], axis=0, keepdims=True)

    ge = lambda a, b: a >= b

    def unsettled(cnt_key):
        return jnp.min(jnp.where((cnt_key == kf) | (n_valid <= kf), 1.0, 0.0)) < 0.5

    cnt = count(ge, jnp.zeros((1, T), F32))
    key = jnp.where(cnt >= kf, 0, INT_MIN).astype(I32)
    cnt_key = jnp.where(cnt >= kf, cnt, n_valid)

    def bisect(i, state):
        key, cnt_key = state
        cand = key + jnp.left_shift(jnp.int32(1), 30 - i)
        cnt = count(ge, _key_to_float(cand))
        return jnp.where(cnt >= kf, cand, key), jnp.where(cnt >= kf, cnt, cnt_key)

    if early_exit:
        state = lax.fori_loop(0, 3, bisect, (key, cnt_key))

        def four_bits(st):
            g, key, cnt_key, _ = st
            for b in range(4):
                key, cnt_key = bisect(3 + 4 * g + b, (key, cnt_key))
            return g + 1, key, cnt_key, unsettled(cnt_key)

        _, key, cnt_key, _ = lax.while_loop(lambda st: (st[0] < 7) & st[3], four_bits,
                                            (jnp.int32(0),) + state + (unsettled(state[1]),))
    else:
        key, cnt_key = lax.fori_loop(0, 31, bisect, (key, cnt_key))
    thr = _key_to_float(jnp.maximum(key, KEY_ABOVE_NEG_INF))

    @pl.when(jnp.max(cnt_key) > kf)
    def _():
        need = kf - count(lambda a, b: a > b, thr)
        s_i = lax.broadcasted_iota(I32, (LANES, LANES), 0)
        j_i = lax.broadcasted_iota(I32, (LANES, LANES), 1)
        earlier = jnp.where(j_i < s_i, 1.0, 0.0).astype(BF16)

        def body(c, seen):
            s0 = pl.multiple_of(c * LANES, LANES)
            blk = sc_ref[pl.ds(s0, LANES), :]
            eq = blk == thr
            eq_f = jnp.where(eq, 1.0, 0.0)
            rank = seen + _dot(earlier, eq_f.astype(BF16))
            sc_ref[pl.ds(s0, LANES), :] = jnp.where(eq & (rank >= need), -jnp.inf, blk)
            return seen + jnp.sum(eq_f, axis=0, keepdims=True)

        lax.fori_loop(0, n_chunks, body, jnp.zeros((1, T), F32))

    return thr


def _softmax_init(m_scr, acc_scr):
    m_scr[...] = jnp.full(m_scr.shape, NEG, F32)
    acc_scr[...] = jnp.zeros(acc_scr.shape, F32)


def _softmax_out(acc_scr, h):
    acc = acc_scr[h]
    return acc[:HEAD_DIM, :] / jnp.maximum(acc[HEAD_DIM:HEAD_DIM + 1, :], 1e-30)


def _exact_softmax(s, valid):
    s = jnp.where(valid, s, NEG)
    m = jnp.max(s, axis=0, keepdims=True)
    e = jnp.where(valid, jnp.exp(s - m), 0.0)
    return e / jnp.maximum(jnp.sum(e, axis=0, keepdims=True), 1e-30)


def _store_heads(o_ref, heads_t):
    for pair in range(2):
        o_ref[0, :, pair * LANES:(pair + 1) * LANES] = jnp.concatenate(
            [t[:HEAD_DIM, :] for t in heads_t[2 * pair:2 * pair + 2]], axis=0).T


def _head_values(v_ref, j):
    return [v_ref[0, j, h * HV_ROWS:(h + 1) * HV_ROWS, :] for h in range(N_HEADS)]


def _attn_scratch(rows):
    return [pltpu.VMEM((N_HEADS, 1, T), F32), pltpu.VMEM((N_HEADS, rows, T), F32)]


def _attn_params():
    return pltpu.CompilerParams(dimension_semantics=("arbitrary", "arbitrary"),
                                vmem_limit_bytes=VMEM_LIMIT)


def _keys_spec(S, units, unit):
    return pl.BlockSpec((1, S, units * LANES), lambda b, i: (b, 0, unit // units))


def _q_spec(rows):
    return pl.BlockSpec((1, 1, rows, T), lambda b, i: (b, i, 0, 0))


def _v_spec(S, rows):
    return pl.BlockSpec((1, S // T, rows, T), lambda b, i: (b, 0, 0, 0))


_OUT_SPEC = pl.BlockSpec((1, T, BR_WIDTH), lambda b, i: (b, i, 0))


def _fox_kernel(q_ref, k_ref, v_ref, o_ref, m_scr, acc_scr):
    qb = pl.program_id(1)
    _softmax_init(m_scr, acc_scr)

    def tiles(js, masked=False):
        scores = [[_dot(k_ref[0, pl.ds(pl.multiple_of(j * T, T), T), h * LANES:(h + 1) * LANES],
                        q_ref[0, 0, h * LANES:(h + 1) * LANES, :]) for j in js]
                  for h in range(N_HEADS)]
        valids = [None] * len(js)
        if masked:
            key, qry = _key_query_index(qb, js[0])
            valids = [key <= qry]
        _softmax_update(scores, valids, [_head_values(v_ref, j) for j in js], m_scr, acc_scr)

    _for_tile_groups(qb, tiles)
    tiles([qb], masked=True)
    _store_heads(o_ref, [_softmax_out(acc_scr, h) for h in range(N_HEADS)])


def _fox_call(pr):
    B, S, _ = pr['keys'].shape
    return pl.pallas_call(
        _fox_kernel,
        out_shape=jax.ShapeDtypeStruct((B, S, BR_WIDTH), F32),
        grid=(B, S // T),
        in_specs=[_q_spec(N_HEADS * LANES), _keys_spec(S, 4, A_FK), _v_spec(S, N_HEADS * HV_ROWS)],
        out_specs=_OUT_SPEC,
        scratch_shapes=_attn_scratch(HV_ROWS),
        compiler_params=_attn_params(),
        name="fox",
    )(pr['fq_t'], pr['keys'], pr['fv_t'])


def _sb_kernel(q_ref, k_ref, v_ref, o_ref, r_scr, acc_scr):
    qb = pl.program_id(1)
    qp = _pair_pads(q_ref[0, 0])
    r_scr[...] = jnp.zeros(r_scr.shape, F32)
    acc_scr[...] = jnp.zeros(acc_scr.shape, F32)
    s_i = lax.broadcasted_iota(I32, (T, T), 0)
    j_i = lax.broadcasted_iota(I32, (T, T), 1)
    later_keys = jnp.where(j_i > s_i, 1.0, 0.0).astype(BF16)

    def tile(j, masked):
        s0 = pl.multiple_of(j * T, T)
        strict = None
        if masked:
            key, qry = _key_query_index(qb, j)
            strict = key < qry
        r_prev = [r_scr[h] for h in range(N_HEADS)]
        r_new, updates = [], []
        for h in range(N_HEADS):
            pair = h // 2
            z = _dot(k_ref[0, pl.ds(s0, T), pair * LANES:(pair + 1) * LANES], qp[h])
            log_take = _log_sigmoid(z)
            log_keep = log_take - z
            if masked:
                log_keep = jnp.where(strict, log_keep, 0.0)
            within = _dot3x(later_keys, log_keep)
            a = jnp.exp(log_take + (r_prev[h] + within))
            if masked:
                a = jnp.where(strict, a, 0.0)
            updates.append(_dot(v_ref[0, j, h * HEAD_DIM:(h + 1) * HEAD_DIM, :], a.astype(BF16)))
            r_new.append(r_prev[h] + within[0:1, :] + log_keep[0:1, :])
        for h in range(N_HEADS):
            acc_scr[h] = acc_scr[h] + updates[h]
            r_scr[h] = r_new[h]

    def live():
        return jnp.max(r_scr[...]) > SB_CUTOFF

    def step(state):
        i, _ = state
        tile(qb - 1 - i, False)
        return i + 1, live()

    tile(qb, True)
    lax.while_loop(lambda st: (st[0] < qb) & st[1], step, (jnp.int32(0), live()))
    _store_heads(o_ref, [acc_scr[h] for h in range(N_HEADS)])


def _sb_call(pr):
    B, S, _ = pr['keys'].shape
    return pl.pallas_call(
        _sb_kernel,
        out_shape=jax.ShapeDtypeStruct((B, S, BR_WIDTH), F32),
        grid=(B, S // T),
        in_specs=[_q_spec(BR_WIDTH), _keys_spec(S, 2, A_SK), _v_spec(S, BR_WIDTH)],
        out_specs=_OUT_SPEC,
        scratch_shapes=_attn_scratch(HEAD_DIM),
        compiler_params=_attn_params(),
        name="stick_breaking",
    )(pr['sq_t'], pr['keys'], pr['sv_t'])


def _dsa_kernel(q_ref, k_ref, v_ref, qi_ref, ki_ref, sm_ref, o_ref, sc_scr, m_scr, acc_scr, *, k_sel):
    qb = pl.program_id(1)
    row = lax.broadcasted_iota(I32, (LANES, T), 0)
    qi = qi_ref[0, 0]
    zero = jnp.zeros_like(qi)
    qi_pad = [jnp.where((row >= IDX_DIM * h) & (row < IDX_DIM * (h + 1)), qi, zero)
              for h in range(IDX_HEADS)]
    wi = [sm_ref[0, 0, SM_WI + h:SM_WI + h + 1, :] for h in range(IDX_HEADS)]

    def score_tile(j, masked):
        s0 = pl.multiple_of(j * T, T)
        kt = ki_ref[0, pl.ds(s0, T), :]
        sc = jnp.zeros((T, T), F32)
        for h in range(IDX_HEADS):
            sc = sc + wi[h] * jnp.maximum(_dot(kt, qi_pad[h]), 0.0)
        if masked:
            key, qry = _key_query_index(qb, j)
            sc = jnp.where(key <= qry, sc, -jnp.inf)
        sc_scr[pl.ds(s0, T), :] = sc

    lax.fori_loop(0, qb, lambda j, c: (score_tile(j, False), c)[1], 0)
    score_tile(qb, True)

    n_valid = (qb * T + 1 + lax.broadcasted_iota(I32, (1, T), 1)).astype(F32)
    thr = _top_k_threshold(sc_scr, qb + 1, T, k_sel, n_valid, early_exit=True)

    qp = _pair_pads(q_ref[0, 0])
    _softmax_init(m_scr, acc_scr)

    def attn_tiles(js):
        scores = [[_dot(k_ref[0, pl.ds(pl.multiple_of(j * T, T), T),
                              (h // 2) * LANES:(h // 2 + 1) * LANES], qp[h]) for j in js]
                  for h in range(N_HEADS)]
        valids = [sc_scr[pl.ds(pl.multiple_of(j * T, T), T), :] >= thr for j in js]
        _softmax_update(scores, valids, [_head_values(v_ref, j) for j in js], m_scr, acc_scr)

    _for_tile_groups(qb + 1, attn_tiles)
    _store_heads(o_ref, [_softmax_out(acc_scr, h) for h in range(N_HEADS)])


def _dsa_call(pr):
    B, S, _ = pr['keys'].shape
    return pl.pallas_call(
        functools.partial(_dsa_kernel, k_sel=min(DSA_TOPK_MAX, S // 4)),
        out_shape=jax.ShapeDtypeStruct((B, S, BR_WIDTH), F32),
        grid=(B, S // T),
        in_specs=[_q_spec(BR_WIDTH), _keys_spec(S, 2, A_KA), _v_spec(S, N_HEADS * HV_ROWS),
                  _q_spec(LANES), _keys_spec(S, 1, A_KI), _q_spec(LANES)],
        out_specs=_OUT_SPEC,
        scratch_shapes=[pltpu.VMEM((S, T), F32)] + _attn_scratch(HV_ROWS),
        compiler_params=_attn_params(),
        name="dsa",
    )(pr['dq_t'], pr['keys'], pr['va_t'], pr['qi_t'], pr['keys'], pr['sm_t'])


def _compress_kernel(xk_ref, xv_ref, pek_ref, pev_ref, w1k_ref, w1v_ref, w2k_ref, w2v_ref,
                     o_ref, ot_ref, *, nc):
    def hidden(x_ref, pe_ref, w1_ref):
        x = x_ref[0]
        first = _dot((x + pe_ref[0:1, :]).astype(BF16), w1_ref[0])
        second = _dot((x + pe_ref[1:2, :]).astype(BF16), w1_ref[1])
        pre = first + pltpu.roll(second, nc - 1, 0)
        return (pre * _sigmoid(pre)).astype(BF16)

    kvc = (_dot(hidden(xk_ref, pek_ref, w1k_ref), w2k_ref[...])
           + _dot(hidden(xv_ref, pev_ref, w1v_ref), w2v_ref[...]))
    o_ref[0] = kvc.astype(BF16)
    ot_ref[0] = kvc.T.astype(BF16)


def _compress_call(xk, xv, pek, pev, w1k, w1v, w2k, w2v):
    B, nc, width = xk.shape
    full = lambda shape: pl.BlockSpec(shape, lambda b: (0,) * len(shape))
    tok = pl.BlockSpec((1, nc, width), lambda b: (b, 0, 0))
    return pl.pallas_call(
        functools.partial(_compress_kernel, nc=nc),
        out_shape=(jax.ShapeDtypeStruct((B, nc, LANES), BF16), jax.ShapeDtypeStruct((B, LANES, nc), BF16)),
        grid=(B,),
        in_specs=[tok, tok, full((2, width)), full((2, width)), full((2, width, CMP_HIDDEN)),
                  full((2, width, CMP_HIDDEN)), full((CMP_HIDDEN, LANES)), full((CMP_HIDDEN, LANES))],
        out_specs=(pl.BlockSpec((1, nc, LANES), lambda b: (b, 0, 0)),
                   pl.BlockSpec((1, LANES, nc), lambda b: (b, 0, 0))),
        compiler_params=pltpu.CompilerParams(dimension_semantics=("arbitrary",),
                                             vmem_limit_bytes=VMEM_LIMIT),
        name="nsa_compress",
    )(xk, xv, pek, pev, w1k, w1v, w2k, w2v)


def _nsa_kernel(q_ref, kvc_ref, kvct_ref, ksw_ref, vsw_ref, sm_ref, o_ref,
                imp_scr, sel_scr, m_scr, acc_scr, *, n_blk, n_sel, nbp):
    qb = pl.program_id(1)
    ncp = kvc_ref.shape[1]
    q_t = q_ref[0, 0]
    zero = jnp.zeros((HEAD_DIM, T), q_t.dtype)
    heads = [q_t[h * HEAD_DIM:(h + 1) * HEAD_DIM, :] for h in range(N_HEADS)]
    q_lo = [jnp.concatenate([q, zero], axis=0) for q in heads]
    q_hi = [jnp.concatenate([zero, q], axis=0) for q in heads]
    gate = lambda br, h: sm_ref[0, 0, SM_GATE + br * N_HEADS + h:SM_GATE + br * N_HEADS + h + 1, :]
    t_row = qb * T + lax.broadcasted_iota(I32, (1, T), 1)

    kvc = kvc_ref[0]
    cmp_end = lax.broadcasted_iota(I32, (ncp, T), 0) * CMP_STRIDE + (CMP_LEN - 1)
    cmp_ok = cmp_end <= t_row
    o_cmp, p_sum = [], jnp.zeros((ncp, T), F32)
    for h in range(N_HEADS):
        pc = _exact_softmax(_dot(kvc, q_lo[h]), cmp_ok)
        o_cmp.append(_dot(kvct_ref[0], pc.astype(BF16))[HEAD_DIM:, :])
        p_sum = p_sum + pc
    per_blk = SLC_BLOCK // CMP_STRIDE
    b_i = lax.broadcasted_iota(I32, (nbp, ncp), 0)
    c_i = lax.broadcasted_iota(I32, (nbp, ncp), 1)
    group = jnp.where(_div_pow2(c_i, per_blk) == b_i, 1.0, 0.0).astype(BF16)
    imp = _dot3x(group, p_sum)
    blk = lax.broadcasted_iota(I32, (nbp, T), 0)
    cur = _div_pow2(t_row, SLC_BLOCK)
    forced = (blk == 0) | (blk == cur) | (blk == cur - 1)
    imp = jnp.where(forced, FORCE_SCORE, jnp.where(blk <= cur, imp, -1.0))
    imp = jnp.where(blk < n_blk, imp, -2.0)
    imp_scr[...] = imp
    thr = _top_k_threshold(imp_scr, 1, nbp, n_sel, jnp.full((1, T), float(nbp), F32),
                           early_exit=False)
    sel_scr[...] = jnp.where(imp_scr[...] >= thr, 1.0, 0.0)

    _softmax_init(m_scr, acc_scr)
    blk_per_tile = T // SLC_BLOCK

    def slc_tiles(js, masked=False):
        valids, scores = [], [[] for _ in range(N_HEADS)]
        for j in js:
            valid = jnp.concatenate(
                [jnp.broadcast_to(sel_scr[pl.ds(j * blk_per_tile + b, 1), :], (SLC_BLOCK, T))
                 for b in range(blk_per_tile)], axis=0) > 0.5
            if masked:
                key, qry = _key_query_index(qb, j)
                valid = valid & (key <= qry)
            valids.append(valid)
            kt = ksw_ref[0, pl.ds(pl.multiple_of(j * T, T), T), :]
            for h in range(N_HEADS):
                scores[h].append(_dot(kt, q_lo[h]))
        _softmax_update(scores, valids, [[vsw_ref[0, j, :HV_ROWS, :]] * N_HEADS for j in js],
                        m_scr, acc_scr)

    _for_tile_groups(qb, slc_tiles)
    slc_tiles([qb], masked=True)

    n_win = WINDOW // T + 1
    j0 = jnp.maximum(qb - (n_win - 1), 0)
    outs = []
    for h in range(N_HEADS):
        logits, oks = [], []
        for i in range(n_win):
            s0 = pl.multiple_of((j0 + i) * T, T)
            key, qry = _key_query_index(qb, j0 + i)
            dist = qry - key
            oks.append((dist >= 0) & (dist < WINDOW))
            logits.append(jnp.where(oks[i], _dot(ksw_ref[0, pl.ds(s0, T), :], q_hi[h]), NEG))
        m = functools.reduce(jnp.maximum, [jnp.max(s, axis=0, keepdims=True) for s in logits])
        es = [jnp.where(ok, jnp.exp(s - m), 0.0) for s, ok in zip(logits, oks)]
        den = functools.reduce(jnp.add, [jnp.sum(e, axis=0, keepdims=True) for e in es])
        o_win = functools.reduce(jnp.add, [
            _dot(vsw_ref[0, j0 + i, HV_ROWS:, :], (e / jnp.maximum(den, 1e-30)).astype(BF16))
            for i, e in enumerate(es)])
        outs.append(gate(0, h) * o_cmp[h] + gate(1, h) * _softmax_out(acc_scr, h)
                    + gate(2, h) * o_win)
    _store_heads(o_ref, outs)


def _nsa_call(pr, kvc, kvc_t):
    B, S, _ = pr['keys'].shape
    n_blk = S // SLC_BLOCK
    nbp = max(LANES, n_blk)
    ncp = kvc.shape[1]
    return pl.pallas_call(
        functools.partial(_nsa_kernel, n_blk=n_blk, n_sel=min(SLC_TOPN, n_blk), nbp=nbp),
        out_shape=jax.ShapeDtypeStruct((B, S, BR_WIDTH), F32),
        grid=(B, S // T),
        in_specs=[_q_spec(BR_WIDTH),
                  pl.BlockSpec((1, ncp, LANES), lambda b, i: (b, 0, 0)),
                  pl.BlockSpec((1, LANES, ncp), lambda b, i: (b, 0, 0)),
                  _keys_spec(S, 1, A_KSW), _v_spec(S, VSW_ROWS), _q_spec(LANES)],
        out_specs=_OUT_SPEC,
        scratch_shapes=[pltpu.VMEM((nbp, T), F32), pltpu.VMEM((nbp, T), F32)] + _attn_scratch(HV_ROWS),
        compiler_params=_attn_params(),
        name="nsa",
    )(pr['nq_t'], kvc, kvc_t, pr['keys'], pr['vsw_t'], pr['sm_t'])


def _mem_kv_kernel(mem_ref, g_ref, w_ref, o_ref):
    o_ref[0] = _dot(_rmsnorm(mem_ref[0], g_ref[...]).astype(BF16), w_ref[...]).astype(BF16)


def _mem_kv_call(mem, g, w):
    B, M, D = mem.shape
    return pl.pallas_call(
        _mem_kv_kernel,
        out_shape=jax.ShapeDtypeStruct((B, M, 2 * BR_WIDTH), BF16),
        grid=(B,),
        in_specs=[pl.BlockSpec((1, M, D), lambda b: (b, 0, 0)), pl.BlockSpec((1, D), lambda b: (0, 0)),
                  pl.BlockSpec((D, 2 * BR_WIDTH), lambda b: (0, 0))],
        out_specs=pl.BlockSpec((1, M, 2 * BR_WIDTH), lambda b: (b, 0, 0)),
        name="mem_kv",
    )(mem, g, w)


def _out_kernel(x_ref, g_ref, ya_ref, yb_ref, yc_ref, yd_ref, mq_ref, mkv_ref, wz_ref, wm_ref,
                wb_ref, wo_ref, fg_ref, o_ref, h_scr, *, final):
    x = x_ref[0]
    h_scr[...] = _rmsnorm(x, g_ref[...]).astype(BF16)
    q = mq_ref[0]
    lane = lax.broadcasted_iota(I32, (T, LANES), 1)
    lo = lane < HEAD_DIM
    zero = jnp.zeros((T, LANES), q.dtype)
    pairs = []
    for pair in range(2):
        chunk = q[:, pair * LANES:(pair + 1) * LANES]
        mk = mkv_ref[0, :, pair * LANES:(pair + 1) * LANES]
        mv = mkv_ref[0, :, BR_WIDTH + pair * LANES:BR_WIDTH + (pair + 1) * LANES]
        outs = []
        for qh in (jnp.where(lo, chunk, zero), jnp.where(lo, zero, chunk)):
            s = _dot_nt(qh, mk)
            e = jnp.exp(s - jnp.max(s, axis=1, keepdims=True))
            outs.append(_dot((e / jnp.sum(e, axis=1, keepdims=True)).astype(BF16), mv))
        pairs.append(jnp.where(lo, outs[0], outs[1]))
    y_e = jnp.concatenate(pairs, axis=1)
    merged = jnp.zeros(x.shape, F32)
    for n, y in enumerate((ya_ref[0], yb_ref[0], yc_ref[0], yd_ref[0], y_e)):
        z = _dot(h_scr[...], wz_ref[n])
        ys = (y * (z * _sigmoid(z))).astype(BF16)
        merged = merged + _sigmoid(_dot(h_scr[...], wm_ref[n])) * _dot(ys, wb_ref[n])
    out = x + _dot(merged.astype(BF16), wo_ref[...])
    o_ref[0] = _rmsnorm(out, fg_ref[...]) if final else out


def _out_call(x, g, ys, keys, mkv, wz, wm, wb, wo, fg, final):
    B, S, D = x.shape
    M = mkv.shape[1]
    full = lambda shape: pl.BlockSpec(shape, lambda b, i: (0,) * len(shape))
    tile = lambda n: pl.BlockSpec((1, T, n), lambda b, i: (b, i, 0))
    return pl.pallas_call(
        functools.partial(_out_kernel, final=final),
        out_shape=jax.ShapeDtypeStruct((B, S, D), F32),
        grid=(B, S // T),
        in_specs=[tile(D), full((1, D))] + [tile(BR_WIDTH)] * 4
                 + [pl.BlockSpec((1, T, BR_WIDTH), lambda b, i: (b, i, A_MQ // 2)),
                    pl.BlockSpec((1, M, 2 * BR_WIDTH), lambda b, i: (b, 0, 0)),
                    full((N_BRANCH, D, BR_WIDTH)), full((N_BRANCH, D, D)), full((N_BRANCH, BR_WIDTH, D)),
                    full((D, D)), full((1, D))],
        out_specs=tile(D),
        scratch_shapes=[pltpu.VMEM((T, D), BF16)],
        compiler_params=pltpu.CompilerParams(dimension_semantics=("arbitrary", "arbitrary"),
                                             vmem_limit_bytes=VMEM_LIMIT),
        name="merge_out",
    )(x, g, *ys, keys, mkv, wz, wm, wb, wo, fg)


def kernel(x, mem, positions, norm_g, w_in, kv_norm, w_uk, w_uv, fox_bias, nsa_pe_k, nsa_pe_v,
           nsa_wc1_k, nsa_wc2_k, nsa_wc1_v, nsa_wc2_v, mem_norm, w_mem_kv, w_branch, w_out, final_norm):
    B, S, D = x.shape
    depth = norm_g.shape[0]
    assert S % T == 0 and S >= WINDOW + T and WINDOW % T == 0 and D == 1024
    cos, sin = _rope_tables(positions)
    w1, wz, wm = _proj_weight(w_in)
    fb = jnp.zeros((depth, 1, LANES), F32).at[:, 0, SM_CUM:SM_CUM + N_HEADS].set(fox_bias)
    nc = S // CMP_STRIDE
    tok_w = CMP_STRIDE * NSA_KV_DIM
    pad_k = jnp.zeros((depth, CMP_HIDDEN, LANES), F32).at[..., :NSA_KV_DIM].set(nsa_wc2_k).astype(BF16)
    pad_v = jnp.zeros((depth, CMP_HIDDEN, LANES), F32).at[..., NSA_KV_DIM:].set(nsa_wc2_v).astype(BF16)
    for l in range(depth):
        pr = _proj_call(x, norm_g[l][None], w1[l], kv_norm[l][None], w_uk[l].astype(BF16),
                        w_uv[l].astype(BF16), fb[l], cos, sin)
        kvc, kvc_t = _compress_call(
            pr['kvtok'][:, :, :NSA_KV_DIM].reshape(B, nc, tok_w),
            pr['kvtok'][:, :, NSA_KV_DIM:].reshape(B, nc, tok_w),
            nsa_pe_k[l].reshape(2, tok_w), nsa_pe_v[l].reshape(2, tok_w),
            nsa_wc1_k[l].reshape(2, tok_w, CMP_HIDDEN).astype(BF16),
            nsa_wc1_v[l].reshape(2, tok_w, CMP_HIDDEN).astype(BF16), pad_k[l], pad_v[l])
        ys = (_dsa_call(pr), _fox_call(pr), _sb_call(pr), _nsa_call(pr, kvc, kvc_t))
        mkv = _mem_kv_call(mem, mem_norm[l][None], w_mem_kv[l].astype(BF16))
        x = _out_call(x, norm_g[l][None], ys, pr['keys'], mkv, wz[l], wm[l], w_branch[l].astype(BF16),
                      w_out[l].astype(BF16), final_norm[None], l == depth - 1)
    return x
```

```python
import functools

import jax
import jax.numpy as jnp
from jax import lax
from jax.experimental import pallas as pl
from jax.experimental.pallas import tpu as pltpu

F32 = jnp.float32
BF16 = jnp.bfloat16
I32 = jnp.int32
I16 = jnp.int16

LANES = 128
SUBLANES = 8
N_HEADS = 4
HEAD_DIM = 64
BR_WIDTH = N_HEADS * HEAD_DIM
N_BRANCH = 5
ROPE_THETA = 10000.0
EPS = 1e-6
DSA_TOPK_MAX = 256
IDX_HEADS = 4
IDX_DIM = 32
KV_LATENT = 128
NSA_KV_DIM = 64
CMP_LEN = 32
CMP_STRIDE = 16
CMP_HIDDEN = 128
SLC_BLOCK = 64
SLC_TOPN = 16
WINDOW = 512
FORCE_SCORE = 1e4
Q_SCALE = HEAD_DIM ** -0.5
IDX_SCALE = (IDX_DIM ** -0.5) * (IDX_HEADS ** -0.5)
NEG = -1e30
SB_CUTOFF = -104.0
VMEM_LIMIT = 56 * 1024 * 1024
T = 256
ONES_ROWS = 16
HV_ROWS = HEAD_DIM + ONES_ROWS
VSW_ROWS = HV_ROWS + NSA_KV_DIM

IN_LAYOUT = (
    ('dsa_q', BR_WIDTH), ('dsa_ckv', KV_LATENT), ('idx_q', IDX_HEADS * IDX_DIM), ('idx_k', IDX_DIM),
    ('idx_w', IDX_HEADS), ('dsa_z', BR_WIDTH),
    ('fox_q', BR_WIDTH), ('fox_k', BR_WIDTH), ('fox_v', BR_WIDTH), ('fox_f', N_HEADS), ('fox_z', BR_WIDTH),
    ('sb_q', BR_WIDTH), ('sb_k', BR_WIDTH), ('sb_v', BR_WIDTH), ('sb_z', BR_WIDTH),
    ('nsa_q', BR_WIDTH), ('nsa_kc', NSA_KV_DIM), ('nsa_vc', NSA_KV_DIM), ('nsa_ks', NSA_KV_DIM),
    ('nsa_vs', NSA_KV_DIM), ('nsa_kw', NSA_KV_DIM), ('nsa_vw', NSA_KV_DIM), ('nsa_g', 3 * N_HEADS),
    ('nsa_z', BR_WIDTH), ('mem_q', BR_WIDTH), ('mem_z', BR_WIDTH), ('merge', N_BRANCH * 1024),
)

A_FK, A_KA, A_SK, A_KI, A_KSW, A_MQ = 0, 4, 6, 8, 9, 10
A_COLS = 12 * LANES
SM_WI, SM_CUM, SM_GATE = 0, 4, 8
FOX_CK, FOX_ONE, FOX_END = HEAD_DIM, HEAD_DIM + 3, HEAD_DIM + 6


def _dot(a, b):
    return jnp.dot(a, b, preferred_element_type=F32)


def _dot_nt(a, b):
    return lax.dot_general(a, b, (((1,), (1,)), ((), ())), preferred_element_type=F32)


def _split3(x):
    hi = x.astype(BF16)
    r1 = x - hi.astype(F32)
    mid = r1.astype(BF16)
    lo = (r1 - mid.astype(F32)).astype(BF16)
    return hi, mid, lo


def _dot3x(m01, x):
    hi, mid, lo = _split3(x)
    return _dot(m01, hi) + _dot(m01, mid) + _dot(m01, lo)


def _div_pow2(x, n):
    assert n & (n - 1) == 0
    return x >> (n.bit_length() - 1)


def _sigmoid(x):
    return 1.0 / (1.0 + jnp.exp(-x))


def _log_sigmoid(x):
    return jnp.minimum(x, 0.0) - jnp.log(1.0 + jnp.exp(-jnp.abs(x)))


def _rmsnorm(x, g):
    return x * lax.rsqrt(jnp.mean(x * x, axis=-1, keepdims=True) + EPS) * g


def _rope128(x, cos, sin_signed, half):
    lane = lax.broadcasted_iota(I32, x.shape, 1)
    first = (lane & (2 * half - 1)) < half
    partner = jnp.where(first, pltpu.roll(x, LANES - half, 1), pltpu.roll(x, half, 1))
    return x * cos + partner * sin_signed


def _rope_table_kernel(pos_ref, inv_ref, sgn_ref, cos_ref, sin_ref):
    ang = pos_ref[0] * inv_ref[...]
    cos_ref[0] = jnp.cos(ang)
    sin_ref[0] = jnp.sin(ang) * sgn_ref[...]


def _rope_tables(positions):
    B, S = positions.shape
    lane = jnp.arange(LANES)

    def inv_row(dh):
        inv = ROPE_THETA ** (-jnp.arange(0, dh, 2, dtype=F32) / dh)
        return inv[(lane % dh) % (dh // 2)]

    def sgn_row(dh):
        return jnp.where((lane % dh) < dh // 2, -1.0, 1.0).astype(F32)

    inv = jnp.concatenate([inv_row(HEAD_DIM), inv_row(IDX_DIM)])[None, :]
    sgn = jnp.concatenate([sgn_row(HEAD_DIM), sgn_row(IDX_DIM)])[None, :]
    pos = positions.astype(F32)[..., None]
    row = pl.BlockSpec((1, 2 * LANES), lambda b, i: (0, 0))
    tab = pl.BlockSpec((1, T, 2 * LANES), lambda b, i: (b, i, 0))
    return pl.pallas_call(
        _rope_table_kernel,
        out_shape=(jax.ShapeDtypeStruct((B, S, 2 * LANES), F32),) * 2,
        grid=(B, S // T),
        in_specs=[pl.BlockSpec((1, T, 1), lambda b, i: (b, i, 0)), row, row],
        out_specs=(tab, tab),
        name="rope_tables",
    )(pos, inv, sgn)


W_DQ, W_CKV, W_QI, W_KI, W_SM, W_FQ, W_SQ, W_NQ, W_MQ, W_KVC, W_KSW, W_VSW = (
    0, 256, 384, 512, 640, 768, 1536, 2304, 2560, 2816, 2944, 3072)
W_COLS = 3200

PROJ_OUTS = (
    ('keys', A_COLS, BF16), ('kvtok', LANES, F32),
    ('dq_t', BR_WIDTH, BF16), ('fq_t', N_HEADS * LANES, BF16), ('sq_t', BR_WIDTH, BF16),
    ('nq_t', BR_WIDTH, BF16), ('qi_t', LANES, BF16), ('sm_t', LANES, F32),
    ('va_t', N_HEADS * HV_ROWS, BF16), ('fv_t', N_HEADS * HV_ROWS, BF16), ('sv_t', BR_WIDTH, BF16),
    ('vsw_t', VSW_ROWS, BF16),
)
ROW_MAJOR = ('keys', 'kvtok')


def _proj_kernel(x_ref, g_ref, w_ref, kvn_ref, wuk_ref, wuv_ref, fb_ref, cos_ref, sin_ref,
                 *refs):
    out = dict(zip([n for n, _, _ in PROJ_OUTS], refs))
    h_scr, carry_scr = refs[len(PROJ_OUTS):]
    keys = out['keys']
    h_scr[...] = _rmsnorm(x_ref[0], g_ref[...]).astype(BF16)
    c64, s64 = cos_ref[0, :, :LANES], sin_ref[0, :, :LANES]
    c32, s32 = cos_ref[0, :, LANES:], sin_ref[0, :, LANES:]
    lane = lax.broadcasted_iota(I32, (T, LANES), 1)
    ones_rows = jnp.ones((ONES_ROWS, T), BF16)

    def proj(c0, n):
        return _dot(h_scr[...], w_ref[:, c0:c0 + n])

    def put(unit, val):
        keys[0, :, unit * LANES:unit * LANES + val.shape[1]] = val.astype(BF16)

    def rope64(v, j):
        return _rope128(v[:, j * LANES:(j + 1) * LANES], c64, s64, HEAD_DIM // 2)

    def put_t(ref, row0, chunk):
        ref[0, 0, row0:row0 + LANES, :] = chunk.T.astype(ref.dtype)

    def put_values_t(ref, v):
        for j in range(2):
            pair_t = v[:, j * LANES:(j + 1) * LANES].T.astype(BF16)
            for odd in range(2):
                r0 = (2 * j + odd) * HV_ROWS
                ref[0, 0, r0:r0 + HEAD_DIM, :] = pair_t[odd * HEAD_DIM:(odd + 1) * HEAD_DIM, :]
                ref[0, 0, r0 + HEAD_DIM:r0 + HV_ROWS, :] = ones_rows

    p = proj(W_DQ, BR_WIDTH)
    for j in range(2):
        put_t(out['dq_t'], j * LANES, rope64(p, j) * Q_SCALE)
    c_kv = _rmsnorm(proj(W_CKV, KV_LATENT), kvn_ref[...]).astype(BF16)
    k_a = _dot(c_kv, wuk_ref[...])
    for j in range(2):
        put(A_KA + j, rope64(k_a, j))
    put_values_t(out['va_t'], _dot(c_kv, wuv_ref[...]))
    put_t(out['qi_t'], 0, _rope128(proj(W_QI, LANES), c32, s32, IDX_DIM // 2))
    put(A_KI, _rope128(proj(W_KI, LANES), c32, s32, IDX_DIM // 2))
    p = proj(W_SM, LANES)
    log_f = _log_sigmoid(p + fb_ref[...])
    r_i = lax.broadcasted_iota(I32, (T, T), 0)
    c_i = lax.broadcasted_iota(I32, (T, T), 1)
    tri = jnp.where(r_i >= c_i, 1.0, 0.0).astype(BF16)

    @pl.when(pl.program_id(1) == 0)
    def _():
        carry_scr[...] = jnp.zeros_like(carry_scr)

    cum = _dot3x(tri, log_f) + carry_scr[...]
    carry_scr[...] = cum[T - 1:T, :]
    small_t = jnp.where(lane < SM_CUM, p * IDX_SCALE,
                        jnp.where(lane < SM_GATE, cum, _sigmoid(p))).T
    out['sm_t'][0, 0] = small_t
    q_f = proj(W_FQ, BR_WIDTH) * Q_SCALE
    k_f = proj(W_FQ + 256, BR_WIDTH)
    row = lax.broadcasted_iota(I32, (HEAD_DIM, T), 0)
    for h in range(N_HEADS):
        pair, odd = h // 2, h % 2
        k_chunk = k_f[:, pair * LANES:(pair + 1) * LANES]
        k_h = pltpu.roll(k_chunk, HEAD_DIM, 1) if odd else k_chunk
        ck = [c.astype(F32) for c in _split3(cum[:, SM_CUM + h:SM_CUM + h + 1])]
        k_aug = jnp.where(lane < HEAD_DIM, k_h,
                          jnp.where(lane == FOX_CK, ck[0],
                                    jnp.where(lane == FOX_CK + 1, ck[1],
                                              jnp.where(lane == FOX_CK + 2, ck[2],
                                                        jnp.where(lane < FOX_END, 1.0, 0.0)))))
        put(A_FK + h, k_aug)
        q_ht = q_f[:, pair * LANES:(pair + 1) * LANES].T[odd * HEAD_DIM:(odd + 1) * HEAD_DIM, :]
        cq = [c.astype(F32) for c in _split3(small_t[SM_CUM + h:SM_CUM + h + 1, :])]
        aug = jnp.where(row < 3, -1.0,
                        jnp.where(row == 3, cq[0], jnp.where(row == 4, cq[1],
                                                             jnp.where(row == 5, cq[2], 0.0))))
        out['fq_t'][0, 0, h * LANES:(h + 1) * LANES, :] = jnp.concatenate(
            [q_ht, aug], axis=0).astype(BF16)
    put_values_t(out['fv_t'], proj(W_FQ + 512, BR_WIDTH))
    p = proj(W_SQ, BR_WIDTH) * Q_SCALE
    for j in range(2):
        put_t(out['sq_t'], j * LANES, p[:, j * LANES:(j + 1) * LANES])
    put(A_SK, proj(W_SQ + 256, BR_WIDTH))
    p = proj(W_SQ + 512, BR_WIDTH)
    for j in range(2):
        put_t(out['sv_t'], j * LANES, p[:, j * LANES:(j + 1) * LANES])
    put(A_MQ, proj(W_MQ, BR_WIDTH) * Q_SCALE)
    p = proj(W_NQ, BR_WIDTH)
    for j in range(2):
        put_t(out['nq_t'], j * LANES, rope64(p, j) * Q_SCALE)
    p = proj(W_KVC, LANES)
    out['kvtok'][0] = jnp.where(lane < NSA_KV_DIM, rope64(p, 0), p)
    put(A_KSW, rope64(proj(W_KSW, LANES), 0))
    vsw_t = proj(W_VSW, LANES).T.astype(BF16)
    out['vsw_t'][0, 0, :NSA_KV_DIM, :] = vsw_t[:NSA_KV_DIM, :]
    out['vsw_t'][0, 0, NSA_KV_DIM:HV_ROWS, :] = ones_rows
    out['vsw_t'][0, 0, HV_ROWS:, :] = vsw_t[NSA_KV_DIM:, :]


def _proj_weight(w_in):
    off, o = {}, 0
    for name, n in IN_LAYOUT:
        off[name] = (o, n)
        o += n

    def col(name):
        s, n = off[name]
        return w_in[..., s:s + n]

    zeros = lambda n: jnp.zeros(w_in.shape[:-1] + (n,), w_in.dtype)
    groups = [col('dsa_q'), col('dsa_ckv'), col('idx_q'), col('idx_k'), col('idx_k'), col('idx_k'),
              col('idx_k'), col('idx_w'), col('fox_f'), col('nsa_g'), zeros(LANES - 20),
              col('fox_q'), col('fox_k'), col('fox_v'), col('sb_q'), col('sb_k'), col('sb_v'),
              col('nsa_q'), col('mem_q'), col('nsa_kc'), col('nsa_vc'), col('nsa_ks'), col('nsa_kw'),
              col('nsa_vs'), col('nsa_vw')]
    w1 = jnp.concatenate(groups, axis=-1).astype(BF16)
    wz = jnp.stack([col(n) for n in ('dsa_z', 'fox_z', 'sb_z', 'nsa_z', 'mem_z')], axis=-3).astype(BF16)
    s, n = off['merge']
    wm = w_in[..., s:s + n].reshape(w_in.shape[:-1] + (N_BRANCH, n // N_BRANCH))
    wm = jnp.moveaxis(wm, -2, -3).astype(BF16)
    return w1, wz, wm


def _proj_call(x, g, w1, kvn, wuk, wuv, fb, cos, sin):
    B, S, D = x.shape
    full = lambda shape: pl.BlockSpec(shape, lambda b, i: (0,) * len(shape))
    tile = lambda n: pl.BlockSpec((1, T, n), lambda b, i: (b, i, 0))
    tile_t = lambda n: pl.BlockSpec((1, 1, n, T), lambda b, i: (b, i, 0, 0))
    shapes, specs = [], []
    for name, n, dt in PROJ_OUTS:
        if name in ROW_MAJOR:
            shapes.append(jax.ShapeDtypeStruct((B, S, n), dt))
            specs.append(tile(n))
        else:
            shapes.append(jax.ShapeDtypeStruct((B, S // T, n, T), dt))
            specs.append(tile_t(n))
    outs = pl.pallas_call(
        _proj_kernel,
        out_shape=tuple(shapes),
        grid=(B, S // T),
        in_specs=[tile(D), full((1, D)), full((D, W_COLS)), full((1, KV_LATENT)),
                  full((KV_LATENT, BR_WIDTH)), full((KV_LATENT, BR_WIDTH)), full((1, LANES)),
                  tile(2 * LANES), tile(2 * LANES)],
        out_specs=tuple(specs),
        scratch_shapes=[pltpu.VMEM((T, D), BF16), pltpu.VMEM((1, LANES), F32)],
        compiler_params=pltpu.CompilerParams(dimension_semantics=("arbitrary", "arbitrary"),
                                             vmem_limit_bytes=VMEM_LIMIT),
        name="proj",
    )(x, g, w1, kvn, wuk, wuv, fb, cos, sin)
    return dict(zip([n for n, _, _ in PROJ_OUTS], outs))


def _key_query_index(qb, j):
    key = j * T + lax.broadcasted_iota(I32, (T, T), 0)
    qry = qb * T + lax.broadcasted_iota(I32, (T, T), 1)
    return key, qry


def _pair_pads(q_t):
    row = lax.broadcasted_iota(I32, (LANES, T), 0)
    top = row < HEAD_DIM
    zero = jnp.zeros((LANES, T), q_t.dtype)
    out = []
    for pair in range(2):
        chunk = q_t[pair * LANES:(pair + 1) * LANES, :]
        out.append(jnp.where(top, chunk, zero))
        out.append(jnp.where(top, zero, chunk))
    return out


def _softmax_update(scores, valids, values, m_scr, acc_scr):
    m_prev = [m_scr[h] for h in range(N_HEADS)]
    m_new, updates = [], []
    for h in range(N_HEADS):
        tiles = [s if v is None else jnp.where(v, s, NEG) for s, v in zip(scores[h], valids)]
        m = functools.reduce(jnp.maximum, [m_prev[h]] + [jnp.max(s, axis=0, keepdims=True) for s in tiles])
        update = None
        for i, (s, v) in enumerate(zip(tiles, valids)):
            p = jnp.exp(s - m)
            if v is not None:
                p = jnp.where(v, p, 0.0)
            u = _dot(values[i][h], p.astype(BF16))
            update = u if update is None else update + u
        m_new.append(m)
        updates.append(update)
    for h in range(N_HEADS):
        acc_scr[h] = jnp.exp(m_prev[h] - m_new[h]) * acc_scr[h] + updates[h]
        m_scr[h] = m_new[h]


def _for_tile_groups(n, group_fn, group):
    def full(i, c):
        group_fn([group * i + g for g in range(group)])
        return c

    lax.fori_loop(0, n // group, full, 0)
    size = group // 2
    while size:
        start = (n // (2 * size)) * (2 * size)

        @pl.when((n & size) != 0)
        def _(start=start, size=size):
            group_fn([start + g for g in range(size)])

        size //= 2


INT_MIN = -2 ** 31
KEY_ABOVE_NEG_INF = 0x80800000 - 2 ** 32


def _key_to_float(c):
    bits = c ^ ((c >> 31) & 0x7FFFFFFF)
    return lax.bitcast_convert_type(bits, F32)


def _count_rows(ref, n_steps, step_rows, pred, cand):
    pack = SUBLANES * (4 // ref.dtype.itemsize)
    cand_b = jnp.broadcast_to(cand, (pack, T)).astype(ref.dtype)
    one, zero = jnp.ones((pack, T), ref.dtype), jnp.zeros((pack, T), ref.dtype)

    def body(c, accs):
        blk = ref[pl.ds(pl.multiple_of(c * step_rows, step_rows), step_rows), :]
        accs = list(accs)
        for r in range(step_rows // pack):
            hit = jnp.where(pred(blk[r * pack:(r + 1) * pack, :], cand_b), one, zero)
            accs[r % len(accs)] = accs[r % len(accs)] + hit
        return tuple(accs)

    accs = lax.fori_loop(0, n_steps, body, (zero,) * 4)
    total = functools.reduce(jnp.add, [a.astype(I32) if a.dtype != F32 else a for a in accs])
    return jnp.sum(total, axis=0, keepdims=True).astype(F32)


_GE = lambda a, b: a >= b
_GT = lambda a, b: a > b


def _bisect(count_ge, target, lowest, n_bits, cnt_lowest, to_cand=lambda c: c):
    cnt = count_ge(to_cand(jnp.zeros((1, T), I32)))
    val = jnp.where(cnt >= target, 0, lowest).astype(I32)
    cnt_val = jnp.where(cnt >= target, cnt, cnt_lowest)

    def step(i, state):
        val, cnt_val = state
        cand = val + jnp.left_shift(jnp.int32(1), n_bits - 2 - i)
        cnt = count_ge(to_cand(cand))
        return jnp.where(cnt >= target, cand, val), jnp.where(cnt >= target, cnt, cnt_val)

    return lax.fori_loop(0, n_bits - 1, step, (val, cnt_val))


def _demote_ties(sc_ref, n_steps, step_rows, thr, k):
    need = float(k) - _count_rows(sc_ref, n_steps, step_rows, _GT, thr)
    s_i = lax.broadcasted_iota(I32, (LANES, LANES), 0)
    j_i = lax.broadcasted_iota(I32, (LANES, LANES), 1)
    earlier = jnp.where(j_i < s_i, 1.0, 0.0).astype(BF16)

    def body(c, seen):
        s0 = pl.multiple_of(c * LANES, LANES)
        blk = sc_ref[pl.ds(s0, LANES), :]
        eq = blk == thr
        eq_f = jnp.where(eq, 1.0, 0.0)
        rank = seen + _dot(earlier, eq_f.astype(BF16))
        sc_ref[pl.ds(s0, LANES), :] = jnp.where(eq & (rank >= need), -jnp.inf, blk)
        return seen + jnp.sum(eq_f, axis=0, keepdims=True)

    lax.fori_loop(0, n_steps * (step_rows // LANES), body, jnp.zeros((1, T), F32))


def _top_k_threshold(sc_ref, n_steps, step_rows, k, n_valid):
    kf = float(k)
    key, cnt_key = _bisect(lambda c: _count_rows(sc_ref, n_steps, step_rows, _GE, c), kf, INT_MIN, 32,
                           n_valid, to_cand=_key_to_float)
    thr = _key_to_float(jnp.maximum(key, KEY_ABOVE_NEG_INF))

    @pl.when(jnp.max(cnt_key) > kf)
    def _():
        _demote_ties(sc_ref, n_steps, step_rows, thr, k)

    return thr


def _split_keys(sc):
    bits = lax.bitcast_convert_type(sc, I32)
    key = bits ^ ((bits >> 31) & 0x7FFFFFFF)
    return (key >> 16).astype(I16), ((key & 0xFFFF) - 2 ** 15).astype(I16)


def _top_k_threshold_split(sc_ref, hi_ref, lo_ref, n_steps, k, n_valid):
    kf = float(k)
    rows = float(T) * n_steps.astype(F32)
    low16 = -2 ** 15
    high, cnt_high = _bisect(lambda c: _count_rows(hi_ref, n_steps, T, _GE, c), kf, low16, 16, rows)
    above = _count_rows(hi_ref, n_steps, T, _GT, high)
    high_b = jnp.broadcast_to(high, (T, T)).astype(I16)

    def keep_bucket(c, carry):
        rs = pl.ds(pl.multiple_of(c * T, T), T)
        lo_ref[rs, :] = jnp.where(hi_ref[rs, :] == high_b, lo_ref[rs, :], jnp.int16(low16))
        return carry

    lax.fori_loop(0, n_steps, keep_bucket, 0)
    low, cnt_low = _bisect(lambda c: _count_rows(lo_ref, n_steps, T, _GE, c), kf - above, low16, 16,
                           cnt_high - above)
    key = jnp.left_shift(high, 16) + (low + 2 ** 15)
    thr = _key_to_float(jnp.maximum(key, KEY_ABOVE_NEG_INF))

    @pl.when(jnp.max(jnp.where(n_valid > kf, above + cnt_low, 0.0)) > kf)
    def _():
        _demote_ties(sc_ref, n_steps, T, thr, k)

    return thr


def _softmax_init(m_scr, acc_scr):
    m_scr[...] = jnp.full(m_scr.shape, NEG, F32)
    acc_scr[...] = jnp.zeros(acc_scr.shape, F32)


def _softmax_out(acc_scr, h):
    acc = acc_scr[h]
    return acc[:HEAD_DIM, :] / jnp.maximum(acc[HEAD_DIM:HEAD_DIM + 1, :], 1e-30)


def _exact_softmax(s, valid):
    s = jnp.where(valid, s, NEG)
    m = jnp.max(s, axis=0, keepdims=True)
    e = jnp.where(valid, jnp.exp(s - m), 0.0)
    return e / jnp.maximum(jnp.sum(e, axis=0, keepdims=True), 1e-30)


def _store_heads(o_ref, heads_t):
    for pair in range(2):
        o_ref[0, :, pair * LANES:(pair + 1) * LANES] = jnp.concatenate(
            [t[:HEAD_DIM, :] for t in heads_t[2 * pair:2 * pair + 2]], axis=0).T


def _head_values(v_ref, j):
    return [v_ref[0, j, h * HV_ROWS:(h + 1) * HV_ROWS, :] for h in range(N_HEADS)]


def _attn_scratch(rows):
    return [pltpu.VMEM((N_HEADS, 1, T), F32), pltpu.VMEM((N_HEADS, rows, T), F32)]


def _attn_params():
    return pltpu.CompilerParams(dimension_semantics=("arbitrary", "arbitrary"),
                                vmem_limit_bytes=VMEM_LIMIT)


def _keys_spec(S, units, unit):
    return pl.BlockSpec((1, S, units * LANES), lambda b, i: (b, 0, unit // units))


def _q_spec(rows):
    return pl.BlockSpec((1, 1, rows, T), lambda b, i: (b, i, 0, 0))


def _v_spec(S, rows):
    return pl.BlockSpec((1, S // T, rows, T), lambda b, i: (b, 0, 0, 0))


_OUT_SPEC = pl.BlockSpec((1, T, BR_WIDTH), lambda b, i: (b, i, 0))


def _fox_kernel(q_ref, k_ref, v_ref, o_ref, m_scr, acc_scr):
    qb = pl.program_id(1)
    _softmax_init(m_scr, acc_scr)

    def tiles(js, masked=False):
        scores = [[_dot(k_ref[0, pl.ds(pl.multiple_of(j * T, T), T), h * LANES:(h + 1) * LANES],
                        q_ref[0, 0, h * LANES:(h + 1) * LANES, :]) for j in js]
                  for h in range(N_HEADS)]
        valids = [None] * len(js)
        if masked:
            key, qry = _key_query_index(qb, js[0])
            valids = [key <= qry]
        _softmax_update(scores, valids, [_head_values(v_ref, j) for j in js], m_scr, acc_scr)

    _for_tile_groups(qb, tiles, group=4)
    tiles([qb], masked=True)
    _store_heads(o_ref, [_softmax_out(acc_scr, h) for h in range(N_HEADS)])


def _fox_call(pr):
    B, S, _ = pr['keys'].shape
    return pl.pallas_call(
        _fox_kernel,
        out_shape=jax.ShapeDtypeStruct((B, S, BR_WIDTH), F32),
        grid=(B, S // T),
        in_specs=[_q_spec(N_HEADS * LANES), _keys_spec(S, 4, A_FK), _v_spec(S, N_HEADS * HV_ROWS)],
        out_specs=_OUT_SPEC,
        scratch_shapes=_attn_scratch(HV_ROWS),
        compiler_params=_attn_params(),
        name="fox",
    )(pr['fq_t'], pr['keys'], pr['fv_t'])


def _sb_kernel(q_ref, k_ref, v_ref, o_ref, r_scr, acc_scr):
    qb = pl.program_id(1)
    qp = _pair_pads(q_ref[0, 0])
    r_scr[...] = jnp.zeros(r_scr.shape, F32)
    acc_scr[...] = jnp.zeros(acc_scr.shape, F32)
    s_i = lax.broadcasted_iota(I32, (T, T), 0)
    j_i = lax.broadcasted_iota(I32, (T, T), 1)
    later_keys = jnp.where(j_i > s_i, 1.0, 0.0).astype(BF16)

    def tile(j, masked):
        s0 = pl.multiple_of(j * T, T)
        strict = None
        if masked:
            key, qry = _key_query_index(qb, j)
            strict = key < qry
        r_prev = [r_scr[h] for h in range(N_HEADS)]
        r_new, updates = [], []
        for h in range(N_HEADS):
            pair = h // 2
            z = _dot(k_ref[0, pl.ds(s0, T), pair * LANES:(pair + 1) * LANES], qp[h])
            log_take = _log_sigmoid(z)
            log_keep = log_take - z
            if masked:
                log_keep = jnp.where(strict, log_keep, 0.0)
            within = _dot3x(later_keys, log_keep)
            a = jnp.exp(log_take + (r_prev[h] + within))
            if masked:
                a = jnp.where(strict, a, 0.0)
            updates.append(_dot(v_ref[0, j, h * HEAD_DIM:(h + 1) * HEAD_DIM, :], a.astype(BF16)))
            r_new.append(r_prev[h] + within[0:1, :] + log_keep[0:1, :])
        for h in range(N_HEADS):
            acc_scr[h] = acc_scr[h] + updates[h]
            r_scr[h] = r_new[h]

    def live():
        return jnp.max(r_scr[...]) > SB_CUTOFF

    def step(state):
        i, _ = state
        tile(qb - 1 - i, False)
        return i + 1, live()

    tile(qb, True)
    lax.while_loop(lambda st: (st[0] < qb) & st[1], step, (jnp.int32(0), live()))
    _store_heads(o_ref, [acc_scr[h] for h in range(N_HEADS)])


def _sb_call(pr):
    B, S, _ = pr['keys'].shape
    return pl.pallas_call(
        _sb_kernel,
        out_shape=jax.ShapeDtypeStruct((B, S, BR_WIDTH), F32),
        grid=(B, S // T),
        in_specs=[_q_spec(BR_WIDTH), _keys_spec(S, 2, A_SK), _v_spec(S, BR_WIDTH)],
        out_specs=_OUT_SPEC,
        scratch_shapes=_attn_scratch(HEAD_DIM),
        compiler_params=_attn_params(),
        name="stick_breaking",
    )(pr['sq_t'], pr['keys'], pr['sv_t'])


def _dsa_kernel(q_ref, k_ref, v_ref, qi_ref, ki_ref, sm_ref, o_ref, sc_scr, hi_scr, lo_scr, m_scr, acc_scr,
                *, k_sel):
    qb = pl.program_id(1)
    row = lax.broadcasted_iota(I32, (LANES, T), 0)
    qi = qi_ref[0, 0]
    zero = jnp.zeros_like(qi)
    qi_pad = [jnp.where((row >= IDX_DIM * h) & (row < IDX_DIM * (h + 1)), qi, zero)
              for h in range(IDX_HEADS)]
    wi = [sm_ref[0, 0, SM_WI + h:SM_WI + h + 1, :] for h in range(IDX_HEADS)]

    def score_tile(j, masked):
        s0 = pl.multiple_of(j * T, T)
        kt = ki_ref[0, pl.ds(s0, T), :]
        sc = jnp.zeros((T, T), F32)
        for h in range(IDX_HEADS):
            sc = sc + wi[h] * jnp.maximum(_dot(kt, qi_pad[h]), 0.0)
        sc = jnp.where(sc == 0.0, 0.0, sc)
        if masked:
            key, qry = _key_query_index(qb, j)
            sc = jnp.where(key <= qry, sc, -jnp.inf)
        sc_scr[pl.ds(s0, T), :] = sc
        hi_scr[pl.ds(s0, T), :], lo_scr[pl.ds(s0, T), :] = _split_keys(sc)

    lax.fori_loop(0, qb, lambda j, c: (score_tile(j, False), c)[1], 0)
    score_tile(qb, True)

    n_valid = (qb * T + 1 + lax.broadcasted_iota(I32, (1, T), 1)).astype(F32)
    thr = _top_k_threshold_split(sc_scr, hi_scr, lo_scr, qb + 1, k_sel, n_valid)

    qp = _pair_pads(q_ref[0, 0])
    _softmax_init(m_scr, acc_scr)

    def attn_tiles(js):
        scores = [[_dot(k_ref[0, pl.ds(pl.multiple_of(j * T, T), T),
                              (h // 2) * LANES:(h // 2 + 1) * LANES], qp[h]) for j in js]
                  for h in range(N_HEADS)]
        valids = [sc_scr[pl.ds(pl.multiple_of(j * T, T), T), :] >= thr for j in js]
        _softmax_update(scores, valids, [_head_values(v_ref, j) for j in js], m_scr, acc_scr)

    _for_tile_groups(qb + 1, attn_tiles, group=4)
    _store_heads(o_ref, [_softmax_out(acc_scr, h) for h in range(N_HEADS)])


def _dsa_call(pr):
    B, S, _ = pr['keys'].shape
    return pl.pallas_call(
        functools.partial(_dsa_kernel, k_sel=min(DSA_TOPK_MAX, S // 4)),
        out_shape=jax.ShapeDtypeStruct((B, S, BR_WIDTH), F32),
        grid=(B, S // T),
        in_specs=[_q_spec(BR_WIDTH), _keys_spec(S, 2, A_KA), _v_spec(S, N_HEADS * HV_ROWS),
                  _q_spec(LANES), _keys_spec(S, 1, A_KI), _q_spec(LANES)],
        out_specs=_OUT_SPEC,
        scratch_shapes=[pltpu.VMEM((S, T), F32), pltpu.VMEM((S, T), I16), pltpu.VMEM((S, T), I16)]
                       + _attn_scratch(HV_ROWS),
        compiler_params=_attn_params(),
        name="dsa",
    )(pr['dq_t'], pr['keys'], pr['va_t'], pr['qi_t'], pr['keys'], pr['sm_t'])


def _compress_kernel(xk_ref, xv_ref, pek_ref, pev_ref, w1k_ref, w1v_ref, w2k_ref, w2v_ref,
                     o_ref, ot_ref, *, nc):
    def hidden(x_ref, pe_ref, w1_ref):
        x = x_ref[0]
        first = _dot((x + pe_ref[0:1, :]).astype(BF16), w1_ref[0])
        second = _dot((x + pe_ref[1:2, :]).astype(BF16), w1_ref[1])
        pre = first + pltpu.roll(second, nc - 1, 0)
        return (pre * _sigmoid(pre)).astype(BF16)

    kvc = (_dot(hidden(xk_ref, pek_ref, w1k_ref), w2k_ref[...])
           + _dot(hidden(xv_ref, pev_ref, w1v_ref), w2v_ref[...]))
    o_ref[0] = kvc.astype(BF16)
    ot_ref[0] = kvc.T.astype(BF16)


def _compress_call(xk, xv, pek, pev, w1k, w1v, w2k, w2v):
    B, nc, width = xk.shape
    full = lambda shape: pl.BlockSpec(shape, lambda b: (0,) * len(shape))
    tok = pl.BlockSpec((1, nc, width), lambda b: (b, 0, 0))
    return pl.pallas_call(
        functools.partial(_compress_kernel, nc=nc),
        out_shape=(jax.ShapeDtypeStruct((B, nc, LANES), BF16), jax.ShapeDtypeStruct((B, LANES, nc), BF16)),
        grid=(B,),
        in_specs=[tok, tok, full((2, width)), full((2, width)), full((2, width, CMP_HIDDEN)),
                  full((2, width, CMP_HIDDEN)), full((CMP_HIDDEN, LANES)), full((CMP_HIDDEN, LANES))],
        out_specs=(pl.BlockSpec((1, nc, LANES), lambda b: (b, 0, 0)),
                   pl.BlockSpec((1, LANES, nc), lambda b: (b, 0, 0))),
        compiler_params=pltpu.CompilerParams(dimension_semantics=("arbitrary",),
                                             vmem_limit_bytes=VMEM_LIMIT),
        name="nsa_compress",
    )(xk, xv, pek, pev, w1k, w1v, w2k, w2v)


def _nsa_kernel(q_ref, kvc_ref, kvct_ref, ksw_ref, vsw_ref, sm_ref, o_ref,
                imp_scr, sel_scr, m_scr, acc_scr, *, n_blk, n_sel, nbp):
    qb = pl.program_id(1)
    ncp = kvc_ref.shape[1]
    q_t = q_ref[0, 0]
    zero = jnp.zeros((HEAD_DIM, T), q_t.dtype)
    heads = [q_t[h * HEAD_DIM:(h + 1) * HEAD_DIM, :] for h in range(N_HEADS)]
    q_lo = [jnp.concatenate([q, zero], axis=0) for q in heads]
    q_hi = [jnp.concatenate([zero, q], axis=0) for q in heads]
    gate = lambda br, h: sm_ref[0, 0, SM_GATE + br * N_HEADS + h:SM_GATE + br * N_HEADS + h + 1, :]
    t_row = qb * T + lax.broadcasted_iota(I32, (1, T), 1)

    kvc = kvc_ref[0]
    cmp_end = lax.broadcasted_iota(I32, (ncp, T), 0) * CMP_STRIDE + (CMP_LEN - 1)
    cmp_ok = cmp_end <= t_row
    o_cmp, p_sum = [], jnp.zeros((ncp, T), F32)
    for h in range(N_HEADS):
        pc = _exact_softmax(_dot(kvc, q_lo[h]), cmp_ok)
        o_cmp.append(_dot(kvct_ref[0], pc.astype(BF16))[HEAD_DIM:, :])
        p_sum = p_sum + pc
    per_blk = SLC_BLOCK // CMP_STRIDE
    b_i = lax.broadcasted_iota(I32, (nbp, ncp), 0)
    c_i = lax.broadcasted_iota(I32, (nbp, ncp), 1)
    group = jnp.where(_div_pow2(c_i, per_blk) == b_i, 1.0, 0.0).astype(BF16)
    imp = _dot3x(group, p_sum)
    blk = lax.broadcasted_iota(I32, (nbp, T), 0)
    cur = _div_pow2(t_row, SLC_BLOCK)
    forced = (blk == 0) | (blk == cur) | (blk == cur - 1)
    imp = jnp.where(forced, FORCE_SCORE, jnp.where(blk <= cur, imp, -1.0))
    imp = jnp.where(blk < n_blk, imp, -2.0)
    imp_scr[...] = imp
    thr = _top_k_threshold(imp_scr, 1, nbp, n_sel, jnp.full((1, T), float(nbp), F32))
    sel_scr[...] = jnp.where(imp_scr[...] >= thr, 1.0, 0.0)

    _softmax_init(m_scr, acc_scr)
    blk_per_tile = T // SLC_BLOCK

    def slc_tiles(js, masked=False):
        valids, scores = [], [[] for _ in range(N_HEADS)]
        for j in js:
            valid = jnp.concatenate(
                [jnp.broadcast_to(sel_scr[pl.ds(j * blk_per_tile + b, 1), :], (SLC_BLOCK, T))
                 for b in range(blk_per_tile)], axis=0) > 0.5
            if masked:
                key, qry = _key_query_index(qb, j)
                valid = valid & (key <= qry)
            valids.append(valid)
            kt = ksw_ref[0, pl.ds(pl.multiple_of(j * T, T), T), :]
            for h in range(N_HEADS):
                scores[h].append(_dot(kt, q_lo[h]))
        _softmax_update(scores, valids, [[vsw_ref[0, j, :HV_ROWS, :]] * N_HEADS for j in js],
                        m_scr, acc_scr)

    _for_tile_groups(qb, slc_tiles, group=4)
    slc_tiles([qb], masked=True)

    n_win = WINDOW // T + 1
    j0 = jnp.maximum(qb - (n_win - 1), 0)
    outs = []
    for h in range(N_HEADS):
        logits, oks = [], []
        for i in range(n_win):
            s0 = pl.multiple_of((j0 + i) * T, T)
            key, qry = _key_query_index(qb, j0 + i)
            dist = qry - key
            oks.append((dist >= 0) & (dist < WINDOW))
            logits.append(jnp.where(oks[i], _dot(ksw_ref[0, pl.ds(s0, T), :], q_hi[h]), NEG))
        m = functools.reduce(jnp.maximum, [jnp.max(s, axis=0, keepdims=True) for s in logits])
        es = [jnp.where(ok, jnp.exp(s - m), 0.0) for s, ok in zip(logits, oks)]
        den = functools.reduce(jnp.add, [jnp.sum(e, axis=0, keepdims=True) for e in es])
        o_win = functools.reduce(jnp.add, [
            _dot(vsw_ref[0, j0 + i, HV_ROWS:, :], (e / jnp.maximum(den, 1e-30)).astype(BF16))
            for i, e in enumerate(es)])
        outs.append(gate(0, h) * o_cmp[h] + gate(1, h) * _softmax_out(acc_scr, h)
                    + gate(2, h) * o_win)
    _store_heads(o_ref, outs)


def _nsa_call(pr, kvc, kvc_t):
    B, S, _ = pr['keys'].shape
    n_blk = S // SLC_BLOCK
    nbp = max(LANES, n_blk)
    ncp = kvc.shape[1]
    return pl.pallas_call(
        functools.partial(_nsa_kernel, n_blk=n_blk, n_sel=min(SLC_TOPN, n_blk), nbp=nbp),
        out_shape=jax.ShapeDtypeStruct((B, S, BR_WIDTH), F32),
        grid=(B, S // T),
        in_specs=[_q_spec(BR_WIDTH),
                  pl.BlockSpec((1, ncp, LANES), lambda b, i: (b, 0, 0)),
                  pl.BlockSpec((1, LANES, ncp), lambda b, i: (b, 0, 0)),
                  _keys_spec(S, 1, A_KSW), _v_spec(S, VSW_ROWS), _q_spec(LANES)],
        out_specs=_OUT_SPEC,
        scratch_shapes=[pltpu.VMEM((nbp, T), F32), pltpu.VMEM((nbp, T), F32)] + _attn_scratch(HV_ROWS),
        compiler_params=_attn_params(),
        name="nsa",
    )(pr['nq_t'], kvc, kvc_t, pr['keys'], pr['vsw_t'], pr['sm_t'])


def _mem_kv_kernel(mem_ref, g_ref, w_ref, o_ref):
    o_ref[0] = _dot(_rmsnorm(mem_ref[0], g_ref[...]).astype(BF16), w_ref[...]).astype(BF16)


def _mem_kv_call(mem, g, w):
    B, M, D = mem.shape
    return pl.pallas_call(
        _mem_kv_kernel,
        out_shape=jax.ShapeDtypeStruct((B, M, 2 * BR_WIDTH), BF16),
        grid=(B,),
        in_specs=[pl.BlockSpec((1, M, D), lambda b: (b, 0, 0)), pl.BlockSpec((1, D), lambda b: (0, 0)),
                  pl.BlockSpec((D, 2 * BR_WIDTH), lambda b: (0, 0))],
        out_specs=pl.BlockSpec((1, M, 2 * BR_WIDTH), lambda b: (b, 0, 0)),
        name="mem_kv",
    )(mem, g, w)


def _out_kernel(x_ref, g_ref, ya_ref, yb_ref, yc_ref, yd_ref, mq_ref, mkv_ref, wz_ref, wm_ref,
                wb_ref, wo_ref, fg_ref, o_ref, h_scr, *, final):
    x = x_ref[0]
    h_scr[...] = _rmsnorm(x, g_ref[...]).astype(BF16)
    q = mq_ref[0]
    lane = lax.broadcasted_iota(I32, (T, LANES), 1)
    lo = lane < HEAD_DIM
    zero = jnp.zeros((T, LANES), q.dtype)
    pairs = []
    for pair in range(2):
        chunk = q[:, pair * LANES:(pair + 1) * LANES]
        mk = mkv_ref[0, :, pair * LANES:(pair + 1) * LANES]
        mv = mkv_ref[0, :, BR_WIDTH + pair * LANES:BR_WIDTH + (pair + 1) * LANES]
        outs = []
        for qh in (jnp.where(lo, chunk, zero), jnp.where(lo, zero, chunk)):
            s = _dot_nt(qh, mk)
            e = jnp.exp(s - jnp.max(s, axis=1, keepdims=True))
            outs.append(_dot((e / jnp.sum(e, axis=1, keepdims=True)).astype(BF16), mv))
        pairs.append(jnp.where(lo, outs[0], outs[1]))
    y_e = jnp.concatenate(pairs, axis=1)
    merged = jnp.zeros(x.shape, F32)
    for n, y in enumerate((ya_ref[0], yb_ref[0], yc_ref[0], yd_ref[0], y_e)):
        z = _dot(h_scr[...], wz_ref[n])
        ys = (y * (z * _sigmoid(z))).astype(BF16)
        merged = merged + _sigmoid(_dot(h_scr[...], wm_ref[n])) * _dot(ys, wb_ref[n])
    out = x + _dot(merged.astype(BF16), wo_ref[...])
    o_ref[0] = _rmsnorm(out, fg_ref[...]) if final else out


def _out_call(x, g, ys, keys, mkv, wz, wm, wb, wo, fg, final):
    B, S, D = x.shape
    M = mkv.shape[1]
    full = lambda shape: pl.BlockSpec(shape, lambda b, i: (0,) * len(shape))
    tile = lambda n: pl.BlockSpec((1, T, n), lambda b, i: (b, i, 0))
    return pl.pallas_call(
        functools.partial(_out_kernel, final=final),
        out_shape=jax.ShapeDtypeStruct((B, S, D), F32),
        grid=(B, S // T),
        in_specs=[tile(D), full((1, D))] + [tile(BR_WIDTH)] * 4
                 + [pl.BlockSpec((1, T, BR_WIDTH), lambda b, i: (b, i, A_MQ // 2)),
                    pl.BlockSpec((1, M, 2 * BR_WIDTH), lambda b, i: (b, 0, 0)),
                    full((N_BRANCH, D, BR_WIDTH)), full((N_BRANCH, D, D)), full((N_BRANCH, BR_WIDTH, D)),
                    full((D, D)), full((1, D))],
        out_specs=tile(D),
        scratch_shapes=[pltpu.VMEM((T, D), BF16)],
        compiler_params=pltpu.CompilerParams(dimension_semantics=("arbitrary", "arbitrary"),
                                             vmem_limit_bytes=VMEM_LIMIT),
        name="merge_out",
    )(x, g, *ys, keys, mkv, wz, wm, wb, wo, fg)


def kernel(x, mem, positions, norm_g, w_in, kv_norm, w_uk, w_uv, fox_bias, nsa_pe_k, nsa_pe_v,
           nsa_wc1_k, nsa_wc2_k, nsa_wc1_v, nsa_wc2_v, mem_norm, w_mem_kv, w_branch, w_out, final_norm):
    B, S, D = x.shape
    depth = norm_g.shape[0]
    assert S % T == 0 and S >= WINDOW + T and WINDOW % T == 0 and D == 1024
    cos, sin = _rope_tables(positions)
    w1, wz, wm = _proj_weight(w_in)
    fb = jnp.zeros((depth, 1, LANES), F32).at[:, 0, SM_CUM:SM_CUM + N_HEADS].set(fox_bias)
    nc = S // CMP_STRIDE
    tok_w = CMP_STRIDE * NSA_KV_DIM
    pad_k = jnp.zeros((depth, CMP_HIDDEN, LANES), F32).at[..., :NSA_KV_DIM].set(nsa_wc2_k).astype(BF16)
    pad_v = jnp.zeros((depth, CMP_HIDDEN, LANES), F32).at[..., NSA_KV_DIM:].set(nsa_wc2_v).astype(BF16)
    for l in range(depth):
        pr = _proj_call(x, norm_g[l][None], w1[l], kv_norm[l][None], w_uk[l].astype(BF16),
                        w_uv[l].astype(BF16), fb[l], cos, sin)
        kvc, kvc_t = _compress_call(
            pr['kvtok'][:, :, :NSA_KV_DIM].reshape(B, nc, tok_w),
            pr['kvtok'][:, :, NSA_KV_DIM:].reshape(B, nc, tok_w),
            nsa_pe_k[l].reshape(2, tok_w), nsa_pe_v[l].reshape(2, tok_w),
            nsa_wc1_k[l].reshape(2, tok_w, CMP_HIDDEN).astype(BF16),
            nsa_wc1_v[l].reshape(2, tok_w, CMP_HIDDEN).astype(BF16), pad_k[l], pad_v[l])
        ys = (_dsa_call(pr), _fox_call(pr), _sb_call(pr), _nsa_call(pr, kvc, kvc_t))
        mkv = _mem_kv_call(mem, mem_norm[l][None], w_mem_kv[l].astype(BF16))
        x = _out_call(x, norm_g[l][None], ys, pr['keys'], mkv, wz[l], wm[l], w_branch[l].astype(BF16),
                      w_out[l].astype(BF16), final_norm[None], l == depth - 1)
    return x
```

```python
import functools

import jax
import jax.numpy as jnp
from jax import lax
from jax.experimental import pallas as pl
from jax.experimental.pallas import tpu as pltpu

F32 = jnp.float32
BF16 = jnp.bfloat16
I32 = jnp.int32
I16 = jnp.int16

LANES = 128
SUBLANES = 8
N_HEADS = 4
HEAD_DIM = 64
BR_WIDTH = N_HEADS * HEAD_DIM
N_BRANCH = 5
ROPE_THETA = 10000.0
EPS = 1e-6
DSA_TOPK_MAX = 256
IDX_HEADS = 4
IDX_DIM = 32
KV_LATENT = 128
NSA_KV_DIM = 64
CMP_LEN = 32
CMP_STRIDE = 16
CMP_HIDDEN = 128
SLC_BLOCK = 64
SLC_TOPN = 16
WINDOW = 512
FORCE_SCORE = 1e4
Q_SCALE = HEAD_DIM ** -0.5
LOG2E = 1.4426950408889634
Q_SCALE2 = Q_SCALE * LOG2E
IDX_SCALE = (IDX_DIM ** -0.5) * (IDX_HEADS ** -0.5)
NEG = -1e30
SB_CUTOFF = -104.0
VMEM_LIMIT = 56 * 1024 * 1024
T = 256
ONES_ROWS = 16
HV_ROWS = HEAD_DIM + ONES_ROWS
VSW_ROWS = HV_ROWS + NSA_KV_DIM

IN_LAYOUT = (
    ('dsa_q', BR_WIDTH), ('dsa_ckv', KV_LATENT), ('idx_q', IDX_HEADS * IDX_DIM), ('idx_k', IDX_DIM),
    ('idx_w', IDX_HEADS), ('dsa_z', BR_WIDTH),
    ('fox_q', BR_WIDTH), ('fox_k', BR_WIDTH), ('fox_v', BR_WIDTH), ('fox_f', N_HEADS), ('fox_z', BR_WIDTH),
    ('sb_q', BR_WIDTH), ('sb_k', BR_WIDTH), ('sb_v', BR_WIDTH), ('sb_z', BR_WIDTH),
    ('nsa_q', BR_WIDTH), ('nsa_kc', NSA_KV_DIM), ('nsa_vc', NSA_KV_DIM), ('nsa_ks', NSA_KV_DIM),
    ('nsa_vs', NSA_KV_DIM), ('nsa_kw', NSA_KV_DIM), ('nsa_vw', NSA_KV_DIM), ('nsa_g', 3 * N_HEADS),
    ('nsa_z', BR_WIDTH), ('mem_q', BR_WIDTH), ('mem_z', BR_WIDTH), ('merge', N_BRANCH * 1024),
)

A_FK, A_KA, A_SK, A_KI, A_KSW, A_MQ = 0, 4, 6, 8, 9, 10
A_COLS = 12 * LANES
SM_WI, SM_CUM, SM_GATE = 0, 4, 8
FOX_CK, FOX_ONE, FOX_END = HEAD_DIM, HEAD_DIM + 3, HEAD_DIM + 6


def _dot(a, b):
    return jnp.dot(a, b, preferred_element_type=F32)


def _dot_nt(a, b):
    return lax.dot_general(a, b, (((1,), (1,)), ((), ())), preferred_element_type=F32)


def _split3(x):
    hi = x.astype(BF16)
    r1 = x - hi.astype(F32)
    mid = r1.astype(BF16)
    lo = (r1 - mid.astype(F32)).astype(BF16)
    return hi, mid, lo


def _dot3x(m01, x):
    hi, mid, lo = _split3(x)
    return _dot(m01, hi) + _dot(m01, mid) + _dot(m01, lo)


def _div_pow2(x, n):
    assert n & (n - 1) == 0
    return x >> (n.bit_length() - 1)


def _sigmoid(x):
    return 1.0 / (1.0 + jnp.exp(-x))


def _log_sigmoid(x):
    return jnp.minimum(x, 0.0) - jnp.log(1.0 + jnp.exp(-jnp.abs(x)))


def _rmsnorm(x, g):
    return x * lax.rsqrt(jnp.mean(x * x, axis=-1, keepdims=True) + EPS) * g


def _rope128(x, cos, sin_signed, half):
    lane = lax.broadcasted_iota(I32, x.shape, 1)
    first = (lane & (2 * half - 1)) < half
    partner = jnp.where(first, pltpu.roll(x, LANES - half, 1), pltpu.roll(x, half, 1))
    return x * cos + partner * sin_signed


def _rope_table_kernel(pos_ref, inv_ref, sgn_ref, cos_ref, sin_ref):
    ang = pos_ref[0] * inv_ref[...]
    cos_ref[0] = jnp.cos(ang)
    sin_ref[0] = jnp.sin(ang) * sgn_ref[...]


def _rope_tables(positions):
    B, S = positions.shape
    lane = jnp.arange(LANES)

    def inv_row(dh):
        inv = ROPE_THETA ** (-jnp.arange(0, dh, 2, dtype=F32) / dh)
        return inv[(lane % dh) % (dh // 2)]

    def sgn_row(dh):
        return jnp.where((lane % dh) < dh // 2, -1.0, 1.0).astype(F32)

    inv = jnp.concatenate([inv_row(HEAD_DIM), inv_row(IDX_DIM)])[None, :]
    sgn = jnp.concatenate([sgn_row(HEAD_DIM), sgn_row(IDX_DIM)])[None, :]
    pos = positions.astype(F32)[..., None]
    row = pl.BlockSpec((1, 2 * LANES), lambda b, i: (0, 0))
    tab = pl.BlockSpec((1, T, 2 * LANES), lambda b, i: (b, i, 0))
    return pl.pallas_call(
        _rope_table_kernel,
        out_shape=(jax.ShapeDtypeStruct((B, S, 2 * LANES), F32),) * 2,
        grid=(B, S // T),
        in_specs=[pl.BlockSpec((1, T, 1), lambda b, i: (b, i, 0)), row, row],
        out_specs=(tab, tab),
        name="rope_tables",
    )(pos, inv, sgn)


W_DQ, W_CKV, W_QI, W_KI, W_SM, W_FQ, W_SQ, W_NQ, W_MQ, W_KVC, W_KSW, W_VSW = (
    0, 256, 384, 512, 640, 768, 1536, 2304, 2560, 2816, 2944, 3072)
W_COLS = 3200

PROJ_OUTS = (
    ('keys', A_COLS, BF16), ('kvtok', LANES, F32),
    ('dq_t', BR_WIDTH, BF16), ('fq_t', N_HEADS * LANES, BF16), ('sq_t', BR_WIDTH, BF16),
    ('nq_t', BR_WIDTH, BF16), ('qi_t', LANES, BF16), ('sm_t', LANES, F32),
    ('va_t', N_HEADS * HV_ROWS, BF16), ('fv_t', N_HEADS * HV_ROWS, BF16), ('sv_t', BR_WIDTH, BF16),
    ('vsw_t', VSW_ROWS, BF16),
)
ROW_MAJOR = ('keys', 'kvtok')


def _proj_kernel(x_ref, g_ref, w_ref, kvn_ref, wuk_ref, wuv_ref, fb_ref, cos_ref, sin_ref,
                 *refs):
    out = dict(zip([n for n, _, _ in PROJ_OUTS], refs))
    h_scr, carry_scr = refs[len(PROJ_OUTS):]
    keys = out['keys']
    h_scr[...] = _rmsnorm(x_ref[0], g_ref[...]).astype(BF16)
    c64, s64 = cos_ref[0, :, :LANES], sin_ref[0, :, :LANES]
    c32, s32 = cos_ref[0, :, LANES:], sin_ref[0, :, LANES:]
    lane = lax.broadcasted_iota(I32, (T, LANES), 1)
    ones_rows = jnp.ones((ONES_ROWS, T), BF16)

    def proj(c0, n):
        return _dot(h_scr[...], w_ref[:, c0:c0 + n])

    def put(unit, val):
        keys[0, :, unit * LANES:unit * LANES + val.shape[1]] = val.astype(BF16)

    def rope64(v, j):
        return _rope128(v[:, j * LANES:(j + 1) * LANES], c64, s64, HEAD_DIM // 2)

    def put_t(ref, row0, chunk):
        ref[0, 0, row0:row0 + LANES, :] = chunk.T.astype(ref.dtype)

    def put_values_t(ref, v):
        for j in range(2):
            pair_t = v[:, j * LANES:(j + 1) * LANES].T.astype(BF16)
            for odd in range(2):
                r0 = (2 * j + odd) * HV_ROWS
                ref[0, 0, r0:r0 + HEAD_DIM, :] = pair_t[odd * HEAD_DIM:(odd + 1) * HEAD_DIM, :]
                ref[0, 0, r0 + HEAD_DIM:r0 + HV_ROWS, :] = ones_rows

    p = proj(W_DQ, BR_WIDTH)
    for j in range(2):
        put_t(out['dq_t'], j * LANES, rope64(p, j) * Q_SCALE2)
    c_kv = _rmsnorm(proj(W_CKV, KV_LATENT), kvn_ref[...]).astype(BF16)
    k_a = _dot(c_kv, wuk_ref[...])
    for j in range(2):
        put(A_KA + j, rope64(k_a, j))
    put_values_t(out['va_t'], _dot(c_kv, wuv_ref[...]))
    put_t(out['qi_t'], 0, _rope128(proj(W_QI, LANES), c32, s32, IDX_DIM // 2))
    put(A_KI, _rope128(proj(W_KI, LANES), c32, s32, IDX_DIM // 2))
    p = proj(W_SM, LANES)
    log_f = _log_sigmoid(p + fb_ref[...])
    r_i = lax.broadcasted_iota(I32, (T, T), 0)
    c_i = lax.broadcasted_iota(I32, (T, T), 1)
    tri = jnp.where(r_i >= c_i, 1.0, 0.0).astype(BF16)

    @pl.when(pl.program_id(1) == 0)
    def _():
        carry_scr[...] = jnp.zeros_like(carry_scr)

    cum = _dot3x(tri, log_f) + carry_scr[...]
    carry_scr[...] = cum[T - 1:T, :]
    small_t = jnp.where(lane < SM_CUM, p * IDX_SCALE,
                        jnp.where(lane < SM_GATE, cum, _sigmoid(p))).T
    out['sm_t'][0, 0] = small_t
    q_f = proj(W_FQ, BR_WIDTH) * Q_SCALE2
    cum2, cum2_t = cum * LOG2E, small_t * LOG2E
    k_f = proj(W_FQ + 256, BR_WIDTH)
    row = lax.broadcasted_iota(I32, (HEAD_DIM, T), 0)
    for h in range(N_HEADS):
        pair, odd = h // 2, h % 2
        k_chunk = k_f[:, pair * LANES:(pair + 1) * LANES]
        k_h = pltpu.roll(k_chunk, HEAD_DIM, 1) if odd else k_chunk
        ck = [c.astype(F32) for c in _split3(cum2[:, SM_CUM + h:SM_CUM + h + 1])]
        k_aug = jnp.where(lane < HEAD_DIM, k_h,
                          jnp.where(lane == FOX_CK, ck[0],
                                    jnp.where(lane == FOX_CK + 1, ck[1],
                                              jnp.where(lane == FOX_CK + 2, ck[2],
                                                        jnp.where(lane < FOX_END, 1.0, 0.0)))))
        put(A_FK + h, k_aug)
        q_ht = q_f[:, pair * LANES:(pair + 1) * LANES].T[odd * HEAD_DIM:(odd + 1) * HEAD_DIM, :]
        cq = [c.astype(F32) for c in _split3(cum2_t[SM_CUM + h:SM_CUM + h + 1, :])]
        aug = jnp.where(row < 3, -1.0,
                        jnp.where(row == 3, cq[0], jnp.where(row == 4, cq[1],
                                                             jnp.where(row == 5, cq[2], 0.0))))
        out['fq_t'][0, 0, h * LANES:(h + 1) * LANES, :] = jnp.concatenate(
            [q_ht, aug], axis=0).astype(BF16)
    put_values_t(out['fv_t'], proj(W_FQ + 512, BR_WIDTH))
    p = proj(W_SQ, BR_WIDTH) * Q_SCALE
    for j in range(2):
        put_t(out['sq_t'], j * LANES, p[:, j * LANES:(j + 1) * LANES])
    put(A_SK, proj(W_SQ + 256, BR_WIDTH))
    p = proj(W_SQ + 512, BR_WIDTH)
    for j in range(2):
        put_t(out['sv_t'], j * LANES, p[:, j * LANES:(j + 1) * LANES])
    put(A_MQ, proj(W_MQ, BR_WIDTH) * Q_SCALE)
    p = proj(W_NQ, BR_WIDTH)
    for j in range(2):
        put_t(out['nq_t'], j * LANES, rope64(p, j) * Q_SCALE2)
    p = proj(W_KVC, LANES)
    out['kvtok'][0] = jnp.where(lane < NSA_KV_DIM, rope64(p, 0), p)
    put(A_KSW, rope64(proj(W_KSW, LANES), 0))
    vsw_t = proj(W_VSW, LANES).T.astype(BF16)
    out['vsw_t'][0, 0, :NSA_KV_DIM, :] = vsw_t[:NSA_KV_DIM, :]
    out['vsw_t'][0, 0, NSA_KV_DIM:HV_ROWS, :] = ones_rows
    out['vsw_t'][0, 0, HV_ROWS:, :] = vsw_t[NSA_KV_DIM:, :]


def _proj_weight(w_in):
    off, o = {}, 0
    for name, n in IN_LAYOUT:
        off[name] = (o, n)
        o += n

    def col(name):
        s, n = off[name]
        return w_in[..., s:s + n]

    zeros = lambda n: jnp.zeros(w_in.shape[:-1] + (n,), w_in.dtype)
    groups = [col('dsa_q'), col('dsa_ckv'), col('idx_q'), col('idx_k'), col('idx_k'), col('idx_k'),
              col('idx_k'), col('idx_w'), col('fox_f'), col('nsa_g'), zeros(LANES - 20),
              col('fox_q'), col('fox_k'), col('fox_v'), col('sb_q'), col('sb_k'), col('sb_v'),
              col('nsa_q'), col('mem_q'), col('nsa_kc'), col('nsa_vc'), col('nsa_ks'), col('nsa_kw'),
              col('nsa_vs'), col('nsa_vw')]
    w1 = jnp.concatenate(groups, axis=-1).astype(BF16)
    wz = jnp.stack([col(n) for n in ('dsa_z', 'fox_z', 'sb_z', 'nsa_z', 'mem_z')], axis=-3).astype(BF16)
    s, n = off['merge']
    wm = w_in[..., s:s + n].reshape(w_in.shape[:-1] + (N_BRANCH, n // N_BRANCH))
    wm = jnp.moveaxis(wm, -2, -3).astype(BF16)
    return w1, wz, wm


def _proj_call(x, g, w1, kvn, wuk, wuv, fb, cos, sin):
    B, S, D = x.shape
    full = lambda shape: pl.BlockSpec(shape, lambda b, i: (0,) * len(shape))
    tile = lambda n: pl.BlockSpec((1, T, n), lambda b, i: (b, i, 0))
    tile_t = lambda n: pl.BlockSpec((1, 1, n, T), lambda b, i: (b, i, 0, 0))
    shapes, specs = [], []
    for name, n, dt in PROJ_OUTS:
        if name in ROW_MAJOR:
            shapes.append(jax.ShapeDtypeStruct((B, S, n), dt))
            specs.append(tile(n))
        else:
            shapes.append(jax.ShapeDtypeStruct((B, S // T, n, T), dt))
            specs.append(tile_t(n))
    outs = pl.pallas_call(
        _proj_kernel,
        out_shape=tuple(shapes),
        grid=(B, S // T),
        in_specs=[tile(D), full((1, D)), full((D, W_COLS)), full((1, KV_LATENT)),
                  full((KV_LATENT, BR_WIDTH)), full((KV_LATENT, BR_WIDTH)), full((1, LANES)),
                  tile(2 * LANES), tile(2 * LANES)],
        out_specs=tuple(specs),
        scratch_shapes=[pltpu.VMEM((T, D), BF16), pltpu.VMEM((1, LANES), F32)],
        compiler_params=pltpu.CompilerParams(dimension_semantics=("arbitrary", "arbitrary"),
                                             vmem_limit_bytes=VMEM_LIMIT),
        name="proj",
    )(x, g, w1, kvn, wuk, wuv, fb, cos, sin)
    return dict(zip([n for n, _, _ in PROJ_OUTS], outs))


def _key_query_index(qb, j):
    key = j * T + lax.broadcasted_iota(I32, (T, T), 0)
    qry = qb * T + lax.broadcasted_iota(I32, (T, T), 1)
    return key, qry


def _pair_pads(q_t):
    row = lax.broadcasted_iota(I32, (LANES, T), 0)
    top = row < HEAD_DIM
    zero = jnp.zeros((LANES, T), q_t.dtype)
    out = []
    for pair in range(2):
        chunk = q_t[pair * LANES:(pair + 1) * LANES, :]
        out.append(jnp.where(top, chunk, zero))
        out.append(jnp.where(top, zero, chunk))
    return out


def _softmax_update(scores, valids, values, m_scr, acc_scr):
    m_prev = [m_scr[h] for h in range(N_HEADS)]
    m_new, updates = [], []
    for h in range(N_HEADS):
        tiles = [s if v is None else jnp.where(v, s, 2 * NEG) for s, v in zip(scores[h], valids)]
        m = functools.reduce(jnp.maximum, [m_prev[h]] + [jnp.max(s, axis=0, keepdims=True) for s in tiles])
        update = None
        for i, s in enumerate(tiles):
            u = _dot(values[i][h], jnp.exp2(s - m).astype(BF16))
            update = u if update is None else update + u
        m_new.append(m)
        updates.append(update)
    for h in range(N_HEADS):
        acc_scr[h] = jnp.exp2(m_prev[h] - m_new[h]) * acc_scr[h] + updates[h]
        m_scr[h] = m_new[h]


def _for_tile_groups(n, group_fn, group):
    def full(i, c):
        group_fn([group * i + g for g in range(group)])
        return c

    lax.fori_loop(0, n // group, full, 0)
    size = group // 2
    while size:
        start = (n // (2 * size)) * (2 * size)

        @pl.when((n & size) != 0)
        def _(start=start, size=size):
            group_fn([start + g for g in range(size)])

        size //= 2


INT_MIN = -2 ** 31
KEY_ABOVE_NEG_INF = 0x80800000 - 2 ** 32


def _key_to_float(c):
    bits = c ^ ((c >> 31) & 0x7FFFFFFF)
    return lax.bitcast_convert_type(bits, F32)


def _count_rows(ref, n_steps, step_rows, pred, cand):
    pack = SUBLANES * (4 // ref.dtype.itemsize)
    cand_b = jnp.broadcast_to(cand, (pack, T)).astype(ref.dtype)
    one, zero = jnp.ones((pack, T), ref.dtype), jnp.zeros((pack, T), ref.dtype)

    def body(c, accs):
        blk = ref[pl.ds(pl.multiple_of(c * step_rows, step_rows), step_rows), :]
        accs = list(accs)
        for r in range(step_rows // pack):
            hit = jnp.where(pred(blk[r * pack:(r + 1) * pack, :], cand_b), one, zero)
            accs[r % len(accs)] = accs[r % len(accs)] + hit
        return tuple(accs)

    accs = lax.fori_loop(0, n_steps, body, (zero,) * 4)
    total = functools.reduce(jnp.add, [a.astype(I32) if a.dtype != F32 else a for a in accs])
    return jnp.sum(total, axis=0, keepdims=True).astype(F32)


_GE = lambda a, b: a >= b
_GT = lambda a, b: a > b


def _bisect(count_ge, target, lowest, n_bits, cnt_lowest, to_cand=lambda c: c):
    cnt = count_ge(to_cand(jnp.zeros((1, T), I32)))
    val = jnp.where(cnt >= target, 0, lowest).astype(I32)
    cnt_val = jnp.where(cnt >= target, cnt, cnt_lowest)

    def step(i, state):
        val, cnt_val = state
        cand = val + jnp.left_shift(jnp.int32(1), n_bits - 2 - i)
        cnt = count_ge(to_cand(cand))
        return jnp.where(cnt >= target, cand, val), jnp.where(cnt >= target, cnt, cnt_val)

    return lax.fori_loop(0, n_bits - 1, step, (val, cnt_val))


def _demote_ties(sc_ref, n_steps, step_rows, thr, k):
    need = float(k) - _count_rows(sc_ref, n_steps, step_rows, _GT, thr)
    s_i = lax.broadcasted_iota(I32, (LANES, LANES), 0)
    j_i = lax.broadcasted_iota(I32, (LANES, LANES), 1)
    earlier = jnp.where(j_i < s_i, 1.0, 0.0).astype(BF16)

    def body(c, seen):
        s0 = pl.multiple_of(c * LANES, LANES)
        blk = sc_ref[pl.ds(s0, LANES), :]
        eq = blk == thr
        eq_f = jnp.where(eq, 1.0, 0.0)
        rank = seen + _dot(earlier, eq_f.astype(BF16))
        sc_ref[pl.ds(s0, LANES), :] = jnp.where(eq & (rank >= need), -jnp.inf, blk)
        return seen + jnp.sum(eq_f, axis=0, keepdims=True)

    lax.fori_loop(0, n_steps * (step_rows // LANES), body, jnp.zeros((1, T), F32))


def _top_k_threshold(sc_ref, n_steps, step_rows, k, n_valid):
    kf = float(k)
    key, cnt_key = _bisect(lambda c: _count_rows(sc_ref, n_steps, step_rows, _GE, c), kf, INT_MIN, 32,
                           n_valid, to_cand=_key_to_float)
    thr = _key_to_float(jnp.maximum(key, KEY_ABOVE_NEG_INF))

    @pl.when(jnp.max(cnt_key) > kf)
    def _():
        _demote_ties(sc_ref, n_steps, step_rows, thr, k)

    return thr


def _split_keys(sc):
    bits = lax.bitcast_convert_type(sc, I32)
    key = bits ^ ((bits >> 31) & 0x7FFFFFFF)
    return (key >> 16).astype(I16), ((key & 0xFFFF) - 2 ** 15).astype(I16)


def _top_k_threshold_split(sc_ref, hi_ref, lo_ref, n_steps, k, n_valid):
    kf = float(k)
    rows = float(T) * n_steps.astype(F32)
    low16 = -2 ** 15
    high, cnt_high = _bisect(lambda c: _count_rows(hi_ref, n_steps, T, _GE, c), kf, low16, 16, rows)
    above = _count_rows(hi_ref, n_steps, T, _GT, high)
    high_b = jnp.broadcast_to(high, (T, T)).astype(I16)

    def keep_bucket(c, carry):
        rs = pl.ds(pl.multiple_of(c * T, T), T)
        lo_ref[rs, :] = jnp.where(hi_ref[rs, :] == high_b, lo_ref[rs, :], jnp.int16(low16))
        return carry

    lax.fori_loop(0, n_steps, keep_bucket, 0)
    low, cnt_low = _bisect(lambda c: _count_rows(lo_ref, n_steps, T, _GE, c), kf - above, low16, 16,
                           cnt_high - above)
    key = jnp.left_shift(high, 16) + (low + 2 ** 15)
    thr = _key_to_float(jnp.maximum(key, KEY_ABOVE_NEG_INF))

    @pl.when(jnp.max(jnp.where(n_valid > kf, above + cnt_low, 0.0)) > kf)
    def _():
        _demote_ties(sc_ref, n_steps, T, thr, k)

    return thr


def _softmax_init(m_scr, acc_scr):
    m_scr[...] = jnp.full(m_scr.shape, NEG, F32)
    acc_scr[...] = jnp.zeros(acc_scr.shape, F32)


def _softmax_out(acc_scr, h):
    acc = acc_scr[h]
    return acc[:HEAD_DIM, :] / jnp.maximum(acc[HEAD_DIM:HEAD_DIM + 1, :], 1e-30)


def _exact_softmax(s, valid):
    s = jnp.where(valid, s, NEG)
    m = jnp.max(s, axis=0, keepdims=True)
    e = jnp.where(valid, jnp.exp2(s - m), 0.0)
    return e / jnp.maximum(jnp.sum(e, axis=0, keepdims=True), 1e-30)


def _store_heads(o_ref, heads_t):
    for pair in range(2):
        o_ref[0, :, pair * LANES:(pair + 1) * LANES] = jnp.concatenate(
            [t[:HEAD_DIM, :] for t in heads_t[2 * pair:2 * pair + 2]], axis=0).T


def _head_values(v_ref, j):
    return [v_ref[0, j, h * HV_ROWS:(h + 1) * HV_ROWS, :] for h in range(N_HEADS)]


def _attn_scratch(rows):
    return [pltpu.VMEM((N_HEADS, 1, T), F32), pltpu.VMEM((N_HEADS, rows, T), F32)]


def _attn_params():
    return pltpu.CompilerParams(dimension_semantics=("arbitrary", "arbitrary"),
                                vmem_limit_bytes=VMEM_LIMIT)


def _keys_spec(S, units, unit):
    return pl.BlockSpec((1, S, units * LANES), lambda b, i: (b, 0, unit // units))


def _q_spec(rows):
    return pl.BlockSpec((1, 1, rows, T), lambda b, i: (b, i, 0, 0))


def _v_spec(S, rows):
    return pl.BlockSpec((1, S // T, rows, T), lambda b, i: (b, 0, 0, 0))


_OUT_SPEC = pl.BlockSpec((1, T, BR_WIDTH), lambda b, i: (b, i, 0))


def _fox_kernel(q_ref, k_ref, v_ref, o_ref, m_scr, acc_scr):
    qb = pl.program_id(1)
    _softmax_init(m_scr, acc_scr)

    def tiles(js, masked=False):
        scores = [[_dot(k_ref[0, pl.ds(pl.multiple_of(j * T, T), T), h * LANES:(h + 1) * LANES],
                        q_ref[0, 0, h * LANES:(h + 1) * LANES, :]) for j in js]
                  for h in range(N_HEADS)]
        valids = [None] * len(js)
        if masked:
            key, qry = _key_query_index(qb, js[0])
            valids = [key <= qry]
        _softmax_update(scores, valids, [_head_values(v_ref, j) for j in js], m_scr, acc_scr)

    _for_tile_groups(qb, tiles, group=4)
    tiles([qb], masked=True)
    _store_heads(o_ref, [_softmax_out(acc_scr, h) for h in range(N_HEADS)])


def _fox_call(pr):
    B, S, _ = pr['keys'].shape
    return pl.pallas_call(
        _fox_kernel,
        out_shape=jax.ShapeDtypeStruct((B, S, BR_WIDTH), F32),
        grid=(B, S // T),
        in_specs=[_q_spec(N_HEADS * LANES), _keys_spec(S, 4, A_FK), _v_spec(S, N_HEADS * HV_ROWS)],
        out_specs=_OUT_SPEC,
        scratch_shapes=_attn_scratch(HV_ROWS),
        compiler_params=_attn_params(),
        name="fox",
    )(pr['fq_t'], pr['keys'], pr['fv_t'])


def _sb_kernel(q_ref, k_ref, v_ref, o_ref, r_scr, acc_scr):
    qb = pl.program_id(1)
    qp = _pair_pads(q_ref[0, 0])
    r_scr[...] = jnp.zeros(r_scr.shape, F32)
    acc_scr[...] = jnp.zeros(acc_scr.shape, F32)
    s_i = lax.broadcasted_iota(I32, (T, T), 0)
    j_i = lax.broadcasted_iota(I32, (T, T), 1)
    later_keys = jnp.where(j_i > s_i, 1.0, 0.0).astype(BF16)

    def tile(j, masked):
        s0 = pl.multiple_of(j * T, T)
        strict = None
        if masked:
            key, qry = _key_query_index(qb, j)
            strict = key < qry
        r_prev = [r_scr[h] for h in range(N_HEADS)]
        r_new, updates = [], []
        for h in range(N_HEADS):
            pair = h // 2
            z = _dot(k_ref[0, pl.ds(s0, T), pair * LANES:(pair + 1) * LANES], qp[h])
            log_take = _log_sigmoid(z)
            log_keep = log_take - z
            if masked:
                log_keep = jnp.where(strict, log_keep, 0.0)
            within = _dot3x(later_keys, log_keep)
            a = jnp.exp(log_take + (r_prev[h] + within))
            if masked:
                a = jnp.where(strict, a, 0.0)
            updates.append(_dot(v_ref[0, j, h * HEAD_DIM:(h + 1) * HEAD_DIM, :], a.astype(BF16)))
            r_new.append(r_prev[h] + within[0:1, :] + log_keep[0:1, :])
        for h in range(N_HEADS):
            acc_scr[h] = acc_scr[h] + updates[h]
            r_scr[h] = r_new[h]

    def live():
        return jnp.max(r_scr[...]) > SB_CUTOFF

    def step(state):
        i, _ = state
        tile(qb - 1 - i, False)
        return i + 1, live()

    tile(qb, True)
    lax.while_loop(lambda st: (st[0] < qb) & st[1], step, (jnp.int32(0), live()))
    _store_heads(o_ref, [acc_scr[h] for h in range(N_HEADS)])


def _sb_call(pr):
    B, S, _ = pr['keys'].shape
    return pl.pallas_call(
        _sb_kernel,
        out_shape=jax.ShapeDtypeStruct((B, S, BR_WIDTH), F32),
        grid=(B, S // T),
        in_specs=[_q_spec(BR_WIDTH), _keys_spec(S, 2, A_SK), _v_spec(S, BR_WIDTH)],
        out_specs=_OUT_SPEC,
        scratch_shapes=_attn_scratch(HEAD_DIM),
        compiler_params=_attn_params(),
        name="stick_breaking",
    )(pr['sq_t'], pr['keys'], pr['sv_t'])


def _dsa_kernel(q_ref, k_ref, v_ref, qi_ref, ki_ref, sm_ref, o_ref, sc_scr, hi_scr, lo_scr, m_scr, acc_scr,
                *, k_sel):
    qb = pl.program_id(1)
    row = lax.broadcasted_iota(I32, (LANES, T), 0)
    qi = qi_ref[0, 0]
    zero = jnp.zeros_like(qi)
    qi_pad = [jnp.where((row >= IDX_DIM * h) & (row < IDX_DIM * (h + 1)), qi, zero)
              for h in range(IDX_HEADS)]
    wi = [sm_ref[0, 0, SM_WI + h:SM_WI + h + 1, :] for h in range(IDX_HEADS)]

    def score_tile(j, masked):
        s0 = pl.multiple_of(j * T, T)
        kt = ki_ref[0, pl.ds(s0, T), :]
        sc = jnp.zeros((T, T), F32)
        for h in range(IDX_HEADS):
            sc = sc + wi[h] * jnp.maximum(_dot(kt, qi_pad[h]), 0.0)
        sc = jnp.where(sc == 0.0, 0.0, sc)
        if masked:
            key, qry = _key_query_index(qb, j)
            sc = jnp.where(key <= qry, sc, -jnp.inf)
        sc_scr[pl.ds(s0, T), :] = sc
        hi_scr[pl.ds(s0, T), :], lo_scr[pl.ds(s0, T), :] = _split_keys(sc)

    _for_tile_groups(qb, lambda js: [score_tile(j, False) for j in js], group=2)
    score_tile(qb, True)

    n_valid = (qb * T + 1 + lax.broadcasted_iota(I32, (1, T), 1)).astype(F32)
    thr = _top_k_threshold_split(sc_scr, hi_scr, lo_scr, qb + 1, k_sel, n_valid)

    qp = _pair_pads(q_ref[0, 0])
    _softmax_init(m_scr, acc_scr)

    def attn_tiles(js):
        scores = [[_dot(k_ref[0, pl.ds(pl.multiple_of(j * T, T), T),
                              (h // 2) * LANES:(h // 2 + 1) * LANES], qp[h]) for j in js]
                  for h in range(N_HEADS)]
        valids = [sc_scr[pl.ds(pl.multiple_of(j * T, T), T), :] >= thr for j in js]
        _softmax_update(scores, valids, [_head_values(v_ref, j) for j in js], m_scr, acc_scr)

    _for_tile_groups(qb + 1, attn_tiles, group=4)
    _store_heads(o_ref, [_softmax_out(acc_scr, h) for h in range(N_HEADS)])


def _dsa_call(pr):
    B, S, _ = pr['keys'].shape
    return pl.pallas_call(
        functools.partial(_dsa_kernel, k_sel=min(DSA_TOPK_MAX, S // 4)),
        out_shape=jax.ShapeDtypeStruct((B, S, BR_WIDTH), F32),
        grid=(B, S // T),
        in_specs=[_q_spec(BR_WIDTH), _keys_spec(S, 2, A_KA), _v_spec(S, N_HEADS * HV_ROWS),
                  _q_spec(LANES), _keys_spec(S, 1, A_KI), _q_spec(LANES)],
        out_specs=_OUT_SPEC,
        scratch_shapes=[pltpu.VMEM((S, T), F32), pltpu.VMEM((S, T), I16), pltpu.VMEM((S, T), I16)]
                       + _attn_scratch(HV_ROWS),
        compiler_params=_attn_params(),
        name="dsa",
    )(pr['dq_t'], pr['keys'], pr['va_t'], pr['qi_t'], pr['keys'], pr['sm_t'])


def _compress_kernel(xk_ref, xv_ref, pek_ref, pev_ref, w1k_ref, w1v_ref, w2k_ref, w2v_ref,
                     o_ref, ot_ref, *, nc):
    def hidden(x_ref, pe_ref, w1_ref):
        x = x_ref[0]
        first = _dot((x + pe_ref[0:1, :]).astype(BF16), w1_ref[0])
        second = _dot((x + pe_ref[1:2, :]).astype(BF16), w1_ref[1])
        pre = first + pltpu.roll(second, nc - 1, 0)
        return (pre * _sigmoid(pre)).astype(BF16)

    kvc = (_dot(hidden(xk_ref, pek_ref, w1k_ref), w2k_ref[...])
           + _dot(hidden(xv_ref, pev_ref, w1v_ref), w2v_ref[...]))
    o_ref[0] = kvc.astype(BF16)
    ot_ref[0] = kvc.T.astype(BF16)


def _compress_call(xk, xv, pek, pev, w1k, w1v, w2k, w2v):
    B, nc, width = xk.shape
    full = lambda shape: pl.BlockSpec(shape, lambda b: (0,) * len(shape))
    tok = pl.BlockSpec((1, nc, width), lambda b: (b, 0, 0))
    return pl.pallas_call(
        functools.partial(_compress_kernel, nc=nc),
        out_shape=(jax.ShapeDtypeStruct((B, nc, LANES), BF16), jax.ShapeDtypeStruct((B, LANES, nc), BF16)),
        grid=(B,),
        in_specs=[tok, tok, full((2, width)), full((2, width)), full((2, width, CMP_HIDDEN)),
                  full((2, width, CMP_HIDDEN)), full((CMP_HIDDEN, LANES)), full((CMP_HIDDEN, LANES))],
        out_specs=(pl.BlockSpec((1, nc, LANES), lambda b: (b, 0, 0)),
                   pl.BlockSpec((1, LANES, nc), lambda b: (b, 0, 0))),
        compiler_params=pltpu.CompilerParams(dimension_semantics=("arbitrary",),
                                             vmem_limit_bytes=VMEM_LIMIT),
        name="nsa_compress",
    )(xk, xv, pek, pev, w1k, w1v, w2k, w2v)


def _nsa_kernel(q_ref, kvc_ref, kvct_ref, ksw_ref, vsw_ref, sm_ref, o_ref,
                imp_scr, sel_scr, m_scr, acc_scr, *, n_blk, n_sel, nbp):
    qb = pl.program_id(1)
    ncp = kvc_ref.shape[1]
    q_t = q_ref[0, 0]
    zero = jnp.zeros((HEAD_DIM, T), q_t.dtype)
    heads = [q_t[h * HEAD_DIM:(h + 1) * HEAD_DIM, :] for h in range(N_HEADS)]
    q_lo = [jnp.concatenate([q, zero], axis=0) for q in heads]
    q_hi = [jnp.concatenate([zero, q], axis=0) for q in heads]
    gate = lambda br, h: sm_ref[0, 0, SM_GATE + br * N_HEADS + h:SM_GATE + br * N_HEADS + h + 1, :]
    t_row = qb * T + lax.broadcasted_iota(I32, (1, T), 1)

    kvc = kvc_ref[0]
    cmp_end = lax.broadcasted_iota(I32, (ncp, T), 0) * CMP_STRIDE + (CMP_LEN - 1)
    cmp_ok = cmp_end <= t_row
    o_cmp, p_sum = [], jnp.zeros((ncp, T), F32)
    for h in range(N_HEADS):
        pc = _exact_softmax(_dot(kvc, q_lo[h]), cmp_ok)
        o_cmp.append(_dot(kvct_ref[0], pc.astype(BF16))[HEAD_DIM:, :])
        p_sum = p_sum + pc
    per_blk = SLC_BLOCK // CMP_STRIDE
    b_i = lax.broadcasted_iota(I32, (nbp, ncp), 0)
    c_i = lax.broadcasted_iota(I32, (nbp, ncp), 1)
    group = jnp.where(_div_pow2(c_i, per_blk) == b_i, 1.0, 0.0).astype(BF16)
    imp = _dot3x(group, p_sum)
    blk = lax.broadcasted_iota(I32, (nbp, T), 0)
    cur = _div_pow2(t_row, SLC_BLOCK)
    forced = (blk == 0) | (blk == cur) | (blk == cur - 1)
    imp = jnp.where(forced, FORCE_SCORE, jnp.where(blk <= cur, imp, -1.0))
    imp = jnp.where(blk < n_blk, imp, -2.0)
    imp_scr[...] = imp
    thr = _top_k_threshold(imp_scr, 1, nbp, n_sel, jnp.full((1, T), float(nbp), F32))
    sel_scr[...] = jnp.where(imp_scr[...] >= thr, 1.0, 0.0)

    _softmax_init(m_scr, acc_scr)
    blk_per_tile = T // SLC_BLOCK

    def slc_tiles(js, masked=False):
        valids, scores = [], [[] for _ in range(N_HEADS)]
        for j in js:
            valid = jnp.concatenate(
                [jnp.broadcast_to(sel_scr[pl.ds(j * blk_per_tile + b, 1), :], (SLC_BLOCK, T))
                 for b in range(blk_per_tile)], axis=0) > 0.5
            if masked:
                key, qry = _key_query_index(qb, j)
                valid = valid & (key <= qry)
            valids.append(valid)
            kt = ksw_ref[0, pl.ds(pl.multiple_of(j * T, T), T), :]
            for h in range(N_HEADS):
                scores[h].append(_dot(kt, q_lo[h]))
        _softmax_update(scores, valids, [[vsw_ref[0, j, :HV_ROWS, :]] * N_HEADS for j in js],
                        m_scr, acc_scr)

    _for_tile_groups(qb, slc_tiles, group=4)
    slc_tiles([qb], masked=True)

    n_win = WINDOW // T + 1
    j0 = jnp.maximum(qb - (n_win - 1), 0)
    outs = []
    for h in range(N_HEADS):
        logits, oks = [], []
        for i in range(n_win):
            s0 = pl.multiple_of((j0 + i) * T, T)
            key, qry = _key_query_index(qb, j0 + i)
            dist = qry - key
            oks.append((dist >= 0) & (dist < WINDOW))
            logits.append(jnp.where(oks[i], _dot(ksw_ref[0, pl.ds(s0, T), :], q_hi[h]), NEG))
        m = functools.reduce(jnp.maximum, [jnp.max(s, axis=0, keepdims=True) for s in logits])
        es = [jnp.where(ok, jnp.exp2(s - m), 0.0) for s, ok in zip(logits, oks)]
        den = functools.reduce(jnp.add, [jnp.sum(e, axis=0, keepdims=True) for e in es])
        o_win = functools.reduce(jnp.add, [
            _dot(vsw_ref[0, j0 + i, HV_ROWS:, :], (e / jnp.maximum(den, 1e-30)).astype(BF16))
            for i, e in enumerate(es)])
        outs.append(gate(0, h) * o_cmp[h] + gate(1, h) * _softmax_out(acc_scr, h)
                    + gate(2, h) * o_win)
    _store_heads(o_ref, outs)


def _nsa_call(pr, kvc, kvc_t):
    B, S, _ = pr['keys'].shape
    n_blk = S // SLC_BLOCK
    nbp = max(LANES, n_blk)
    ncp = kvc.shape[1]
    return pl.pallas_call(
        functools.partial(_nsa_kernel, n_blk=n_blk, n_sel=min(SLC_TOPN, n_blk), nbp=nbp),
        out_shape=jax.ShapeDtypeStruct((B, S, BR_WIDTH), F32),
        grid=(B, S // T),
        in_specs=[_q_spec(BR_WIDTH),
                  pl.BlockSpec((1, ncp, LANES), lambda b, i: (b, 0, 0)),
                  pl.BlockSpec((1, LANES, ncp), lambda b, i: (b, 0, 0)),
                  _keys_spec(S, 1, A_KSW), _v_spec(S, VSW_ROWS), _q_spec(LANES)],
        out_specs=_OUT_SPEC,
        scratch_shapes=[pltpu.VMEM((nbp, T), F32), pltpu.VMEM((nbp, T), F32)] + _attn_scratch(HV_ROWS),
        compiler_params=_attn_params(),
        name="nsa",
    )(pr['nq_t'], kvc, kvc_t, pr['keys'], pr['vsw_t'], pr['sm_t'])


def _mem_kv_kernel(mem_ref, g_ref, w_ref, o_ref):
    o_ref[0] = _dot(_rmsnorm(mem_ref[0], g_ref[...]).astype(BF16), w_ref[...]).astype(BF16)


def _mem_kv_call(mem, g, w):
    B, M, D = mem.shape
    return pl.pallas_call(
        _mem_kv_kernel,
        out_shape=jax.ShapeDtypeStruct((B, M, 2 * BR_WIDTH), BF16),
        grid=(B,),
        in_specs=[pl.BlockSpec((1, M, D), lambda b: (b, 0, 0)), pl.BlockSpec((1, D), lambda b: (0, 0)),
                  pl.BlockSpec((D, 2 * BR_WIDTH), lambda b: (0, 0))],
        out_specs=pl.BlockSpec((1, M, 2 * BR_WIDTH), lambda b: (b, 0, 0)),
        name="mem_kv",
    )(mem, g, w)


def _out_kernel(x_ref, g_ref, ya_ref, yb_ref, yc_ref, yd_ref, mq_ref, mkv_ref, wz_ref, wm_ref,
                wb_ref, wo_ref, fg_ref, o_ref, h_scr, *, final):
    x = x_ref[0]
    h_scr[...] = _rmsnorm(x, g_ref[...]).astype(BF16)
    q = mq_ref[0]
    lane = lax.broadcasted_iota(I32, (T, LANES), 1)
    lo = lane < HEAD_DIM
    zero = jnp.zeros((T, LANES), q.dtype)
    pairs = []
    for pair in range(2):
        chunk = q[:, pair * LANES:(pair + 1) * LANES]
        mk = mkv_ref[0, :, pair * LANES:(pair + 1) * LANES]
        mv = mkv_ref[0, :, BR_WIDTH + pair * LANES:BR_WIDTH + (pair + 1) * LANES]
        outs = []
        for qh in (jnp.where(lo, chunk, zero), jnp.where(lo, zero, chunk)):
            s = _dot_nt(qh, mk)
            e = jnp.exp(s - jnp.max(s, axis=1, keepdims=True))
            outs.append(_dot((e / jnp.sum(e, axis=1, keepdims=True)).astype(BF16), mv))
        pairs.append(jnp.where(lo, outs[0], outs[1]))
    y_e = jnp.concatenate(pairs, axis=1)
    merged = jnp.zeros(x.shape, F32)
    for n, y in enumerate((ya_ref[0], yb_ref[0], yc_ref[0], yd_ref[0], y_e)):
        z = _dot(h_scr[...], wz_ref[n])
        ys = (y * (z * _sigmoid(z))).astype(BF16)
        merged = merged + _sigmoid(_dot(h_scr[...], wm_ref[n])) * _dot(ys, wb_ref[n])
    out = x + _dot(merged.astype(BF16), wo_ref[...])
    o_ref[0] = _rmsnorm(out, fg_ref[...]) if final else out


def _out_call(x, g, ys, keys, mkv, wz, wm, wb, wo, fg, final):
    B, S, D = x.shape
    M = mkv.shape[1]
    full = lambda shape: pl.BlockSpec(shape, lambda b, i: (0,) * len(shape))
    tile = lambda n: pl.BlockSpec((1, T, n), lambda b, i: (b, i, 0))
    return pl.pallas_call(
        functools.partial(_out_kernel, final=final),
        out_shape=jax.ShapeDtypeStruct((B, S, D), F32),
        grid=(B, S // T),
        in_specs=[tile(D), full((1, D))] + [tile(BR_WIDTH)] * 4
                 + [pl.BlockSpec((1, T, BR_WIDTH), lambda b, i: (b, i, A_MQ // 2)),
                    pl.BlockSpec((1, M, 2 * BR_WIDTH), lambda b, i: (b, 0, 0)),
                    full((N_BRANCH, D, BR_WIDTH)), full((N_BRANCH, D, D)), full((N_BRANCH, BR_WIDTH, D)),
                    full((D, D)), full((1, D))],
        out_specs=tile(D),
        scratch_shapes=[pltpu.VMEM((T, D), BF16)],
        compiler_params=pltpu.CompilerParams(dimension_semantics=("arbitrary", "arbitrary"),
                                             vmem_limit_bytes=VMEM_LIMIT),
        name="merge_out",
    )(x, g, *ys, keys, mkv, wz, wm, wb, wo, fg)


def kernel(x, mem, positions, norm_g, w_in, kv_norm, w_uk, w_uv, fox_bias, nsa_pe_k, nsa_pe_v,
           nsa_wc1_k, nsa_wc2_k, nsa_wc1_v, nsa_wc2_v, mem_norm, w_mem_kv, w_branch, w_out, final_norm):
    B, S, D = x.shape
    depth = norm_g.shape[0]
    assert S % T == 0 and S >= WINDOW + T and WINDOW % T == 0 and D == 1024
    cos, sin = _rope_tables(positions)
    w1, wz, wm = _proj_weight(w_in)
    fb = jnp.zeros((depth, 1, LANES), F32).at[:, 0, SM_CUM:SM_CUM + N_HEADS].set(fox_bias)
    nc = S // CMP_STRIDE
    tok_w = CMP_STRIDE * NSA_KV_DIM
    pad_k = jnp.zeros((depth, CMP_HIDDEN, LANES), F32).at[..., :NSA_KV_DIM].set(nsa_wc2_k).astype(BF16)
    pad_v = jnp.zeros((depth, CMP_HIDDEN, LANES), F32).at[..., NSA_KV_DIM:].set(nsa_wc2_v).astype(BF16)
    for l in range(depth):
        pr = _proj_call(x, norm_g[l][None], w1[l], kv_norm[l][None], w_uk[l].astype(BF16),
                        w_uv[l].astype(BF16), fb[l], cos, sin)
        kvc, kvc_t = _compress_call(
            pr['kvtok'][:, :, :NSA_KV_DIM].reshape(B, nc, tok_w),
            pr['kvtok'][:, :, NSA_KV_DIM:].reshape(B, nc, tok_w),
            nsa_pe_k[l].reshape(2, tok_w), nsa_pe_v[l].reshape(2, tok_w),
            nsa_wc1_k[l].reshape(2, tok_w, CMP_HIDDEN).astype(BF16),
            nsa_wc1_v[l].reshape(2, tok_w, CMP_HIDDEN).astype(BF16), pad_k[l], pad_v[l])
        ys = (_dsa_call(pr), _fox_call(pr), _sb_call(pr), _nsa_call(pr, kvc, kvc_t))
        mkv = _mem_kv_call(mem, mem_norm[l][None], w_mem_kv[l].astype(BF16))
        x = _out_call(x, norm_g[l][None], ys, pr['keys'], mkv, wz[l], wm[l], w_branch[l].astype(BF16),
                      w_out[l].astype(BF16), final_norm[None], l == depth - 1)
    return x
```

```python
import functools

import jax
import jax.numpy as jnp
from jax import lax
from jax.experimental import pallas as pl
from jax.experimental.pallas import tpu as pltpu

F32 = jnp.float32
BF16 = jnp.bfloat16
I32 = jnp.int32
I16 = jnp.int16

LANES = 128
SUBLANES = 8
N_HEADS = 4
HEAD_DIM = 64
BR_WIDTH = N_HEADS * HEAD_DIM
N_BRANCH = 5
ROPE_THETA = 10000.0
EPS = 1e-6
DSA_TOPK_MAX = 256
IDX_HEADS = 4
IDX_DIM = 32
KV_LATENT = 128
NSA_KV_DIM = 64
CMP_LEN = 32
CMP_STRIDE = 16
CMP_HIDDEN = 128
SLC_BLOCK = 64
SLC_TOPN = 16
WINDOW = 512
FORCE_SCORE = 1e4
Q_SCALE = HEAD_DIM ** -0.5
LOG2E = 1.4426950408889634
Q_SCALE2 = Q_SCALE * LOG2E
IDX_SCALE = (IDX_DIM ** -0.5) * (IDX_HEADS ** -0.5)
NEG = -1e30
SB_CUTOFF = -104.0
LAZY_LIMIT = 64.0
VMEM_LIMIT = 56 * 1024 * 1024
T = 256
ONES_ROWS = 16
HV_ROWS = HEAD_DIM + ONES_ROWS
VSW_ROWS = HV_ROWS + NSA_KV_DIM

IN_LAYOUT = (
    ('dsa_q', BR_WIDTH), ('dsa_ckv', KV_LATENT), ('idx_q', IDX_HEADS * IDX_DIM), ('idx_k', IDX_DIM),
    ('idx_w', IDX_HEADS), ('dsa_z', BR_WIDTH),
    ('fox_q', BR_WIDTH), ('fox_k', BR_WIDTH), ('fox_v', BR_WIDTH), ('fox_f', N_HEADS), ('fox_z', BR_WIDTH),
    ('sb_q', BR_WIDTH), ('sb_k', BR_WIDTH), ('sb_v', BR_WIDTH), ('sb_z', BR_WIDTH),
    ('nsa_q', BR_WIDTH), ('nsa_kc', NSA_KV_DIM), ('nsa_vc', NSA_KV_DIM), ('nsa_ks', NSA_KV_DIM),
    ('nsa_vs', NSA_KV_DIM), ('nsa_kw', NSA_KV_DIM), ('nsa_vw', NSA_KV_DIM), ('nsa_g', 3 * N_HEADS),
    ('nsa_z', BR_WIDTH), ('mem_q', BR_WIDTH), ('mem_z', BR_WIDTH), ('merge', N_BRANCH * 1024),
)

A_FK, A_KA, A_SK, A_KI, A_KSW, A_MQ = 0, 4, 6, 8, 9, 10
A_COLS = 12 * LANES
SM_WI, SM_CUM, SM_GATE = 0, 4, 8
FOX_CK, FOX_ONE, FOX_END = HEAD_DIM, HEAD_DIM + 3, HEAD_DIM + 6


def _dot(a, b):
    return jnp.dot(a, b, preferred_element_type=F32)


def _dot_nt(a, b):
    return lax.dot_general(a, b, (((1,), (1,)), ((), ())), preferred_element_type=F32)


def _split3(x):
    hi = x.astype(BF16)
    r1 = x - hi.astype(F32)
    mid = r1.astype(BF16)
    lo = (r1 - mid.astype(F32)).astype(BF16)
    return hi, mid, lo


def _dot3x(m01, x):
    hi, mid, lo = _split3(x)
    return _dot(m01, hi) + _dot(m01, mid) + _dot(m01, lo)


def _div_pow2(x, n):
    assert n & (n - 1) == 0
    return x >> (n.bit_length() - 1)


def _sigmoid(x):
    return 1.0 / (1.0 + jnp.exp(-x))


def _log_sigmoid(x):
    return jnp.minimum(x, 0.0) - jnp.log(1.0 + jnp.exp(-jnp.abs(x)))


def _rmsnorm(x, g):
    return x * lax.rsqrt(jnp.mean(x * x, axis=-1, keepdims=True) + EPS) * g


def _rope128(x, cos, sin_signed, half):
    lane = lax.broadcasted_iota(I32, x.shape, 1)
    first = (lane & (2 * half - 1)) < half
    partner = jnp.where(first, pltpu.roll(x, LANES - half, 1), pltpu.roll(x, half, 1))
    return x * cos + partner * sin_signed


def _rope_table_kernel(pos_ref, inv_ref, sgn_ref, cos_ref, sin_ref):
    ang = pos_ref[0] * inv_ref[...]
    cos_ref[0] = jnp.cos(ang)
    sin_ref[0] = jnp.sin(ang) * sgn_ref[...]


def _rope_tables(positions):
    B, S = positions.shape
    lane = jnp.arange(LANES)

    def inv_row(dh):
        inv = ROPE_THETA ** (-jnp.arange(0, dh, 2, dtype=F32) / dh)
        return inv[(lane % dh) % (dh // 2)]

    def sgn_row(dh):
        return jnp.where((lane % dh) < dh // 2, -1.0, 1.0).astype(F32)

    inv = jnp.concatenate([inv_row(HEAD_DIM), inv_row(IDX_DIM)])[None, :]
    sgn = jnp.concatenate([sgn_row(HEAD_DIM), sgn_row(IDX_DIM)])[None, :]
    pos = positions.astype(F32)[..., None]
    row = pl.BlockSpec((1, 2 * LANES), lambda b, i: (0, 0))
    tab = pl.BlockSpec((1, T, 2 * LANES), lambda b, i: (b, i, 0))
    return pl.pallas_call(
        _rope_table_kernel,
        out_shape=(jax.ShapeDtypeStruct((B, S, 2 * LANES), F32),) * 2,
        grid=(B, S // T),
        in_specs=[pl.BlockSpec((1, T, 1), lambda b, i: (b, i, 0)), row, row],
        out_specs=(tab, tab),
        name="rope_tables",
    )(pos, inv, sgn)


W_DQ, W_CKV, W_QI, W_KI, W_SM, W_FQ, W_SQ, W_NQ, W_MQ, W_KVC, W_KSW, W_VSW = (
    0, 256, 384, 512, 640, 768, 1536, 2304, 2560, 2816, 2944, 3072)
W_COLS = 3200

PROJ_OUTS = (
    ('keys', A_COLS, BF16), ('kvtok', LANES, F32),
    ('dq_t', BR_WIDTH, BF16), ('fq_t', N_HEADS * LANES, BF16), ('sq_t', BR_WIDTH, BF16),
    ('nq_t', BR_WIDTH, BF16), ('qi_t', LANES, BF16), ('sm_t', LANES, F32),
    ('va_t', N_HEADS * HV_ROWS, BF16), ('fv_t', N_HEADS * HV_ROWS, BF16), ('sv_t', BR_WIDTH, BF16),
    ('vsw_t', VSW_ROWS, BF16),
)
ROW_MAJOR = ('keys', 'kvtok')


def _proj_kernel(x_ref, g_ref, w_ref, kvn_ref, wuk_ref, wuv_ref, fb_ref, cos_ref, sin_ref,
                 *refs):
    out = dict(zip([n for n, _, _ in PROJ_OUTS], refs))
    h_scr, carry_scr = refs[len(PROJ_OUTS):]
    keys = out['keys']
    h_scr[...] = _rmsnorm(x_ref[0], g_ref[...]).astype(BF16)
    c64, s64 = cos_ref[0, :, :LANES], sin_ref[0, :, :LANES]
    c32, s32 = cos_ref[0, :, LANES:], sin_ref[0, :, LANES:]
    lane = lax.broadcasted_iota(I32, (T, LANES), 1)
    ones_rows = jnp.ones((ONES_ROWS, T), BF16)

    def proj(c0, n):
        return _dot(h_scr[...], w_ref[:, c0:c0 + n])

    def put(unit, val):
        keys[0, :, unit * LANES:unit * LANES + val.shape[1]] = val.astype(BF16)

    def rope64(v, j):
        return _rope128(v[:, j * LANES:(j + 1) * LANES], c64, s64, HEAD_DIM // 2)

    def put_t(ref, row0, chunk):
        ref[0, 0, row0:row0 + LANES, :] = chunk.T.astype(ref.dtype)

    def put_values_t(ref, v):
        for j in range(2):
            pair_t = v[:, j * LANES:(j + 1) * LANES].T.astype(BF16)
            for odd in range(2):
                r0 = (2 * j + odd) * HV_ROWS
                ref[0, 0, r0:r0 + HEAD_DIM, :] = pair_t[odd * HEAD_DIM:(odd + 1) * HEAD_DIM, :]
                ref[0, 0, r0 + HEAD_DIM:r0 + HV_ROWS, :] = ones_rows

    p = proj(W_DQ, BR_WIDTH)
    for j in range(2):
        put_t(out['dq_t'], j * LANES, rope64(p, j) * Q_SCALE2)
    c_kv = _rmsnorm(proj(W_CKV, KV_LATENT), kvn_ref[...]).astype(BF16)
    k_a = _dot(c_kv, wuk_ref[...])
    for j in range(2):
        put(A_KA + j, rope64(k_a, j))
    put_values_t(out['va_t'], _dot(c_kv, wuv_ref[...]))
    put_t(out['qi_t'], 0, _rope128(proj(W_QI, LANES), c32, s32, IDX_DIM // 2))
    put(A_KI, _rope128(proj(W_KI, LANES), c32, s32, IDX_DIM // 2))
    p = proj(W_SM, LANES)
    log_f = _log_sigmoid(p + fb_ref[...])
    r_i = lax.broadcasted_iota(I32, (T, T), 0)
    c_i = lax.broadcasted_iota(I32, (T, T), 1)
    tri = jnp.where(r_i >= c_i, 1.0, 0.0).astype(BF16)

    @pl.when(pl.program_id(1) == 0)
    def _():
        carry_scr[...] = jnp.zeros_like(carry_scr)

    cum = _dot3x(tri, log_f) + carry_scr[...]
    carry_scr[...] = cum[T - 1:T, :]
    small_t = jnp.where(lane < SM_CUM, p * IDX_SCALE,
                        jnp.where(lane < SM_GATE, cum, _sigmoid(p))).T
    out['sm_t'][0, 0] = small_t
    q_f = proj(W_FQ, BR_WIDTH) * Q_SCALE2
    cum2, cum2_t = cum * LOG2E, small_t * LOG2E
    k_f = proj(W_FQ + 256, BR_WIDTH)
    row = lax.broadcasted_iota(I32, (HEAD_DIM, T), 0)
    for h in range(N_HEADS):
        pair, odd = h // 2, h % 2
        k_chunk = k_f[:, pair * LANES:(pair + 1) * LANES]
        k_h = pltpu.roll(k_chunk, HEAD_DIM, 1) if odd else k_chunk
        ck = [c.astype(F32) for c in _split3(cum2[:, SM_CUM + h:SM_CUM + h + 1])]
        k_aug = jnp.where(lane < HEAD_DIM, k_h,
                          jnp.where(lane == FOX_CK, ck[0],
                                    jnp.where(lane == FOX_CK + 1, ck[1],
                                              jnp.where(lane == FOX_CK + 2, ck[2],
                                                        jnp.where(lane < FOX_END, 1.0, 0.0)))))
        put(A_FK + h, k_aug)
        q_ht = q_f[:, pair * LANES:(pair + 1) * LANES].T[odd * HEAD_DIM:(odd + 1) * HEAD_DIM, :]
        cq = [c.astype(F32) for c in _split3(cum2_t[SM_CUM + h:SM_CUM + h + 1, :])]
        aug = jnp.where(row < 3, -1.0,
                        jnp.where(row == 3, cq[0], jnp.where(row == 4, cq[1],
                                                             jnp.where(row == 5, cq[2], 0.0))))
        out['fq_t'][0, 0, h * LANES:(h + 1) * LANES, :] = jnp.concatenate(
            [q_ht, aug], axis=0).astype(BF16)
    put_values_t(out['fv_t'], proj(W_FQ + 512, BR_WIDTH))
    p = proj(W_SQ, BR_WIDTH) * Q_SCALE
    for j in range(2):
        put_t(out['sq_t'], j * LANES, p[:, j * LANES:(j + 1) * LANES])
    put(A_SK, proj(W_SQ + 256, BR_WIDTH))
    p = proj(W_SQ + 512, BR_WIDTH)
    for j in range(2):
        put_t(out['sv_t'], j * LANES, p[:, j * LANES:(j + 1) * LANES])
    put(A_MQ, proj(W_MQ, BR_WIDTH) * Q_SCALE)
    p = proj(W_NQ, BR_WIDTH)
    for j in range(2):
        put_t(out['nq_t'], j * LANES, rope64(p, j) * Q_SCALE2)
    p = proj(W_KVC, LANES)
    out['kvtok'][0] = jnp.where(lane < NSA_KV_DIM, rope64(p, 0), p)
    put(A_KSW, rope64(proj(W_KSW, LANES), 0))
    vsw_t = proj(W_VSW, LANES).T.astype(BF16)
    out['vsw_t'][0, 0, :NSA_KV_DIM, :] = vsw_t[:NSA_KV_DIM, :]
    out['vsw_t'][0, 0, NSA_KV_DIM:HV_ROWS, :] = ones_rows
    out['vsw_t'][0, 0, HV_ROWS:, :] = vsw_t[NSA_KV_DIM:, :]


def _proj_weight(w_in):
    off, o = {}, 0
    for name, n in IN_LAYOUT:
        off[name] = (o, n)
        o += n

    def col(name):
        s, n = off[name]
        return w_in[..., s:s + n]

    zeros = lambda n: jnp.zeros(w_in.shape[:-1] + (n,), w_in.dtype)
    groups = [col('dsa_q'), col('dsa_ckv'), col('idx_q'), col('idx_k'), col('idx_k'), col('idx_k'),
              col('idx_k'), col('idx_w'), col('fox_f'), col('nsa_g'), zeros(LANES - 20),
              col('fox_q'), col('fox_k'), col('fox_v'), col('sb_q'), col('sb_k'), col('sb_v'),
              col('nsa_q'), col('mem_q'), col('nsa_kc'), col('nsa_vc'), col('nsa_ks'), col('nsa_kw'),
              col('nsa_vs'), col('nsa_vw')]
    w1 = jnp.concatenate(groups, axis=-1).astype(BF16)
    wz = jnp.stack([col(n) for n in ('dsa_z', 'fox_z', 'sb_z', 'nsa_z', 'mem_z')], axis=-3).astype(BF16)
    s, n = off['merge']
    wm = w_in[..., s:s + n].reshape(w_in.shape[:-1] + (N_BRANCH, n // N_BRANCH))
    wm = jnp.moveaxis(wm, -2, -3).astype(BF16)
    return w1, wz, wm


def _proj_call(x, g, w1, kvn, wuk, wuv, fb, cos, sin):
    B, S, D = x.shape
    full = lambda shape: pl.BlockSpec(shape, lambda b, i: (0,) * len(shape))
    tile = lambda n: pl.BlockSpec((1, T, n), lambda b, i: (b, i, 0))
    tile_t = lambda n: pl.BlockSpec((1, 1, n, T), lambda b, i: (b, i, 0, 0))
    shapes, specs = [], []
    for name, n, dt in PROJ_OUTS:
        if name in ROW_MAJOR:
            shapes.append(jax.ShapeDtypeStruct((B, S, n), dt))
            specs.append(tile(n))
        else:
            shapes.append(jax.ShapeDtypeStruct((B, S // T, n, T), dt))
            specs.append(tile_t(n))
    outs = pl.pallas_call(
        _proj_kernel,
        out_shape=tuple(shapes),
        grid=(B, S // T),
        in_specs=[tile(D), full((1, D)), full((D, W_COLS)), full((1, KV_LATENT)),
                  full((KV_LATENT, BR_WIDTH)), full((KV_LATENT, BR_WIDTH)), full((1, LANES)),
                  tile(2 * LANES), tile(2 * LANES)],
        out_specs=tuple(specs),
        scratch_shapes=[pltpu.VMEM((T, D), BF16), pltpu.VMEM((1, LANES), F32)],
        compiler_params=pltpu.CompilerParams(dimension_semantics=("arbitrary", "arbitrary"),
                                             vmem_limit_bytes=VMEM_LIMIT),
        name="proj",
    )(x, g, w1, kvn, wuk, wuv, fb, cos, sin)
    return dict(zip([n for n, _, _ in PROJ_OUTS], outs))


def _key_query_index(qb, j):
    key = j * T + lax.broadcasted_iota(I32, (T, T), 0)
    qry = qb * T + lax.broadcasted_iota(I32, (T, T), 1)
    return key, qry


def _pair_pads(q_t):
    row = lax.broadcasted_iota(I32, (LANES, T), 0)
    top = row < HEAD_DIM
    zero = jnp.zeros((LANES, T), q_t.dtype)
    out = []
    for pair in range(2):
        chunk = q_t[pair * LANES:(pair + 1) * LANES, :]
        out.append(jnp.where(top, chunk, zero))
        out.append(jnp.where(top, zero, chunk))
    return out


def _masked_tiles(scores_h, valids):
    return [s if v is None else jnp.where(v, s, 2 * NEG) for s, v in zip(scores_h, valids)]


def _weighted_values(tiles, values, h, m):
    update = None
    for i, s in enumerate(tiles):
        u = _dot(values[i][h], jnp.exp2(s - m).astype(BF16))
        update = u if update is None else update + u
    return update


def _softmax_update(inputs_fn, m_scr, acc_scr, lazy=False):
    if not lazy:
        _softmax_update_exact(*inputs_fn(), m_scr, acc_scr)
        return
    scores, valids, values = inputs_fn()
    m_ref = [m_scr[h] for h in range(N_HEADS)]
    tops, updates = [], []
    for h in range(N_HEADS):
        tiles = _masked_tiles(scores[h], valids)
        tops.append(functools.reduce(jnp.maximum, [jnp.max(s, axis=0, keepdims=True) for s in tiles]))
        updates.append(_weighted_values(tiles, values, h, m_ref[h]))
    overshoot = functools.reduce(jnp.maximum, [jnp.max(tops[h] - m_ref[h]) for h in range(N_HEADS)])
    safe = overshoot <= LAZY_LIMIT

    @pl.when(safe)
    def _():
        for h in range(N_HEADS):
            m_new = jnp.maximum(m_ref[h], tops[h])
            acc_scr[h] = jnp.exp2(m_ref[h] - m_new) * (acc_scr[h] + updates[h])
            m_scr[h] = m_new

    @pl.when(jnp.logical_not(safe))
    def _():
        _softmax_update_exact(*inputs_fn(), m_scr, acc_scr)


def _softmax_update_exact(scores, valids, values, m_scr, acc_scr):
    m_prev = [m_scr[h] for h in range(N_HEADS)]
    m_new, updates = [], []
    for h in range(N_HEADS):
        tiles = _masked_tiles(scores[h], valids)
        m = functools.reduce(jnp.maximum, [m_prev[h]] + [jnp.max(s, axis=0, keepdims=True) for s in tiles])
        m_new.append(m)
        updates.append(_weighted_values(tiles, values, h, m))
    for h in range(N_HEADS):
        acc_scr[h] = jnp.exp2(m_prev[h] - m_new[h]) * acc_scr[h] + updates[h]
        m_scr[h] = m_new[h]


def _for_tile_groups(n, group_fn, group, first_fn=None):
    def full(i, c):
        group_fn([group * i + g for g in range(group)])
        return c

    if first_fn is not None:
        @pl.when(n >= group)
        def _():
            first_fn(list(range(group)))

    lax.fori_loop(0 if first_fn is None else 1, n // group, full, 0)
    size = group // 2
    while size:
        start = (n // (2 * size)) * (2 * size)

        @pl.when((n & size) != 0)
        def _(start=start, size=size):
            group_fn([start + g for g in range(size)])

        size //= 2


INT_MIN = -2 ** 31
KEY_ABOVE_NEG_INF = 0x80800000 - 2 ** 32


def _key_to_float(c):
    bits = c ^ ((c >> 31) & 0x7FFFFFFF)
    return lax.bitcast_convert_type(bits, F32)


def _count_rows(ref, n_steps, step_rows, pred, cand):
    pack = SUBLANES * (4 // ref.dtype.itemsize)
    cand_b = jnp.broadcast_to(cand, (pack, T)).astype(ref.dtype)
    one, zero = jnp.ones((pack, T), ref.dtype), jnp.zeros((pack, T), ref.dtype)

    def body(c, accs):
        blk = ref[pl.ds(pl.multiple_of(c * step_rows, step_rows), step_rows), :]
        accs = list(accs)
        for r in range(step_rows // pack):
            hit = jnp.where(pred(blk[r * pack:(r + 1) * pack, :], cand_b), one, zero)
            accs[r % len(accs)] = accs[r % len(accs)] + hit
        return tuple(accs)

    accs = lax.fori_loop(0, n_steps, body, (zero,) * 4)
    total = functools.reduce(jnp.add, [a.astype(I32) if a.dtype != F32 else a for a in accs])
    return jnp.sum(total, axis=0, keepdims=True).astype(F32)


_GE = lambda a, b: a >= b
_GT = lambda a, b: a > b


def _bisect(count_ge, target, lowest, n_bits, cnt_lowest, to_cand=lambda c: c):
    cnt = count_ge(to_cand(jnp.zeros((1, T), I32)))
    val = jnp.where(cnt >= target, 0, lowest).astype(I32)
    cnt_val = jnp.where(cnt >= target, cnt, cnt_lowest)

    def step(i, state):
        val, cnt_val = state
        cand = val + jnp.left_shift(jnp.int32(1), n_bits - 2 - i)
        cnt = count_ge(to_cand(cand))
        return jnp.where(cnt >= target, cand, val), jnp.where(cnt >= target, cnt, cnt_val)

    return lax.fori_loop(0, n_bits - 1, step, (val, cnt_val))


def _demote_ties(sc_ref, n_steps, step_rows, thr, k):
    need = float(k) - _count_rows(sc_ref, n_steps, step_rows, _GT, thr)
    s_i = lax.broadcasted_iota(I32, (LANES, LANES), 0)
    j_i = lax.broadcasted_iota(I32, (LANES, LANES), 1)
    earlier = jnp.where(j_i < s_i, 1.0, 0.0).astype(BF16)

    def body(c, seen):
        s0 = pl.multiple_of(c * LANES, LANES)
        blk = sc_ref[pl.ds(s0, LANES), :]
        eq = blk == thr
        eq_f = jnp.where(eq, 1.0, 0.0)
        rank = seen + _dot(earlier, eq_f.astype(BF16))
        sc_ref[pl.ds(s0, LANES), :] = jnp.where(eq & (rank >= need), -jnp.inf, blk)
        return seen + jnp.sum(eq_f, axis=0, keepdims=True)

    lax.fori_loop(0, n_steps * (step_rows // LANES), body, jnp.zeros((1, T), F32))


def _top_k_threshold(sc_ref, n_steps, step_rows, k, n_valid):
    kf = float(k)
    key, cnt_key = _bisect(lambda c: _count_rows(sc_ref, n_steps, step_rows, _GE, c), kf, INT_MIN, 32,
                           n_valid, to_cand=_key_to_float)
    thr = _key_to_float(jnp.maximum(key, KEY_ABOVE_NEG_INF))

    @pl.when(jnp.max(cnt_key) > kf)
    def _():
        _demote_ties(sc_ref, n_steps, step_rows, thr, k)

    return thr


def _split_keys(sc):
    bits = lax.bitcast_convert_type(sc, I32)
    key = bits ^ ((bits >> 31) & 0x7FFFFFFF)
    return (key >> 16).astype(I16), ((key & 0xFFFF) - 2 ** 15).astype(I16)


def _top_k_threshold_split(sc_ref, hi_ref, lo_ref, n_steps, k, n_valid):
    kf = float(k)
    rows = float(T) * n_steps.astype(F32)
    low16 = -2 ** 15
    high, cnt_high = _bisect(lambda c: _count_rows(hi_ref, n_steps, T, _GE, c), kf, low16, 16, rows)
    above = _count_rows(hi_ref, n_steps, T, _GT, high)
    high_b = jnp.broadcast_to(high, (T, T)).astype(I16)

    def keep_bucket(c, carry):
        rs = pl.ds(pl.multiple_of(c * T, T), T)
        lo_ref[rs, :] = jnp.where(hi_ref[rs, :] == high_b, lo_ref[rs, :], jnp.int16(low16))
        return carry

    lax.fori_loop(0, n_steps, keep_bucket, 0)
    low, cnt_low = _bisect(lambda c: _count_rows(lo_ref, n_steps, T, _GE, c), kf - above, low16, 16,
                           cnt_high - above)
    key = jnp.left_shift(high, 16) + (low + 2 ** 15)
    thr = _key_to_float(jnp.maximum(key, KEY_ABOVE_NEG_INF))

    @pl.when(jnp.max(jnp.where(n_valid > kf, above + cnt_low, 0.0)) > kf)
    def _():
        _demote_ties(sc_ref, n_steps, T, thr, k)

    return thr


def _softmax_init(m_scr, acc_scr):
    m_scr[...] = jnp.full(m_scr.shape, NEG, F32)
    acc_scr[...] = jnp.zeros(acc_scr.shape, F32)


def _softmax_out(acc_scr, h):
    acc = acc_scr[h]
    return acc[:HEAD_DIM, :] / jnp.maximum(acc[HEAD_DIM:HEAD_DIM + 1, :], 1e-30)


def _exact_softmax(s, valid):
    s = jnp.where(valid, s, NEG)
    m = jnp.max(s, axis=0, keepdims=True)
    e = jnp.where(valid, jnp.exp2(s - m), 0.0)
    return e / jnp.maximum(jnp.sum(e, axis=0, keepdims=True), 1e-30)


def _store_heads(o_ref, heads_t):
    for pair in range(2):
        o_ref[0, :, pair * LANES:(pair + 1) * LANES] = jnp.concatenate(
            [t[:HEAD_DIM, :] for t in heads_t[2 * pair:2 * pair + 2]], axis=0).T


def _head_values(v_ref, j):
    return [v_ref[0, j, h * HV_ROWS:(h + 1) * HV_ROWS, :] for h in range(N_HEADS)]


def _attn_scratch(rows):
    return [pltpu.VMEM((N_HEADS, 1, T), F32), pltpu.VMEM((N_HEADS, rows, T), F32)]


def _attn_params():
    return pltpu.CompilerParams(dimension_semantics=("arbitrary", "arbitrary"),
                                vmem_limit_bytes=VMEM_LIMIT)


def _keys_spec(S, units, unit):
    return pl.BlockSpec((1, S, units * LANES), lambda b, i: (b, 0, unit // units))


def _q_spec(rows):
    return pl.BlockSpec((1, 1, rows, T), lambda b, i: (b, i, 0, 0))


def _v_spec(S, rows):
    return pl.BlockSpec((1, S // T, rows, T), lambda b, i: (b, 0, 0, 0))


_OUT_SPEC = pl.BlockSpec((1, T, BR_WIDTH), lambda b, i: (b, i, 0))


def _fox_kernel(q_ref, k_ref, v_ref, o_ref, m_scr, acc_scr):
    qb = pl.program_id(1)
    _softmax_init(m_scr, acc_scr)

    def tiles(js, diagonal=False):
        def inputs():
            scores = [[_dot(k_ref[0, pl.ds(pl.multiple_of(j * T, T), T), h * LANES:(h + 1) * LANES],
                            q_ref[0, 0, h * LANES:(h + 1) * LANES, :]) for j in js]
                      for h in range(N_HEADS)]
            valids = [None] * len(js)
            if diagonal:
                key, qry = _key_query_index(qb, js[0])
                valids = [key <= qry]
            return scores, valids, [_head_values(v_ref, j) for j in js]

        _softmax_update(inputs, m_scr, acc_scr, lazy=not diagonal)

    tiles([qb], diagonal=True)
    _for_tile_groups(qb, tiles, group=4)
    _store_heads(o_ref, [_softmax_out(acc_scr, h) for h in range(N_HEADS)])


def _fox_call(pr):
    B, S, _ = pr['keys'].shape
    return pl.pallas_call(
        _fox_kernel,
        out_shape=jax.ShapeDtypeStruct((B, S, BR_WIDTH), F32),
        grid=(B, S // T),
        in_specs=[_q_spec(N_HEADS * LANES), _keys_spec(S, 4, A_FK), _v_spec(S, N_HEADS * HV_ROWS)],
        out_specs=_OUT_SPEC,
        scratch_shapes=_attn_scratch(HV_ROWS),
        compiler_params=_attn_params(),
        name="fox",
    )(pr['fq_t'], pr['keys'], pr['fv_t'])


def _sb_kernel(q_ref, k_ref, v_ref, o_ref, r_scr, acc_scr):
    qb = pl.program_id(1)
    qp = _pair_pads(q_ref[0, 0])
    r_scr[...] = jnp.zeros(r_scr.shape, F32)
    acc_scr[...] = jnp.zeros(acc_scr.shape, F32)
    s_i = lax.broadcasted_iota(I32, (T, T), 0)
    j_i = lax.broadcasted_iota(I32, (T, T), 1)
    later_keys = jnp.where(j_i > s_i, 1.0, 0.0).astype(BF16)

    def tile(j, masked):
        s0 = pl.multiple_of(j * T, T)
        strict = None
        if masked:
            key, qry = _key_query_index(qb, j)
            strict = key < qry
        r_prev = [r_scr[h] for h in range(N_HEADS)]
        r_new, updates = [], []
        for h in range(N_HEADS):
            pair = h // 2
            z = _dot(k_ref[0, pl.ds(s0, T), pair * LANES:(pair + 1) * LANES], qp[h])
            log_take = _log_sigmoid(z)
            log_keep = log_take - z
            if masked:
                log_keep = jnp.where(strict, log_keep, 0.0)
            within = _dot3x(later_keys, log_keep)
            a = jnp.exp(log_take + (r_prev[h] + within))
            if masked:
                a = jnp.where(strict, a, 0.0)
            updates.append(_dot(v_ref[0, j, h * HEAD_DIM:(h + 1) * HEAD_DIM, :], a.astype(BF16)))
            r_new.append(r_prev[h] + within[0:1, :] + log_keep[0:1, :])
        for h in range(N_HEADS):
            acc_scr[h] = acc_scr[h] + updates[h]
            r_scr[h] = r_new[h]

    def live():
        return jnp.max(r_scr[...]) > SB_CUTOFF

    def step(state):
        i, _ = state
        tile(qb - 1 - i, False)
        return i + 1, live()

    tile(qb, True)
    lax.while_loop(lambda st: (st[0] < qb) & st[1], step, (jnp.int32(0), live()))
    _store_heads(o_ref, [acc_scr[h] for h in range(N_HEADS)])


def _sb_call(pr):
    B, S, _ = pr['keys'].shape
    return pl.pallas_call(
        _sb_kernel,
        out_shape=jax.ShapeDtypeStruct((B, S, BR_WIDTH), F32),
        grid=(B, S // T),
        in_specs=[_q_spec(BR_WIDTH), _keys_spec(S, 2, A_SK), _v_spec(S, BR_WIDTH)],
        out_specs=_OUT_SPEC,
        scratch_shapes=_attn_scratch(HEAD_DIM),
        compiler_params=_attn_params(),
        name="stick_breaking",
    )(pr['sq_t'], pr['keys'], pr['sv_t'])


def _dsa_kernel(q_ref, k_ref, v_ref, qi_ref, ki_ref, sm_ref, o_ref, sc_scr, hi_scr, lo_scr, m_scr, acc_scr,
                *, k_sel):
    qb = pl.program_id(1)
    row = lax.broadcasted_iota(I32, (LANES, T), 0)
    qi = qi_ref[0, 0]
    zero = jnp.zeros_like(qi)
    qi_pad = [jnp.where((row >= IDX_DIM * h) & (row < IDX_DIM * (h + 1)), qi, zero)
              for h in range(IDX_HEADS)]
    wi = [sm_ref[0, 0, SM_WI + h:SM_WI + h + 1, :] for h in range(IDX_HEADS)]

    def score_tile(j, masked):
        s0 = pl.multiple_of(j * T, T)
        kt = ki_ref[0, pl.ds(s0, T), :]
        sc = jnp.zeros((T, T), F32)
        for h in range(IDX_HEADS):
            sc = sc + wi[h] * jnp.maximum(_dot(kt, qi_pad[h]), 0.0)
        sc = jnp.where(sc == 0.0, 0.0, sc)
        if masked:
            key, qry = _key_query_index(qb, j)
            sc = jnp.where(key <= qry, sc, -jnp.inf)
        sc_scr[pl.ds(s0, T), :] = sc
        hi_scr[pl.ds(s0, T), :], lo_scr[pl.ds(s0, T), :] = _split_keys(sc)

    _for_tile_groups(qb, lambda js: [score_tile(j, False) for j in js], group=2)
    score_tile(qb, True)

    n_valid = (qb * T + 1 + lax.broadcasted_iota(I32, (1, T), 1)).astype(F32)
    thr = _top_k_threshold_split(sc_scr, hi_scr, lo_scr, qb + 1, k_sel, n_valid)

    qp = _pair_pads(q_ref[0, 0])
    _softmax_init(m_scr, acc_scr)

    def attn_tiles(js, lazy=True):
        def inputs():
            scores = [[_dot(k_ref[0, pl.ds(pl.multiple_of(j * T, T), T),
                                  (h // 2) * LANES:(h // 2 + 1) * LANES], qp[h]) for j in js]
                      for h in range(N_HEADS)]
            valids = [sc_scr[pl.ds(pl.multiple_of(j * T, T), T), :] >= thr for j in js]
            return scores, valids, [_head_values(v_ref, j) for j in js]

        _softmax_update(inputs, m_scr, acc_scr, lazy)

    attn_tiles([qb], lazy=False)
    _for_tile_groups(qb, attn_tiles, group=4, first_fn=functools.partial(attn_tiles, lazy=False))
    _store_heads(o_ref, [_softmax_out(acc_scr, h) for h in range(N_HEADS)])


def _dsa_call(pr):
    B, S, _ = pr['keys'].shape
    return pl.pallas_call(
        functools.partial(_dsa_kernel, k_sel=min(DSA_TOPK_MAX, S // 4)),
        out_shape=jax.ShapeDtypeStruct((B, S, BR_WIDTH), F32),
        grid=(B, S // T),
        in_specs=[_q_spec(BR_WIDTH), _keys_spec(S, 2, A_KA), _v_spec(S, N_HEADS * HV_ROWS),
                  _q_spec(LANES), _keys_spec(S, 1, A_KI), _q_spec(LANES)],
        out_specs=_OUT_SPEC,
        scratch_shapes=[pltpu.VMEM((S, T), F32), pltpu.VMEM((S, T), I16), pltpu.VMEM((S, T), I16)]
                       + _attn_scratch(HV_ROWS),
        compiler_params=_attn_params(),
        name="dsa",
    )(pr['dq_t'], pr['keys'], pr['va_t'], pr['qi_t'], pr['keys'], pr['sm_t'])


def _compress_kernel(xk_ref, xv_ref, pek_ref, pev_ref, w1k_ref, w1v_ref, w2k_ref, w2v_ref,
                     o_ref, ot_ref, *, nc):
    def hidden(x_ref, pe_ref, w1_ref):
        x = x_ref[0]
        first = _dot((x + pe_ref[0:1, :]).astype(BF16), w1_ref[0])
        second = _dot((x + pe_ref[1:2, :]).astype(BF16), w1_ref[1])
        pre = first + pltpu.roll(second, nc - 1, 0)
        return (pre * _sigmoid(pre)).astype(BF16)

    kvc = (_dot(hidden(xk_ref, pek_ref, w1k_ref), w2k_ref[...])
           + _dot(hidden(xv_ref, pev_ref, w1v_ref), w2v_ref[...]))
    o_ref[0] = kvc.astype(BF16)
    ot_ref[0] = kvc.T.astype(BF16)


def _compress_call(xk, xv, pek, pev, w1k, w1v, w2k, w2v):
    B, nc, width = xk.shape
    full = lambda shape: pl.BlockSpec(shape, lambda b: (0,) * len(shape))
    tok = pl.BlockSpec((1, nc, width), lambda b: (b, 0, 0))
    return pl.pallas_call(
        functools.partial(_compress_kernel, nc=nc),
        out_shape=(jax.ShapeDtypeStruct((B, nc, LANES), BF16), jax.ShapeDtypeStruct((B, LANES, nc), BF16)),
        grid=(B,),
        in_specs=[tok, tok, full((2, width)), full((2, width)), full((2, width, CMP_HIDDEN)),
                  full((2, width, CMP_HIDDEN)), full((CMP_HIDDEN, LANES)), full((CMP_HIDDEN, LANES))],
        out_specs=(pl.BlockSpec((1, nc, LANES), lambda b: (b, 0, 0)),
                   pl.BlockSpec((1, LANES, nc), lambda b: (b, 0, 0))),
        compiler_params=pltpu.CompilerParams(dimension_semantics=("arbitrary",),
                                             vmem_limit_bytes=VMEM_LIMIT),
        name="nsa_compress",
    )(xk, xv, pek, pev, w1k, w1v, w2k, w2v)


def _nsa_kernel(q_ref, kvc_ref, kvct_ref, ksw_ref, vsw_ref, sm_ref, o_ref,
                imp_scr, sel_scr, m_scr, acc_scr, *, n_blk, n_sel, nbp):
    qb = pl.program_id(1)
    ncp = kvc_ref.shape[1]
    q_t = q_ref[0, 0]
    zero = jnp.zeros((HEAD_DIM, T), q_t.dtype)
    heads = [q_t[h * HEAD_DIM:(h + 1) * HEAD_DIM, :] for h in range(N_HEADS)]
    q_lo = [jnp.concatenate([q, zero], axis=0) for q in heads]
    q_hi = [jnp.concatenate([zero, q], axis=0) for q in heads]
    gate = lambda br, h: sm_ref[0, 0, SM_GATE + br * N_HEADS + h:SM_GATE + br * N_HEADS + h + 1, :]
    t_row = qb * T + lax.broadcasted_iota(I32, (1, T), 1)

    kvc = kvc_ref[0]
    cmp_end = lax.broadcasted_iota(I32, (ncp, T), 0) * CMP_STRIDE + (CMP_LEN - 1)
    cmp_ok = cmp_end <= t_row
    o_cmp, p_sum = [], jnp.zeros((ncp, T), F32)
    for h in range(N_HEADS):
        pc = _exact_softmax(_dot(kvc, q_lo[h]), cmp_ok)
        o_cmp.append(_dot(kvct_ref[0], pc.astype(BF16))[HEAD_DIM:, :])
        p_sum = p_sum + pc
    per_blk = SLC_BLOCK // CMP_STRIDE
    b_i = lax.broadcasted_iota(I32, (nbp, ncp), 0)
    c_i = lax.broadcasted_iota(I32, (nbp, ncp), 1)
    group = jnp.where(_div_pow2(c_i, per_blk) == b_i, 1.0, 0.0).astype(BF16)
    imp = _dot3x(group, p_sum)
    blk = lax.broadcasted_iota(I32, (nbp, T), 0)
    cur = _div_pow2(t_row, SLC_BLOCK)
    forced = (blk == 0) | (blk == cur) | (blk == cur - 1)
    imp = jnp.where(forced, FORCE_SCORE, jnp.where(blk <= cur, imp, -1.0))
    imp = jnp.where(blk < n_blk, imp, -2.0)
    imp_scr[...] = imp
    thr = _top_k_threshold(imp_scr, 1, nbp, n_sel, jnp.full((1, T), float(nbp), F32))
    sel_scr[...] = jnp.where(imp_scr[...] >= thr, 1.0, 0.0)

    _softmax_init(m_scr, acc_scr)
    blk_per_tile = T // SLC_BLOCK

    def slc_tiles(js, diagonal=False):
        def inputs():
            valids, scores = [], [[] for _ in range(N_HEADS)]
            for j in js:
                valid = jnp.concatenate(
                    [jnp.broadcast_to(sel_scr[pl.ds(j * blk_per_tile + b, 1), :], (SLC_BLOCK, T))
                     for b in range(blk_per_tile)], axis=0) > 0.5
                if diagonal:
                    key, qry = _key_query_index(qb, j)
                    valid = valid & (key <= qry)
                valids.append(valid)
                kt = ksw_ref[0, pl.ds(pl.multiple_of(j * T, T), T), :]
                for h in range(N_HEADS):
                    scores[h].append(_dot(kt, q_lo[h]))
            return scores, valids, [[vsw_ref[0, j, :HV_ROWS, :]] * N_HEADS for j in js]

        _softmax_update(inputs, m_scr, acc_scr, lazy=not diagonal)

    slc_tiles([qb], diagonal=True)
    _for_tile_groups(qb, slc_tiles, group=4)

    n_win = WINDOW // T + 1
    j0 = jnp.maximum(qb - (n_win - 1), 0)
    outs = []
    for h in range(N_HEADS):
        logits, oks = [], []
        for i in range(n_win):
            s0 = pl.multiple_of((j0 + i) * T, T)
            key, qry = _key_query_index(qb, j0 + i)
            dist = qry - key
            oks.append((dist >= 0) & (dist < WINDOW))
            logits.append(jnp.where(oks[i], _dot(ksw_ref[0, pl.ds(s0, T), :], q_hi[h]), NEG))
        m = functools.reduce(jnp.maximum, [jnp.max(s, axis=0, keepdims=True) for s in logits])
        es = [jnp.where(ok, jnp.exp2(s - m), 0.0) for s, ok in zip(logits, oks)]
        den = functools.reduce(jnp.add, [jnp.sum(e, axis=0, keepdims=True) for e in es])
        o_win = functools.reduce(jnp.add, [
            _dot(vsw_ref[0, j0 + i, HV_ROWS:, :], (e / jnp.maximum(den, 1e-30)).astype(BF16))
            for i, e in enumerate(es)])
        outs.append(gate(0, h) * o_cmp[h] + gate(1, h) * _softmax_out(acc_scr, h)
                    + gate(2, h) * o_win)
    _store_heads(o_ref, outs)


def _nsa_call(pr, kvc, kvc_t):
    B, S, _ = pr['keys'].shape
    n_blk = S // SLC_BLOCK
    nbp = max(LANES, n_blk)
    ncp = kvc.shape[1]
    return pl.pallas_call(
        functools.partial(_nsa_kernel, n_blk=n_blk, n_sel=min(SLC_TOPN, n_blk), nbp=nbp),
        out_shape=jax.ShapeDtypeStruct((B, S, BR_WIDTH), F32),
        grid=(B, S // T),
        in_specs=[_q_spec(BR_WIDTH),
                  pl.BlockSpec((1, ncp, LANES), lambda b, i: (b, 0, 0)),
                  pl.BlockSpec((1, LANES, ncp), lambda b, i: (b, 0, 0)),
                  _keys_spec(S, 1, A_KSW), _v_spec(S, VSW_ROWS), _q_spec(LANES)],
        out_specs=_OUT_SPEC,
        scratch_shapes=[pltpu.VMEM((nbp, T), F32), pltpu.VMEM((nbp, T), F32)] + _attn_scratch(HV_ROWS),
        compiler_params=_attn_params(),
        name="nsa",
    )(pr['nq_t'], kvc, kvc_t, pr['keys'], pr['vsw_t'], pr['sm_t'])


def _mem_kv_kernel(mem_ref, g_ref, w_ref, o_ref):
    o_ref[0] = _dot(_rmsnorm(mem_ref[0], g_ref[...]).astype(BF16), w_ref[...]).astype(BF16)


def _mem_kv_call(mem, g, w):
    B, M, D = mem.shape
    return pl.pallas_call(
        _mem_kv_kernel,
        out_shape=jax.ShapeDtypeStruct((B, M, 2 * BR_WIDTH), BF16),
        grid=(B,),
        in_specs=[pl.BlockSpec((1, M, D), lambda b: (b, 0, 0)), pl.BlockSpec((1, D), lambda b: (0, 0)),
                  pl.BlockSpec((D, 2 * BR_WIDTH), lambda b: (0, 0))],
        out_specs=pl.BlockSpec((1, M, 2 * BR_WIDTH), lambda b: (b, 0, 0)),
        name="mem_kv",
    )(mem, g, w)


def _out_kernel(x_ref, g_ref, ya_ref, yb_ref, yc_ref, yd_ref, mq_ref, mkv_ref, wz_ref, wm_ref,
                wb_ref, wo_ref, fg_ref, o_ref, h_scr, *, final):
    x = x_ref[0]
    h_scr[...] = _rmsnorm(x, g_ref[...]).astype(BF16)
    q = mq_ref[0]
    lane = lax.broadcasted_iota(I32, (T, LANES), 1)
    lo = lane < HEAD_DIM
    zero = jnp.zeros((T, LANES), q.dtype)
    pairs = []
    for pair in range(2):
        chunk = q[:, pair * LANES:(pair + 1) * LANES]
        mk = mkv_ref[0, :, pair * LANES:(pair + 1) * LANES]
        mv = mkv_ref[0, :, BR_WIDTH + pair * LANES:BR_WIDTH + (pair + 1) * LANES]
        outs = []
        for qh in (jnp.where(lo, chunk, zero), jnp.where(lo, zero, chunk)):
            s = _dot_nt(qh, mk)
            e = jnp.exp(s - jnp.max(s, axis=1, keepdims=True))
            outs.append(_dot((e / jnp.sum(e, axis=1, keepdims=True)).astype(BF16), mv))
        pairs.append(jnp.where(lo, outs[0], outs[1]))
    y_e = jnp.concatenate(pairs, axis=1)
    merged = jnp.zeros(x.shape, F32)
    for n, y in enumerate((ya_ref[0], yb_ref[0], yc_ref[0], yd_ref[0], y_e)):
        z = _dot(h_scr[...], wz_ref[n])
        ys = (y * (z * _sigmoid(z))).astype(BF16)
        merged = merged + _sigmoid(_dot(h_scr[...], wm_ref[n])) * _dot(ys, wb_ref[n])
    out = x + _dot(merged.astype(BF16), wo_ref[...])
    o_ref[0] = _rmsnorm(out, fg_ref[...]) if final else out


def _out_call(x, g, ys, keys, mkv, wz, wm, wb, wo, fg, final):
    B, S, D = x.shape
    M = mkv.shape[1]
    full = lambda shape: pl.BlockSpec(shape, lambda b, i: (0,) * len(shape))
    tile = lambda n: pl.BlockSpec((1, T, n), lambda b, i: (b, i, 0))
    return pl.pallas_call(
        functools.partial(_out_kernel, final=final),
        out_shape=jax.ShapeDtypeStruct((B, S, D), F32),
        grid=(B, S // T),
        in_specs=[tile(D), full((1, D))] + [tile(BR_WIDTH)] * 4
                 + [pl.BlockSpec((1, T, BR_WIDTH), lambda b, i: (b, i, A_MQ // 2)),
                    pl.BlockSpec((1, M, 2 * BR_WIDTH), lambda b, i: (b, 0, 0)),
                    full((N_BRANCH, D, BR_WIDTH)), full((N_BRANCH, D, D)), full((N_BRANCH, BR_WIDTH, D)),
                    full((D, D)), full((1, D))],
        out_specs=tile(D),
        scratch_shapes=[pltpu.VMEM((T, D), BF16)],
        compiler_params=pltpu.CompilerParams(dimension_semantics=("arbitrary", "arbitrary"),
                                             vmem_limit_bytes=VMEM_LIMIT),
        name="merge_out",
    )(x, g, *ys, keys, mkv, wz, wm, wb, wo, fg)


def kernel(x, mem, positions, norm_g, w_in, kv_norm, w_uk, w_uv, fox_bias, nsa_pe_k, nsa_pe_v,
           nsa_wc1_k, nsa_wc2_k, nsa_wc1_v, nsa_wc2_v, mem_norm, w_mem_kv, w_branch, w_out, final_norm):
    B, S, D = x.shape
    depth = norm_g.shape[0]
    assert S % T == 0 and S >= WINDOW + T and WINDOW % T == 0 and D == 1024
    cos, sin = _rope_tables(positions)
    w1, wz, wm = _proj_weight(w_in)
    fb = jnp.zeros((depth, 1, LANES), F32).at[:, 0, SM_CUM:SM_CUM + N_HEADS].set(fox_bias)
    nc = S // CMP_STRIDE
    tok_w = CMP_STRIDE * NSA_KV_DIM
    pad_k = jnp.zeros((depth, CMP_HIDDEN, LANES), F32).at[..., :NSA_KV_DIM].set(nsa_wc2_k).astype(BF16)
    pad_v = jnp.zeros((depth, CMP_HIDDEN, LANES), F32).at[..., NSA_KV_DIM:].set(nsa_wc2_v).astype(BF16)
    for l in range(depth):
        pr = _proj_call(x, norm_g[l][None], w1[l], kv_norm[l][None], w_uk[l].astype(BF16),
                        w_uv[l].astype(BF16), fb[l], cos, sin)
        kvc, kvc_t = _compress_call(
            pr['kvtok'][:, :, :NSA_KV_DIM].reshape(B, nc, tok_w),
            pr['kvtok'][:, :, NSA_KV_DIM:].reshape(B, nc, tok_w),
            nsa_pe_k[l].reshape(2, tok_w), nsa_pe_v[l].reshape(2, tok_w),
            nsa_wc1_k[l].reshape(2, tok_w, CMP_HIDDEN).astype(BF16),
            nsa_wc1_v[l].reshape(2, tok_w, CMP_HIDDEN).astype(BF16), pad_k[l], pad_v[l])
        ys = (_dsa_call(pr), _fox_call(pr), _sb_call(pr), _nsa_call(pr, kvc, kvc_t))
        mkv = _mem_kv_call(mem, mem_norm[l][None], w_mem_kv[l].astype(BF16))
        x = _out_call(x, norm_g[l][None], ys, pr['keys'], mkv, wz[l], wm[l], w_branch[l].astype(BF16),
                      w_out[l].astype(BF16), final_norm[None], l == depth - 1)
    return x
```

```python
import functools

import jax
import jax.numpy as jnp
from jax import lax
from jax.experimental import pallas as pl
from jax.experimental.pallas import tpu as pltpu

F32 = jnp.float32
BF16 = jnp.bfloat16
I32 = jnp.int32
I16 = jnp.int16

LANES = 128
SUBLANES = 8
N_HEADS = 4
HEAD_DIM = 64
BR_WIDTH = N_HEADS * HEAD_DIM
N_BRANCH = 5
ROPE_THETA = 10000.0
EPS = 1e-6
DSA_TOPK_MAX = 256
IDX_HEADS = 4
IDX_DIM = 32
KV_LATENT = 128
NSA_KV_DIM = 64
CMP_LEN = 32
CMP_STRIDE = 16
CMP_HIDDEN = 128
SLC_BLOCK = 64
SLC_TOPN = 16
WINDOW = 512
FORCE_SCORE = 1e4
Q_SCALE = HEAD_DIM ** -0.5
LOG2E = 1.4426950408889634
Q_SCALE2 = Q_SCALE * LOG2E
IDX_SCALE = (IDX_DIM ** -0.5) * (IDX_HEADS ** -0.5)
NEG = -1e30
SB_CUTOFF = -104.0
LAZY_LIMIT = 64.0
VMEM_LIMIT = 56 * 1024 * 1024
T = 256
ONES_ROWS = 16
HV_ROWS = HEAD_DIM + ONES_ROWS
VSW_ROWS = HV_ROWS + NSA_KV_DIM

IN_LAYOUT = (
    ('dsa_q', BR_WIDTH), ('dsa_ckv', KV_LATENT), ('idx_q', IDX_HEADS * IDX_DIM), ('idx_k', IDX_DIM),
    ('idx_w', IDX_HEADS), ('dsa_z', BR_WIDTH),
    ('fox_q', BR_WIDTH), ('fox_k', BR_WIDTH), ('fox_v', BR_WIDTH), ('fox_f', N_HEADS), ('fox_z', BR_WIDTH),
    ('sb_q', BR_WIDTH), ('sb_k', BR_WIDTH), ('sb_v', BR_WIDTH), ('sb_z', BR_WIDTH),
    ('nsa_q', BR_WIDTH), ('nsa_kc', NSA_KV_DIM), ('nsa_vc', NSA_KV_DIM), ('nsa_ks', NSA_KV_DIM),
    ('nsa_vs', NSA_KV_DIM), ('nsa_kw', NSA_KV_DIM), ('nsa_vw', NSA_KV_DIM), ('nsa_g', 3 * N_HEADS),
    ('nsa_z', BR_WIDTH), ('mem_q', BR_WIDTH), ('mem_z', BR_WIDTH), ('merge', N_BRANCH * 1024),
)

A_FK, A_KA, A_SK, A_KI, A_KSW, A_MQ = 0, 4, 6, 8, 9, 10
A_COLS = 12 * LANES
SM_WI, SM_CUM, SM_GATE = 0, 4, 8
FOX_CK, FOX_ONE, FOX_END = HEAD_DIM, HEAD_DIM + 3, HEAD_DIM + 6


def _dot(a, b):
    return jnp.dot(a, b, preferred_element_type=F32)


def _dot_nt(a, b):
    return lax.dot_general(a, b, (((1,), (1,)), ((), ())), preferred_element_type=F32)


def _split3(x):
    hi = x.astype(BF16)
    r1 = x - hi.astype(F32)
    mid = r1.astype(BF16)
    lo = (r1 - mid.astype(F32)).astype(BF16)
    return hi, mid, lo


def _dot3x(m01, x):
    hi, mid, lo = _split3(x)
    return _dot(m01, hi) + _dot(m01, mid) + _dot(m01, lo)


def _div_pow2(x, n):
    assert n & (n - 1) == 0
    return x >> (n.bit_length() - 1)


def _sigmoid(x):
    return 1.0 / (1.0 + jnp.exp(-x))


def _log_sigmoid(x):
    return jnp.minimum(x, 0.0) - jnp.log(1.0 + jnp.exp(-jnp.abs(x)))


def _rmsnorm(x, g):
    return x * lax.rsqrt(jnp.mean(x * x, axis=-1, keepdims=True) + EPS) * g


def _rope128(x, cos, sin_signed, half):
    lane = lax.broadcasted_iota(I32, x.shape, 1)
    first = (lane & (2 * half - 1)) < half
    partner = jnp.where(first, pltpu.roll(x, LANES - half, 1), pltpu.roll(x, half, 1))
    return x * cos + partner * sin_signed


def _rope_table_kernel(pos_ref, inv_ref, sgn_ref, cos_ref, sin_ref):
    ang = pos_ref[0] * inv_ref[...]
    cos_ref[0] = jnp.cos(ang)
    sin_ref[0] = jnp.sin(ang) * sgn_ref[...]


def _rope_tables(positions):
    B, S = positions.shape
    lane = jnp.arange(LANES)

    def inv_row(dh):
        inv = ROPE_THETA ** (-jnp.arange(0, dh, 2, dtype=F32) / dh)
        return inv[(lane % dh) % (dh // 2)]

    def sgn_row(dh):
        return jnp.where((lane % dh) < dh // 2, -1.0, 1.0).astype(F32)

    inv = jnp.concatenate([inv_row(HEAD_DIM), inv_row(IDX_DIM)])[None, :]
    sgn = jnp.concatenate([sgn_row(HEAD_DIM), sgn_row(IDX_DIM)])[None, :]
    pos = positions.astype(F32)[..., None]
    row = pl.BlockSpec((1, 2 * LANES), lambda b, i: (0, 0))
    tab = pl.BlockSpec((1, T, 2 * LANES), lambda b, i: (b, i, 0))
    return pl.pallas_call(
        _rope_table_kernel,
        out_shape=(jax.ShapeDtypeStruct((B, S, 2 * LANES), F32),) * 2,
        grid=(B, S // T),
        in_specs=[pl.BlockSpec((1, T, 1), lambda b, i: (b, i, 0)), row, row],
        out_specs=(tab, tab),
        name="rope_tables",
    )(pos, inv, sgn)


W_DQ, W_CKV, W_QI, W_KI, W_SM, W_FQ, W_SQ, W_NQ, W_MQ, W_KVC, W_KSW, W_VSW = (
    0, 256, 384, 512, 640, 768, 1536, 2304, 2560, 2816, 2944, 3072)
W_COLS = 3200

PROJ_OUTS = (
    ('keys', A_COLS, BF16), ('kc_tok', NSA_KV_DIM, F32), ('vc_tok', NSA_KV_DIM, F32),
    ('dq_t', BR_WIDTH, BF16), ('fq_t', N_HEADS * LANES, BF16), ('sq_t', BR_WIDTH, BF16),
    ('nq_t', BR_WIDTH, BF16), ('qi_t', LANES, BF16), ('sm_t', LANES, F32),
    ('va_t', N_HEADS * HV_ROWS, BF16), ('fv_t', N_HEADS * HV_ROWS, BF16), ('sv_t', BR_WIDTH, BF16),
    ('vsw_t', VSW_ROWS, BF16),
)
ROW_MAJOR = ('keys', 'kc_tok', 'vc_tok')


def _proj_kernel(x_ref, g_ref, w_ref, kvn_ref, wuk_ref, wuv_ref, fb_ref, cos_ref, sin_ref,
                 *refs):
    out = dict(zip([n for n, _, _ in PROJ_OUTS], refs))
    h_scr, carry_scr = refs[len(PROJ_OUTS):]
    keys = out['keys']
    h_scr[...] = _rmsnorm(x_ref[0], g_ref[...]).astype(BF16)
    c64, s64 = cos_ref[0, :, :LANES], sin_ref[0, :, :LANES]
    c32, s32 = cos_ref[0, :, LANES:], sin_ref[0, :, LANES:]
    lane = lax.broadcasted_iota(I32, (T, LANES), 1)
    ones_rows = jnp.ones((ONES_ROWS, T), BF16)

    def proj(c0, n):
        return _dot(h_scr[...], w_ref[:, c0:c0 + n])

    def put(unit, val):
        keys[0, :, unit * LANES:unit * LANES + val.shape[1]] = val.astype(BF16)

    def rope64(v, j):
        return _rope128(v[:, j * LANES:(j + 1) * LANES], c64, s64, HEAD_DIM // 2)

    def put_t(ref, row0, chunk):
        ref[0, 0, row0:row0 + LANES, :] = chunk.T.astype(ref.dtype)

    def put_values_t(ref, v):
        for j in range(2):
            pair_t = v[:, j * LANES:(j + 1) * LANES].T.astype(BF16)
            for odd in range(2):
                r0 = (2 * j + odd) * HV_ROWS
                ref[0, 0, r0:r0 + HEAD_DIM, :] = pair_t[odd * HEAD_DIM:(odd + 1) * HEAD_DIM, :]
                ref[0, 0, r0 + HEAD_DIM:r0 + HV_ROWS, :] = ones_rows

    p = proj(W_DQ, BR_WIDTH)
    for j in range(2):
        put_t(out['dq_t'], j * LANES, rope64(p, j) * Q_SCALE2)
    c_kv = _rmsnorm(proj(W_CKV, KV_LATENT), kvn_ref[...]).astype(BF16)
    k_a = _dot(c_kv, wuk_ref[...])
    for j in range(2):
        put(A_KA + j, rope64(k_a, j))
    put_values_t(out['va_t'], _dot(c_kv, wuv_ref[...]))
    put_t(out['qi_t'], 0, _rope128(proj(W_QI, LANES), c32, s32, IDX_DIM // 2))
    put(A_KI, _rope128(proj(W_KI, LANES), c32, s32, IDX_DIM // 2))
    p = proj(W_SM, LANES)
    log_f = _log_sigmoid(p + fb_ref[...])
    r_i = lax.broadcasted_iota(I32, (T, T), 0)
    c_i = lax.broadcasted_iota(I32, (T, T), 1)
    tri = jnp.where(r_i >= c_i, 1.0, 0.0).astype(BF16)

    @pl.when(pl.program_id(1) == 0)
    def _():
        carry_scr[...] = jnp.zeros_like(carry_scr)

    cum = _dot3x(tri, log_f) + carry_scr[...]
    carry_scr[...] = cum[T - 1:T, :]
    small_t = jnp.where(lane < SM_CUM, p * IDX_SCALE,
                        jnp.where(lane < SM_GATE, cum, _sigmoid(p))).T
    out['sm_t'][0, 0] = small_t
    q_f = proj(W_FQ, BR_WIDTH) * Q_SCALE2
    cum2, cum2_t = cum * LOG2E, small_t * LOG2E
    k_f = proj(W_FQ + 256, BR_WIDTH)
    row = lax.broadcasted_iota(I32, (HEAD_DIM, T), 0)
    for h in range(N_HEADS):
        pair, odd = h // 2, h % 2
        k_chunk = k_f[:, pair * LANES:(pair + 1) * LANES]
        k_h = pltpu.roll(k_chunk, HEAD_DIM, 1) if odd else k_chunk
        ck = [c.astype(F32) for c in _split3(cum2[:, SM_CUM + h:SM_CUM + h + 1])]
        k_aug = jnp.where(lane < HEAD_DIM, k_h,
                          jnp.where(lane == FOX_CK, ck[0],
                                    jnp.where(lane == FOX_CK + 1, ck[1],
                                              jnp.where(lane == FOX_CK + 2, ck[2],
                                                        jnp.where(lane < FOX_END, 1.0, 0.0)))))
        put(A_FK + h, k_aug)
        q_ht = q_f[:, pair * LANES:(pair + 1) * LANES].T[odd * HEAD_DIM:(odd + 1) * HEAD_DIM, :]
        cq = [c.astype(F32) for c in _split3(cum2_t[SM_CUM + h:SM_CUM + h + 1, :])]
        aug = jnp.where(row < 3, -1.0,
                        jnp.where(row == 3, cq[0], jnp.where(row == 4, cq[1],
                                                             jnp.where(row == 5, cq[2], 0.0))))
        out['fq_t'][0, 0, h * LANES:(h + 1) * LANES, :] = jnp.concatenate(
            [q_ht, aug], axis=0).astype(BF16)
    put_values_t(out['fv_t'], proj(W_FQ + 512, BR_WIDTH))
    p = proj(W_SQ, BR_WIDTH) * Q_SCALE
    for j in range(2):
        put_t(out['sq_t'], j * LANES, p[:, j * LANES:(j + 1) * LANES])
    put(A_SK, proj(W_SQ + 256, BR_WIDTH))
    p = proj(W_SQ + 512, BR_WIDTH)
    for j in range(2):
        put_t(out['sv_t'], j * LANES, p[:, j * LANES:(j + 1) * LANES])
    put(A_MQ, proj(W_MQ, BR_WIDTH) * Q_SCALE)
    p = proj(W_NQ, BR_WIDTH)
    for j in range(2):
        put_t(out['nq_t'], j * LANES, rope64(p, j) * Q_SCALE2)
    p = proj(W_KVC, LANES)
    out['kc_tok'][0] = rope64(p, 0)[:, :NSA_KV_DIM]
    out['vc_tok'][0] = p[:, NSA_KV_DIM:]
    put(A_KSW, rope64(proj(W_KSW, LANES), 0))
    vsw_t = proj(W_VSW, LANES).T.astype(BF16)
    out['vsw_t'][0, 0, :NSA_KV_DIM, :] = vsw_t[:NSA_KV_DIM, :]
    out['vsw_t'][0, 0, NSA_KV_DIM:HV_ROWS, :] = ones_rows
    out['vsw_t'][0, 0, HV_ROWS:, :] = vsw_t[NSA_KV_DIM:, :]


def _proj_weight(w_in):
    off, o = {}, 0
    for name, n in IN_LAYOUT:
        off[name] = (o, n)
        o += n

    def col(name):
        s, n = off[name]
        return w_in[..., s:s + n]

    zeros = lambda n: jnp.zeros(w_in.shape[:-1] + (n,), w_in.dtype)
    groups = [col('dsa_q'), col('dsa_ckv'), col('idx_q'), col('idx_k'), col('idx_k'), col('idx_k'),
              col('idx_k'), col('idx_w'), col('fox_f'), col('nsa_g'), zeros(LANES - 20),
              col('fox_q'), col('fox_k'), col('fox_v'), col('sb_q'), col('sb_k'), col('sb_v'),
              col('nsa_q'), col('mem_q'), col('nsa_kc'), col('nsa_vc'), col('nsa_ks'), col('nsa_kw'),
              col('nsa_vs'), col('nsa_vw')]
    w1 = jnp.concatenate(groups, axis=-1).astype(BF16)
    wz = jnp.stack([col(n) for n in ('dsa_z', 'fox_z', 'sb_z', 'nsa_z', 'mem_z')], axis=-3).astype(BF16)
    return w1, wz, col('merge').astype(BF16)


def _layer_spec(l, shape, grid_rank):
    if grid_rank == 1:
        return pl.BlockSpec((None,) + shape, lambda b: (l,) + (0,) * len(shape))
    return pl.BlockSpec((None,) + shape, lambda b, i: (l,) + (0,) * len(shape))


def _proj_call(l, x, g, w1, kvn, wuk, wuv, fb, cos, sin):
    B, S, D = x.shape
    full = lambda shape: _layer_spec(l, shape, 2)
    tile = lambda n: pl.BlockSpec((1, T, n), lambda b, i: (b, i, 0))
    tile_t = lambda n: pl.BlockSpec((1, 1, n, T), lambda b, i: (b, i, 0, 0))
    shapes, specs = [], []
    for name, n, dt in PROJ_OUTS:
        if name in ROW_MAJOR:
            shapes.append(jax.ShapeDtypeStruct((B, S, n), dt))
            specs.append(tile(n))
        else:
            shapes.append(jax.ShapeDtypeStruct((B, S // T, n, T), dt))
            specs.append(tile_t(n))
    outs = pl.pallas_call(
        _proj_kernel,
        out_shape=tuple(shapes),
        grid=(B, S // T),
        in_specs=[tile(D), full((1, D)), full((D, W_COLS)), full((1, KV_LATENT)),
                  full((KV_LATENT, BR_WIDTH)), full((KV_LATENT, BR_WIDTH)), full((1, LANES)),
                  tile(2 * LANES), tile(2 * LANES)],
        out_specs=tuple(specs),
        scratch_shapes=[pltpu.VMEM((T, D), BF16), pltpu.VMEM((1, LANES), F32)],
        compiler_params=pltpu.CompilerParams(dimension_semantics=("arbitrary", "arbitrary"),
                                             vmem_limit_bytes=VMEM_LIMIT),
        name="proj",
    )(x, g, w1, kvn, wuk, wuv, fb, cos, sin)
    return dict(zip([n for n, _, _ in PROJ_OUTS], outs))


def _key_query_index(qb, j):
    key = j * T + lax.broadcasted_iota(I32, (T, T), 0)
    qry = qb * T + lax.broadcasted_iota(I32, (T, T), 1)
    return key, qry


def _pair_pads(q_t):
    row = lax.broadcasted_iota(I32, (LANES, T), 0)
    top = row < HEAD_DIM
    zero = jnp.zeros((LANES, T), q_t.dtype)
    out = []
    for pair in range(2):
        chunk = q_t[pair * LANES:(pair + 1) * LANES, :]
        out.append(jnp.where(top, chunk, zero))
        out.append(jnp.where(top, zero, chunk))
    return out


def _masked_tiles(scores_h, valids):
    return [s if v is None else jnp.where(v, s, 2 * NEG) for s, v in zip(scores_h, valids)]


def _weighted_values(tiles, values, h, m):
    update = None
    for i, s in enumerate(tiles):
        u = _dot(values[i][h], jnp.exp2(s - m).astype(BF16))
        update = u if update is None else update + u
    return update


def _softmax_update(inputs_fn, m_scr, acc_scr, lazy=False):
    if not lazy:
        _softmax_update_exact(*inputs_fn(), m_scr, acc_scr)
        return
    scores, valids, values = inputs_fn()
    m_ref = [m_scr[h] for h in range(N_HEADS)]
    tops, updates = [], []
    for h in range(N_HEADS):
        tiles = _masked_tiles(scores[h], valids)
        tops.append(functools.reduce(jnp.maximum, [jnp.max(s, axis=0, keepdims=True) for s in tiles]))
        updates.append(_weighted_values(tiles, values, h, m_ref[h]))
    overshoot = functools.reduce(jnp.maximum, [jnp.max(tops[h] - m_ref[h]) for h in range(N_HEADS)])
    safe = overshoot <= LAZY_LIMIT

    @pl.when(safe)
    def _():
        for h in range(N_HEADS):
            m_new = jnp.maximum(m_ref[h], tops[h])
            acc_scr[h] = jnp.exp2(m_ref[h] - m_new) * (acc_scr[h] + updates[h])
            m_scr[h] = m_new

    @pl.when(jnp.logical_not(safe))
    def _():
        _softmax_update_exact(*inputs_fn(), m_scr, acc_scr)


def _softmax_update_exact(scores, valids, values, m_scr, acc_scr):
    m_prev = [m_scr[h] for h in range(N_HEADS)]
    m_new, updates = [], []
    for h in range(N_HEADS):
        tiles = _masked_tiles(scores[h], valids)
        m = functools.reduce(jnp.maximum, [m_prev[h]] + [jnp.max(s, axis=0, keepdims=True) for s in tiles])
        m_new.append(m)
        updates.append(_weighted_values(tiles, values, h, m))
    for h in range(N_HEADS):
        acc_scr[h] = jnp.exp2(m_prev[h] - m_new[h]) * acc_scr[h] + updates[h]
        m_scr[h] = m_new[h]


def _for_tile_groups(n, group_fn, group, first_fn=None):
    def full(i, c):
        group_fn([group * i + g for g in range(group)])
        return c

    if first_fn is not None:
        @pl.when(n >= group)
        def _():
            first_fn(list(range(group)))

    lax.fori_loop(0 if first_fn is None else 1, n // group, full, 0)
    size = group // 2
    while size:
        start = (n // (2 * size)) * (2 * size)

        @pl.when((n & size) != 0)
        def _(start=start, size=size):
            group_fn([start + g for g in range(size)])

        size //= 2


INT_MIN = -2 ** 31
KEY_ABOVE_NEG_INF = 0x80800000 - 2 ** 32


def _key_to_float(c):
    bits = c ^ ((c >> 31) & 0x7FFFFFFF)
    return lax.bitcast_convert_type(bits, F32)


def _count_rows(ref, n_steps, step_rows, pred, cand):
    pack = SUBLANES * (4 // ref.dtype.itemsize)
    cand_b = jnp.broadcast_to(cand, (pack, T)).astype(ref.dtype)
    one, zero = jnp.ones((pack, T), ref.dtype), jnp.zeros((pack, T), ref.dtype)

    def body(c, accs):
        blk = ref[pl.ds(pl.multiple_of(c * step_rows, step_rows), step_rows), :]
        accs = list(accs)
        for r in range(step_rows // pack):
            hit = jnp.where(pred(blk[r * pack:(r + 1) * pack, :], cand_b), one, zero)
            accs[r % len(accs)] = accs[r % len(accs)] + hit
        return tuple(accs)

    accs = lax.fori_loop(0, n_steps, body, (zero,) * 4)
    total = functools.reduce(jnp.add, [a.astype(I32) if a.dtype != F32 else a for a in accs])
    return jnp.sum(total, axis=0, keepdims=True).astype(F32)


_GE = lambda a, b: a >= b
_GT = lambda a, b: a > b


def _bisect(count_ge, target, lowest, n_bits, cnt_lowest, to_cand=lambda c: c, settled=None):
    cnt = count_ge(to_cand(jnp.zeros((1, T), I32)))
    val = jnp.where(cnt >= target, 0, lowest).astype(I32)
    cnt_val = jnp.where(cnt >= target, cnt, cnt_lowest)

    def step(i, state):
        val, cnt_val = state
        cand = val + jnp.left_shift(jnp.int32(1), n_bits - 2 - i)
        cnt = count_ge(to_cand(cand))
        return jnp.where(cnt >= target, cand, val), jnp.where(cnt >= target, cnt, cnt_val)

    if settled is None:
        return lax.fori_loop(0, n_bits - 1, step, (val, cnt_val))
    n_early = n_bits - 1 - 8
    state = lax.fori_loop(0, n_early, step, (val, cnt_val))
    unsettled = lambda cnt_val: jnp.min(jnp.where(settled(cnt_val), 1.0, 0.0)) < 0.5

    def four_bits(st):
        g, val, cnt_val, _ = st
        for b in range(4):
            val, cnt_val = step(n_early + 4 * g + b, (val, cnt_val))
        return g + 1, val, cnt_val, unsettled(cnt_val)

    _, val, cnt_val, _ = lax.while_loop(lambda st: (st[0] < 2) & st[3], four_bits,
                                        (jnp.int32(0),) + state + (unsettled(state[1]),))
    return val, cnt_val


def _demote_ties(sc_ref, n_steps, step_rows, thr, k):
    need = float(k) - _count_rows(sc_ref, n_steps, step_rows, _GT, thr)
    s_i = lax.broadcasted_iota(I32, (LANES, LANES), 0)
    j_i = lax.broadcasted_iota(I32, (LANES, LANES), 1)
    earlier = jnp.where(j_i < s_i, 1.0, 0.0).astype(BF16)

    def body(c, seen):
        s0 = pl.multiple_of(c * LANES, LANES)
        blk = sc_ref[pl.ds(s0, LANES), :]
        eq = blk == thr
        eq_f = jnp.where(eq, 1.0, 0.0)
        rank = seen + _dot(earlier, eq_f.astype(BF16))
        sc_ref[pl.ds(s0, LANES), :] = jnp.where(eq & (rank >= need), -jnp.inf, blk)
        return seen + jnp.sum(eq_f, axis=0, keepdims=True)

    lax.fori_loop(0, n_steps * (step_rows // LANES), body, jnp.zeros((1, T), F32))


def _top_k_threshold(sc_ref, n_steps, step_rows, k, n_valid):
    kf = float(k)
    key, cnt_key = _bisect(lambda c: _count_rows(sc_ref, n_steps, step_rows, _GE, c), kf, INT_MIN, 32,
                           n_valid, to_cand=_key_to_float)
    thr = _key_to_float(jnp.maximum(key, KEY_ABOVE_NEG_INF))

    @pl.when(jnp.max(cnt_key) > kf)
    def _():
        _demote_ties(sc_ref, n_steps, step_rows, thr, k)

    return thr


def _split_keys(sc):
    bits = lax.bitcast_convert_type(sc, I32)
    key = bits ^ ((bits >> 31) & 0x7FFFFFFF)
    return (key >> 16).astype(I16), ((key & 0xFFFF) - 2 ** 15).astype(I16)


def _top_k_threshold_split(sc_ref, hi_ref, lo_ref, n_steps, k, n_valid):
    kf = float(k)
    rows = float(T) * n_steps.astype(F32)
    low16 = -2 ** 15
    high, cnt_high = _bisect(lambda c: _count_rows(hi_ref, n_steps, T, _GE, c), kf, low16, 16, rows)
    above = _count_rows(hi_ref, n_steps, T, _GT, high)
    high_b = jnp.broadcast_to(high, (T, T)).astype(I16)

    def keep_bucket(c, carry):
        rs = pl.ds(pl.multiple_of(c * T, T), T)
        lo_ref[rs, :] = jnp.where(hi_ref[rs, :] == high_b, lo_ref[rs, :], jnp.int16(low16))
        return carry

    lax.fori_loop(0, n_steps, keep_bucket, 0)
    need = kf - above
    low, cnt_low = _bisect(lambda c: _count_rows(lo_ref, n_steps, T, _GE, c), need, low16, 16,
                           cnt_high - above, settled=lambda cnt: (cnt == need) | (n_valid <= kf))
    key = jnp.left_shift(high, 16) + (low + 2 ** 15)
    thr = _key_to_float(jnp.maximum(key, KEY_ABOVE_NEG_INF))

    @pl.when(jnp.max(jnp.where(n_valid > kf, above + cnt_low, 0.0)) > kf)
    def _():
        _demote_ties(sc_ref, n_steps, T, thr, k)

    return thr


def _softmax_init(m_scr, acc_scr):
    m_scr[...] = jnp.full(m_scr.shape, NEG, F32)
    acc_scr[...] = jnp.zeros(acc_scr.shape, F32)


def _softmax_out(acc_scr, h):
    acc = acc_scr[h]
    return acc[:HEAD_DIM, :] / jnp.maximum(acc[HEAD_DIM:HEAD_DIM + 1, :], 1e-30)


def _exact_softmax(s, valid):
    s = jnp.where(valid, s, NEG)
    m = jnp.max(s, axis=0, keepdims=True)
    e = jnp.where(valid, jnp.exp2(s - m), 0.0)
    return e / jnp.maximum(jnp.sum(e, axis=0, keepdims=True), 1e-30)


def _store_heads(o_ref, heads_t):
    for pair in range(2):
        o_ref[0, :, pair * LANES:(pair + 1) * LANES] = jnp.concatenate(
            [t[:HEAD_DIM, :] for t in heads_t[2 * pair:2 * pair + 2]], axis=0).T


def _head_values(v_ref, j):
    return [v_ref[0, j, h * HV_ROWS:(h + 1) * HV_ROWS, :] for h in range(N_HEADS)]


def _attn_scratch(rows):
    return [pltpu.VMEM((N_HEADS, 1, T), F32), pltpu.VMEM((N_HEADS, rows, T), F32)]


def _attn_params():
    return pltpu.CompilerParams(dimension_semantics=("arbitrary", "arbitrary"),
                                vmem_limit_bytes=VMEM_LIMIT)


def _keys_spec(S, units, unit):
    return pl.BlockSpec((1, S, units * LANES), lambda b, i: (b, 0, unit // units))


def _q_spec(rows):
    return pl.BlockSpec((1, 1, rows, T), lambda b, i: (b, i, 0, 0))


def _v_spec(S, rows):
    return pl.BlockSpec((1, S // T, rows, T), lambda b, i: (b, 0, 0, 0))


_OUT_SPEC = pl.BlockSpec((1, T, BR_WIDTH), lambda b, i: (b, i, 0))


def _fox_kernel(q_ref, k_ref, v_ref, o_ref, m_scr, acc_scr):
    qb = pl.program_id(1)
    _softmax_init(m_scr, acc_scr)

    def tiles(js, diagonal=False):
        def inputs():
            scores = [[_dot(k_ref[0, pl.ds(pl.multiple_of(j * T, T), T), h * LANES:(h + 1) * LANES],
                            q_ref[0, 0, h * LANES:(h + 1) * LANES, :]) for j in js]
                      for h in range(N_HEADS)]
            valids = [None] * len(js)
            if diagonal:
                key, qry = _key_query_index(qb, js[0])
                valids = [key <= qry]
            return scores, valids, [_head_values(v_ref, j) for j in js]

        _softmax_update(inputs, m_scr, acc_scr, lazy=not diagonal)

    tiles([qb], diagonal=True)
    _for_tile_groups(qb, tiles, group=4)
    _store_heads(o_ref, [_softmax_out(acc_scr, h) for h in range(N_HEADS)])


def _fox_call(pr):
    B, S, _ = pr['keys'].shape
    return pl.pallas_call(
        _fox_kernel,
        out_shape=jax.ShapeDtypeStruct((B, S, BR_WIDTH), F32),
        grid=(B, S // T),
        in_specs=[_q_spec(N_HEADS * LANES), _keys_spec(S, 4, A_FK), _v_spec(S, N_HEADS * HV_ROWS)],
        out_specs=_OUT_SPEC,
        scratch_shapes=_attn_scratch(HV_ROWS),
        compiler_params=_attn_params(),
        name="fox",
    )(pr['fq_t'], pr['keys'], pr['fv_t'])


def _sb_kernel(q_ref, k_ref, v_ref, o_ref, r_scr, acc_scr):
    qb = pl.program_id(1)
    qp = _pair_pads(q_ref[0, 0])
    r_scr[...] = jnp.zeros(r_scr.shape, F32)
    acc_scr[...] = jnp.zeros(acc_scr.shape, F32)
    s_i = lax.broadcasted_iota(I32, (T, T), 0)
    j_i = lax.broadcasted_iota(I32, (T, T), 1)
    later_keys = jnp.where(j_i > s_i, 1.0, 0.0).astype(BF16)

    def tile(j, masked):
        s0 = pl.multiple_of(j * T, T)
        strict = None
        if masked:
            key, qry = _key_query_index(qb, j)
            strict = key < qry
        r_prev = [r_scr[h] for h in range(N_HEADS)]
        r_new, updates = [], []
        for h in range(N_HEADS):
            pair = h // 2
            z = _dot(k_ref[0, pl.ds(s0, T), pair * LANES:(pair + 1) * LANES], qp[h])
            log_take = _log_sigmoid(z)
            log_keep = log_take - z
            if masked:
                log_keep = jnp.where(strict, log_keep, 0.0)
            within = _dot3x(later_keys, log_keep)
            a = jnp.exp(log_take + (r_prev[h] + within))
            if masked:
                a = jnp.where(strict, a, 0.0)
            updates.append(_dot(v_ref[0, j, h * HEAD_DIM:(h + 1) * HEAD_DIM, :], a.astype(BF16)))
            r_new.append(r_prev[h] + within[0:1, :] + log_keep[0:1, :])
        for h in range(N_HEADS):
            acc_scr[h] = acc_scr[h] + updates[h]
            r_scr[h] = r_new[h]

    def live():
        return jnp.max(r_scr[...]) > SB_CUTOFF

    def step(state):
        i, _ = state
        tile(qb - 1 - i, False)
        return i + 1, live()

    tile(qb, True)
    lax.while_loop(lambda st: (st[0] < qb) & st[1], step, (jnp.int32(0), live()))
    _store_heads(o_ref, [acc_scr[h] for h in range(N_HEADS)])


def _sb_call(pr):
    B, S, _ = pr['keys'].shape
    return pl.pallas_call(
        _sb_kernel,
        out_shape=jax.ShapeDtypeStruct((B, S, BR_WIDTH), F32),
        grid=(B, S // T),
        in_specs=[_q_spec(BR_WIDTH), _keys_spec(S, 2, A_SK), _v_spec(S, BR_WIDTH)],
        out_specs=_OUT_SPEC,
        scratch_shapes=_attn_scratch(HEAD_DIM),
        compiler_params=_attn_params(),
        name="stick_breaking",
    )(pr['sq_t'], pr['keys'], pr['sv_t'])


def _dsa_kernel(q_ref, k_ref, v_ref, qi_ref, ki_ref, sm_ref, o_ref, sc_scr, hi_scr, lo_scr, m_scr, acc_scr,
                *, k_sel):
    qb = pl.program_id(1)
    row = lax.broadcasted_iota(I32, (LANES, T), 0)
    qi = qi_ref[0, 0]
    zero = jnp.zeros_like(qi)
    qi_pad = [jnp.where((row >= IDX_DIM * h) & (row < IDX_DIM * (h + 1)), qi, zero)
              for h in range(IDX_HEADS)]
    wi = [sm_ref[0, 0, SM_WI + h:SM_WI + h + 1, :] for h in range(IDX_HEADS)]

    def score_tile(j, masked):
        s0 = pl.multiple_of(j * T, T)
        kt = ki_ref[0, pl.ds(s0, T), :]
        sc = jnp.zeros((T, T), F32)
        for h in range(IDX_HEADS):
            sc = sc + wi[h] * jnp.maximum(_dot(kt, qi_pad[h]), 0.0)
        sc = jnp.where(sc == 0.0, 0.0, sc)
        if masked:
            key, qry = _key_query_index(qb, j)
            sc = jnp.where(key <= qry, sc, -jnp.inf)
        sc_scr[pl.ds(s0, T), :] = sc
        hi_scr[pl.ds(s0, T), :], lo_scr[pl.ds(s0, T), :] = _split_keys(sc)

    _for_tile_groups(qb, lambda js: [score_tile(j, False) for j in js], group=2)
    score_tile(qb, True)

    n_valid = (qb * T + 1 + lax.broadcasted_iota(I32, (1, T), 1)).astype(F32)
    thr = _top_k_threshold_split(sc_scr, hi_scr, lo_scr, qb + 1, k_sel, n_valid)

    qp = _pair_pads(q_ref[0, 0])
    _softmax_init(m_scr, acc_scr)

    def attn_tiles(js, lazy=True):
        def inputs():
            scores = [[_dot(k_ref[0, pl.ds(pl.multiple_of(j * T, T), T),
                                  (h // 2) * LANES:(h // 2 + 1) * LANES], qp[h]) for j in js]
                      for h in range(N_HEADS)]
            valids = [sc_scr[pl.ds(pl.multiple_of(j * T, T), T), :] >= thr for j in js]
            return scores, valids, [_head_values(v_ref, j) for j in js]

        _softmax_update(inputs, m_scr, acc_scr, lazy)

    attn_tiles([qb], lazy=False)
    _for_tile_groups(qb, attn_tiles, group=4, first_fn=functools.partial(attn_tiles, lazy=False))
    _store_heads(o_ref, [_softmax_out(acc_scr, h) for h in range(N_HEADS)])


def _dsa_call(pr):
    B, S, _ = pr['keys'].shape
    return pl.pallas_call(
        functools.partial(_dsa_kernel, k_sel=min(DSA_TOPK_MAX, S // 4)),
        out_shape=jax.ShapeDtypeStruct((B, S, BR_WIDTH), F32),
        grid=(B, S // T),
        in_specs=[_q_spec(BR_WIDTH), _keys_spec(S, 2, A_KA), _v_spec(S, N_HEADS * HV_ROWS),
                  _q_spec(LANES), _keys_spec(S, 1, A_KI), _q_spec(LANES)],
        out_specs=_OUT_SPEC,
        scratch_shapes=[pltpu.VMEM((S, T), F32), pltpu.VMEM((S, T), I16), pltpu.VMEM((S, T), I16)]
                       + _attn_scratch(HV_ROWS),
        compiler_params=_attn_params(),
        name="dsa",
    )(pr['dq_t'], pr['keys'], pr['va_t'], pr['qi_t'], pr['keys'], pr['sm_t'])


def _compress_kernel(xk_ref, xv_ref, pek_ref, pev_ref, w1k_ref, w1v_ref, w2k_ref, w2v_ref,
                     o_ref, ot_ref, *, nc):
    def hidden(x_ref, pe_ref, w1_ref):
        x = x_ref[0]
        first = _dot((x + pe_ref[0:1, :]).astype(BF16), w1_ref[0])
        second = _dot((x + pe_ref[1:2, :]).astype(BF16), w1_ref[1])
        pre = first + pltpu.roll(second, nc - 1, 0)
        return (pre * _sigmoid(pre)).astype(BF16)

    kvc = (_dot(hidden(xk_ref, pek_ref, w1k_ref), w2k_ref[...])
           + _dot(hidden(xv_ref, pev_ref, w1v_ref), w2v_ref[...]))
    o_ref[0] = kvc.astype(BF16)
    ot_ref[0] = kvc.T.astype(BF16)


def _compress_call(l, xk, xv, pek, pev, w1k, w1v, w2k, w2v):
    B, nc, width = xk.shape
    full = lambda shape: _layer_spec(l, shape, 1)
    tok = pl.BlockSpec((1, nc, width), lambda b: (b, 0, 0))
    return pl.pallas_call(
        functools.partial(_compress_kernel, nc=nc),
        out_shape=(jax.ShapeDtypeStruct((B, nc, LANES), BF16), jax.ShapeDtypeStruct((B, LANES, nc), BF16)),
        grid=(B,),
        in_specs=[tok, tok, full((2, width)), full((2, width)), full((2, width, CMP_HIDDEN)),
                  full((2, width, CMP_HIDDEN)), full((CMP_HIDDEN, LANES)), full((CMP_HIDDEN, LANES))],
        out_specs=(pl.BlockSpec((1, nc, LANES), lambda b: (b, 0, 0)),
                   pl.BlockSpec((1, LANES, nc), lambda b: (b, 0, 0))),
        compiler_params=pltpu.CompilerParams(dimension_semantics=("arbitrary",),
                                             vmem_limit_bytes=VMEM_LIMIT),
        name="nsa_compress",
    )(xk, xv, pek, pev, w1k, w1v, w2k, w2v)


def _nsa_kernel(q_ref, kvc_ref, kvct_ref, ksw_ref, vsw_ref, sm_ref, o_ref,
                imp_scr, sel_scr, m_scr, acc_scr, *, n_blk, n_sel, nbp):
    qb = pl.program_id(1)
    ncp = kvc_ref.shape[1]
    q_t = q_ref[0, 0]
    zero = jnp.zeros((HEAD_DIM, T), q_t.dtype)
    heads = [q_t[h * HEAD_DIM:(h + 1) * HEAD_DIM, :] for h in range(N_HEADS)]
    q_lo = [jnp.concatenate([q, zero], axis=0) for q in heads]
    q_hi = [jnp.concatenate([zero, q], axis=0) for q in heads]
    gate = lambda br, h: sm_ref[0, 0, SM_GATE + br * N_HEADS + h:SM_GATE + br * N_HEADS + h + 1, :]
    t_row = qb * T + lax.broadcasted_iota(I32, (1, T), 1)

    kvc = kvc_ref[0]
    cmp_end = lax.broadcasted_iota(I32, (ncp, T), 0) * CMP_STRIDE + (CMP_LEN - 1)
    cmp_ok = cmp_end <= t_row
    o_cmp, p_sum = [], jnp.zeros((ncp, T), F32)
    for h in range(N_HEADS):
        pc = _exact_softmax(_dot(kvc, q_lo[h]), cmp_ok)
        o_cmp.append(_dot(kvct_ref[0], pc.astype(BF16))[HEAD_DIM:, :])
        p_sum = p_sum + pc
    per_blk = SLC_BLOCK // CMP_STRIDE
    b_i = lax.broadcasted_iota(I32, (nbp, ncp), 0)
    c_i = lax.broadcasted_iota(I32, (nbp, ncp), 1)
    group = jnp.where(_div_pow2(c_i, per_blk) == b_i, 1.0, 0.0).astype(BF16)
    imp = _dot3x(group, p_sum)
    blk = lax.broadcasted_iota(I32, (nbp, T), 0)
    cur = _div_pow2(t_row, SLC_BLOCK)
    forced = (blk == 0) | (blk == cur) | (blk == cur - 1)
    imp = jnp.where(forced, FORCE_SCORE, jnp.where(blk <= cur, imp, -1.0))
    imp = jnp.where(blk < n_blk, imp, -2.0)
    imp_scr[...] = imp
    thr = _top_k_threshold(imp_scr, 1, nbp, n_sel, jnp.full((1, T), float(nbp), F32))
    sel_scr[...] = jnp.where(imp_scr[...] >= thr, 1.0, 0.0)

    _softmax_init(m_scr, acc_scr)
    blk_per_tile = T // SLC_BLOCK

    def slc_tiles(js, diagonal=False):
        def inputs():
            valids, scores = [], [[] for _ in range(N_HEADS)]
            for j in js:
                valid = jnp.concatenate(
                    [jnp.broadcast_to(sel_scr[pl.ds(j * blk_per_tile + b, 1), :], (SLC_BLOCK, T))
                     for b in range(blk_per_tile)], axis=0) > 0.5
                if diagonal:
                    key, qry = _key_query_index(qb, j)
                    valid = valid & (key <= qry)
                valids.append(valid)
                kt = ksw_ref[0, pl.ds(pl.multiple_of(j * T, T), T), :]
                for h in range(N_HEADS):
                    scores[h].append(_dot(kt, q_lo[h]))
            return scores, valids, [[vsw_ref[0, j, :HV_ROWS, :]] * N_HEADS for j in js]

        _softmax_update(inputs, m_scr, acc_scr, lazy=not diagonal)

    slc_tiles([qb], diagonal=True)
    _for_tile_groups(qb, slc_tiles, group=4)

    n_win = WINDOW // T + 1
    j0 = jnp.maximum(qb - (n_win - 1), 0)
    outs = []
    for h in range(N_HEADS):
        logits, oks = [], []
        for i in range(n_win):
            s0 = pl.multiple_of((j0 + i) * T, T)
            key, qry = _key_query_index(qb, j0 + i)
            dist = qry - key
            oks.append((dist >= 0) & (dist < WINDOW))
            logits.append(jnp.where(oks[i], _dot(ksw_ref[0, pl.ds(s0, T), :], q_hi[h]), NEG))
        m = functools.reduce(jnp.maximum, [jnp.max(s, axis=0, keepdims=True) for s in logits])
        es = [jnp.where(ok, jnp.exp2(s - m), 0.0) for s, ok in zip(logits, oks)]
        den = functools.reduce(jnp.add, [jnp.sum(e, axis=0, keepdims=True) for e in es])
        o_win = functools.reduce(jnp.add, [
            _dot(vsw_ref[0, j0 + i, HV_ROWS:, :], (e / jnp.maximum(den, 1e-30)).astype(BF16))
            for i, e in enumerate(es)])
        outs.append(gate(0, h) * o_cmp[h] + gate(1, h) * _softmax_out(acc_scr, h)
                    + gate(2, h) * o_win)
    _store_heads(o_ref, outs)


def _nsa_call(pr, kvc, kvc_t):
    B, S, _ = pr['keys'].shape
    n_blk = S // SLC_BLOCK
    nbp = max(LANES, n_blk)
    ncp = kvc.shape[1]
    return pl.pallas_call(
        functools.partial(_nsa_kernel, n_blk=n_blk, n_sel=min(SLC_TOPN, n_blk), nbp=nbp),
        out_shape=jax.ShapeDtypeStruct((B, S, BR_WIDTH), F32),
        grid=(B, S // T),
        in_specs=[_q_spec(BR_WIDTH),
                  pl.BlockSpec((1, ncp, LANES), lambda b, i: (b, 0, 0)),
                  pl.BlockSpec((1, LANES, ncp), lambda b, i: (b, 0, 0)),
                  _keys_spec(S, 1, A_KSW), _v_spec(S, VSW_ROWS), _q_spec(LANES)],
        out_specs=_OUT_SPEC,
        scratch_shapes=[pltpu.VMEM((nbp, T), F32), pltpu.VMEM((nbp, T), F32)] + _attn_scratch(HV_ROWS),
        compiler_params=_attn_params(),
        name="nsa",
    )(pr['nq_t'], kvc, kvc_t, pr['keys'], pr['vsw_t'], pr['sm_t'])


def _mem_kv_kernel(mem_ref, g_ref, w_ref, o_ref):
    o_ref[0] = _dot(_rmsnorm(mem_ref[0], g_ref[...]).astype(BF16), w_ref[...]).astype(BF16)


def _mem_kv_call(l, mem, g, w):
    B, M, D = mem.shape
    return pl.pallas_call(
        _mem_kv_kernel,
        out_shape=jax.ShapeDtypeStruct((B, M, 2 * BR_WIDTH), BF16),
        grid=(B,),
        in_specs=[pl.BlockSpec((1, M, D), lambda b: (b, 0, 0)), _layer_spec(l, (1, D), 1),
                  _layer_spec(l, (D, 2 * BR_WIDTH), 1)],
        out_specs=pl.BlockSpec((1, M, 2 * BR_WIDTH), lambda b: (b, 0, 0)),
        name="mem_kv",
    )(mem, g, w)


def _out_kernel(x_ref, g_ref, ya_ref, yb_ref, yc_ref, yd_ref, mq_ref, mkv_ref, wz_ref, wm_ref,
                wb_ref, wo_ref, fg_ref, o_ref, h_scr, *, final):
    x = x_ref[0]
    h_scr[...] = _rmsnorm(x, g_ref[...]).astype(BF16)
    q = mq_ref[0]
    lane = lax.broadcasted_iota(I32, (T, LANES), 1)
    lo = lane < HEAD_DIM
    zero = jnp.zeros((T, LANES), q.dtype)
    pairs = []
    for pair in range(2):
        chunk = q[:, pair * LANES:(pair + 1) * LANES]
        mk = mkv_ref[0, :, pair * LANES:(pair + 1) * LANES]
        mv = mkv_ref[0, :, BR_WIDTH + pair * LANES:BR_WIDTH + (pair + 1) * LANES]
        outs = []
        for qh in (jnp.where(lo, chunk, zero), jnp.where(lo, zero, chunk)):
            s = _dot_nt(qh, mk)
            e = jnp.exp(s - jnp.max(s, axis=1, keepdims=True))
            outs.append(_dot((e / jnp.sum(e, axis=1, keepdims=True)).astype(BF16), mv))
        pairs.append(jnp.where(lo, outs[0], outs[1]))
    y_e = jnp.concatenate(pairs, axis=1)
    merged = jnp.zeros(x.shape, F32)
    for n, y in enumerate((ya_ref[0], yb_ref[0], yc_ref[0], yd_ref[0], y_e)):
        z = _dot(h_scr[...], wz_ref[n])
        ys = (y * (z * _sigmoid(z))).astype(BF16)
        gate = _sigmoid(_dot(h_scr[...], wm_ref[:, n * x.shape[1]:(n + 1) * x.shape[1]]))
        merged = merged + gate * _dot(ys, wb_ref[n])
    out = x + _dot(merged.astype(BF16), wo_ref[...])
    o_ref[0] = _rmsnorm(out, fg_ref[...]) if final else out


def _out_call(l, x, g, ys, keys, mkv, wz, wm, wb, wo, fg, final):
    B, S, D = x.shape
    M = mkv.shape[1]
    full = lambda shape: _layer_spec(l, shape, 2)
    tile = lambda n: pl.BlockSpec((1, T, n), lambda b, i: (b, i, 0))
    return pl.pallas_call(
        functools.partial(_out_kernel, final=final),
        out_shape=jax.ShapeDtypeStruct((B, S, D), F32),
        grid=(B, S // T),
        in_specs=[tile(D), full((1, D))] + [tile(BR_WIDTH)] * 4
                 + [pl.BlockSpec((1, T, BR_WIDTH), lambda b, i: (b, i, A_MQ // 2)),
                    pl.BlockSpec((1, M, 2 * BR_WIDTH), lambda b, i: (b, 0, 0)),
                    full((N_BRANCH, D, BR_WIDTH)), full((D, N_BRANCH * D)), full((N_BRANCH, BR_WIDTH, D)),
                    full((D, D)), pl.BlockSpec((1, D), lambda b, i: (0, 0))],
        out_specs=tile(D),
        scratch_shapes=[pltpu.VMEM((T, D), BF16)],
        compiler_params=pltpu.CompilerParams(dimension_semantics=("arbitrary", "arbitrary"),
                                             vmem_limit_bytes=VMEM_LIMIT),
        name="merge_out",
    )(x, g, *ys, keys, mkv, wz, wm, wb, wo, fg)


def kernel(x, mem, positions, norm_g, w_in, kv_norm, w_uk, w_uv, fox_bias, nsa_pe_k, nsa_pe_v,
           nsa_wc1_k, nsa_wc2_k, nsa_wc1_v, nsa_wc2_v, mem_norm, w_mem_kv, w_branch, w_out, final_norm):
    B, S, D = x.shape
    depth = norm_g.shape[0]
    assert S % T == 0 and S >= WINDOW + T and WINDOW % T == 0 and D == 1024
    cos, sin = _rope_tables(positions)
    w1, wz, wm = _proj_weight(w_in)
    fb = jnp.zeros((depth, 1, LANES), F32).at[:, 0, SM_CUM:SM_CUM + N_HEADS].set(fox_bias)
    nc = S // CMP_STRIDE
    tok_w = CMP_STRIDE * NSA_KV_DIM
    pad_k = jnp.zeros((depth, CMP_HIDDEN, LANES), F32).at[..., :NSA_KV_DIM].set(nsa_wc2_k).astype(BF16)
    pad_v = jnp.zeros((depth, CMP_HIDDEN, LANES), F32).at[..., NSA_KV_DIM:].set(nsa_wc2_v).astype(BF16)
    g, kvn, mn = norm_g[:, None, :], kv_norm[:, None, :], mem_norm[:, None, :]
    wuk, wuv, wmkv = w_uk.astype(BF16), w_uv.astype(BF16), w_mem_kv.astype(BF16)
    pek, pev = nsa_pe_k.reshape(depth, 2, tok_w), nsa_pe_v.reshape(depth, 2, tok_w)
    w1k = nsa_wc1_k.reshape(depth, 2, tok_w, CMP_HIDDEN).astype(BF16)
    w1v = nsa_wc1_v.reshape(depth, 2, tok_w, CMP_HIDDEN).astype(BF16)
    wb, wo = w_branch.astype(BF16), w_out.astype(BF16)
    for l in range(depth):
        pr = _proj_call(l, x, g, w1, kvn, wuk, wuv, fb, cos, sin)
        kvc, kvc_t = _compress_call(l, pr['kc_tok'].reshape(B, nc, tok_w), pr['vc_tok'].reshape(B, nc, tok_w),
                                    pek, pev, w1k, w1v, pad_k, pad_v)
        ys = (_dsa_call(pr), _fox_call(pr), _sb_call(pr), _nsa_call(pr, kvc, kvc_t))
        mkv = _mem_kv_call(l, mem, mn, wmkv)
        x = _out_call(l, x, g, ys, pr['keys'], mkv, wz, wm, wb, wo, final_norm[None], l == depth - 1)
    return x
```

```python
import functools

import jax
import jax.numpy as jnp
from jax import lax
from jax.experimental import pallas as pl
from jax.experimental.pallas import tpu as pltpu

F32 = jnp.float32
BF16 = jnp.bfloat16
I32 = jnp.int32
I16 = jnp.int16

LANES = 128
SUBLANES = 8
N_HEADS = 4
HEAD_DIM = 64
BR_WIDTH = N_HEADS * HEAD_DIM
N_BRANCH = 5
ROPE_THETA = 10000.0
EPS = 1e-6
DSA_TOPK_MAX = 256
IDX_HEADS = 4
IDX_DIM = 32
KV_LATENT = 128
NSA_KV_DIM = 64
CMP_LEN = 32
CMP_STRIDE = 16
CMP_HIDDEN = 128
SLC_BLOCK = 64
SLC_TOPN = 16
WINDOW = 512
FORCE_SCORE = 1e4
Q_SCALE = HEAD_DIM ** -0.5
LOG2E = 1.4426950408889634
Q_SCALE2 = Q_SCALE * LOG2E
IDX_SCALE = (IDX_DIM ** -0.5) * (IDX_HEADS ** -0.5)
NEG = -1e30
SB_CUTOFF = -104.0
LAZY_LIMIT = 64.0
VMEM_LIMIT = 56 * 1024 * 1024
T = 256
ONES_ROWS = 16
HV_ROWS = HEAD_DIM + ONES_ROWS
VSW_ROWS = HV_ROWS + NSA_KV_DIM

IN_LAYOUT = (
    ('dsa_q', BR_WIDTH), ('dsa_ckv', KV_LATENT), ('idx_q', IDX_HEADS * IDX_DIM), ('idx_k', IDX_DIM),
    ('idx_w', IDX_HEADS), ('dsa_z', BR_WIDTH),
    ('fox_q', BR_WIDTH), ('fox_k', BR_WIDTH), ('fox_v', BR_WIDTH), ('fox_f', N_HEADS), ('fox_z', BR_WIDTH),
    ('sb_q', BR_WIDTH), ('sb_k', BR_WIDTH), ('sb_v', BR_WIDTH), ('sb_z', BR_WIDTH),
    ('nsa_q', BR_WIDTH), ('nsa_kc', NSA_KV_DIM), ('nsa_vc', NSA_KV_DIM), ('nsa_ks', NSA_KV_DIM),
    ('nsa_vs', NSA_KV_DIM), ('nsa_kw', NSA_KV_DIM), ('nsa_vw', NSA_KV_DIM), ('nsa_g', 3 * N_HEADS),
    ('nsa_z', BR_WIDTH), ('mem_q', BR_WIDTH), ('mem_z', BR_WIDTH), ('merge', N_BRANCH * 1024),
)

A_FK, A_KA, A_SK, A_KI, A_KSW, A_MQ = 0, 4, 6, 8, 9, 10
A_COLS = 12 * LANES
SM_WI, SM_CUM, SM_GATE = 0, 4, 8
FOX_CK, FOX_ONE, FOX_END = HEAD_DIM, HEAD_DIM + 3, HEAD_DIM + 6


def _dot(a, b):
    return jnp.dot(a, b, preferred_element_type=F32)


def _dot_nt(a, b):
    return lax.dot_general(a, b, (((1,), (1,)), ((), ())), preferred_element_type=F32)


def _split3(x):
    hi = x.astype(BF16)
    r1 = x - hi.astype(F32)
    mid = r1.astype(BF16)
    lo = (r1 - mid.astype(F32)).astype(BF16)
    return hi, mid, lo


def _dot3x(m01, x):
    hi, mid, lo = _split3(x)
    return _dot(m01, hi) + _dot(m01, mid) + _dot(m01, lo)


def _div_pow2(x, n):
    assert n & (n - 1) == 0
    return x >> (n.bit_length() - 1)


def _sigmoid(x):
    return 1.0 / (1.0 + jnp.exp(-x))


def _log_sigmoid(x):
    return jnp.minimum(x, 0.0) - jnp.log(1.0 + jnp.exp(-jnp.abs(x)))


def _rmsnorm(x, g):
    return x * lax.rsqrt(jnp.mean(x * x, axis=-1, keepdims=True) + EPS) * g


def _rope128(x, cos, sin_signed, half):
    lane = lax.broadcasted_iota(I32, x.shape, 1)
    first = (lane & (2 * half - 1)) < half
    partner = jnp.where(first, pltpu.roll(x, LANES - half, 1), pltpu.roll(x, half, 1))
    return x * cos + partner * sin_signed


def _rope_table_kernel(pos_ref, inv_ref, sgn_ref, cos_ref, sin_ref):
    ang = pos_ref[0] * inv_ref[...]
    cos_ref[0] = jnp.cos(ang)
    sin_ref[0] = jnp.sin(ang) * sgn_ref[...]


def _rope_tables(positions):
    B, S = positions.shape
    lane = jnp.arange(LANES)

    def inv_row(dh):
        inv = ROPE_THETA ** (-jnp.arange(0, dh, 2, dtype=F32) / dh)
        return inv[(lane % dh) % (dh // 2)]

    def sgn_row(dh):
        return jnp.where((lane % dh) < dh // 2, -1.0, 1.0).astype(F32)

    inv = jnp.concatenate([inv_row(HEAD_DIM), inv_row(IDX_DIM)])[None, :]
    sgn = jnp.concatenate([sgn_row(HEAD_DIM), sgn_row(IDX_DIM)])[None, :]
    pos = positions.astype(F32)[..., None]
    row = pl.BlockSpec((1, 2 * LANES), lambda b, i: (0, 0))
    tab = pl.BlockSpec((1, T, 2 * LANES), lambda b, i: (b, i, 0))
    return pl.pallas_call(
        _rope_table_kernel,
        out_shape=(jax.ShapeDtypeStruct((B, S, 2 * LANES), F32),) * 2,
        grid=(B, S // T),
        in_specs=[pl.BlockSpec((1, T, 1), lambda b, i: (b, i, 0)), row, row],
        out_specs=(tab, tab),
        name="rope_tables",
    )(pos, inv, sgn)


W_DQ, W_CKV, W_QI, W_KI, W_SM, W_FQ, W_SQ, W_NQ, W_MQ, W_KVC, W_KSW, W_VSW = (
    0, 256, 384, 512, 640, 768, 1536, 2304, 2560, 2816, 2944, 3072)
W_COLS = 3200

PROJ_OUTS = (
    ('keys', A_COLS, BF16), ('kc_tok', NSA_KV_DIM, F32), ('vc_tok', NSA_KV_DIM, F32),
    ('dq_t', BR_WIDTH, BF16), ('fq_t', N_HEADS * LANES, BF16), ('sq_t', BR_WIDTH, BF16),
    ('nq_t', BR_WIDTH, BF16), ('qi_t', LANES, BF16), ('sm_t', LANES, F32),
    ('va_t', N_HEADS * HV_ROWS, BF16), ('fv_t', N_HEADS * HV_ROWS, BF16), ('sv_t', BR_WIDTH, BF16),
    ('vsw_t', VSW_ROWS, BF16),
)
ROW_MAJOR = ('keys', 'kc_tok', 'vc_tok')


def _proj_kernel(x_ref, g_ref, w_ref, kvn_ref, wuk_ref, wuv_ref, fb_ref, cos_ref, sin_ref,
                 *refs):
    out = dict(zip([n for n, _, _ in PROJ_OUTS], refs))
    h_scr, carry_scr = refs[len(PROJ_OUTS):]
    keys = out['keys']
    h_scr[...] = _rmsnorm(x_ref[0], g_ref[...]).astype(BF16)
    c64, s64 = cos_ref[0, :, :LANES], sin_ref[0, :, :LANES]
    c32, s32 = cos_ref[0, :, LANES:], sin_ref[0, :, LANES:]
    lane = lax.broadcasted_iota(I32, (T, LANES), 1)
    ones_rows = jnp.ones((ONES_ROWS, T), BF16)

    def proj(c0, n):
        return _dot(h_scr[...], w_ref[:, c0:c0 + n])

    def put(unit, val):
        keys[0, :, unit * LANES:unit * LANES + val.shape[1]] = val.astype(BF16)

    def rope64(v, j):
        return _rope128(v[:, j * LANES:(j + 1) * LANES], c64, s64, HEAD_DIM // 2)

    def put_t(ref, row0, chunk):
        ref[0, 0, row0:row0 + LANES, :] = chunk.T.astype(ref.dtype)

    def put_values_t(ref, v):
        for j in range(2):
            pair_t = v[:, j * LANES:(j + 1) * LANES].T.astype(BF16)
            for odd in range(2):
                r0 = (2 * j + odd) * HV_ROWS
                ref[0, 0, r0:r0 + HEAD_DIM, :] = pair_t[odd * HEAD_DIM:(odd + 1) * HEAD_DIM, :]
                ref[0, 0, r0 + HEAD_DIM:r0 + HV_ROWS, :] = ones_rows

    p = proj(W_DQ, BR_WIDTH)
    for j in range(2):
        put_t(out['dq_t'], j * LANES, rope64(p, j) * Q_SCALE2)
    c_kv = _rmsnorm(proj(W_CKV, KV_LATENT), kvn_ref[...]).astype(BF16)
    k_a = _dot(c_kv, wuk_ref[...])
    for j in range(2):
        put(A_KA + j, rope64(k_a, j))
    put_values_t(out['va_t'], _dot(c_kv, wuv_ref[...]))
    put_t(out['qi_t'], 0, _rope128(proj(W_QI, LANES), c32, s32, IDX_DIM // 2))
    put(A_KI, _rope128(proj(W_KI, LANES), c32, s32, IDX_DIM // 2))
    p = proj(W_SM, LANES)
    log_f = _log_sigmoid(p + fb_ref[...])
    r_i = lax.broadcasted_iota(I32, (T, T), 0)
    c_i = lax.broadcasted_iota(I32, (T, T), 1)
    tri = jnp.where(r_i >= c_i, 1.0, 0.0).astype(BF16)

    @pl.when(pl.program_id(1) == 0)
    def _():
        carry_scr[...] = jnp.zeros_like(carry_scr)

    cum = _dot3x(tri, log_f) + carry_scr[...]
    carry_scr[...] = cum[T - 1:T, :]
    small_t = jnp.where(lane < SM_CUM, p * IDX_SCALE,
                        jnp.where(lane < SM_GATE, cum, _sigmoid(p))).T
    out['sm_t'][0, 0] = small_t
    q_f = proj(W_FQ, BR_WIDTH) * Q_SCALE2
    cum2, cum2_t = cum * LOG2E, small_t * LOG2E
    k_f = proj(W_FQ + 256, BR_WIDTH)
    row = lax.broadcasted_iota(I32, (HEAD_DIM, T), 0)
    for h in range(N_HEADS):
        pair, odd = h // 2, h % 2
        k_chunk = k_f[:, pair * LANES:(pair + 1) * LANES]
        k_h = pltpu.roll(k_chunk, HEAD_DIM, 1) if odd else k_chunk
        ck = [c.astype(F32) for c in _split3(cum2[:, SM_CUM + h:SM_CUM + h + 1])]
        k_aug = jnp.where(lane < HEAD_DIM, k_h,
                          jnp.where(lane == FOX_CK, ck[0],
                                    jnp.where(lane == FOX_CK + 1, ck[1],
                                              jnp.where(lane == FOX_CK + 2, ck[2],
                                                        jnp.where(lane < FOX_END, 1.0, 0.0)))))
        put(A_FK + h, k_aug)
        q_ht = q_f[:, pair * LANES:(pair + 1) * LANES].T[odd * HEAD_DIM:(odd + 1) * HEAD_DIM, :]
        cq = [c.astype(F32) for c in _split3(cum2_t[SM_CUM + h:SM_CUM + h + 1, :])]
        aug = jnp.where(row < 3, -1.0,
                        jnp.where(row == 3, cq[0], jnp.where(row == 4, cq[1],
                                                             jnp.where(row == 5, cq[2], 0.0))))
        out['fq_t'][0, 0, h * LANES:(h + 1) * LANES, :] = jnp.concatenate(
            [q_ht, aug], axis=0).astype(BF16)
    put_values_t(out['fv_t'], proj(W_FQ + 512, BR_WIDTH))
    p = proj(W_SQ, BR_WIDTH) * Q_SCALE
    for j in range(2):
        put_t(out['sq_t'], j * LANES, p[:, j * LANES:(j + 1) * LANES])
    put(A_SK, proj(W_SQ + 256, BR_WIDTH))
    p = proj(W_SQ + 512, BR_WIDTH)
    for j in range(2):
        put_t(out['sv_t'], j * LANES, p[:, j * LANES:(j + 1) * LANES])
    put(A_MQ, proj(W_MQ, BR_WIDTH) * Q_SCALE)
    p = proj(W_NQ, BR_WIDTH)
    for j in range(2):
        put_t(out['nq_t'], j * LANES, rope64(p, j) * Q_SCALE2)
    p = proj(W_KVC, LANES)
    out['kc_tok'][0] = rope64(p, 0)[:, :NSA_KV_DIM]
    out['vc_tok'][0] = p[:, NSA_KV_DIM:]
    put(A_KSW, rope64(proj(W_KSW, LANES), 0))
    vsw_t = proj(W_VSW, LANES).T.astype(BF16)
    out['vsw_t'][0, 0, :NSA_KV_DIM, :] = vsw_t[:NSA_KV_DIM, :]
    out['vsw_t'][0, 0, NSA_KV_DIM:HV_ROWS, :] = ones_rows
    out['vsw_t'][0, 0, HV_ROWS:, :] = vsw_t[NSA_KV_DIM:, :]


def _proj_weight(w_in):
    off, o = {}, 0
    for name, n in IN_LAYOUT:
        off[name] = (o, n)
        o += n

    def col(name):
        s, n = off[name]
        return w_in[..., s:s + n]

    zeros = lambda n: jnp.zeros(w_in.shape[:-1] + (n,), w_in.dtype)
    groups = [col('dsa_q'), col('dsa_ckv'), col('idx_q'), col('idx_k'), col('idx_k'), col('idx_k'),
              col('idx_k'), col('idx_w'), col('fox_f'), col('nsa_g'), zeros(LANES - 20),
              col('fox_q'), col('fox_k'), col('fox_v'), col('sb_q'), col('sb_k'), col('sb_v'),
              col('nsa_q'), col('mem_q'), col('nsa_kc'), col('nsa_vc'), col('nsa_ks'), col('nsa_kw'),
              col('nsa_vs'), col('nsa_vw')]
    w1 = jnp.concatenate(groups, axis=-1).astype(BF16)
    wz = jnp.stack([col(n) for n in ('dsa_z', 'fox_z', 'sb_z', 'nsa_z', 'mem_z')], axis=-3).astype(BF16)
    return w1, wz, col('merge').astype(BF16)


def _layer_spec(l, shape, grid_rank):
    if grid_rank == 1:
        return pl.BlockSpec((None,) + shape, lambda b: (l,) + (0,) * len(shape))
    return pl.BlockSpec((None,) + shape, lambda b, i: (l,) + (0,) * len(shape))


def _proj_call(l, x, g, w1, kvn, wuk, wuv, fb, cos, sin):
    B, S, D = x.shape
    full = lambda shape: _layer_spec(l, shape, 2)
    tile = lambda n: pl.BlockSpec((1, T, n), lambda b, i: (b, i, 0))
    tile_t = lambda n: pl.BlockSpec((1, 1, n, T), lambda b, i: (b, i, 0, 0))
    shapes, specs = [], []
    for name, n, dt in PROJ_OUTS:
        if name in ROW_MAJOR:
            shapes.append(jax.ShapeDtypeStruct((B, S, n), dt))
            specs.append(tile(n))
        else:
            shapes.append(jax.ShapeDtypeStruct((B, S // T, n, T), dt))
            specs.append(tile_t(n))
    outs = pl.pallas_call(
        _proj_kernel,
        out_shape=tuple(shapes),
        grid=(B, S // T),
        in_specs=[tile(D), full((1, D)), full((D, W_COLS)), full((1, KV_LATENT)),
                  full((KV_LATENT, BR_WIDTH)), full((KV_LATENT, BR_WIDTH)), full((1, LANES)),
                  tile(2 * LANES), tile(2 * LANES)],
        out_specs=tuple(specs),
        scratch_shapes=[pltpu.VMEM((T, D), BF16), pltpu.VMEM((1, LANES), F32)],
        compiler_params=pltpu.CompilerParams(dimension_semantics=("arbitrary", "arbitrary"),
                                             vmem_limit_bytes=VMEM_LIMIT),
        name="proj",
    )(x, g, w1, kvn, wuk, wuv, fb, cos, sin)
    return dict(zip([n for n, _, _ in PROJ_OUTS], outs))


def _key_query_index(qb, j):
    key = j * T + lax.broadcasted_iota(I32, (T, T), 0)
    qry = qb * T + lax.broadcasted_iota(I32, (T, T), 1)
    return key, qry


def _pair_pads(q_t):
    row = lax.broadcasted_iota(I32, (LANES, T), 0)
    top = row < HEAD_DIM
    zero = jnp.zeros((LANES, T), q_t.dtype)
    out = []
    for pair in range(2):
        chunk = q_t[pair * LANES:(pair + 1) * LANES, :]
        out.append(jnp.where(top, chunk, zero))
        out.append(jnp.where(top, zero, chunk))
    return out


def _masked_tiles(scores_h, valids):
    return [s if v is None else jnp.where(v, s, 2 * NEG) for s, v in zip(scores_h, valids)]


def _weighted_values(tiles, values, h, m):
    update = None
    for i, s in enumerate(tiles):
        u = _dot(values[i][h], jnp.exp2(s - m).astype(BF16))
        update = u if update is None else update + u
    return update


def _softmax_update(inputs_fn, m_scr, acc_scr, lazy=False):
    if not lazy:
        _softmax_update_exact(*inputs_fn(), m_scr, acc_scr)
        return
    scores, valids, values = inputs_fn()
    m_ref = [m_scr[h] for h in range(N_HEADS)]
    tops, updates = [], []
    for h in range(N_HEADS):
        tiles = _masked_tiles(scores[h], valids)
        tops.append(functools.reduce(jnp.maximum, [jnp.max(s, axis=0, keepdims=True) for s in tiles]))
        updates.append(_weighted_values(tiles, values, h, m_ref[h]))
    overshoot = functools.reduce(jnp.maximum, [jnp.max(tops[h] - m_ref[h]) for h in range(N_HEADS)])
    safe = overshoot <= LAZY_LIMIT

    @pl.when(safe)
    def _():
        for h in range(N_HEADS):
            m_new = jnp.maximum(m_ref[h], tops[h])
            acc_scr[h] = jnp.exp2(m_ref[h] - m_new) * (acc_scr[h] + updates[h])
            m_scr[h] = m_new

    @pl.when(jnp.logical_not(safe))
    def _():
        _softmax_update_exact(*inputs_fn(), m_scr, acc_scr)


def _softmax_update_exact(scores, valids, values, m_scr, acc_scr):
    m_prev = [m_scr[h] for h in range(N_HEADS)]
    m_new, updates = [], []
    for h in range(N_HEADS):
        tiles = _masked_tiles(scores[h], valids)
        m = functools.reduce(jnp.maximum, [m_prev[h]] + [jnp.max(s, axis=0, keepdims=True) for s in tiles])
        m_new.append(m)
        updates.append(_weighted_values(tiles, values, h, m))
    for h in range(N_HEADS):
        acc_scr[h] = jnp.exp2(m_prev[h] - m_new[h]) * acc_scr[h] + updates[h]
        m_scr[h] = m_new[h]


def _for_tile_groups(n, group_fn, group, first_fn=None):
    def full(i, c):
        group_fn([group * i + g for g in range(group)])
        return c

    if first_fn is not None:
        @pl.when(n >= group)
        def _():
            first_fn(list(range(group)))

    lax.fori_loop(0 if first_fn is None else 1, n // group, full, 0)
    size = group // 2
    while size:
        start = (n // (2 * size)) * (2 * size)

        @pl.when((n & size) != 0)
        def _(start=start, size=size):
            group_fn([start + g for g in range(size)])

        size //= 2


INT_MIN = -2 ** 31
LOW16 = -2 ** 15
KEY_ABOVE_NEG_INF = 0x80800000 - 2 ** 32


def _key_to_float(c):
    bits = c ^ ((c >> 31) & 0x7FFFFFFF)
    return lax.bitcast_convert_type(bits, F32)


def _count_rows(ref, n_steps, step_rows, pred, cand):
    pack = SUBLANES * (4 // ref.dtype.itemsize)
    cand_b = jnp.broadcast_to(cand, (pack, T)).astype(ref.dtype)
    one, zero = jnp.ones((pack, T), ref.dtype), jnp.zeros((pack, T), ref.dtype)

    def body(c, accs):
        blk = ref[pl.ds(pl.multiple_of(c * step_rows, step_rows), step_rows), :]
        accs = list(accs)
        for r in range(step_rows // pack):
            hit = jnp.where(pred(blk[r * pack:(r + 1) * pack, :], cand_b), one, zero)
            accs[r % len(accs)] = accs[r % len(accs)] + hit
        return tuple(accs)

    accs = lax.fori_loop(0, n_steps, body, (zero,) * 4)
    total = functools.reduce(jnp.add, [a.astype(I32) if a.dtype != F32 else a for a in accs])
    return jnp.sum(total, axis=0, keepdims=True).astype(F32)


_GE = lambda a, b: a >= b
_GT = lambda a, b: a > b


def _bisect(count_ge, target, lowest, n_bits, cnt_lowest, to_cand=lambda c: c):
    cnt = count_ge(to_cand(jnp.zeros((1, T), I32)))
    take = cnt >= target
    state = (jnp.where(take, 0, lowest).astype(I32), jnp.where(take, cnt, cnt_lowest),
             jnp.where(take, 0.0, cnt))

    def step(i, state):
        val, cnt_val, cnt_next = state
        cand = val + jnp.left_shift(jnp.int32(1), n_bits - 2 - i)
        cnt = count_ge(to_cand(cand))
        take = cnt >= target
        return jnp.where(take, cand, val), jnp.where(take, cnt, cnt_val), jnp.where(take, cnt_next, cnt)

    return lax.fori_loop(0, n_bits - 1, step, state)


def _demote_ties(sc_ref, n_steps, step_rows, thr, k):
    need = float(k) - _count_rows(sc_ref, n_steps, step_rows, _GT, thr)
    s_i = lax.broadcasted_iota(I32, (LANES, LANES), 0)
    j_i = lax.broadcasted_iota(I32, (LANES, LANES), 1)
    earlier = jnp.where(j_i < s_i, 1.0, 0.0).astype(BF16)

    def body(c, seen):
        s0 = pl.multiple_of(c * LANES, LANES)
        blk = sc_ref[pl.ds(s0, LANES), :]
        eq = blk == thr
        eq_f = jnp.where(eq, 1.0, 0.0)
        rank = seen + _dot(earlier, eq_f.astype(BF16))
        sc_ref[pl.ds(s0, LANES), :] = jnp.where(eq & (rank >= need), -jnp.inf, blk)
        return seen + jnp.sum(eq_f, axis=0, keepdims=True)

    lax.fori_loop(0, n_steps * (step_rows // LANES), body, jnp.zeros((1, T), F32))


def _top_k_threshold(sc_ref, n_steps, step_rows, k, n_valid):
    kf = float(k)
    key, cnt_key, _ = _bisect(lambda c: _count_rows(sc_ref, n_steps, step_rows, _GE, c), kf, INT_MIN, 32,
                              n_valid, to_cand=_key_to_float)
    thr = _key_to_float(jnp.maximum(key, KEY_ABOVE_NEG_INF))

    @pl.when(jnp.max(cnt_key) > kf)
    def _():
        _demote_ties(sc_ref, n_steps, step_rows, thr, k)

    return thr


def _split_keys(sc):
    bits = lax.bitcast_convert_type(sc, I32)
    key = bits ^ ((bits >> 31) & 0x7FFFFFFF)
    return (key >> 16).astype(I16), ((key & 0xFFFF) + LOW16).astype(I16)


def _top_k_threshold_split(sc_ref, hi_ref, lo_ref, n_steps, step_rows, k, n_valid):
    kf = float(k)
    rows = float(step_rows) * n_steps.astype(F32)
    high, cnt_high, above = _bisect(lambda c: _count_rows(hi_ref, n_steps, step_rows, _GE, c), kf,
                                    LOW16, 16, rows)
    high_b = jnp.broadcast_to(high, (step_rows, T)).astype(I16)

    def keep_bucket(c, carry):
        rs = pl.ds(pl.multiple_of(c * step_rows, step_rows), step_rows)
        lo_ref[rs, :] = jnp.where(hi_ref[rs, :] == high_b, lo_ref[rs, :], jnp.int16(LOW16))
        return carry

    lax.fori_loop(0, n_steps, keep_bucket, 0)
    low, cnt_low, _ = _bisect(lambda c: _count_rows(lo_ref, n_steps, step_rows, _GE, c), kf - above,
                              LOW16, 16, cnt_high - above)
    key = jnp.left_shift(high, 16) + (low - LOW16)
    thr = _key_to_float(jnp.maximum(key, KEY_ABOVE_NEG_INF))

    @pl.when(jnp.max(jnp.where(n_valid > kf, above + cnt_low, 0.0)) > kf)
    def _():
        _demote_ties(sc_ref, n_steps, step_rows, thr, k)

    return thr


def _softmax_init(m_scr, acc_scr):
    m_scr[...] = jnp.full(m_scr.shape, NEG, F32)
    acc_scr[...] = jnp.zeros(acc_scr.shape, F32)


def _softmax_out(acc_scr, h):
    acc = acc_scr[h]
    return acc[:HEAD_DIM, :] / jnp.maximum(acc[HEAD_DIM:HEAD_DIM + 1, :], 1e-30)


def _exact_softmax(s, valid):
    s = jnp.where(valid, s, NEG)
    m = jnp.max(s, axis=0, keepdims=True)
    e = jnp.where(valid, jnp.exp2(s - m), 0.0)
    return e / jnp.maximum(jnp.sum(e, axis=0, keepdims=True), 1e-30)


def _store_heads(o_ref, heads_t):
    for pair in range(2):
        o_ref[0, :, pair * LANES:(pair + 1) * LANES] = jnp.concatenate(
            [t[:HEAD_DIM, :] for t in heads_t[2 * pair:2 * pair + 2]], axis=0).T


def _head_values(v_ref, j):
    return [v_ref[0, j, h * HV_ROWS:(h + 1) * HV_ROWS, :] for h in range(N_HEADS)]


def _attn_scratch(rows):
    return [pltpu.VMEM((N_HEADS, 1, T), F32), pltpu.VMEM((N_HEADS, rows, T), F32)]


def _attn_params():
    return pltpu.CompilerParams(dimension_semantics=("arbitrary", "arbitrary"),
                                vmem_limit_bytes=VMEM_LIMIT)


def _keys_spec(S, units, unit):
    return pl.BlockSpec((1, S, units * LANES), lambda b, i: (b, 0, unit // units))


def _q_spec(rows):
    return pl.BlockSpec((1, 1, rows, T), lambda b, i: (b, i, 0, 0))


def _v_spec(S, rows):
    return pl.BlockSpec((1, S // T, rows, T), lambda b, i: (b, 0, 0, 0))


_OUT_SPEC = pl.BlockSpec((1, T, BR_WIDTH), lambda b, i: (b, i, 0))


def _fox_kernel(q_ref, k_ref, v_ref, o_ref, m_scr, acc_scr):
    qb = pl.program_id(1)
    _softmax_init(m_scr, acc_scr)

    def tiles(js, diagonal=False):
        def inputs():
            scores = [[_dot(k_ref[0, pl.ds(pl.multiple_of(j * T, T), T), h * LANES:(h + 1) * LANES],
                            q_ref[0, 0, h * LANES:(h + 1) * LANES, :]) for j in js]
                      for h in range(N_HEADS)]
            valids = [None] * len(js)
            if diagonal:
                key, qry = _key_query_index(qb, js[0])
                valids = [key <= qry]
            return scores, valids, [_head_values(v_ref, j) for j in js]

        _softmax_update(inputs, m_scr, acc_scr, lazy=not diagonal)

    tiles([qb], diagonal=True)
    _for_tile_groups(qb, tiles, group=4)
    _store_heads(o_ref, [_softmax_out(acc_scr, h) for h in range(N_HEADS)])


def _fox_call(pr):
    B, S, _ = pr['keys'].shape
    return pl.pallas_call(
        _fox_kernel,
        out_shape=jax.ShapeDtypeStruct((B, S, BR_WIDTH), F32),
        grid=(B, S // T),
        in_specs=[_q_spec(N_HEADS * LANES), _keys_spec(S, 4, A_FK), _v_spec(S, N_HEADS * HV_ROWS)],
        out_specs=_OUT_SPEC,
        scratch_shapes=_attn_scratch(HV_ROWS),
        compiler_params=_attn_params(),
        name="fox",
    )(pr['fq_t'], pr['keys'], pr['fv_t'])


def _sb_kernel(q_ref, k_ref, v_ref, o_ref, r_scr, acc_scr):
    qb = pl.program_id(1)
    qp = _pair_pads(q_ref[0, 0])
    r_scr[...] = jnp.zeros(r_scr.shape, F32)
    acc_scr[...] = jnp.zeros(acc_scr.shape, F32)
    s_i = lax.broadcasted_iota(I32, (T, T), 0)
    j_i = lax.broadcasted_iota(I32, (T, T), 1)
    later_keys = jnp.where(j_i > s_i, 1.0, 0.0).astype(BF16)

    def tile(j, masked):
        s0 = pl.multiple_of(j * T, T)
        strict = None
        if masked:
            key, qry = _key_query_index(qb, j)
            strict = key < qry
        r_prev = [r_scr[h] for h in range(N_HEADS)]
        r_new, updates = [], []
        for h in range(N_HEADS):
            pair = h // 2
            z = _dot(k_ref[0, pl.ds(s0, T), pair * LANES:(pair + 1) * LANES], qp[h])
            log_take = _log_sigmoid(z)
            log_keep = log_take - z
            if masked:
                log_keep = jnp.where(strict, log_keep, 0.0)
            within = _dot3x(later_keys, log_keep)
            a = jnp.exp(log_take + (r_prev[h] + within))
            if masked:
                a = jnp.where(strict, a, 0.0)
            updates.append(_dot(v_ref[0, j, h * HEAD_DIM:(h + 1) * HEAD_DIM, :], a.astype(BF16)))
            r_new.append(r_prev[h] + within[0:1, :] + log_keep[0:1, :])
        for h in range(N_HEADS):
            acc_scr[h] = acc_scr[h] + updates[h]
            r_scr[h] = r_new[h]

    def live():
        return jnp.max(r_scr[...]) > SB_CUTOFF

    def step(state):
        i, _ = state
        tile(qb - 1 - i, False)
        return i + 1, live()

    tile(qb, True)
    lax.while_loop(lambda st: (st[0] < qb) & st[1], step, (jnp.int32(0), live()))
    _store_heads(o_ref, [acc_scr[h] for h in range(N_HEADS)])


def _sb_call(pr):
    B, S, _ = pr['keys'].shape
    return pl.pallas_call(
        _sb_kernel,
        out_shape=jax.ShapeDtypeStruct((B, S, BR_WIDTH), F32),
        grid=(B, S // T),
        in_specs=[_q_spec(BR_WIDTH), _keys_spec(S, 2, A_SK), _v_spec(S, BR_WIDTH)],
        out_specs=_OUT_SPEC,
        scratch_shapes=_attn_scratch(HEAD_DIM),
        compiler_params=_attn_params(),
        name="stick_breaking",
    )(pr['sq_t'], pr['keys'], pr['sv_t'])


def _dsa_kernel(q_ref, k_ref, v_ref, qi_ref, ki_ref, sm_ref, o_ref, sc_scr, hi_scr, lo_scr, m_scr, acc_scr,
                *, k_sel):
    qb = pl.program_id(1)
    row = lax.broadcasted_iota(I32, (LANES, T), 0)
    qi = qi_ref[0, 0]
    zero = jnp.zeros_like(qi)
    qi_pad = [jnp.where((row >= IDX_DIM * h) & (row < IDX_DIM * (h + 1)), qi, zero)
              for h in range(IDX_HEADS)]
    wi = [sm_ref[0, 0, SM_WI + h:SM_WI + h + 1, :] for h in range(IDX_HEADS)]

    def score_tile(j, masked):
        s0 = pl.multiple_of(j * T, T)
        kt = ki_ref[0, pl.ds(s0, T), :]
        sc = jnp.zeros((T, T), F32)
        for h in range(IDX_HEADS):
            sc = sc + wi[h] * jnp.maximum(_dot(kt, qi_pad[h]), 0.0)
        sc = jnp.where(sc == 0.0, 0.0, sc)
        if masked:
            key, qry = _key_query_index(qb, j)
            sc = jnp.where(key <= qry, sc, -jnp.inf)
        sc_scr[pl.ds(s0, T), :] = sc
        hi_scr[pl.ds(s0, T), :], lo_scr[pl.ds(s0, T), :] = _split_keys(sc)

    _for_tile_groups(qb, lambda js: [score_tile(j, False) for j in js], group=2)
    score_tile(qb, True)

    @pl.when(qb % 2 == 0)
    def _():
        pad = pl.ds(pl.multiple_of((qb + 1) * T, T), T)
        sc_scr[pad, :] = jnp.full((T, T), -jnp.inf, F32)
        hi_scr[pad, :] = jnp.full((T, T), LOW16, I16)
        lo_scr[pad, :] = jnp.full((T, T), LOW16, I16)

    n_valid = (qb * T + 1 + lax.broadcasted_iota(I32, (1, T), 1)).astype(F32)
    thr = _top_k_threshold_split(sc_scr, hi_scr, lo_scr, qb // 2 + 1, 2 * T, k_sel, n_valid)

    qp = _pair_pads(q_ref[0, 0])
    _softmax_init(m_scr, acc_scr)

    def attn_tiles(js, lazy=True):
        def inputs():
            scores = [[_dot(k_ref[0, pl.ds(pl.multiple_of(j * T, T), T),
                                  (h // 2) * LANES:(h // 2 + 1) * LANES], qp[h]) for j in js]
                      for h in range(N_HEADS)]
            valids = [sc_scr[pl.ds(pl.multiple_of(j * T, T), T), :] >= thr for j in js]
            return scores, valids, [_head_values(v_ref, j) for j in js]

        _softmax_update(inputs, m_scr, acc_scr, lazy)

    attn_tiles([qb], lazy=False)
    _for_tile_groups(qb, attn_tiles, group=4, first_fn=functools.partial(attn_tiles, lazy=False))
    _store_heads(o_ref, [_softmax_out(acc_scr, h) for h in range(N_HEADS)])


def _dsa_call(pr):
    B, S, _ = pr['keys'].shape
    return pl.pallas_call(
        functools.partial(_dsa_kernel, k_sel=min(DSA_TOPK_MAX, S // 4)),
        out_shape=jax.ShapeDtypeStruct((B, S, BR_WIDTH), F32),
        grid=(B, S // T),
        in_specs=[_q_spec(BR_WIDTH), _keys_spec(S, 2, A_KA), _v_spec(S, N_HEADS * HV_ROWS),
                  _q_spec(LANES), _keys_spec(S, 1, A_KI), _q_spec(LANES)],
        out_specs=_OUT_SPEC,
        scratch_shapes=[pltpu.VMEM((S, T), F32), pltpu.VMEM((S, T), I16), pltpu.VMEM((S, T), I16)]
                       + _attn_scratch(HV_ROWS),
        compiler_params=_attn_params(),
        name="dsa",
    )(pr['dq_t'], pr['keys'], pr['va_t'], pr['qi_t'], pr['keys'], pr['sm_t'])


def _compress_kernel(xk_ref, xv_ref, pek_ref, pev_ref, w1k_ref, w1v_ref, w2k_ref, w2v_ref,
                     o_ref, ot_ref, *, nc):
    def hidden(x_ref, pe_ref, w1_ref):
        x = x_ref[0]
        first = _dot((x + pe_ref[0:1, :]).astype(BF16), w1_ref[0])
        second = _dot((x + pe_ref[1:2, :]).astype(BF16), w1_ref[1])
        pre = first + pltpu.roll(second, nc - 1, 0)
        return (pre * _sigmoid(pre)).astype(BF16)

    kvc = (_dot(hidden(xk_ref, pek_ref, w1k_ref), w2k_ref[...])
           + _dot(hidden(xv_ref, pev_ref, w1v_ref), w2v_ref[...]))
    o_ref[0] = kvc.astype(BF16)
    ot_ref[0] = kvc.T.astype(BF16)


def _compress_call(l, xk, xv, pek, pev, w1k, w1v, w2k, w2v):
    B, nc, width = xk.shape
    full = lambda shape: _layer_spec(l, shape, 1)
    tok = pl.BlockSpec((1, nc, width), lambda b: (b, 0, 0))
    return pl.pallas_call(
        functools.partial(_compress_kernel, nc=nc),
        out_shape=(jax.ShapeDtypeStruct((B, nc, LANES), BF16), jax.ShapeDtypeStruct((B, LANES, nc), BF16)),
        grid=(B,),
        in_specs=[tok, tok, full((2, width)), full((2, width)), full((2, width, CMP_HIDDEN)),
                  full((2, width, CMP_HIDDEN)), full((CMP_HIDDEN, LANES)), full((CMP_HIDDEN, LANES))],
        out_specs=(pl.BlockSpec((1, nc, LANES), lambda b: (b, 0, 0)),
                   pl.BlockSpec((1, LANES, nc), lambda b: (b, 0, 0))),
        compiler_params=pltpu.CompilerParams(dimension_semantics=("arbitrary",),
                                             vmem_limit_bytes=VMEM_LIMIT),
        name="nsa_compress",
    )(xk, xv, pek, pev, w1k, w1v, w2k, w2v)


def _nsa_kernel(q_ref, kvc_ref, kvct_ref, ksw_ref, vsw_ref, sm_ref, o_ref,
                imp_scr, sel_scr, m_scr, acc_scr, *, n_blk, n_sel, nbp):
    qb = pl.program_id(1)
    ncp = kvc_ref.shape[1]
    q_t = q_ref[0, 0]
    zero = jnp.zeros((HEAD_DIM, T), q_t.dtype)
    heads = [q_t[h * HEAD_DIM:(h + 1) * HEAD_DIM, :] for h in range(N_HEADS)]
    q_lo = [jnp.concatenate([q, zero], axis=0) for q in heads]
    q_hi = [jnp.concatenate([zero, q], axis=0) for q in heads]
    gate = lambda br, h: sm_ref[0, 0, SM_GATE + br * N_HEADS + h:SM_GATE + br * N_HEADS + h + 1, :]
    t_row = qb * T + lax.broadcasted_iota(I32, (1, T), 1)

    kvc = kvc_ref[0]
    cmp_end = lax.broadcasted_iota(I32, (ncp, T), 0) * CMP_STRIDE + (CMP_LEN - 1)
    cmp_ok = cmp_end <= t_row
    o_cmp, p_sum = [], jnp.zeros((ncp, T), F32)
    for h in range(N_HEADS):
        pc = _exact_softmax(_dot(kvc, q_lo[h]), cmp_ok)
        o_cmp.append(_dot(kvct_ref[0], pc.astype(BF16))[HEAD_DIM:, :])
        p_sum = p_sum + pc
    per_blk = SLC_BLOCK // CMP_STRIDE
    b_i = lax.broadcasted_iota(I32, (nbp, ncp), 0)
    c_i = lax.broadcasted_iota(I32, (nbp, ncp), 1)
    group = jnp.where(_div_pow2(c_i, per_blk) == b_i, 1.0, 0.0).astype(BF16)
    imp = _dot3x(group, p_sum)
    blk = lax.broadcasted_iota(I32, (nbp, T), 0)
    cur = _div_pow2(t_row, SLC_BLOCK)
    forced = (blk == 0) | (blk == cur) | (blk == cur - 1)
    imp = jnp.where(forced, FORCE_SCORE, jnp.where(blk <= cur, imp, -1.0))
    imp = jnp.where(blk < n_blk, imp, -2.0)
    imp_scr[...] = imp
    thr = _top_k_threshold(imp_scr, 1, nbp, n_sel, jnp.full((1, T), float(nbp), F32))
    sel_scr[...] = jnp.where(imp_scr[...] >= thr, 1.0, 0.0)

    _softmax_init(m_scr, acc_scr)
    blk_per_tile = T // SLC_BLOCK

    def slc_tiles(js, diagonal=False):
        def inputs():
            valids, scores = [], [[] for _ in range(N_HEADS)]
            for j in js:
                valid = jnp.concatenate(
                    [jnp.broadcast_to(sel_scr[pl.ds(j * blk_per_tile + b, 1), :], (SLC_BLOCK, T))
                     for b in range(blk_per_tile)], axis=0) > 0.5
                if diagonal:
                    key, qry = _key_query_index(qb, j)
                    valid = valid & (key <= qry)
                valids.append(valid)
                kt = ksw_ref[0, pl.ds(pl.multiple_of(j * T, T), T), :]
                for h in range(N_HEADS):
                    scores[h].append(_dot(kt, q_lo[h]))
            return scores, valids, [[vsw_ref[0, j, :HV_ROWS, :]] * N_HEADS for j in js]

        _softmax_update(inputs, m_scr, acc_scr, lazy=not diagonal)

    slc_tiles([qb], diagonal=True)
    _for_tile_groups(qb, slc_tiles, group=4)

    n_win = WINDOW // T + 1
    j0 = jnp.maximum(qb - (n_win - 1), 0)
    outs = []
    for h in range(N_HEADS):
        logits, oks = [], []
        for i in range(n_win):
            s0 = pl.multiple_of((j0 + i) * T, T)
            key, qry = _key_query_index(qb, j0 + i)
            dist = qry - key
            oks.append((dist >= 0) & (dist < WINDOW))
            logits.append(jnp.where(oks[i], _dot(ksw_ref[0, pl.ds(s0, T), :], q_hi[h]), NEG))
        m = functools.reduce(jnp.maximum, [jnp.max(s, axis=0, keepdims=True) for s in logits])
        es = [jnp.where(ok, jnp.exp2(s - m), 0.0) for s, ok in zip(logits, oks)]
        den = functools.reduce(jnp.add, [jnp.sum(e, axis=0, keepdims=True) for e in es])
        o_win = functools.reduce(jnp.add, [
            _dot(vsw_ref[0, j0 + i, HV_ROWS:, :], (e / jnp.maximum(den, 1e-30)).astype(BF16))
            for i, e in enumerate(es)])
        outs.append(gate(0, h) * o_cmp[h] + gate(1, h) * _softmax_out(acc_scr, h)
                    + gate(2, h) * o_win)
    _store_heads(o_ref, outs)


def _nsa_call(pr, kvc, kvc_t):
    B, S, _ = pr['keys'].shape
    n_blk = S // SLC_BLOCK
    nbp = max(LANES, n_blk)
    ncp = kvc.shape[1]
    return pl.pallas_call(
        functools.partial(_nsa_kernel, n_blk=n_blk, n_sel=min(SLC_TOPN, n_blk), nbp=nbp),
        out_shape=jax.ShapeDtypeStruct((B, S, BR_WIDTH), F32),
        grid=(B, S // T),
        in_specs=[_q_spec(BR_WIDTH),
                  pl.BlockSpec((1, ncp, LANES), lambda b, i: (b, 0, 0)),
                  pl.BlockSpec((1, LANES, ncp), lambda b, i: (b, 0, 0)),
                  _keys_spec(S, 1, A_KSW), _v_spec(S, VSW_ROWS), _q_spec(LANES)],
        out_specs=_OUT_SPEC,
        scratch_shapes=[pltpu.VMEM((nbp, T), F32), pltpu.VMEM((nbp, T), F32)] + _attn_scratch(HV_ROWS),
        compiler_params=_attn_params(),
        name="nsa",
    )(pr['nq_t'], kvc, kvc_t, pr['keys'], pr['vsw_t'], pr['sm_t'])


def _mem_kv_kernel(mem_ref, g_ref, w_ref, o_ref):
    o_ref[0] = _dot(_rmsnorm(mem_ref[0], g_ref[...]).astype(BF16), w_ref[...]).astype(BF16)


def _mem_kv_call(l, mem, g, w):
    B, M, D = mem.shape
    return pl.pallas_call(
        _mem_kv_kernel,
        out_shape=jax.ShapeDtypeStruct((B, M, 2 * BR_WIDTH), BF16),
        grid=(B,),
        in_specs=[pl.BlockSpec((1, M, D), lambda b: (b, 0, 0)), _layer_spec(l, (1, D), 1),
                  _layer_spec(l, (D, 2 * BR_WIDTH), 1)],
        out_specs=pl.BlockSpec((1, M, 2 * BR_WIDTH), lambda b: (b, 0, 0)),
        name="mem_kv",
    )(mem, g, w)


def _out_kernel(x_ref, g_ref, ya_ref, yb_ref, yc_ref, yd_ref, mq_ref, mkv_ref, wz_ref, wm_ref,
                wb_ref, wo_ref, fg_ref, o_ref, h_scr, *, final):
    x = x_ref[0]
    h_scr[...] = _rmsnorm(x, g_ref[...]).astype(BF16)
    q = mq_ref[0]
    lane = lax.broadcasted_iota(I32, (T, LANES), 1)
    lo = lane < HEAD_DIM
    zero = jnp.zeros((T, LANES), q.dtype)
    pairs = []
    for pair in range(2):
        chunk = q[:, pair * LANES:(pair + 1) * LANES]
        mk = mkv_ref[0, :, pair * LANES:(pair + 1) * LANES]
        mv = mkv_ref[0, :, BR_WIDTH + pair * LANES:BR_WIDTH + (pair + 1) * LANES]
        outs = []
        for qh in (jnp.where(lo, chunk, zero), jnp.where(lo, zero, chunk)):
            s = _dot_nt(qh, mk)
            e = jnp.exp(s - jnp.max(s, axis=1, keepdims=True))
            outs.append(_dot((e / jnp.sum(e, axis=1, keepdims=True)).astype(BF16), mv))
        pairs.append(jnp.where(lo, outs[0], outs[1]))
    y_e = jnp.concatenate(pairs, axis=1)
    merged = jnp.zeros(x.shape, F32)
    for n, y in enumerate((ya_ref[0], yb_ref[0], yc_ref[0], yd_ref[0], y_e)):
        z = _dot(h_scr[...], wz_ref[n])
        ys = (y * (z * _sigmoid(z))).astype(BF16)
        gate = _sigmoid(_dot(h_scr[...], wm_ref[:, n * x.shape[1]:(n + 1) * x.shape[1]]))
        merged = merged + gate * _dot(ys, wb_ref[n])
    out = x + _dot(merged.astype(BF16), wo_ref[...])
    o_ref[0] = _rmsnorm(out, fg_ref[...]) if final else out


def _out_call(l, x, g, ys, keys, mkv, wz, wm, wb, wo, fg, final):
    B, S, D = x.shape
    M = mkv.shape[1]
    full = lambda shape: _layer_spec(l, shape, 2)
    tile = lambda n: pl.BlockSpec((1, T, n), lambda b, i: (b, i, 0))
    return pl.pallas_call(
        functools.partial(_out_kernel, final=final),
        out_shape=jax.ShapeDtypeStruct((B, S, D), F32),
        grid=(B, S // T),
        in_specs=[tile(D), full((1, D))] + [tile(BR_WIDTH)] * 4
                 + [pl.BlockSpec((1, T, BR_WIDTH), lambda b, i: (b, i, A_MQ // 2)),
                    pl.BlockSpec((1, M, 2 * BR_WIDTH), lambda b, i: (b, 0, 0)),
                    full((N_BRANCH, D, BR_WIDTH)), full((D, N_BRANCH * D)), full((N_BRANCH, BR_WIDTH, D)),
                    full((D, D)), pl.BlockSpec((1, D), lambda b, i: (0, 0))],
        out_specs=tile(D),
        scratch_shapes=[pltpu.VMEM((T, D), BF16)],
        compiler_params=pltpu.CompilerParams(dimension_semantics=("arbitrary", "arbitrary"),
                                             vmem_limit_bytes=VMEM_LIMIT),
        name="merge_out",
    )(x, g, *ys, keys, mkv, wz, wm, wb, wo, fg)


def kernel(x, mem, positions, norm_g, w_in, kv_norm, w_uk, w_uv, fox_bias, nsa_pe_k, nsa_pe_v,
           nsa_wc1_k, nsa_wc2_k, nsa_wc1_v, nsa_wc2_v, mem_norm, w_mem_kv, w_branch, w_out, final_norm):
    B, S, D = x.shape
    depth = norm_g.shape[0]
    assert S % (2 * T) == 0 and S >= WINDOW + T and WINDOW % T == 0 and D == 1024
    cos, sin = _rope_tables(positions)
    w1, wz, wm = _proj_weight(w_in)
    fb = jnp.zeros((depth, 1, LANES), F32).at[:, 0, SM_CUM:SM_CUM + N_HEADS].set(fox_bias)
    nc = S // CMP_STRIDE
    tok_w = CMP_STRIDE * NSA_KV_DIM
    pad_k = jnp.zeros((depth, CMP_HIDDEN, LANES), F32).at[..., :NSA_KV_DIM].set(nsa_wc2_k).astype(BF16)
    pad_v = jnp.zeros((depth, CMP_HIDDEN, LANES), F32).at[..., NSA_KV_DIM:].set(nsa_wc2_v).astype(BF16)
    g, kvn, mn = norm_g[:, None, :], kv_norm[:, None, :], mem_norm[:, None, :]
    wuk, wuv, wmkv = w_uk.astype(BF16), w_uv.astype(BF16), w_mem_kv.astype(BF16)
    pek, pev = nsa_pe_k.reshape(depth, 2, tok_w), nsa_pe_v.reshape(depth, 2, tok_w)
    w1k = nsa_wc1_k.reshape(depth, 2, tok_w, CMP_HIDDEN).astype(BF16)
    w1v = nsa_wc1_v.reshape(depth, 2, tok_w, CMP_HIDDEN).astype(BF16)
    wb, wo = w_branch.astype(BF16), w_out.astype(BF16)
    for l in range(depth):
        pr = _proj_call(l, x, g, w1, kvn, wuk, wuv, fb, cos, sin)
        kvc, kvc_t = _compress_call(l, pr['kc_tok'].reshape(B, nc, tok_w), pr['vc_tok'].reshape(B, nc, tok_w),
                                    pek, pev, w1k, w1v, pad_k, pad_v)
        ys = (_dsa_call(pr), _fox_call(pr), _sb_call(pr), _nsa_call(pr, kvc, kvc_t))
        mkv = _mem_kv_call(l, mem, mn, wmkv)
        x = _out_call(l, x, g, ys, pr['keys'], mkv, wz, wm, wb, wo, final_norm[None], l == depth - 1)
    return x
```

```python
import functools

import jax
import jax.numpy as jnp
from jax import lax
from jax.experimental import pallas as pl
from jax.experimental.pallas import tpu as pltpu

F32 = jnp.float32
BF16 = jnp.bfloat16
I32 = jnp.int32
I16 = jnp.int16

LANES = 128
SUBLANES = 8
N_HEADS = 4
HEAD_DIM = 64
BR_WIDTH = N_HEADS * HEAD_DIM
N_BRANCH = 5
ROPE_THETA = 10000.0
EPS = 1e-6
DSA_TOPK_MAX = 256
IDX_HEADS = 4
IDX_DIM = 32
KV_LATENT = 128
NSA_KV_DIM = 64
CMP_LEN = 32
CMP_STRIDE = 16
CMP_HIDDEN = 128
SLC_BLOCK = 64
SLC_TOPN = 16
WINDOW = 512
FORCE_SCORE = 1e4
Q_SCALE = HEAD_DIM ** -0.5
LOG2E = 1.4426950408889634
Q_SCALE2 = Q_SCALE * LOG2E
IDX_SCALE = (IDX_DIM ** -0.5) * (IDX_HEADS ** -0.5)
NEG = -1e30
SB_CUTOFF = -104.0
LAZY_LIMIT = 64.0
VMEM_LIMIT = 56 * 1024 * 1024
T = 256
ONES_ROWS = 16
HV_ROWS = HEAD_DIM + ONES_ROWS
VSW_ROWS = HV_ROWS + NSA_KV_DIM

IN_LAYOUT = (
    ('dsa_q', BR_WIDTH), ('dsa_ckv', KV_LATENT), ('idx_q', IDX_HEADS * IDX_DIM), ('idx_k', IDX_DIM),
    ('idx_w', IDX_HEADS), ('dsa_z', BR_WIDTH),
    ('fox_q', BR_WIDTH), ('fox_k', BR_WIDTH), ('fox_v', BR_WIDTH), ('fox_f', N_HEADS), ('fox_z', BR_WIDTH),
    ('sb_q', BR_WIDTH), ('sb_k', BR_WIDTH), ('sb_v', BR_WIDTH), ('sb_z', BR_WIDTH),
    ('nsa_q', BR_WIDTH), ('nsa_kc', NSA_KV_DIM), ('nsa_vc', NSA_KV_DIM), ('nsa_ks', NSA_KV_DIM),
    ('nsa_vs', NSA_KV_DIM), ('nsa_kw', NSA_KV_DIM), ('nsa_vw', NSA_KV_DIM), ('nsa_g', 3 * N_HEADS),
    ('nsa_z', BR_WIDTH), ('mem_q', BR_WIDTH), ('mem_z', BR_WIDTH), ('merge', N_BRANCH * 1024),
)

A_FK, A_KA, A_SK, A_KI, A_KSW, A_MQ = 0, 4, 6, 8, 9, 10
A_COLS = 12 * LANES
SM_WI, SM_CUM, SM_GATE = 0, 4, 8
FOX_CK, FOX_ONE, FOX_END = HEAD_DIM, HEAD_DIM + 3, HEAD_DIM + 6


def _dot(a, b):
    return jnp.dot(a, b, preferred_element_type=F32)


def _dot_nt(a, b):
    return lax.dot_general(a, b, (((1,), (1,)), ((), ())), preferred_element_type=F32)


def _split3(x):
    hi = x.astype(BF16)
    r1 = x - hi.astype(F32)
    mid = r1.astype(BF16)
    lo = (r1 - mid.astype(F32)).astype(BF16)
    return hi, mid, lo


def _dot3x(m01, x):
    hi, mid, lo = _split3(x)
    return _dot(m01, hi) + _dot(m01, mid) + _dot(m01, lo)


def _div_pow2(x, n):
    assert n & (n - 1) == 0
    return x >> (n.bit_length() - 1)


def _sigmoid(x):
    return 1.0 / (1.0 + jnp.exp(-x))


def _log_sigmoid(x):
    return jnp.minimum(x, 0.0) - jnp.log(1.0 + jnp.exp(-jnp.abs(x)))


def _rmsnorm(x, g):
    return x * lax.rsqrt(jnp.mean(x * x, axis=-1, keepdims=True) + EPS) * g


def _rope128(x, cos, sin_signed, half):
    lane = lax.broadcasted_iota(I32, x.shape, 1)
    first = (lane & (2 * half - 1)) < half
    partner = jnp.where(first, pltpu.roll(x, LANES - half, 1), pltpu.roll(x, half, 1))
    return x * cos + partner * sin_signed


def _rope_table_kernel(pos_ref, inv_ref, sgn_ref, cos_ref, sin_ref):
    ang = pos_ref[0] * inv_ref[...]
    cos_ref[0] = jnp.cos(ang)
    sin_ref[0] = jnp.sin(ang) * sgn_ref[...]


def _rope_tables(positions):
    B, S = positions.shape
    lane = jnp.arange(LANES)

    def inv_row(dh):
        inv = ROPE_THETA ** (-jnp.arange(0, dh, 2, dtype=F32) / dh)
        return inv[(lane % dh) % (dh // 2)]

    def sgn_row(dh):
        return jnp.where((lane % dh) < dh // 2, -1.0, 1.0).astype(F32)

    inv = jnp.concatenate([inv_row(HEAD_DIM), inv_row(IDX_DIM)])[None, :]
    sgn = jnp.concatenate([sgn_row(HEAD_DIM), sgn_row(IDX_DIM)])[None, :]
    pos = positions.astype(F32)[..., None]
    row = pl.BlockSpec((1, 2 * LANES), lambda b, i: (0, 0))
    tab = pl.BlockSpec((1, T, 2 * LANES), lambda b, i: (b, i, 0))
    return pl.pallas_call(
        _rope_table_kernel,
        out_shape=(jax.ShapeDtypeStruct((B, S, 2 * LANES), F32),) * 2,
        grid=(B, S // T),
        in_specs=[pl.BlockSpec((1, T, 1), lambda b, i: (b, i, 0)), row, row],
        out_specs=(tab, tab),
        name="rope_tables",
    )(pos, inv, sgn)


W_DQ, W_CKV, W_QI, W_KI, W_SM, W_FQ, W_SQ, W_NQ, W_MQ, W_KVC, W_KSW, W_VSW = (
    0, 256, 384, 512, 640, 768, 1536, 2304, 2560, 2816, 2944, 3072)
W_COLS = 3200

PROJ_OUTS = (
    ('keys', A_COLS, BF16), ('kc_tok', NSA_KV_DIM, F32), ('vc_tok', NSA_KV_DIM, F32),
    ('dq_t', BR_WIDTH, BF16), ('fq_t', N_HEADS * LANES, BF16), ('sq_t', BR_WIDTH, BF16),
    ('nq_t', BR_WIDTH, BF16), ('qi_t', LANES, BF16), ('sm_t', LANES, F32),
    ('va_t', N_HEADS * HV_ROWS, BF16), ('fv_t', N_HEADS * HV_ROWS, BF16), ('sv_t', BR_WIDTH, BF16),
    ('vsw_t', VSW_ROWS, BF16),
)
ROW_MAJOR = ('keys', 'kc_tok', 'vc_tok')


def _proj_kernel(x_ref, g_ref, w_ref, kvn_ref, wuk_ref, wuv_ref, fb_ref, cos_ref, sin_ref,
                 *refs):
    out = dict(zip([n for n, _, _ in PROJ_OUTS], refs))
    h_scr, carry_scr = refs[len(PROJ_OUTS):]
    keys = out['keys']
    h_scr[...] = _rmsnorm(x_ref[0], g_ref[...]).astype(BF16)
    c64, s64 = cos_ref[0, :, :LANES], sin_ref[0, :, :LANES]
    c32, s32 = cos_ref[0, :, LANES:], sin_ref[0, :, LANES:]
    lane = lax.broadcasted_iota(I32, (T, LANES), 1)
    ones_rows = jnp.ones((ONES_ROWS, T), BF16)

    def proj(c0, n):
        return _dot(h_scr[...], w_ref[:, c0:c0 + n])

    def put(unit, val):
        keys[0, :, unit * LANES:unit * LANES + val.shape[1]] = val.astype(BF16)

    def rope64(v, j):
        return _rope128(v[:, j * LANES:(j + 1) * LANES], c64, s64, HEAD_DIM // 2)

    def put_t(ref, row0, chunk):
        ref[0, 0, row0:row0 + LANES, :] = chunk.T.astype(ref.dtype)

    def put_values_t(ref, v):
        for j in range(2):
            pair_t = v[:, j * LANES:(j + 1) * LANES].T.astype(BF16)
            for odd in range(2):
                r0 = (2 * j + odd) * HV_ROWS
                ref[0, 0, r0:r0 + HEAD_DIM, :] = pair_t[odd * HEAD_DIM:(odd + 1) * HEAD_DIM, :]
                ref[0, 0, r0 + HEAD_DIM:r0 + HV_ROWS, :] = ones_rows

    p = proj(W_DQ, BR_WIDTH)
    for j in range(2):
        put_t(out['dq_t'], j * LANES, rope64(p, j) * Q_SCALE2)
    c_kv = _rmsnorm(proj(W_CKV, KV_LATENT), kvn_ref[...]).astype(BF16)
    k_a = _dot(c_kv, wuk_ref[...])
    for j in range(2):
        put(A_KA + j, rope64(k_a, j))
    put_values_t(out['va_t'], _dot(c_kv, wuv_ref[...]))
    put_t(out['qi_t'], 0, _rope128(proj(W_QI, LANES), c32, s32, IDX_DIM // 2))
    put(A_KI, _rope128(proj(W_KI, LANES), c32, s32, IDX_DIM // 2))
    p = proj(W_SM, LANES)
    log_f = _log_sigmoid(p + fb_ref[...])
    r_i = lax.broadcasted_iota(I32, (T, T), 0)
    c_i = lax.broadcasted_iota(I32, (T, T), 1)
    tri = jnp.where(r_i >= c_i, 1.0, 0.0).astype(BF16)

    @pl.when(pl.program_id(1) == 0)
    def _():
        carry_scr[...] = jnp.zeros_like(carry_scr)

    cum = _dot3x(tri, log_f) + carry_scr[...]
    carry_scr[...] = cum[T - 1:T, :]
    small_t = jnp.where(lane < SM_CUM, p * IDX_SCALE,
                        jnp.where(lane < SM_GATE, cum, _sigmoid(p))).T
    out['sm_t'][0, 0] = small_t
    q_f = proj(W_FQ, BR_WIDTH) * Q_SCALE2
    cum2, cum2_t = cum * LOG2E, small_t * LOG2E
    k_f = proj(W_FQ + 256, BR_WIDTH)
    row = lax.broadcasted_iota(I32, (HEAD_DIM, T), 0)
    for h in range(N_HEADS):
        pair, odd = h // 2, h % 2
        k_chunk = k_f[:, pair * LANES:(pair + 1) * LANES]
        k_h = pltpu.roll(k_chunk, HEAD_DIM, 1) if odd else k_chunk
        ck = [c.astype(F32) for c in _split3(cum2[:, SM_CUM + h:SM_CUM + h + 1])]
        k_aug = jnp.where(lane < HEAD_DIM, k_h,
                          jnp.where(lane == FOX_CK, ck[0],
                                    jnp.where(lane == FOX_CK + 1, ck[1],
                                              jnp.where(lane == FOX_CK + 2, ck[2],
                                                        jnp.where(lane < FOX_END, 1.0, 0.0)))))
        put(A_FK + h, k_aug)
        q_ht = q_f[:, pair * LANES:(pair + 1) * LANES].T[odd * HEAD_DIM:(odd + 1) * HEAD_DIM, :]
        cq = [c.astype(F32) for c in _split3(cum2_t[SM_CUM + h:SM_CUM + h + 1, :])]
        aug = jnp.where(row < 3, -1.0,
                        jnp.where(row == 3, cq[0], jnp.where(row == 4, cq[1],
                                                             jnp.where(row == 5, cq[2], 0.0))))
        out['fq_t'][0, 0, h * LANES:(h + 1) * LANES, :] = jnp.concatenate(
            [q_ht, aug], axis=0).astype(BF16)
    put_values_t(out['fv_t'], proj(W_FQ + 512, BR_WIDTH))
    p = proj(W_SQ, BR_WIDTH) * Q_SCALE
    for j in range(2):
        put_t(out['sq_t'], j * LANES, p[:, j * LANES:(j + 1) * LANES])
    put(A_SK, proj(W_SQ + 256, BR_WIDTH))
    p = proj(W_SQ + 512, BR_WIDTH)
    for j in range(2):
        put_t(out['sv_t'], j * LANES, p[:, j * LANES:(j + 1) * LANES])
    put(A_MQ, proj(W_MQ, BR_WIDTH) * Q_SCALE)
    p = proj(W_NQ, BR_WIDTH)
    for j in range(2):
        put_t(out['nq_t'], j * LANES, rope64(p, j) * Q_SCALE2)
    p = proj(W_KVC, LANES)
    out['kc_tok'][0] = rope64(p, 0)[:, :NSA_KV_DIM]
    out['vc_tok'][0] = p[:, NSA_KV_DIM:]
    put(A_KSW, rope64(proj(W_KSW, LANES), 0))
    vsw_t = proj(W_VSW, LANES).T.astype(BF16)
    out['vsw_t'][0, 0, :NSA_KV_DIM, :] = vsw_t[:NSA_KV_DIM, :]
    out['vsw_t'][0, 0, NSA_KV_DIM:HV_ROWS, :] = ones_rows
    out['vsw_t'][0, 0, HV_ROWS:, :] = vsw_t[NSA_KV_DIM:, :]


def _proj_weight(w_in):
    off, o = {}, 0
    for name, n in IN_LAYOUT:
        off[name] = (o, n)
        o += n

    def col(name):
        s, n = off[name]
        return w_in[..., s:s + n]

    zeros = lambda n: jnp.zeros(w_in.shape[:-1] + (n,), w_in.dtype)
    groups = [col('dsa_q'), col('dsa_ckv'), col('idx_q'), col('idx_k'), col('idx_k'), col('idx_k'),
              col('idx_k'), col('idx_w'), col('fox_f'), col('nsa_g'), zeros(LANES - 20),
              col('fox_q'), col('fox_k'), col('fox_v'), col('sb_q'), col('sb_k'), col('sb_v'),
              col('nsa_q'), col('mem_q'), col('nsa_kc'), col('nsa_vc'), col('nsa_ks'), col('nsa_kw'),
              col('nsa_vs'), col('nsa_vw')]
    w1 = jnp.concatenate(groups, axis=-1).astype(BF16)
    wz = jnp.stack([col(n) for n in ('dsa_z', 'fox_z', 'sb_z', 'nsa_z', 'mem_z')], axis=-3).astype(BF16)
    return w1, wz, col('merge').astype(BF16)


def _layer_spec(l, shape, grid_rank):
    if grid_rank == 1:
        return pl.BlockSpec((None,) + shape, lambda b: (l,) + (0,) * len(shape))
    return pl.BlockSpec((None,) + shape, lambda b, i: (l,) + (0,) * len(shape))


def _proj_call(l, x, g, w1, kvn, wuk, wuv, fb, cos, sin):
    B, S, D = x.shape
    full = lambda shape: _layer_spec(l, shape, 2)
    tile = lambda n: pl.BlockSpec((1, T, n), lambda b, i: (b, i, 0))
    tile_t = lambda n: pl.BlockSpec((1, 1, n, T), lambda b, i: (b, i, 0, 0))
    shapes, specs = [], []
    for name, n, dt in PROJ_OUTS:
        if name in ROW_MAJOR:
            shapes.append(jax.ShapeDtypeStruct((B, S, n), dt))
            specs.append(tile(n))
        else:
            shapes.append(jax.ShapeDtypeStruct((B, S // T, n, T), dt))
            specs.append(tile_t(n))
    outs = pl.pallas_call(
        _proj_kernel,
        out_shape=tuple(shapes),
        grid=(B, S // T),
        in_specs=[tile(D), full((1, D)), full((D, W_COLS)), full((1, KV_LATENT)),
                  full((KV_LATENT, BR_WIDTH)), full((KV_LATENT, BR_WIDTH)), full((1, LANES)),
                  tile(2 * LANES), tile(2 * LANES)],
        out_specs=tuple(specs),
        scratch_shapes=[pltpu.VMEM((T, D), BF16), pltpu.VMEM((1, LANES), F32)],
        compiler_params=pltpu.CompilerParams(dimension_semantics=("arbitrary", "arbitrary"),
                                             vmem_limit_bytes=VMEM_LIMIT),
        name="proj",
    )(x, g, w1, kvn, wuk, wuv, fb, cos, sin)
    return dict(zip([n for n, _, _ in PROJ_OUTS], outs))


def _key_query_index(qb, j):
    key = j * T + lax.broadcasted_iota(I32, (T, T), 0)
    qry = qb * T + lax.broadcasted_iota(I32, (T, T), 1)
    return key, qry


def _pair_pads(q_t):
    row = lax.broadcasted_iota(I32, (LANES, T), 0)
    top = row < HEAD_DIM
    zero = jnp.zeros((LANES, T), q_t.dtype)
    out = []
    for pair in range(2):
        chunk = q_t[pair * LANES:(pair + 1) * LANES, :]
        out.append(jnp.where(top, chunk, zero))
        out.append(jnp.where(top, zero, chunk))
    return out


def _masked_tiles(scores, valids):
    return [[s if v is None else jnp.where(v, s, 2 * NEG) for s, v in zip(scores_h, valids)]
            for scores_h in scores]


def _tile_tops(tiles):
    return [functools.reduce(jnp.maximum, [jnp.max(s, axis=0, keepdims=True) for s in tiles_h])
            for tiles_h in tiles]


def _weighted_values(tiles, values, m):
    weights = [[jnp.exp2(s - m[h]).astype(BF16) for s in tiles_h] for h, tiles_h in enumerate(tiles)]
    parts = [[_dot(values[i][h], w) for i, w in enumerate(weights_h)] for h, weights_h in enumerate(weights)]
    return [functools.reduce(jnp.add, parts_h) for parts_h in parts]


def _softmax_update(inputs_fn, m_scr, acc_scr, lazy=False):
    if not lazy:
        _softmax_update_exact(*inputs_fn(), m_scr, acc_scr)
        return
    scores, valids, values = inputs_fn()
    m_ref = [m_scr[h] for h in range(N_HEADS)]
    tiles = _masked_tiles(scores, valids)
    updates = _weighted_values(tiles, values, m_ref)
    tops = _tile_tops(tiles)
    overshoot = functools.reduce(jnp.maximum, [jnp.max(tops[h] - m_ref[h]) for h in range(N_HEADS)])
    safe = overshoot <= LAZY_LIMIT

    @pl.when(safe)
    def _():
        for h in range(N_HEADS):
            m_new = jnp.maximum(m_ref[h], tops[h])
            acc_scr[h] = jnp.exp2(m_ref[h] - m_new) * (acc_scr[h] + updates[h])
            m_scr[h] = m_new

    @pl.when(jnp.logical_not(safe))
    def _():
        _softmax_update_exact(*inputs_fn(), m_scr, acc_scr)


def _softmax_update_exact(scores, valids, values, m_scr, acc_scr):
    m_prev = [m_scr[h] for h in range(N_HEADS)]
    tiles = _masked_tiles(scores, valids)
    m_new = [jnp.maximum(m, top) for m, top in zip(m_prev, _tile_tops(tiles))]
    updates = _weighted_values(tiles, values, m_new)
    for h in range(N_HEADS):
        acc_scr[h] = jnp.exp2(m_prev[h] - m_new[h]) * acc_scr[h] + updates[h]
        m_scr[h] = m_new[h]


def _for_tile_groups(n, group_fn, group, first_fn=None):
    def full(i, c):
        group_fn([group * i + g for g in range(group)])
        return c

    if first_fn is not None:
        @pl.when(n >= group)
        def _():
            first_fn(list(range(group)))

    lax.fori_loop(0 if first_fn is None else 1, n // group, full, 0)
    size = group // 2
    while size:
        start = (n // (2 * size)) * (2 * size)

        @pl.when((n & size) != 0)
        def _(start=start, size=size):
            group_fn([start + g for g in range(size)])

        size //= 2


INT_MIN = -2 ** 31
LOW16 = -2 ** 15
KEY_ABOVE_NEG_INF = 0x80800000 - 2 ** 32


def _key_to_float(c):
    bits = c ^ ((c >> 31) & 0x7FFFFFFF)
    return lax.bitcast_convert_type(bits, F32)


def _count_rows(ref, n_steps, step_rows, pred, cand):
    pack = SUBLANES * (4 // ref.dtype.itemsize)
    cand_b = jnp.broadcast_to(cand, (pack, T)).astype(ref.dtype)
    one, zero = jnp.ones((pack, T), ref.dtype), jnp.zeros((pack, T), ref.dtype)

    def body(c, accs):
        blk = ref[pl.ds(pl.multiple_of(c * step_rows, step_rows), step_rows), :]
        accs = list(accs)
        for r in range(step_rows // pack):
            hit = jnp.where(pred(blk[r * pack:(r + 1) * pack, :], cand_b), one, zero)
            accs[r % len(accs)] = accs[r % len(accs)] + hit
        return tuple(accs)

    accs = lax.fori_loop(0, n_steps, body, (zero,) * 4)
    total = functools.reduce(jnp.add, [a.astype(I32) if a.dtype != F32 else a for a in accs])
    return jnp.sum(total, axis=0, keepdims=True).astype(F32)


_GE = lambda a, b: a >= b
_GT = lambda a, b: a > b


def _bisect(count_ge, target, lowest, n_bits, cnt_lowest, to_cand=lambda c: c):
    cnt = count_ge(to_cand(jnp.zeros((1, T), I32)))
    take = cnt >= target
    state = (jnp.where(take, 0, lowest).astype(I32), jnp.where(take, cnt, cnt_lowest),
             jnp.where(take, 0.0, cnt))

    def step(i, state):
        val, cnt_val, cnt_next = state
        cand = val + jnp.left_shift(jnp.int32(1), n_bits - 2 - i)
        cnt = count_ge(to_cand(cand))
        take = cnt >= target
        return jnp.where(take, cand, val), jnp.where(take, cnt, cnt_val), jnp.where(take, cnt_next, cnt)

    return lax.fori_loop(0, n_bits - 1, step, state)


def _demote_ties(sc_ref, n_steps, step_rows, thr, k):
    need = float(k) - _count_rows(sc_ref, n_steps, step_rows, _GT, thr)
    s_i = lax.broadcasted_iota(I32, (LANES, LANES), 0)
    j_i = lax.broadcasted_iota(I32, (LANES, LANES), 1)
    earlier = jnp.where(j_i < s_i, 1.0, 0.0).astype(BF16)

    def body(c, seen):
        s0 = pl.multiple_of(c * LANES, LANES)
        blk = sc_ref[pl.ds(s0, LANES), :]
        eq = blk == thr
        eq_f = jnp.where(eq, 1.0, 0.0)
        rank = seen + _dot(earlier, eq_f.astype(BF16))
        sc_ref[pl.ds(s0, LANES), :] = jnp.where(eq & (rank >= need), -jnp.inf, blk)
        return seen + jnp.sum(eq_f, axis=0, keepdims=True)

    lax.fori_loop(0, n_steps * (step_rows // LANES), body, jnp.zeros((1, T), F32))


def _top_k_threshold(sc_ref, n_steps, step_rows, k, n_valid):
    kf = float(k)
    key, cnt_key, _ = _bisect(lambda c: _count_rows(sc_ref, n_steps, step_rows, _GE, c), kf, INT_MIN, 32,
                              n_valid, to_cand=_key_to_float)
    thr = _key_to_float(jnp.maximum(key, KEY_ABOVE_NEG_INF))

    @pl.when(jnp.max(cnt_key) > kf)
    def _():
        _demote_ties(sc_ref, n_steps, step_rows, thr, k)

    return thr


def _split_keys(sc):
    bits = lax.bitcast_convert_type(sc, I32)
    key = bits ^ ((bits >> 31) & 0x7FFFFFFF)
    return (key >> 16).astype(I16), ((key & 0xFFFF) + LOW16).astype(I16)


def _top_k_threshold_split(sc_ref, hi_ref, lo_ref, n_steps, step_rows, k, n_valid):
    kf = float(k)
    rows = float(step_rows) * n_steps.astype(F32)
    high, cnt_high, above = _bisect(lambda c: _count_rows(hi_ref, n_steps, step_rows, _GE, c), kf,
                                    LOW16, 16, rows)
    high_b = jnp.broadcast_to(high, (step_rows, T)).astype(I16)

    def keep_bucket(c, carry):
        rs = pl.ds(pl.multiple_of(c * step_rows, step_rows), step_rows)
        lo_ref[rs, :] = jnp.where(hi_ref[rs, :] == high_b, lo_ref[rs, :], jnp.int16(LOW16))
        return carry

    lax.fori_loop(0, n_steps, keep_bucket, 0)
    low, cnt_low, _ = _bisect(lambda c: _count_rows(lo_ref, n_steps, step_rows, _GE, c), kf - above,
                              LOW16, 16, cnt_high - above)
    key = jnp.left_shift(high, 16) + (low - LOW16)
    thr = _key_to_float(jnp.maximum(key, KEY_ABOVE_NEG_INF))

    @pl.when(jnp.max(jnp.where(n_valid > kf, above + cnt_low, 0.0)) > kf)
    def _():
        _demote_ties(sc_ref, n_steps, step_rows, thr, k)

    return thr


def _softmax_init(m_scr, acc_scr):
    m_scr[...] = jnp.full(m_scr.shape, NEG, F32)
    acc_scr[...] = jnp.zeros(acc_scr.shape, F32)


def _softmax_out(acc_scr, h):
    acc = acc_scr[h]
    return acc[:HEAD_DIM, :] / jnp.maximum(acc[HEAD_DIM:HEAD_DIM + 1, :], 1e-30)


def _exact_softmax(s, valid):
    s = jnp.where(valid, s, NEG)
    m = jnp.max(s, axis=0, keepdims=True)
    e = jnp.where(valid, jnp.exp2(s - m), 0.0)
    return e / jnp.maximum(jnp.sum(e, axis=0, keepdims=True), 1e-30)


def _store_heads(o_ref, heads_t):
    for pair in range(2):
        o_ref[0, :, pair * LANES:(pair + 1) * LANES] = jnp.concatenate(
            [t[:HEAD_DIM, :] for t in heads_t[2 * pair:2 * pair + 2]], axis=0).T


def _head_values(v_ref, j):
    return [v_ref[0, j, h * HV_ROWS:(h + 1) * HV_ROWS, :] for h in range(N_HEADS)]


def _attn_scratch(rows):
    return [pltpu.VMEM((N_HEADS, 1, T), F32), pltpu.VMEM((N_HEADS, rows, T), F32)]


def _attn_params():
    return pltpu.CompilerParams(dimension_semantics=("arbitrary", "arbitrary"),
                                vmem_limit_bytes=VMEM_LIMIT)


def _keys_spec(S, units, unit):
    return pl.BlockSpec((1, S, units * LANES), lambda b, i: (b, 0, unit // units))


def _q_spec(rows):
    return pl.BlockSpec((1, 1, rows, T), lambda b, i: (b, i, 0, 0))


def _v_spec(S, rows):
    return pl.BlockSpec((1, S // T, rows, T), lambda b, i: (b, 0, 0, 0))


_OUT_SPEC = pl.BlockSpec((1, T, BR_WIDTH), lambda b, i: (b, i, 0))


def _fox_kernel(q_ref, k_ref, v_ref, o_ref, m_scr, acc_scr):
    qb = pl.program_id(1)
    _softmax_init(m_scr, acc_scr)

    def tiles(js, diagonal=False):
        def inputs():
            scores = [[_dot(k_ref[0, pl.ds(pl.multiple_of(j * T, T), T), h * LANES:(h + 1) * LANES],
                            q_ref[0, 0, h * LANES:(h + 1) * LANES, :]) for j in js]
                      for h in range(N_HEADS)]
            valids = [None] * len(js)
            if diagonal:
                key, qry = _key_query_index(qb, js[0])
                valids = [key <= qry]
            return scores, valids, [_head_values(v_ref, j) for j in js]

        _softmax_update(inputs, m_scr, acc_scr, lazy=not diagonal)

    tiles([qb], diagonal=True)
    _for_tile_groups(qb, tiles, group=4)
    _store_heads(o_ref, [_softmax_out(acc_scr, h) for h in range(N_HEADS)])


def _fox_call(pr):
    B, S, _ = pr['keys'].shape
    return pl.pallas_call(
        _fox_kernel,
        out_shape=jax.ShapeDtypeStruct((B, S, BR_WIDTH), F32),
        grid=(B, S // T),
        in_specs=[_q_spec(N_HEADS * LANES), _keys_spec(S, 4, A_FK), _v_spec(S, N_HEADS * HV_ROWS)],
        out_specs=_OUT_SPEC,
        scratch_shapes=_attn_scratch(HV_ROWS),
        compiler_params=_attn_params(),
        name="fox",
    )(pr['fq_t'], pr['keys'], pr['fv_t'])


def _sb_kernel(q_ref, k_ref, v_ref, o_ref, r_scr, acc_scr):
    qb = pl.program_id(1)
    qp = _pair_pads(q_ref[0, 0])
    r_scr[...] = jnp.zeros(r_scr.shape, F32)
    acc_scr[...] = jnp.zeros(acc_scr.shape, F32)
    s_i = lax.broadcasted_iota(I32, (T, T), 0)
    j_i = lax.broadcasted_iota(I32, (T, T), 1)
    later_keys = jnp.where(j_i > s_i, 1.0, 0.0).astype(BF16)

    def tile(j, masked):
        s0 = pl.multiple_of(j * T, T)
        strict = None
        if masked:
            key, qry = _key_query_index(qb, j)
            strict = key < qry
        heads = range(N_HEADS)
        r_prev = [r_scr[h] for h in heads]
        z = [_dot(k_ref[0, pl.ds(s0, T), (h // 2) * LANES:(h // 2 + 1) * LANES], qp[h]) for h in heads]
        log_take = [_log_sigmoid(z[h]) for h in heads]
        log_keep = [log_take[h] - z[h] for h in heads]
        if masked:
            log_keep = [jnp.where(strict, lk, 0.0) for lk in log_keep]
        pieces = [_split3(lk) for lk in log_keep]
        within = [functools.reduce(jnp.add, [_dot(later_keys, p) for p in pieces[h]])
                  for h in heads]
        a = [jnp.exp(log_take[h] + (r_prev[h] + within[h])) for h in heads]
        if masked:
            a = [jnp.where(strict, w, 0.0) for w in a]
        updates = [_dot(v_ref[0, j, h * HEAD_DIM:(h + 1) * HEAD_DIM, :], a[h].astype(BF16)) for h in heads]
        for h in heads:
            acc_scr[h] = acc_scr[h] + updates[h]
            r_scr[h] = r_prev[h] + within[h][0:1, :] + log_keep[h][0:1, :]

    def live():
        return jnp.max(r_scr[...]) > SB_CUTOFF

    def step(state):
        i, _ = state
        tile(qb - 1 - i, False)
        return i + 1, live()

    tile(qb, True)
    lax.while_loop(lambda st: (st[0] < qb) & st[1], step, (jnp.int32(0), live()))
    _store_heads(o_ref, [acc_scr[h] for h in range(N_HEADS)])


def _sb_call(pr):
    B, S, _ = pr['keys'].shape
    return pl.pallas_call(
        _sb_kernel,
        out_shape=jax.ShapeDtypeStruct((B, S, BR_WIDTH), F32),
        grid=(B, S // T),
        in_specs=[_q_spec(BR_WIDTH), _keys_spec(S, 2, A_SK), _v_spec(S, BR_WIDTH)],
        out_specs=_OUT_SPEC,
        scratch_shapes=_attn_scratch(HEAD_DIM),
        compiler_params=_attn_params(),
        name="stick_breaking",
    )(pr['sq_t'], pr['keys'], pr['sv_t'])


def _dsa_kernel(q_ref, k_ref, v_ref, qi_ref, ki_ref, sm_ref, o_ref, sc_scr, hi_scr, lo_scr, m_scr, acc_scr,
                *, k_sel):
    qb = pl.program_id(1)
    row = lax.broadcasted_iota(I32, (LANES, T), 0)
    qi = qi_ref[0, 0]
    zero = jnp.zeros_like(qi)
    qi_pad = [jnp.where((row >= IDX_DIM * h) & (row < IDX_DIM * (h + 1)), qi, zero)
              for h in range(IDX_HEADS)]
    wi = [sm_ref[0, 0, SM_WI + h:SM_WI + h + 1, :] for h in range(IDX_HEADS)]

    def score_tile(j, masked):
        s0 = pl.multiple_of(j * T, T)
        kt = ki_ref[0, pl.ds(s0, T), :]
        rel = [jnp.maximum(_dot(kt, qi_pad[h]), 0.0) for h in range(IDX_HEADS)]
        sc = functools.reduce(jnp.add, [wi[h] * rel[h] for h in range(IDX_HEADS)])
        sc = jnp.where(sc == 0.0, 0.0, sc)
        if masked:
            key, qry = _key_query_index(qb, j)
            sc = jnp.where(key <= qry, sc, -jnp.inf)
        sc_scr[pl.ds(s0, T), :] = sc
        hi_scr[pl.ds(s0, T), :], lo_scr[pl.ds(s0, T), :] = _split_keys(sc)

    _for_tile_groups(qb, lambda js: [score_tile(j, False) for j in js], group=2)
    score_tile(qb, True)

    @pl.when(qb % 2 == 0)
    def _():
        pad = pl.ds(pl.multiple_of((qb + 1) * T, T), T)
        sc_scr[pad, :] = jnp.full((T, T), -jnp.inf, F32)
        hi_scr[pad, :] = jnp.full((T, T), LOW16, I16)
        lo_scr[pad, :] = jnp.full((T, T), LOW16, I16)

    n_valid = (qb * T + 1 + lax.broadcasted_iota(I32, (1, T), 1)).astype(F32)
    thr = _top_k_threshold_split(sc_scr, hi_scr, lo_scr, qb // 2 + 1, 2 * T, k_sel, n_valid)

    qp = _pair_pads(q_ref[0, 0])
    _softmax_init(m_scr, acc_scr)

    def attn_tiles(js, lazy=True):
        def inputs():
            scores = [[_dot(k_ref[0, pl.ds(pl.multiple_of(j * T, T), T),
                                  (h // 2) * LANES:(h // 2 + 1) * LANES], qp[h]) for j in js]
                      for h in range(N_HEADS)]
            valids = [sc_scr[pl.ds(pl.multiple_of(j * T, T), T), :] >= thr for j in js]
            return scores, valids, [_head_values(v_ref, j) for j in js]

        _softmax_update(inputs, m_scr, acc_scr, lazy)

    attn_tiles([qb], lazy=False)
    _for_tile_groups(qb, attn_tiles, group=4, first_fn=functools.partial(attn_tiles, lazy=False))
    _store_heads(o_ref, [_softmax_out(acc_scr, h) for h in range(N_HEADS)])


def _dsa_call(pr):
    B, S, _ = pr['keys'].shape
    return pl.pallas_call(
        functools.partial(_dsa_kernel, k_sel=min(DSA_TOPK_MAX, S // 4)),
        out_shape=jax.ShapeDtypeStruct((B, S, BR_WIDTH), F32),
        grid=(B, S // T),
        in_specs=[_q_spec(BR_WIDTH), _keys_spec(S, 2, A_KA), _v_spec(S, N_HEADS * HV_ROWS),
                  _q_spec(LANES), _keys_spec(S, 1, A_KI), _q_spec(LANES)],
        out_specs=_OUT_SPEC,
        scratch_shapes=[pltpu.VMEM((S, T), F32), pltpu.VMEM((S, T), I16), pltpu.VMEM((S, T), I16)]
                       + _attn_scratch(HV_ROWS),
        compiler_params=_attn_params(),
        name="dsa",
    )(pr['dq_t'], pr['keys'], pr['va_t'], pr['qi_t'], pr['keys'], pr['sm_t'])


def _compress_kernel(xk_ref, xv_ref, pek_ref, pev_ref, w1k_ref, w1v_ref, w2k_ref, w2v_ref,
                     o_ref, ot_ref, *, nc):
    def hidden(x_ref, pe_ref, w1_ref):
        x = x_ref[0]
        first = _dot((x + pe_ref[0:1, :]).astype(BF16), w1_ref[0])
        second = _dot((x + pe_ref[1:2, :]).astype(BF16), w1_ref[1])
        pre = first + pltpu.roll(second, nc - 1, 0)
        return (pre * _sigmoid(pre)).astype(BF16)

    kvc = (_dot(hidden(xk_ref, pek_ref, w1k_ref), w2k_ref[...])
           + _dot(hidden(xv_ref, pev_ref, w1v_ref), w2v_ref[...]))
    o_ref[0] = kvc.astype(BF16)
    ot_ref[0] = kvc.T.astype(BF16)


def _compress_call(l, xk, xv, pek, pev, w1k, w1v, w2k, w2v):
    B, nc, width = xk.shape
    full = lambda shape: _layer_spec(l, shape, 1)
    tok = pl.BlockSpec((1, nc, width), lambda b: (b, 0, 0))
    return pl.pallas_call(
        functools.partial(_compress_kernel, nc=nc),
        out_shape=(jax.ShapeDtypeStruct((B, nc, LANES), BF16), jax.ShapeDtypeStruct((B, LANES, nc), BF16)),
        grid=(B,),
        in_specs=[tok, tok, full((2, width)), full((2, width)), full((2, width, CMP_HIDDEN)),
                  full((2, width, CMP_HIDDEN)), full((CMP_HIDDEN, LANES)), full((CMP_HIDDEN, LANES))],
        out_specs=(pl.BlockSpec((1, nc, LANES), lambda b: (b, 0, 0)),
                   pl.BlockSpec((1, LANES, nc), lambda b: (b, 0, 0))),
        compiler_params=pltpu.CompilerParams(dimension_semantics=("arbitrary",),
                                             vmem_limit_bytes=VMEM_LIMIT),
        name="nsa_compress",
    )(xk, xv, pek, pev, w1k, w1v, w2k, w2v)


def _nsa_kernel(q_ref, kvc_ref, kvct_ref, ksw_ref, vsw_ref, sm_ref, o_ref,
                imp_scr, sel_scr, m_scr, acc_scr, *, n_blk, n_sel, nbp):
    qb = pl.program_id(1)
    ncp = kvc_ref.shape[1]
    q_t = q_ref[0, 0]
    zero = jnp.zeros((HEAD_DIM, T), q_t.dtype)
    heads = [q_t[h * HEAD_DIM:(h + 1) * HEAD_DIM, :] for h in range(N_HEADS)]
    q_lo = [jnp.concatenate([q, zero], axis=0) for q in heads]
    q_hi = [jnp.concatenate([zero, q], axis=0) for q in heads]
    gate = lambda br, h: sm_ref[0, 0, SM_GATE + br * N_HEADS + h:SM_GATE + br * N_HEADS + h + 1, :]
    t_row = qb * T + lax.broadcasted_iota(I32, (1, T), 1)

    kvc = kvc_ref[0]
    cmp_end = lax.broadcasted_iota(I32, (ncp, T), 0) * CMP_STRIDE + (CMP_LEN - 1)
    cmp_ok = cmp_end <= t_row
    cmp_logits = [_dot(kvc, q_lo[h]) for h in range(N_HEADS)]
    pcs = [_exact_softmax(s, cmp_ok) for s in cmp_logits]
    o_cmp = [_dot(kvct_ref[0], pc.astype(BF16))[HEAD_DIM:, :] for pc in pcs]
    p_sum = functools.reduce(jnp.add, pcs)
    per_blk = SLC_BLOCK // CMP_STRIDE
    b_i = lax.broadcasted_iota(I32, (nbp, ncp), 0)
    c_i = lax.broadcasted_iota(I32, (nbp, ncp), 1)
    group = jnp.where(_div_pow2(c_i, per_blk) == b_i, 1.0, 0.0).astype(BF16)
    imp = _dot3x(group, p_sum)
    blk = lax.broadcasted_iota(I32, (nbp, T), 0)
    cur = _div_pow2(t_row, SLC_BLOCK)
    forced = (blk == 0) | (blk == cur) | (blk == cur - 1)
    imp = jnp.where(forced, FORCE_SCORE, jnp.where(blk <= cur, imp, -1.0))
    imp = jnp.where(blk < n_blk, imp, -2.0)
    imp_scr[...] = imp
    thr = _top_k_threshold(imp_scr, 1, nbp, n_sel, jnp.full((1, T), float(nbp), F32))
    sel_scr[...] = jnp.where(imp_scr[...] >= thr, 1.0, 0.0)

    _softmax_init(m_scr, acc_scr)
    blk_per_tile = T // SLC_BLOCK

    def slc_tiles(js, diagonal=False):
        def inputs():
            valids, scores = [], [[] for _ in range(N_HEADS)]
            for j in js:
                valid = jnp.concatenate(
                    [jnp.broadcast_to(sel_scr[pl.ds(j * blk_per_tile + b, 1), :], (SLC_BLOCK, T))
                     for b in range(blk_per_tile)], axis=0) > 0.5
                if diagonal:
                    key, qry = _key_query_index(qb, j)
                    valid = valid & (key <= qry)
                valids.append(valid)
                kt = ksw_ref[0, pl.ds(pl.multiple_of(j * T, T), T), :]
                for h in range(N_HEADS):
                    scores[h].append(_dot(kt, q_lo[h]))
            return scores, valids, [[vsw_ref[0, j, :HV_ROWS, :]] * N_HEADS for j in js]

        _softmax_update(inputs, m_scr, acc_scr, lazy=not diagonal)

    slc_tiles([qb], diagonal=True)
    _for_tile_groups(qb, slc_tiles, group=4)

    n_win = WINDOW // T + 1
    j0 = jnp.maximum(qb - (n_win - 1), 0)
    heads, wins = range(N_HEADS), range(n_win)
    oks = []
    for i in wins:
        key, qry = _key_query_index(qb, j0 + i)
        dist = qry - key
        oks.append((dist >= 0) & (dist < WINDOW))
    raw = [[_dot(ksw_ref[0, pl.ds(pl.multiple_of((j0 + i) * T, T), T), :], q_hi[h]) for i in wins]
           for h in heads]
    logits = [[jnp.where(oks[i], raw[h][i], NEG) for i in wins] for h in heads]
    tops = _tile_tops(logits)
    es = [[jnp.where(oks[i], jnp.exp2(logits[h][i] - tops[h]), 0.0) for i in wins] for h in heads]
    dens = [functools.reduce(jnp.add, [jnp.sum(e, axis=0, keepdims=True) for e in es[h]]) for h in heads]
    pws = [[(es[h][i] / jnp.maximum(dens[h], 1e-30)).astype(BF16) for i in wins] for h in heads]
    o_win = [functools.reduce(jnp.add, [_dot(vsw_ref[0, j0 + i, HV_ROWS:, :], pws[h][i]) for i in wins])
             for h in heads]
    _store_heads(o_ref, [gate(0, h) * o_cmp[h] + gate(1, h) * _softmax_out(acc_scr, h)
                         + gate(2, h) * o_win[h] for h in heads])


def _nsa_call(pr, kvc, kvc_t):
    B, S, _ = pr['keys'].shape
    n_blk = S // SLC_BLOCK
    nbp = max(LANES, n_blk)
    ncp = kvc.shape[1]
    return pl.pallas_call(
        functools.partial(_nsa_kernel, n_blk=n_blk, n_sel=min(SLC_TOPN, n_blk), nbp=nbp),
        out_shape=jax.ShapeDtypeStruct((B, S, BR_WIDTH), F32),
        grid=(B, S // T),
        in_specs=[_q_spec(BR_WIDTH),
                  pl.BlockSpec((1, ncp, LANES), lambda b, i: (b, 0, 0)),
                  pl.BlockSpec((1, LANES, ncp), lambda b, i: (b, 0, 0)),
                  _keys_spec(S, 1, A_KSW), _v_spec(S, VSW_ROWS), _q_spec(LANES)],
        out_specs=_OUT_SPEC,
        scratch_shapes=[pltpu.VMEM((nbp, T), F32), pltpu.VMEM((nbp, T), F32)] + _attn_scratch(HV_ROWS),
        compiler_params=_attn_params(),
        name="nsa",
    )(pr['nq_t'], kvc, kvc_t, pr['keys'], pr['vsw_t'], pr['sm_t'])


def _mem_kv_kernel(mem_ref, g_ref, w_ref, o_ref):
    o_ref[0] = _dot(_rmsnorm(mem_ref[0], g_ref[...]).astype(BF16), w_ref[...]).astype(BF16)


def _mem_kv_call(l, mem, g, w):
    B, M, D = mem.shape
    return pl.pallas_call(
        _mem_kv_kernel,
        out_shape=jax.ShapeDtypeStruct((B, M, 2 * BR_WIDTH), BF16),
        grid=(B,),
        in_specs=[pl.BlockSpec((1, M, D), lambda b: (b, 0, 0)), _layer_spec(l, (1, D), 1),
                  _layer_spec(l, (D, 2 * BR_WIDTH), 1)],
        out_specs=pl.BlockSpec((1, M, 2 * BR_WIDTH), lambda b: (b, 0, 0)),
        name="mem_kv",
    )(mem, g, w)


def _out_kernel(x_ref, g_ref, ya_ref, yb_ref, yc_ref, yd_ref, mq_ref, mkv_ref, wz_ref, wm_ref,
                wb_ref, wo_ref, fg_ref, o_ref, h_scr, *, final):
    x = x_ref[0]
    h_scr[...] = _rmsnorm(x, g_ref[...]).astype(BF16)
    q = mq_ref[0]
    lane = lax.broadcasted_iota(I32, (T, LANES), 1)
    lo = lane < HEAD_DIM
    zero = jnp.zeros((T, LANES), q.dtype)
    heads = range(N_HEADS)
    chunks = [q[:, (h // 2) * LANES:(h // 2 + 1) * LANES] for h in heads]
    qh = [jnp.where(lo, c, zero) if h % 2 == 0 else jnp.where(lo, zero, c) for h, c in zip(heads, chunks)]
    mk = [mkv_ref[0, :, (h // 2) * LANES:(h // 2 + 1) * LANES] for h in heads]
    mv = [mkv_ref[0, :, BR_WIDTH + (h // 2) * LANES:BR_WIDTH + (h // 2 + 1) * LANES] for h in heads]
    s = [_dot_nt(qh[h], mk[h]) for h in heads]
    e = [jnp.exp(s[h] - jnp.max(s[h], axis=1, keepdims=True)) for h in heads]
    p = [(e[h] / jnp.sum(e[h], axis=1, keepdims=True)).astype(BF16) for h in heads]
    o = [_dot(p[h], mv[h]) for h in heads]
    y_e = jnp.concatenate([jnp.where(lo, o[0], o[1]), jnp.where(lo, o[2], o[3])], axis=1)
    branches = range(N_BRANCH)
    d = x.shape[1]
    ys_in = (ya_ref[0], yb_ref[0], yc_ref[0], yd_ref[0], y_e)
    z = [_dot(h_scr[...], wz_ref[n]) for n in branches]
    ys = [(ys_in[n] * (z[n] * _sigmoid(z[n]))).astype(BF16) for n in branches]
    gates = [_sigmoid(_dot(h_scr[...], wm_ref[:, n * d:(n + 1) * d])) for n in branches]
    projected = [_dot(ys[n], wb_ref[n]) for n in branches]
    merged = functools.reduce(jnp.add, [gates[n] * projected[n] for n in branches])
    out = x + _dot(merged.astype(BF16), wo_ref[...])
    o_ref[0] = _rmsnorm(out, fg_ref[...]) if final else out


def _out_call(l, x, g, ys, keys, mkv, wz, wm, wb, wo, fg, final):
    B, S, D = x.shape
    M = mkv.shape[1]
    full = lambda shape: _layer_spec(l, shape, 2)
    tile = lambda n: pl.BlockSpec((1, T, n), lambda b, i: (b, i, 0))
    return pl.pallas_call(
        functools.partial(_out_kernel, final=final),
        out_shape=jax.ShapeDtypeStruct((B, S, D), F32),
        grid=(B, S // T),
        in_specs=[tile(D), full((1, D))] + [tile(BR_WIDTH)] * 4
                 + [pl.BlockSpec((1, T, BR_WIDTH), lambda b, i: (b, i, A_MQ // 2)),
                    pl.BlockSpec((1, M, 2 * BR_WIDTH), lambda b, i: (b, 0, 0)),
                    full((N_BRANCH, D, BR_WIDTH)), full((D, N_BRANCH * D)), full((N_BRANCH, BR_WIDTH, D)),
                    full((D, D)), pl.BlockSpec((1, D), lambda b, i: (0, 0))],
        out_specs=tile(D),
        scratch_shapes=[pltpu.VMEM((T, D), BF16)],
        compiler_params=pltpu.CompilerParams(dimension_semantics=("arbitrary", "arbitrary"),
                                             vmem_limit_bytes=VMEM_LIMIT),
        name="merge_out",
    )(x, g, *ys, keys, mkv, wz, wm, wb, wo, fg)


def kernel(x, mem, positions, norm_g, w_in, kv_norm, w_uk, w_uv, fox_bias, nsa_pe_k, nsa_pe_v,
           nsa_wc1_k, nsa_wc2_k, nsa_wc1_v, nsa_wc2_v, mem_norm, w_mem_kv, w_branch, w_out, final_norm):
    B, S, D = x.shape
    depth = norm_g.shape[0]
    assert S % (2 * T) == 0 and S >= WINDOW + T and WINDOW % T == 0 and D == 1024
    cos, sin = _rope_tables(positions)
    w1, wz, wm = _proj_weight(w_in)
    fb = jnp.zeros((depth, 1, LANES), F32).at[:, 0, SM_CUM:SM_CUM + N_HEADS].set(fox_bias)
    nc = S // CMP_STRIDE
    tok_w = CMP_STRIDE * NSA_KV_DIM
    pad_k = jnp.zeros((depth, CMP_HIDDEN, LANES), F32).at[..., :NSA_KV_DIM].set(nsa_wc2_k).astype(BF16)
    pad_v = jnp.zeros((depth, CMP_HIDDEN, LANES), F32).at[..., NSA_KV_DIM:].set(nsa_wc2_v).astype(BF16)
    g, kvn, mn = norm_g[:, None, :], kv_norm[:, None, :], mem_norm[:, None, :]
    wuk, wuv, wmkv = w_uk.astype(BF16), w_uv.astype(BF16), w_mem_kv.astype(BF16)
    pek, pev = nsa_pe_k.reshape(depth, 2, tok_w), nsa_pe_v.reshape(depth, 2, tok_w)
    w1k = nsa_wc1_k.reshape(depth, 2, tok_w, CMP_HIDDEN).astype(BF16)
    w1v = nsa_wc1_v.reshape(depth, 2, tok_w, CMP_HIDDEN).astype(BF16)
    wb, wo = w_branch.astype(BF16), w_out.astype(BF16)
    for l in range(depth):
        pr = _proj_call(l, x, g, w1, kvn, wuk, wuv, fb, cos, sin)
        kvc, kvc_t = _compress_call(l, pr['kc_tok'].reshape(B, nc, tok_w), pr['vc_tok'].reshape(B, nc, tok_w),
                                    pek, pev, w1k, w1v, pad_k, pad_v)
        ys = (_dsa_call(pr), _fox_call(pr), _sb_call(pr), _nsa_call(pr, kvc, kvc_t))
        mkv = _mem_kv_call(l, mem, mn, wmkv)
        x = _out_call(l, x, g, ys, pr['keys'], mkv, wz, wm, wb, wo, final_norm[None], l == depth - 1)
    return x
```

```python
import functools

import jax
import jax.numpy as jnp
from jax import lax
from jax.experimental import pallas as pl
from jax.experimental.pallas import tpu as pltpu

F32 = jnp.float32
BF16 = jnp.bfloat16
I32 = jnp.int32
I16 = jnp.int16

LANES = 128
SUBLANES = 8
N_HEADS = 4
HEAD_DIM = 64
BR_WIDTH = N_HEADS * HEAD_DIM
N_BRANCH = 5
ROPE_THETA = 10000.0
EPS = 1e-6
DSA_TOPK_MAX = 256
IDX_HEADS = 4
IDX_DIM = 32
KV_LATENT = 128
NSA_KV_DIM = 64
CMP_LEN = 32
CMP_STRIDE = 16
CMP_HIDDEN = 128
SLC_BLOCK = 64
SLC_TOPN = 16
WINDOW = 512
FORCE_SCORE = 1e4
Q_SCALE = HEAD_DIM ** -0.5
LOG2E = 1.4426950408889634
Q_SCALE2 = Q_SCALE * LOG2E
IDX_SCALE = (IDX_DIM ** -0.5) * (IDX_HEADS ** -0.5)
NEG = -1e30
SB_CUTOFF = -104.0
LAZY_LIMIT = 64.0
VMEM_LIMIT = 56 * 1024 * 1024
T = 256
ONES_ROWS = 16
HV_ROWS = HEAD_DIM + ONES_ROWS
VSW_ROWS = HV_ROWS + NSA_KV_DIM

IN_LAYOUT = (
    ('dsa_q', BR_WIDTH), ('dsa_ckv', KV_LATENT), ('idx_q', IDX_HEADS * IDX_DIM), ('idx_k', IDX_DIM),
    ('idx_w', IDX_HEADS), ('dsa_z', BR_WIDTH),
    ('fox_q', BR_WIDTH), ('fox_k', BR_WIDTH), ('fox_v', BR_WIDTH), ('fox_f', N_HEADS), ('fox_z', BR_WIDTH),
    ('sb_q', BR_WIDTH), ('sb_k', BR_WIDTH), ('sb_v', BR_WIDTH), ('sb_z', BR_WIDTH),
    ('nsa_q', BR_WIDTH), ('nsa_kc', NSA_KV_DIM), ('nsa_vc', NSA_KV_DIM), ('nsa_ks', NSA_KV_DIM),
    ('nsa_vs', NSA_KV_DIM), ('nsa_kw', NSA_KV_DIM), ('nsa_vw', NSA_KV_DIM), ('nsa_g', 3 * N_HEADS),
    ('nsa_z', BR_WIDTH), ('mem_q', BR_WIDTH), ('mem_z', BR_WIDTH), ('merge', N_BRANCH * 1024),
)

A_FK, A_KA, A_SK, A_KI, A_KSW, A_MQ = 0, 4, 6, 8, 9, 10
A_COLS = 12 * LANES
SM_WI, SM_CUM, SM_GATE = 0, 4, 8
FOX_CK, FOX_ONE, FOX_END = HEAD_DIM, HEAD_DIM + 3, HEAD_DIM + 6


def _dot(a, b):
    return jnp.dot(a, b, preferred_element_type=F32)


def _dot_nt(a, b):
    return lax.dot_general(a, b, (((1,), (1,)), ((), ())), preferred_element_type=F32)


def _split3(x):
    hi = x.astype(BF16)
    r1 = x - hi.astype(F32)
    mid = r1.astype(BF16)
    lo = (r1 - mid.astype(F32)).astype(BF16)
    return hi, mid, lo


def _dot3x(m01, x):
    hi, mid, lo = _split3(x)
    return _dot(m01, hi) + _dot(m01, mid) + _dot(m01, lo)


def _div_pow2(x, n):
    assert n & (n - 1) == 0
    return x >> (n.bit_length() - 1)


def _sigmoid(x):
    return 1.0 / (1.0 + jnp.exp(-x))


def _log_sigmoid(x):
    return jnp.minimum(x, 0.0) - jnp.log(1.0 + jnp.exp(-jnp.abs(x)))


def _rmsnorm(x, g):
    return x * lax.rsqrt(jnp.mean(x * x, axis=-1, keepdims=True) + EPS) * g


def _rope128(x, cos, sin_signed, half):
    lane = lax.broadcasted_iota(I32, x.shape, 1)
    first = (lane & (2 * half - 1)) < half
    partner = jnp.where(first, pltpu.roll(x, LANES - half, 1), pltpu.roll(x, half, 1))
    return x * cos + partner * sin_signed


def _rope_table_kernel(pos_ref, inv_ref, sgn_ref, cos_ref, sin_ref):
    ang = pos_ref[0] * inv_ref[...]
    cos_ref[0] = jnp.cos(ang)
    sin_ref[0] = jnp.sin(ang) * sgn_ref[...]


def _rope_tables(positions):
    B, S = positions.shape
    lane = jnp.arange(LANES)

    def inv_row(dh):
        inv = ROPE_THETA ** (-jnp.arange(0, dh, 2, dtype=F32) / dh)
        return inv[(lane % dh) % (dh // 2)]

    def sgn_row(dh):
        return jnp.where((lane % dh) < dh // 2, -1.0, 1.0).astype(F32)

    inv = jnp.concatenate([inv_row(HEAD_DIM), inv_row(IDX_DIM)])[None, :]
    sgn = jnp.concatenate([sgn_row(HEAD_DIM), sgn_row(IDX_DIM)])[None, :]
    pos = positions.astype(F32)[..., None]
    row = pl.BlockSpec((1, 2 * LANES), lambda b, i: (0, 0))
    tab = pl.BlockSpec((1, T, 2 * LANES), lambda b, i: (b, i, 0))
    return pl.pallas_call(
        _rope_table_kernel,
        out_shape=(jax.ShapeDtypeStruct((B, S, 2 * LANES), F32),) * 2,
        grid=(B, S // T),
        in_specs=[pl.BlockSpec((1, T, 1), lambda b, i: (b, i, 0)), row, row],
        out_specs=(tab, tab),
        name="rope_tables",
    )(pos, inv, sgn)


W_DQ, W_CKV, W_QI, W_KI, W_SM, W_FQ, W_SQ, W_NQ, W_MQ, W_KVC, W_KSW, W_VSW = (
    0, 256, 384, 512, 640, 768, 1536, 2304, 2560, 2816, 2944, 3072)
W_COLS = 3200

PROJ_OUTS = (
    ('keys', A_COLS, BF16), ('kc_tok', NSA_KV_DIM, F32), ('vc_tok', NSA_KV_DIM, F32),
    ('dq_t', BR_WIDTH, BF16), ('fq_t', N_HEADS * LANES, BF16), ('sq_t', BR_WIDTH, BF16),
    ('nq_t', BR_WIDTH, BF16), ('qi_t', LANES, BF16), ('sm_t', LANES, F32),
    ('va_t', N_HEADS * HV_ROWS, BF16), ('fv_t', N_HEADS * HV_ROWS, BF16), ('sv_t', BR_WIDTH, BF16),
    ('vsw_t', VSW_ROWS, BF16),
)
ROW_MAJOR = ('keys', 'kc_tok', 'vc_tok')


def _proj_kernel(x_ref, g_ref, w_ref, kvn_ref, wuk_ref, wuv_ref, fb_ref, cos_ref, sin_ref,
                 *refs):
    out = dict(zip([n for n, _, _ in PROJ_OUTS], refs))
    h_scr, carry_scr = refs[len(PROJ_OUTS):]
    keys = out['keys']
    h_scr[...] = _rmsnorm(x_ref[0], g_ref[...]).astype(BF16)
    c64, s64 = cos_ref[0, :, :LANES], sin_ref[0, :, :LANES]
    c32, s32 = cos_ref[0, :, LANES:], sin_ref[0, :, LANES:]
    lane = lax.broadcasted_iota(I32, (T, LANES), 1)
    ones_rows = jnp.ones((ONES_ROWS, T), BF16)

    def proj(c0, n):
        return _dot(h_scr[...], w_ref[:, c0:c0 + n])

    def put(unit, val):
        keys[0, :, unit * LANES:unit * LANES + val.shape[1]] = val.astype(BF16)

    def rope64(v, j):
        return _rope128(v[:, j * LANES:(j + 1) * LANES], c64, s64, HEAD_DIM // 2)

    def put_t(ref, row0, chunk):
        ref[0, 0, row0:row0 + LANES, :] = chunk.T.astype(ref.dtype)

    def put_values_t(ref, v):
        for j in range(2):
            pair_t = v[:, j * LANES:(j + 1) * LANES].T.astype(BF16)
            for odd in range(2):
                r0 = (2 * j + odd) * HV_ROWS
                ref[0, 0, r0:r0 + HEAD_DIM, :] = pair_t[odd * HEAD_DIM:(odd + 1) * HEAD_DIM, :]
                ref[0, 0, r0 + HEAD_DIM:r0 + HV_ROWS, :] = ones_rows

    p = proj(W_DQ, BR_WIDTH)
    for j in range(2):
        put_t(out['dq_t'], j * LANES, rope64(p, j) * Q_SCALE2)
    c_kv = _rmsnorm(proj(W_CKV, KV_LATENT), kvn_ref[...]).astype(BF16)
    k_a = _dot(c_kv, wuk_ref[...])
    for j in range(2):
        put(A_KA + j, rope64(k_a, j))
    put_values_t(out['va_t'], _dot(c_kv, wuv_ref[...]))
    put_t(out['qi_t'], 0, _rope128(proj(W_QI, LANES), c32, s32, IDX_DIM // 2))
    put(A_KI, _rope128(proj(W_KI, LANES), c32, s32, IDX_DIM // 2))
    p = proj(W_SM, LANES)
    log_f = _log_sigmoid(p + fb_ref[...])
    r_i = lax.broadcasted_iota(I32, (T, T), 0)
    c_i = lax.broadcasted_iota(I32, (T, T), 1)
    tri = jnp.where(r_i >= c_i, 1.0, 0.0).astype(BF16)

    @pl.when(pl.program_id(1) == 0)
    def _():
        carry_scr[...] = jnp.zeros_like(carry_scr)

    cum = _dot3x(tri, log_f) + carry_scr[...]
    carry_scr[...] = cum[T - 1:T, :]
    small_t = jnp.where(lane < SM_CUM, p * IDX_SCALE,
                        jnp.where(lane < SM_GATE, cum, _sigmoid(p))).T
    out['sm_t'][0, 0] = small_t
    q_f = proj(W_FQ, BR_WIDTH) * Q_SCALE2
    cum2, cum2_t = cum * LOG2E, small_t * LOG2E
    k_f = proj(W_FQ + 256, BR_WIDTH)
    row = lax.broadcasted_iota(I32, (HEAD_DIM, T), 0)
    for h in range(N_HEADS):
        pair, odd = h // 2, h % 2
        k_chunk = k_f[:, pair * LANES:(pair + 1) * LANES]
        k_h = pltpu.roll(k_chunk, HEAD_DIM, 1) if odd else k_chunk
        ck = [c.astype(F32) for c in _split3(cum2[:, SM_CUM + h:SM_CUM + h + 1])]
        k_aug = jnp.where(lane < HEAD_DIM, k_h,
                          jnp.where(lane == FOX_CK, ck[0],
                                    jnp.where(lane == FOX_CK + 1, ck[1],
                                              jnp.where(lane == FOX_CK + 2, ck[2],
                                                        jnp.where(lane < FOX_END, 1.0, 0.0)))))
        put(A_FK + h, k_aug)
        q_ht = q_f[:, pair * LANES:(pair + 1) * LANES].T[odd * HEAD_DIM:(odd + 1) * HEAD_DIM, :]
        cq = [c.astype(F32) for c in _split3(cum2_t[SM_CUM + h:SM_CUM + h + 1, :])]
        aug = jnp.where(row < 3, -1.0,
                        jnp.where(row == 3, cq[0], jnp.where(row == 4, cq[1],
                                                             jnp.where(row == 5, cq[2], 0.0))))
        out['fq_t'][0, 0, h * LANES:(h + 1) * LANES, :] = jnp.concatenate(
            [q_ht, aug], axis=0).astype(BF16)
    put_values_t(out['fv_t'], proj(W_FQ + 512, BR_WIDTH))
    p = proj(W_SQ, BR_WIDTH) * Q_SCALE
    for j in range(2):
        put_t(out['sq_t'], j * LANES, p[:, j * LANES:(j + 1) * LANES])
    put(A_SK, proj(W_SQ + 256, BR_WIDTH))
    p = proj(W_SQ + 512, BR_WIDTH)
    for j in range(2):
        put_t(out['sv_t'], j * LANES, p[:, j * LANES:(j + 1) * LANES])
    put(A_MQ, proj(W_MQ, BR_WIDTH) * Q_SCALE)
    p = proj(W_NQ, BR_WIDTH)
    for j in range(2):
        put_t(out['nq_t'], j * LANES, rope64(p, j) * Q_SCALE2)
    p = proj(W_KVC, LANES)
    out['kc_tok'][0] = rope64(p, 0)[:, :NSA_KV_DIM]
    out['vc_tok'][0] = p[:, NSA_KV_DIM:]
    put(A_KSW, rope64(proj(W_KSW, LANES), 0))
    vsw_t = proj(W_VSW, LANES).T.astype(BF16)
    out['vsw_t'][0, 0, :NSA_KV_DIM, :] = vsw_t[:NSA_KV_DIM, :]
    out['vsw_t'][0, 0, NSA_KV_DIM:HV_ROWS, :] = ones_rows
    out['vsw_t'][0, 0, HV_ROWS:, :] = vsw_t[NSA_KV_DIM:, :]


def _proj_weight(w_in):
    off, o = {}, 0
    for name, n in IN_LAYOUT:
        off[name] = (o, n)
        o += n

    def col(name):
        s, n = off[name]
        return w_in[..., s:s + n]

    zeros = lambda n: jnp.zeros(w_in.shape[:-1] + (n,), w_in.dtype)
    groups = [col('dsa_q'), col('dsa_ckv'), col('idx_q'), col('idx_k'), col('idx_k'), col('idx_k'),
              col('idx_k'), col('idx_w'), col('fox_f'), col('nsa_g'), zeros(LANES - 20),
              col('fox_q'), col('fox_k'), col('fox_v'), col('sb_q'), col('sb_k'), col('sb_v'),
              col('nsa_q'), col('mem_q'), col('nsa_kc'), col('nsa_vc'), col('nsa_ks'), col('nsa_kw'),
              col('nsa_vs'), col('nsa_vw')]
    w1 = jnp.concatenate(groups, axis=-1).astype(BF16)
    wz = jnp.stack([col(n) for n in ('dsa_z', 'fox_z', 'sb_z', 'nsa_z', 'mem_z')], axis=-3).astype(BF16)
    return w1, wz, col('merge').astype(BF16)


def _layer_spec(l, shape, grid_rank):
    if grid_rank == 1:
        return pl.BlockSpec((None,) + shape, lambda b: (l,) + (0,) * len(shape))
    return pl.BlockSpec((None,) + shape, lambda b, i: (l,) + (0,) * len(shape))


def _proj_call(l, x, g, w1, kvn, wuk, wuv, fb, cos, sin):
    B, S, D = x.shape
    full = lambda shape: _layer_spec(l, shape, 2)
    tile = lambda n: pl.BlockSpec((1, T, n), lambda b, i: (b, i, 0))
    tile_t = lambda n: pl.BlockSpec((1, 1, n, T), lambda b, i: (b, i, 0, 0))
    shapes, specs = [], []
    for name, n, dt in PROJ_OUTS:
        if name in ROW_MAJOR:
            shapes.append(jax.ShapeDtypeStruct((B, S, n), dt))
            specs.append(tile(n))
        else:
            shapes.append(jax.ShapeDtypeStruct((B, S // T, n, T), dt))
            specs.append(tile_t(n))
    outs = pl.pallas_call(
        _proj_kernel,
        out_shape=tuple(shapes),
        grid=(B, S // T),
        in_specs=[tile(D), full((1, D)), full((D, W_COLS)), full((1, KV_LATENT)),
                  full((KV_LATENT, BR_WIDTH)), full((KV_LATENT, BR_WIDTH)), full((1, LANES)),
                  tile(2 * LANES), tile(2 * LANES)],
        out_specs=tuple(specs),
        scratch_shapes=[pltpu.VMEM((T, D), BF16), pltpu.VMEM((1, LANES), F32)],
        compiler_params=pltpu.CompilerParams(dimension_semantics=("arbitrary", "arbitrary"),
                                             vmem_limit_bytes=VMEM_LIMIT),
        name="proj",
    )(x, g, w1, kvn, wuk, wuv, fb, cos, sin)
    return dict(zip([n for n, _, _ in PROJ_OUTS], outs))


def _key_query_index(qb, j):
    key = j * T + lax.broadcasted_iota(I32, (T, T), 0)
    qry = qb * T + lax.broadcasted_iota(I32, (T, T), 1)
    return key, qry


def _pair_pads(q_t):
    row = lax.broadcasted_iota(I32, (LANES, T), 0)
    top = row < HEAD_DIM
    zero = jnp.zeros((LANES, T), q_t.dtype)
    out = []
    for pair in range(2):
        chunk = q_t[pair * LANES:(pair + 1) * LANES, :]
        out.append(jnp.where(top, chunk, zero))
        out.append(jnp.where(top, zero, chunk))
    return out


def _masked_tiles(scores, valids):
    return [[s if v is None else jnp.where(v, s, 2 * NEG) for s, v in zip(scores_h, valids)]
            for scores_h in scores]


def _tile_tops(tiles):
    return [functools.reduce(jnp.maximum, [jnp.max(s, axis=0, keepdims=True) for s in tiles_h])
            for tiles_h in tiles]


def _weighted_values(tiles, values, m):
    weights = [[jnp.exp2(s - m[h]).astype(BF16) for s in tiles_h] for h, tiles_h in enumerate(tiles)]
    parts = [[_dot(values[i][h], w) for i, w in enumerate(weights_h)] for h, weights_h in enumerate(weights)]
    return [functools.reduce(jnp.add, parts_h) for parts_h in parts]


def _softmax_update(inputs_fn, m_scr, acc_scr, lazy=False):
    if not lazy:
        _softmax_update_exact(*inputs_fn(), m_scr, acc_scr)
        return
    scores, valids, values = inputs_fn()
    m_ref = [m_scr[h] for h in range(N_HEADS)]
    tiles = _masked_tiles(scores, valids)
    updates = _weighted_values(tiles, values, m_ref)
    tops = _tile_tops(tiles)
    overshoot = functools.reduce(jnp.maximum, [jnp.max(tops[h] - m_ref[h]) for h in range(N_HEADS)])
    safe = overshoot <= LAZY_LIMIT

    @pl.when(safe)
    def _():
        for h in range(N_HEADS):
            m_new = jnp.maximum(m_ref[h], tops[h])
            acc_scr[h] = jnp.exp2(m_ref[h] - m_new) * (acc_scr[h] + updates[h])
            m_scr[h] = m_new

    @pl.when(jnp.logical_not(safe))
    def _():
        _softmax_update_exact(*inputs_fn(), m_scr, acc_scr)


def _softmax_update_exact(scores, valids, values, m_scr, acc_scr):
    m_prev = [m_scr[h] for h in range(N_HEADS)]
    tiles = _masked_tiles(scores, valids)
    m_new = [jnp.maximum(m, top) for m, top in zip(m_prev, _tile_tops(tiles))]
    updates = _weighted_values(tiles, values, m_new)
    for h in range(N_HEADS):
        acc_scr[h] = jnp.exp2(m_prev[h] - m_new[h]) * acc_scr[h] + updates[h]
        m_scr[h] = m_new[h]


def _for_tile_groups(n, group_fn, group, first_fn=None):
    def full(i, c):
        group_fn([group * i + g for g in range(group)])
        return c

    if first_fn is not None:
        @pl.when(n >= group)
        def _():
            first_fn(list(range(group)))

    lax.fori_loop(0 if first_fn is None else 1, n // group, full, 0)
    size = group // 2
    while size:
        start = (n // (2 * size)) * (2 * size)

        @pl.when((n & size) != 0)
        def _(start=start, size=size):
            group_fn([start + g for g in range(size)])

        size //= 2


INT_MIN = -2 ** 31
LOW16 = -2 ** 15
KEY_ABOVE_NEG_INF = 0x80800000 - 2 ** 32


def _key_to_float(c):
    bits = c ^ ((c >> 31) & 0x7FFFFFFF)
    return lax.bitcast_convert_type(bits, F32)


def _count_rows(ref, n_steps, step_rows, pred, cand):
    pack = SUBLANES * (4 // ref.dtype.itemsize)
    cand_b = jnp.broadcast_to(cand, (pack, T)).astype(ref.dtype)
    one, zero = jnp.ones((pack, T), ref.dtype), jnp.zeros((pack, T), ref.dtype)

    def body(c, accs):
        blk = ref[pl.ds(pl.multiple_of(c * step_rows, step_rows), step_rows), :]
        accs = list(accs)
        for r in range(step_rows // pack):
            hit = jnp.where(pred(blk[r * pack:(r + 1) * pack, :], cand_b), one, zero)
            accs[r % len(accs)] = accs[r % len(accs)] + hit
        return tuple(accs)

    accs = lax.fori_loop(0, n_steps, body, (zero,) * 4)
    total = functools.reduce(jnp.add, [a.astype(I32) if a.dtype != F32 else a for a in accs])
    return jnp.sum(total, axis=0, keepdims=True).astype(F32)


_GE = lambda a, b: a >= b
_GT = lambda a, b: a > b


def _bisect(count_ge, target, lowest, n_bits, cnt_lowest, to_cand=lambda c: c):
    cnt = count_ge(to_cand(jnp.zeros((1, T), I32)))
    take = cnt >= target
    state = (jnp.where(take, 0, lowest).astype(I32), jnp.where(take, cnt, cnt_lowest),
             jnp.where(take, 0.0, cnt))

    def step(i, state):
        val, cnt_val, cnt_next = state
        cand = val + jnp.left_shift(jnp.int32(1), n_bits - 2 - i)
        cnt = count_ge(to_cand(cand))
        take = cnt >= target
        return jnp.where(take, cand, val), jnp.where(take, cnt, cnt_val), jnp.where(take, cnt_next, cnt)

    return lax.fori_loop(0, n_bits - 1, step, state)


def _demote_ties(sc_ref, n_steps, step_rows, thr, k):
    need = float(k) - _count_rows(sc_ref, n_steps, step_rows, _GT, thr)
    s_i = lax.broadcasted_iota(I32, (LANES, LANES), 0)
    j_i = lax.broadcasted_iota(I32, (LANES, LANES), 1)
    earlier = jnp.where(j_i < s_i, 1.0, 0.0).astype(BF16)

    def body(c, seen):
        s0 = pl.multiple_of(c * LANES, LANES)
        blk = sc_ref[pl.ds(s0, LANES), :]
        eq = blk == thr
        eq_f = jnp.where(eq, 1.0, 0.0)
        rank = seen + _dot(earlier, eq_f.astype(BF16))
        sc_ref[pl.ds(s0, LANES), :] = jnp.where(eq & (rank >= need), -jnp.inf, blk)
        return seen + jnp.sum(eq_f, axis=0, keepdims=True)

    lax.fori_loop(0, n_steps * (step_rows // LANES), body, jnp.zeros((1, T), F32))


def _top_k_threshold(sc_ref, n_steps, step_rows, k, n_valid):
    kf = float(k)
    key, cnt_key, _ = _bisect(lambda c: _count_rows(sc_ref, n_steps, step_rows, _GE, c), kf, INT_MIN, 32,
                              n_valid, to_cand=_key_to_float)
    thr = _key_to_float(jnp.maximum(key, KEY_ABOVE_NEG_INF))

    @pl.when(jnp.max(cnt_key) > kf)
    def _():
        _demote_ties(sc_ref, n_steps, step_rows, thr, k)

    return thr


def _split_keys(sc):
    bits = lax.bitcast_convert_type(sc, I32)
    key = bits ^ ((bits >> 31) & 0x7FFFFFFF)
    return (key >> 16).astype(I16), ((key & 0xFFFF) + LOW16).astype(I16)


def _top_k_threshold_split(sc_ref, hi_ref, lo_ref, n_steps, step_rows, k, n_valid):
    kf = float(k)
    rows = float(step_rows) * n_steps.astype(F32)
    high, cnt_high, above = _bisect(lambda c: _count_rows(hi_ref, n_steps, step_rows, _GE, c), kf,
                                    LOW16, 16, rows)
    high_b = jnp.broadcast_to(high, (step_rows, T)).astype(I16)

    def keep_bucket(c, carry):
        rs = pl.ds(pl.multiple_of(c * step_rows, step_rows), step_rows)
        lo_ref[rs, :] = jnp.where(hi_ref[rs, :] == high_b, lo_ref[rs, :], jnp.int16(LOW16))
        return carry

    lax.fori_loop(0, n_steps, keep_bucket, 0)
    low, cnt_low, _ = _bisect(lambda c: _count_rows(lo_ref, n_steps, step_rows, _GE, c), kf - above,
                              LOW16, 16, cnt_high - above)
    key = jnp.left_shift(high, 16) + (low - LOW16)
    thr = _key_to_float(jnp.maximum(key, KEY_ABOVE_NEG_INF))

    @pl.when(jnp.max(jnp.where(n_valid > kf, above + cnt_low, 0.0)) > kf)
    def _():
        _demote_ties(sc_ref, n_steps, step_rows, thr, k)

    return thr


def _softmax_init(m_scr, acc_scr):
    m_scr[...] = jnp.full(m_scr.shape, NEG, F32)
    acc_scr[...] = jnp.zeros(acc_scr.shape, F32)


def _softmax_out(acc_scr, h):
    acc = acc_scr[h]
    return acc[:HEAD_DIM, :] / jnp.maximum(acc[HEAD_DIM:HEAD_DIM + 1, :], 1e-30)


def _store_heads(o_ref, heads_t):
    for pair in range(2):
        o_ref[0, :, pair * LANES:(pair + 1) * LANES] = jnp.concatenate(
            [t[:HEAD_DIM, :] for t in heads_t[2 * pair:2 * pair + 2]], axis=0).T


def _head_values(v_ref, j):
    return [v_ref[0, j, h * HV_ROWS:(h + 1) * HV_ROWS, :] for h in range(N_HEADS)]


def _attn_scratch(rows):
    return [pltpu.VMEM((N_HEADS, 1, T), F32), pltpu.VMEM((N_HEADS, rows, T), F32)]


def _attn_params():
    return pltpu.CompilerParams(dimension_semantics=("arbitrary", "arbitrary"),
                                vmem_limit_bytes=VMEM_LIMIT)


def _keys_spec(S, units, unit):
    return pl.BlockSpec((1, S, units * LANES), lambda b, i: (b, 0, unit // units))


def _q_spec(rows):
    return pl.BlockSpec((1, 1, rows, T), lambda b, i: (b, i, 0, 0))


def _v_spec(S, rows):
    return pl.BlockSpec((1, S // T, rows, T), lambda b, i: (b, 0, 0, 0))


_OUT_SPEC = pl.BlockSpec((1, T, BR_WIDTH), lambda b, i: (b, i, 0))


def _fox_kernel(q_ref, k_ref, v_ref, o_ref, m_scr, acc_scr):
    qb = pl.program_id(1)
    _softmax_init(m_scr, acc_scr)

    def tiles(js, diagonal=False):
        def inputs():
            scores = [[_dot(k_ref[0, pl.ds(pl.multiple_of(j * T, T), T), h * LANES:(h + 1) * LANES],
                            q_ref[0, 0, h * LANES:(h + 1) * LANES, :]) for j in js]
                      for h in range(N_HEADS)]
            valids = [None] * len(js)
            if diagonal:
                key, qry = _key_query_index(qb, js[0])
                valids = [key <= qry]
            return scores, valids, [_head_values(v_ref, j) for j in js]

        _softmax_update(inputs, m_scr, acc_scr, lazy=not diagonal)

    tiles([qb], diagonal=True)
    _for_tile_groups(qb, tiles, group=4)
    _store_heads(o_ref, [_softmax_out(acc_scr, h) for h in range(N_HEADS)])


def _fox_call(pr):
    B, S, _ = pr['keys'].shape
    return pl.pallas_call(
        _fox_kernel,
        out_shape=jax.ShapeDtypeStruct((B, S, BR_WIDTH), F32),
        grid=(B, S // T),
        in_specs=[_q_spec(N_HEADS * LANES), _keys_spec(S, 4, A_FK), _v_spec(S, N_HEADS * HV_ROWS)],
        out_specs=_OUT_SPEC,
        scratch_shapes=_attn_scratch(HV_ROWS),
        compiler_params=_attn_params(),
        name="fox",
    )(pr['fq_t'], pr['keys'], pr['fv_t'])


def _sb_kernel(q_ref, k_ref, v_ref, o_ref, r_scr, acc_scr):
    qb = pl.program_id(1)
    qp = _pair_pads(q_ref[0, 0])
    r_scr[...] = jnp.zeros(r_scr.shape, F32)
    acc_scr[...] = jnp.zeros(acc_scr.shape, F32)
    s_i = lax.broadcasted_iota(I32, (T, T), 0)
    j_i = lax.broadcasted_iota(I32, (T, T), 1)
    later_keys = jnp.where(j_i > s_i, 1.0, 0.0).astype(BF16)

    def tile(j, masked):
        s0 = pl.multiple_of(j * T, T)
        strict = None
        if masked:
            key, qry = _key_query_index(qb, j)
            strict = key < qry
        heads = range(N_HEADS)
        r_prev = [r_scr[h] for h in heads]
        z = [_dot(k_ref[0, pl.ds(s0, T), (h // 2) * LANES:(h // 2 + 1) * LANES], qp[h]) for h in heads]
        log_take = [_log_sigmoid(z[h]) for h in heads]
        log_keep = [log_take[h] - z[h] for h in heads]
        if masked:
            log_keep = [jnp.where(strict, lk, 0.0) for lk in log_keep]
        pieces = [_split3(lk) for lk in log_keep]
        within = [functools.reduce(jnp.add, [_dot(later_keys, p) for p in pieces[h]])
                  for h in heads]
        a = [jnp.exp(log_take[h] + (r_prev[h] + within[h])) for h in heads]
        if masked:
            a = [jnp.where(strict, w, 0.0) for w in a]
        updates = [_dot(v_ref[0, j, h * HEAD_DIM:(h + 1) * HEAD_DIM, :], a[h].astype(BF16)) for h in heads]
        for h in heads:
            acc_scr[h] = acc_scr[h] + updates[h]
            r_scr[h] = r_prev[h] + within[h][0:1, :] + log_keep[h][0:1, :]

    def live():
        return jnp.max(r_scr[...]) > SB_CUTOFF

    def step(state):
        i, _ = state
        tile(qb - 1 - i, False)
        return i + 1, live()

    tile(qb, True)
    lax.while_loop(lambda st: (st[0] < qb) & st[1], step, (jnp.int32(0), live()))
    _store_heads(o_ref, [acc_scr[h] for h in range(N_HEADS)])


def _sb_call(pr):
    B, S, _ = pr['keys'].shape
    return pl.pallas_call(
        _sb_kernel,
        out_shape=jax.ShapeDtypeStruct((B, S, BR_WIDTH), F32),
        grid=(B, S // T),
        in_specs=[_q_spec(BR_WIDTH), _keys_spec(S, 2, A_SK), _v_spec(S, BR_WIDTH)],
        out_specs=_OUT_SPEC,
        scratch_shapes=_attn_scratch(HEAD_DIM),
        compiler_params=_attn_params(),
        name="stick_breaking",
    )(pr['sq_t'], pr['keys'], pr['sv_t'])


def _dsa_kernel(q_ref, k_ref, v_ref, qi_ref, ki_ref, sm_ref, o_ref, sc_scr, hi_scr, lo_scr, m_scr, acc_scr,
                *, k_sel):
    qb = pl.program_id(1)
    row = lax.broadcasted_iota(I32, (LANES, T), 0)
    qi = qi_ref[0, 0]
    zero = jnp.zeros_like(qi)
    qi_pad = [jnp.where((row >= IDX_DIM * h) & (row < IDX_DIM * (h + 1)), qi, zero)
              for h in range(IDX_HEADS)]
    wi = [sm_ref[0, 0, SM_WI + h:SM_WI + h + 1, :] for h in range(IDX_HEADS)]

    def score_tiles(js, diagonal=False):
        rows = [pl.ds(pl.multiple_of(j * T, T), T) for j in js]
        heads = range(IDX_HEADS)
        raw = [[_dot(ki_ref[0, r, :], qi_pad[h]) for h in heads] for r in rows]
        rel = [[jnp.maximum(raw_t[h], 0.0) for h in heads] for raw_t in raw]
        scs = [functools.reduce(jnp.add, [wi[h] * rel_t[h] for h in heads]) for rel_t in rel]
        scs = [jnp.where(sc == 0.0, 0.0, sc) for sc in scs]
        if diagonal:
            key, qry = _key_query_index(qb, js[0])
            scs = [jnp.where(key <= qry, scs[0], -jnp.inf)]
        halves = [_split_keys(sc) for sc in scs]
        for r, sc, (hi, lo) in zip(rows, scs, halves):
            sc_scr[r, :] = sc
            hi_scr[r, :] = hi
            lo_scr[r, :] = lo

    _for_tile_groups(qb, score_tiles, group=4)
    score_tiles([qb], diagonal=True)

    @pl.when(qb % 2 == 0)
    def _():
        pad = pl.ds(pl.multiple_of((qb + 1) * T, T), T)
        sc_scr[pad, :] = jnp.full((T, T), -jnp.inf, F32)
        hi_scr[pad, :] = jnp.full((T, T), LOW16, I16)
        lo_scr[pad, :] = jnp.full((T, T), LOW16, I16)

    n_valid = (qb * T + 1 + lax.broadcasted_iota(I32, (1, T), 1)).astype(F32)
    thr = _top_k_threshold_split(sc_scr, hi_scr, lo_scr, qb // 2 + 1, 2 * T, k_sel, n_valid)

    qp = _pair_pads(q_ref[0, 0])
    _softmax_init(m_scr, acc_scr)

    def attn_tiles(js, lazy=True):
        def inputs():
            scores = [[_dot(k_ref[0, pl.ds(pl.multiple_of(j * T, T), T),
                                  (h // 2) * LANES:(h // 2 + 1) * LANES], qp[h]) for j in js]
                      for h in range(N_HEADS)]
            valids = [sc_scr[pl.ds(pl.multiple_of(j * T, T), T), :] >= thr for j in js]
            return scores, valids, [_head_values(v_ref, j) for j in js]

        _softmax_update(inputs, m_scr, acc_scr, lazy)

    attn_tiles([qb], lazy=False)
    _for_tile_groups(qb, attn_tiles, group=4, first_fn=functools.partial(attn_tiles, lazy=False))
    _store_heads(o_ref, [_softmax_out(acc_scr, h) for h in range(N_HEADS)])


def _dsa_call(pr):
    B, S, _ = pr['keys'].shape
    return pl.pallas_call(
        functools.partial(_dsa_kernel, k_sel=min(DSA_TOPK_MAX, S // 4)),
        out_shape=jax.ShapeDtypeStruct((B, S, BR_WIDTH), F32),
        grid=(B, S // T),
        in_specs=[_q_spec(BR_WIDTH), _keys_spec(S, 2, A_KA), _v_spec(S, N_HEADS * HV_ROWS),
                  _q_spec(LANES), _keys_spec(S, 1, A_KI), _q_spec(LANES)],
        out_specs=_OUT_SPEC,
        scratch_shapes=[pltpu.VMEM((S, T), F32), pltpu.VMEM((S, T), I16), pltpu.VMEM((S, T), I16)]
                       + _attn_scratch(HV_ROWS),
        compiler_params=_attn_params(),
        name="dsa",
    )(pr['dq_t'], pr['keys'], pr['va_t'], pr['qi_t'], pr['keys'], pr['sm_t'])


def _compress_kernel(xk_ref, xv_ref, pek_ref, pev_ref, w1k_ref, w1v_ref, w2k_ref, w2v_ref,
                     o_ref, ot_ref, *, nc):
    def hidden(x_ref, pe_ref, w1_ref):
        x = x_ref[0]
        first = _dot((x + pe_ref[0:1, :]).astype(BF16), w1_ref[0])
        second = _dot((x + pe_ref[1:2, :]).astype(BF16), w1_ref[1])
        pre = first + pltpu.roll(second, nc - 1, 0)
        return (pre * _sigmoid(pre)).astype(BF16)

    kvc = (_dot(hidden(xk_ref, pek_ref, w1k_ref), w2k_ref[...])
           + _dot(hidden(xv_ref, pev_ref, w1v_ref), w2v_ref[...]))
    o_ref[0] = kvc.astype(BF16)
    ot_ref[0] = kvc.T.astype(BF16)


def _compress_call(l, xk, xv, pek, pev, w1k, w1v, w2k, w2v):
    B, nc, width = xk.shape
    full = lambda shape: _layer_spec(l, shape, 1)
    tok = pl.BlockSpec((1, nc, width), lambda b: (b, 0, 0))
    return pl.pallas_call(
        functools.partial(_compress_kernel, nc=nc),
        out_shape=(jax.ShapeDtypeStruct((B, nc, LANES), BF16), jax.ShapeDtypeStruct((B, LANES, nc), BF16)),
        grid=(B,),
        in_specs=[tok, tok, full((2, width)), full((2, width)), full((2, width, CMP_HIDDEN)),
                  full((2, width, CMP_HIDDEN)), full((CMP_HIDDEN, LANES)), full((CMP_HIDDEN, LANES))],
        out_specs=(pl.BlockSpec((1, nc, LANES), lambda b: (b, 0, 0)),
                   pl.BlockSpec((1, LANES, nc), lambda b: (b, 0, 0))),
        compiler_params=pltpu.CompilerParams(dimension_semantics=("arbitrary",),
                                             vmem_limit_bytes=VMEM_LIMIT),
        name="nsa_compress",
    )(xk, xv, pek, pev, w1k, w1v, w2k, w2v)


def _nsa_kernel(q_ref, kvc_ref, kvct_ref, ksw_ref, vsw_ref, sm_ref, o_ref,
                imp_scr, sel_scr, m_scr, acc_scr, *, n_blk, n_sel, nbp):
    qb = pl.program_id(1)
    ncp = kvc_ref.shape[1]
    q_t = q_ref[0, 0]
    zero = jnp.zeros((HEAD_DIM, T), q_t.dtype)
    heads = [q_t[h * HEAD_DIM:(h + 1) * HEAD_DIM, :] for h in range(N_HEADS)]
    q_lo = [jnp.concatenate([q, zero], axis=0) for q in heads]
    q_hi = [jnp.concatenate([zero, q], axis=0) for q in heads]
    gate = lambda br, h: sm_ref[0, 0, SM_GATE + br * N_HEADS + h:SM_GATE + br * N_HEADS + h + 1, :]
    t_row = qb * T + lax.broadcasted_iota(I32, (1, T), 1)

    kvc = kvc_ref[0]
    cmp_end = lax.broadcasted_iota(I32, (ncp, T), 0) * CMP_STRIDE + (CMP_LEN - 1)
    cmp_ok = cmp_end <= t_row
    cmp_logits = [jnp.where(cmp_ok, _dot(kvc, q_lo[h]), NEG) for h in range(N_HEADS)]
    cmp_tops = [jnp.max(s, axis=0, keepdims=True) for s in cmp_logits]
    cmp_e = [jnp.where(cmp_ok, jnp.exp2(s - m), 0.0) for s, m in zip(cmp_logits, cmp_tops)]
    cmp_den = [jnp.maximum(jnp.sum(e, axis=0, keepdims=True), 1e-30) for e in cmp_e]
    pcs = [e / d for e, d in zip(cmp_e, cmp_den)]
    o_cmp = [_dot(kvct_ref[0], pc.astype(BF16))[HEAD_DIM:, :] for pc in pcs]
    p_sum = functools.reduce(jnp.add, pcs)
    per_blk = SLC_BLOCK // CMP_STRIDE
    b_i = lax.broadcasted_iota(I32, (nbp, ncp), 0)
    c_i = lax.broadcasted_iota(I32, (nbp, ncp), 1)
    group = jnp.where(_div_pow2(c_i, per_blk) == b_i, 1.0, 0.0).astype(BF16)
    imp = _dot3x(group, p_sum)
    blk = lax.broadcasted_iota(I32, (nbp, T), 0)
    cur = _div_pow2(t_row, SLC_BLOCK)
    forced = (blk == 0) | (blk == cur) | (blk == cur - 1)
    imp = jnp.where(forced, FORCE_SCORE, jnp.where(blk <= cur, imp, -1.0))
    imp = jnp.where(blk < n_blk, imp, -2.0)
    imp_scr[...] = imp
    thr = _top_k_threshold(imp_scr, 1, nbp, n_sel, jnp.full((1, T), float(nbp), F32))
    sel_scr[...] = jnp.where(imp_scr[...] >= thr, 1.0, 0.0)

    _softmax_init(m_scr, acc_scr)
    blk_per_tile = T // SLC_BLOCK

    def slc_tiles(js, diagonal=False):
        def inputs():
            valids, scores = [], [[] for _ in range(N_HEADS)]
            for j in js:
                valid = jnp.concatenate(
                    [jnp.broadcast_to(sel_scr[pl.ds(j * blk_per_tile + b, 1), :], (SLC_BLOCK, T))
                     for b in range(blk_per_tile)], axis=0) > 0.5
                if diagonal:
                    key, qry = _key_query_index(qb, j)
                    valid = valid & (key <= qry)
                valids.append(valid)
                kt = ksw_ref[0, pl.ds(pl.multiple_of(j * T, T), T), :]
                for h in range(N_HEADS):
                    scores[h].append(_dot(kt, q_lo[h]))
            return scores, valids, [[vsw_ref[0, j, :HV_ROWS, :]] * N_HEADS for j in js]

        _softmax_update(inputs, m_scr, acc_scr, lazy=not diagonal)

    slc_tiles([qb], diagonal=True)
    _for_tile_groups(qb, slc_tiles, group=4)

    n_win = WINDOW // T + 1
    j0 = jnp.maximum(qb - (n_win - 1), 0)
    heads, wins = range(N_HEADS), range(n_win)
    oks = []
    for i in wins:
        key, qry = _key_query_index(qb, j0 + i)
        dist = qry - key
        oks.append((dist >= 0) & (dist < WINDOW))
    raw = [[_dot(ksw_ref[0, pl.ds(pl.multiple_of((j0 + i) * T, T), T), :], q_hi[h]) for i in wins]
           for h in heads]
    logits = [[jnp.where(oks[i], raw[h][i], NEG) for i in wins] for h in heads]
    tops = _tile_tops(logits)
    es = [[jnp.where(oks[i], jnp.exp2(logits[h][i] - tops[h]), 0.0) for i in wins] for h in heads]
    dens = [functools.reduce(jnp.add, [jnp.sum(e, axis=0, keepdims=True) for e in es[h]]) for h in heads]
    pws = [[(es[h][i] / jnp.maximum(dens[h], 1e-30)).astype(BF16) for i in wins] for h in heads]
    o_win = [functools.reduce(jnp.add, [_dot(vsw_ref[0, j0 + i, HV_ROWS:, :], pws[h][i]) for i in wins])
             for h in heads]
    _store_heads(o_ref, [gate(0, h) * o_cmp[h] + gate(1, h) * _softmax_out(acc_scr, h)
                         + gate(2, h) * o_win[h] for h in heads])


def _nsa_call(pr, kvc, kvc_t):
    B, S, _ = pr['keys'].shape
    n_blk = S // SLC_BLOCK
    nbp = max(LANES, n_blk)
    ncp = kvc.shape[1]
    return pl.pallas_call(
        functools.partial(_nsa_kernel, n_blk=n_blk, n_sel=min(SLC_TOPN, n_blk), nbp=nbp),
        out_shape=jax.ShapeDtypeStruct((B, S, BR_WIDTH), F32),
        grid=(B, S // T),
        in_specs=[_q_spec(BR_WIDTH),
                  pl.BlockSpec((1, ncp, LANES), lambda b, i: (b, 0, 0)),
                  pl.BlockSpec((1, LANES, ncp), lambda b, i: (b, 0, 0)),
                  _keys_spec(S, 1, A_KSW), _v_spec(S, VSW_ROWS), _q_spec(LANES)],
        out_specs=_OUT_SPEC,
        scratch_shapes=[pltpu.VMEM((nbp, T), F32), pltpu.VMEM((nbp, T), F32)] + _attn_scratch(HV_ROWS),
        compiler_params=_attn_params(),
        name="nsa",
    )(pr['nq_t'], kvc, kvc_t, pr['keys'], pr['vsw_t'], pr['sm_t'])


def _mem_kv_kernel(mem_ref, g_ref, w_ref, o_ref):
    o_ref[0] = _dot(_rmsnorm(mem_ref[0], g_ref[...]).astype(BF16), w_ref[...]).astype(BF16)


def _mem_kv_call(l, mem, g, w):
    B, M, D = mem.shape
    return pl.pallas_call(
        _mem_kv_kernel,
        out_shape=jax.ShapeDtypeStruct((B, M, 2 * BR_WIDTH), BF16),
        grid=(B,),
        in_specs=[pl.BlockSpec((1, M, D), lambda b: (b, 0, 0)), _layer_spec(l, (1, D), 1),
                  _layer_spec(l, (D, 2 * BR_WIDTH), 1)],
        out_specs=pl.BlockSpec((1, M, 2 * BR_WIDTH), lambda b: (b, 0, 0)),
        name="mem_kv",
    )(mem, g, w)


def _out_kernel(x_ref, g_ref, ya_ref, yb_ref, yc_ref, yd_ref, mq_ref, mkv_ref, wz_ref, wm_ref,
                wb_ref, wo_ref, fg_ref, o_ref, h_scr, *, final):
    x = x_ref[0]
    h_scr[...] = _rmsnorm(x, g_ref[...]).astype(BF16)
    q = mq_ref[0]
    lane = lax.broadcasted_iota(I32, (T, LANES), 1)
    lo = lane < HEAD_DIM
    zero = jnp.zeros((T, LANES), q.dtype)
    heads = range(N_HEADS)
    chunks = [q[:, (h // 2) * LANES:(h // 2 + 1) * LANES] for h in heads]
    qh = [jnp.where(lo, c, zero) if h % 2 == 0 else jnp.where(lo, zero, c) for h, c in zip(heads, chunks)]
    mk = [mkv_ref[0, :, (h // 2) * LANES:(h // 2 + 1) * LANES] for h in heads]
    mv = [mkv_ref[0, :, BR_WIDTH + (h // 2) * LANES:BR_WIDTH + (h // 2 + 1) * LANES] for h in heads]
    s = [_dot_nt(qh[h], mk[h]) for h in heads]
    e = [jnp.exp(s[h] - jnp.max(s[h], axis=1, keepdims=True)) for h in heads]
    p = [(e[h] / jnp.sum(e[h], axis=1, keepdims=True)).astype(BF16) for h in heads]
    o = [_dot(p[h], mv[h]) for h in heads]
    y_e = jnp.concatenate([jnp.where(lo, o[0], o[1]), jnp.where(lo, o[2], o[3])], axis=1)
    branches = range(N_BRANCH)
    d = x.shape[1]
    ys_in = (ya_ref[0], yb_ref[0], yc_ref[0], yd_ref[0], y_e)
    z = [_dot(h_scr[...], wz_ref[n]) for n in branches]
    ys = [(ys_in[n] * (z[n] * _sigmoid(z[n]))).astype(BF16) for n in branches]
    gates = [_sigmoid(_dot(h_scr[...], wm_ref[:, n * d:(n + 1) * d])) for n in branches]
    projected = [_dot(ys[n], wb_ref[n]) for n in branches]
    merged = functools.reduce(jnp.add, [gates[n] * projected[n] for n in branches])
    out = x + _dot(merged.astype(BF16), wo_ref[...])
    o_ref[0] = _rmsnorm(out, fg_ref[...]) if final else out


def _out_call(l, x, g, ys, keys, mkv, wz, wm, wb, wo, fg, final):
    B, S, D = x.shape
    M = mkv.shape[1]
    full = lambda shape: _layer_spec(l, shape, 2)
    tile = lambda n: pl.BlockSpec((1, T, n), lambda b, i: (b, i, 0))
    return pl.pallas_call(
        functools.partial(_out_kernel, final=final),
        out_shape=jax.ShapeDtypeStruct((B, S, D), F32),
        grid=(B, S // T),
        in_specs=[tile(D), full((1, D))] + [tile(BR_WIDTH)] * 4
                 + [pl.BlockSpec((1, T, BR_WIDTH), lambda b, i: (b, i, A_MQ // 2)),
                    pl.BlockSpec((1, M, 2 * BR_WIDTH), lambda b, i: (b, 0, 0)),
                    full((N_BRANCH, D, BR_WIDTH)), full((D, N_BRANCH * D)), full((N_BRANCH, BR_WIDTH, D)),
                    full((D, D)), pl.BlockSpec((1, D), lambda b, i: (0, 0))],
        out_specs=tile(D),
        scratch_shapes=[pltpu.VMEM((T, D), BF16)],
        compiler_params=pltpu.CompilerParams(dimension_semantics=("arbitrary", "arbitrary"),
                                             vmem_limit_bytes=VMEM_LIMIT),
        name="merge_out",
    )(x, g, *ys, keys, mkv, wz, wm, wb, wo, fg)


def kernel(x, mem, positions, norm_g, w_in, kv_norm, w_uk, w_uv, fox_bias, nsa_pe_k, nsa_pe_v,
           nsa_wc1_k, nsa_wc2_k, nsa_wc1_v, nsa_wc2_v, mem_norm, w_mem_kv, w_branch, w_out, final_norm):
    B, S, D = x.shape
    depth = norm_g.shape[0]
    assert S % (2 * T) == 0 and S >= WINDOW + T and WINDOW % T == 0 and D == 1024
    cos, sin = _rope_tables(positions)
    w1, wz, wm = _proj_weight(w_in)
    fb = jnp.zeros((depth, 1, LANES), F32).at[:, 0, SM_CUM:SM_CUM + N_HEADS].set(fox_bias)
    nc = S // CMP_STRIDE
    tok_w = CMP_STRIDE * NSA_KV_DIM
    pad_k = jnp.zeros((depth, CMP_HIDDEN, LANES), F32).at[..., :NSA_KV_DIM].set(nsa_wc2_k).astype(BF16)
    pad_v = jnp.zeros((depth, CMP_HIDDEN, LANES), F32).at[..., NSA_KV_DIM:].set(nsa_wc2_v).astype(BF16)
    g, kvn, mn = norm_g[:, None, :], kv_norm[:, None, :], mem_norm[:, None, :]
    wuk, wuv, wmkv = w_uk.astype(BF16), w_uv.astype(BF16), w_mem_kv.astype(BF16)
    pek, pev = nsa_pe_k.reshape(depth, 2, tok_w), nsa_pe_v.reshape(depth, 2, tok_w)
    w1k = nsa_wc1_k.reshape(depth, 2, tok_w, CMP_HIDDEN).astype(BF16)
    w1v = nsa_wc1_v.reshape(depth, 2, tok_w, CMP_HIDDEN).astype(BF16)
    wb, wo = w_branch.astype(BF16), w_out.astype(BF16)
    for l in range(depth):
        pr = _proj_call(l, x, g, w1, kvn, wuk, wuv, fb, cos, sin)
        kvc, kvc_t = _compress_call(l, pr['kc_tok'].reshape(B, nc, tok_w), pr['vc_tok'].reshape(B, nc, tok_w),
                                    pek, pev, w1k, w1v, pad_k, pad_v)
        ys = (_dsa_call(pr), _fox_call(pr), _sb_call(pr), _nsa_call(pr, kvc, kvc_t))
        mkv = _mem_kv_call(l, mem, mn, wmkv)
        x = _out_call(l, x, g, ys, pr['keys'], mkv, wz, wm, wb, wo, final_norm[None], l == depth - 1)
    return x
```

```python
import functools

import jax
import jax.numpy as jnp
from jax import lax
from jax.experimental import pallas as pl
from jax.experimental.pallas import tpu as pltpu

F32 = jnp.float32
BF16 = jnp.bfloat16
I32 = jnp.int32
I16 = jnp.int16

LANES = 128
SUBLANES = 8
N_HEADS = 4
HEAD_DIM = 64
BR_WIDTH = N_HEADS * HEAD_DIM
N_BRANCH = 5
ROPE_THETA = 10000.0
EPS = 1e-6
DSA_TOPK_MAX = 256
IDX_HEADS = 4
IDX_DIM = 32
KV_LATENT = 128
NSA_KV_DIM = 64
CMP_LEN = 32
CMP_STRIDE = 16
CMP_HIDDEN = 128
SLC_BLOCK = 64
SLC_TOPN = 16
WINDOW = 512
FORCE_SCORE = 1e4
Q_SCALE = HEAD_DIM ** -0.5
LOG2E = 1.4426950408889634
Q_SCALE2 = Q_SCALE * LOG2E
IDX_SCALE = (IDX_DIM ** -0.5) * (IDX_HEADS ** -0.5)
NEG = -1e30
SB_CUTOFF = -104.0
LAZY_LIMIT = 64.0
VMEM_LIMIT = 56 * 1024 * 1024
T = 256
ONES_ROWS = 16
HV_ROWS = HEAD_DIM + ONES_ROWS
VSW_ROWS = HV_ROWS + NSA_KV_DIM

IN_LAYOUT = (
    ('dsa_q', BR_WIDTH), ('dsa_ckv', KV_LATENT), ('idx_q', IDX_HEADS * IDX_DIM), ('idx_k', IDX_DIM),
    ('idx_w', IDX_HEADS), ('dsa_z', BR_WIDTH),
    ('fox_q', BR_WIDTH), ('fox_k', BR_WIDTH), ('fox_v', BR_WIDTH), ('fox_f', N_HEADS), ('fox_z', BR_WIDTH),
    ('sb_q', BR_WIDTH), ('sb_k', BR_WIDTH), ('sb_v', BR_WIDTH), ('sb_z', BR_WIDTH),
    ('nsa_q', BR_WIDTH), ('nsa_kc', NSA_KV_DIM), ('nsa_vc', NSA_KV_DIM), ('nsa_ks', NSA_KV_DIM),
    ('nsa_vs', NSA_KV_DIM), ('nsa_kw', NSA_KV_DIM), ('nsa_vw', NSA_KV_DIM), ('nsa_g', 3 * N_HEADS),
    ('nsa_z', BR_WIDTH), ('mem_q', BR_WIDTH), ('mem_z', BR_WIDTH), ('merge', N_BRANCH * 1024),
)

A_FK, A_KA, A_SK, A_KI, A_KSW, A_MQ = 0, 4, 6, 8, 9, 10
A_COLS = 12 * LANES
SM_WI, SM_CUM, SM_GATE = 0, 4, 8
FOX_CK, FOX_ONE, FOX_END = HEAD_DIM, HEAD_DIM + 3, HEAD_DIM + 6


def _dot(a, b):
    return jnp.dot(a, b, preferred_element_type=F32)


def _dot_nt(a, b):
    return lax.dot_general(a, b, (((1,), (1,)), ((), ())), preferred_element_type=F32)


def _split3(x):
    hi = x.astype(BF16)
    r1 = x - hi.astype(F32)
    mid = r1.astype(BF16)
    lo = (r1 - mid.astype(F32)).astype(BF16)
    return hi, mid, lo


def _dot3x(m01, x):
    hi, mid, lo = _split3(x)
    return _dot(m01, hi) + _dot(m01, mid) + _dot(m01, lo)


def _div_pow2(x, n):
    assert n & (n - 1) == 0
    return x >> (n.bit_length() - 1)


def _sigmoid(x):
    return 1.0 / (1.0 + jnp.exp(-x))


def _log_sigmoid(x):
    return jnp.minimum(x, 0.0) - jnp.log(1.0 + jnp.exp(-jnp.abs(x)))


def _rmsnorm(x, g):
    return x * lax.rsqrt(jnp.mean(x * x, axis=-1, keepdims=True) + EPS) * g


def _rope128(x, cos, sin_signed, half):
    lane = lax.broadcasted_iota(I32, x.shape, 1)
    first = (lane & (2 * half - 1)) < half
    partner = jnp.where(first, pltpu.roll(x, LANES - half, 1), pltpu.roll(x, half, 1))
    return x * cos + partner * sin_signed


def _rope_table_kernel(pos_ref, inv_ref, sgn_ref, cos_ref, sin_ref):
    ang = pos_ref[0] * inv_ref[...]
    cos_ref[0] = jnp.cos(ang)
    sin_ref[0] = jnp.sin(ang) * sgn_ref[...]


def _rope_tables(positions):
    B, S = positions.shape
    lane = jnp.arange(LANES)

    def inv_row(dh):
        inv = ROPE_THETA ** (-jnp.arange(0, dh, 2, dtype=F32) / dh)
        return inv[(lane % dh) % (dh // 2)]

    def sgn_row(dh):
        return jnp.where((lane % dh) < dh // 2, -1.0, 1.0).astype(F32)

    inv = jnp.concatenate([inv_row(HEAD_DIM), inv_row(IDX_DIM)])[None, :]
    sgn = jnp.concatenate([sgn_row(HEAD_DIM), sgn_row(IDX_DIM)])[None, :]
    pos = positions.astype(F32)[..., None]
    row = pl.BlockSpec((1, 2 * LANES), lambda b, i: (0, 0))
    tab = pl.BlockSpec((1, T, 2 * LANES), lambda b, i: (b, i, 0))
    return pl.pallas_call(
        _rope_table_kernel,
        out_shape=(jax.ShapeDtypeStruct((B, S, 2 * LANES), F32),) * 2,
        grid=(B, S // T),
        in_specs=[pl.BlockSpec((1, T, 1), lambda b, i: (b, i, 0)), row, row],
        out_specs=(tab, tab),
        name="rope_tables",
    )(pos, inv, sgn)


W_DQ, W_CKV, W_QI, W_KI, W_SM, W_FQ, W_SQ, W_NQ, W_MQ, W_KVC, W_KSW, W_VSW = (
    0, 256, 384, 512, 640, 768, 1536, 2304, 2560, 2816, 2944, 3072)
W_COLS = 3200

PROJ_OUTS = (
    ('keys', A_COLS, BF16), ('kc_tok', NSA_KV_DIM, F32), ('vc_tok', NSA_KV_DIM, F32),
    ('dq_t', BR_WIDTH, BF16), ('fq_t', N_HEADS * LANES, BF16), ('sq_t', BR_WIDTH, BF16),
    ('nq_t', BR_WIDTH, BF16), ('qi_t', LANES, BF16), ('sm_t', LANES, F32),
    ('va_t', N_HEADS * HV_ROWS, BF16), ('fv_t', N_HEADS * HV_ROWS, BF16), ('sv_t', BR_WIDTH, BF16),
    ('vsw_t', VSW_ROWS, BF16),
)
ROW_MAJOR = ('keys', 'kc_tok', 'vc_tok')


def _proj_kernel(x_ref, g_ref, w_ref, kvn_ref, wuk_ref, wuv_ref, fb_ref, cos_ref, sin_ref,
                 *refs):
    out = dict(zip([n for n, _, _ in PROJ_OUTS], refs))
    h_scr, carry_scr = refs[len(PROJ_OUTS):]
    keys = out['keys']
    h_scr[...] = _rmsnorm(x_ref[0], g_ref[...]).astype(BF16)
    c64, s64 = cos_ref[0, :, :LANES], sin_ref[0, :, :LANES]
    c32, s32 = cos_ref[0, :, LANES:], sin_ref[0, :, LANES:]
    lane = lax.broadcasted_iota(I32, (T, LANES), 1)
    ones_rows = jnp.ones((ONES_ROWS, T), BF16)

    def proj(c0, n):
        return _dot(h_scr[...], w_ref[:, c0:c0 + n])

    def put(unit, val):
        keys[0, :, unit * LANES:unit * LANES + val.shape[1]] = val.astype(BF16)

    def rope64(v, j):
        return _rope128(v[:, j * LANES:(j + 1) * LANES], c64, s64, HEAD_DIM // 2)

    def put_t(ref, row0, chunk):
        ref[0, 0, row0:row0 + LANES, :] = chunk.T.astype(ref.dtype)

    def put_values_t(ref, v):
        for j in range(2):
            pair_t = v[:, j * LANES:(j + 1) * LANES].T.astype(BF16)
            for odd in range(2):
                r0 = (2 * j + odd) * HV_ROWS
                ref[0, 0, r0:r0 + HEAD_DIM, :] = pair_t[odd * HEAD_DIM:(odd + 1) * HEAD_DIM, :]
                ref[0, 0, r0 + HEAD_DIM:r0 + HV_ROWS, :] = ones_rows

    p = proj(W_DQ, BR_WIDTH)
    for j in range(2):
        put_t(out['dq_t'], j * LANES, rope64(p, j) * Q_SCALE2)
    c_kv = _rmsnorm(proj(W_CKV, KV_LATENT), kvn_ref[...]).astype(BF16)
    k_a = _dot(c_kv, wuk_ref[...])
    for j in range(2):
        put(A_KA + j, rope64(k_a, j))
    put_values_t(out['va_t'], _dot(c_kv, wuv_ref[...]))
    put_t(out['qi_t'], 0, _rope128(proj(W_QI, LANES), c32, s32, IDX_DIM // 2))
    put(A_KI, _rope128(proj(W_KI, LANES), c32, s32, IDX_DIM // 2))
    p = proj(W_SM, LANES)
    log_f = _log_sigmoid(p + fb_ref[...])
    r_i = lax.broadcasted_iota(I32, (T, T), 0)
    c_i = lax.broadcasted_iota(I32, (T, T), 1)
    tri = jnp.where(r_i >= c_i, 1.0, 0.0).astype(BF16)

    @pl.when(pl.program_id(1) == 0)
    def _():
        carry_scr[...] = jnp.zeros_like(carry_scr)

    cum = _dot3x(tri, log_f) + carry_scr[...]
    carry_scr[...] = cum[T - 1:T, :]
    small_t = jnp.where(lane < SM_CUM, p * IDX_SCALE,
                        jnp.where(lane < SM_GATE, cum, _sigmoid(p))).T
    out['sm_t'][0, 0] = small_t
    q_f = proj(W_FQ, BR_WIDTH) * Q_SCALE2
    cum2, cum2_t = cum * LOG2E, small_t * LOG2E
    k_f = proj(W_FQ + 256, BR_WIDTH)
    row = lax.broadcasted_iota(I32, (HEAD_DIM, T), 0)
    for h in range(N_HEADS):
        pair, odd = h // 2, h % 2
        k_chunk = k_f[:, pair * LANES:(pair + 1) * LANES]
        k_h = pltpu.roll(k_chunk, HEAD_DIM, 1) if odd else k_chunk
        ck = [c.astype(F32) for c in _split3(cum2[:, SM_CUM + h:SM_CUM + h + 1])]
        k_aug = jnp.where(lane < HEAD_DIM, k_h,
                          jnp.where(lane == FOX_CK, ck[0],
                                    jnp.where(lane == FOX_CK + 1, ck[1],
                                              jnp.where(lane == FOX_CK + 2, ck[2],
                                                        jnp.where(lane < FOX_END, 1.0, 0.0)))))
        put(A_FK + h, k_aug)
        q_ht = q_f[:, pair * LANES:(pair + 1) * LANES].T[odd * HEAD_DIM:(odd + 1) * HEAD_DIM, :]
        cq = [c.astype(F32) for c in _split3(cum2_t[SM_CUM + h:SM_CUM + h + 1, :])]
        aug = jnp.where(row < 3, -1.0,
                        jnp.where(row == 3, cq[0], jnp.where(row == 4, cq[1],
                                                             jnp.where(row == 5, cq[2], 0.0))))
        out['fq_t'][0, 0, h * LANES:(h + 1) * LANES, :] = jnp.concatenate(
            [q_ht, aug], axis=0).astype(BF16)
    put_values_t(out['fv_t'], proj(W_FQ + 512, BR_WIDTH))
    p = proj(W_SQ, BR_WIDTH) * Q_SCALE
    for j in range(2):
        put_t(out['sq_t'], j * LANES, p[:, j * LANES:(j + 1) * LANES])
    put(A_SK, proj(W_SQ + 256, BR_WIDTH))
    p = proj(W_SQ + 512, BR_WIDTH)
    for j in range(2):
        put_t(out['sv_t'], j * LANES, p[:, j * LANES:(j + 1) * LANES])
    put(A_MQ, proj(W_MQ, BR_WIDTH) * Q_SCALE)
    p = proj(W_NQ, BR_WIDTH)
    for j in range(2):
        put_t(out['nq_t'], j * LANES, rope64(p, j) * Q_SCALE2)
    p = proj(W_KVC, LANES)
    out['kc_tok'][0] = rope64(p, 0)[:, :NSA_KV_DIM]
    out['vc_tok'][0] = p[:, NSA_KV_DIM:]
    put(A_KSW, rope64(proj(W_KSW, LANES), 0))
    vsw_t = proj(W_VSW, LANES).T.astype(BF16)
    out['vsw_t'][0, 0, :NSA_KV_DIM, :] = vsw_t[:NSA_KV_DIM, :]
    out['vsw_t'][0, 0, NSA_KV_DIM:HV_ROWS, :] = ones_rows
    out['vsw_t'][0, 0, HV_ROWS:, :] = vsw_t[NSA_KV_DIM:, :]


def _proj_weight(w_in):
    off, o = {}, 0
    for name, n in IN_LAYOUT:
        off[name] = (o, n)
        o += n

    def col(name):
        s, n = off[name]
        return w_in[..., s:s + n]

    zeros = lambda n: jnp.zeros(w_in.shape[:-1] + (n,), w_in.dtype)
    groups = [col('dsa_q'), col('dsa_ckv'), col('idx_q'), col('idx_k'), col('idx_k'), col('idx_k'),
              col('idx_k'), col('idx_w'), col('fox_f'), col('nsa_g'), zeros(LANES - 20),
              col('fox_q'), col('fox_k'), col('fox_v'), col('sb_q'), col('sb_k'), col('sb_v'),
              col('nsa_q'), col('mem_q'), col('nsa_kc'), col('nsa_vc'), col('nsa_ks'), col('nsa_kw'),
              col('nsa_vs'), col('nsa_vw')]
    w1 = jnp.concatenate(groups, axis=-1).astype(BF16)
    wz = jnp.stack([col(n) for n in ('dsa_z', 'fox_z', 'sb_z', 'nsa_z', 'mem_z')], axis=-3).astype(BF16)
    return w1, wz, col('merge').astype(BF16)


def _layer_spec(l, shape, grid_rank):
    if grid_rank == 1:
        return pl.BlockSpec((None,) + shape, lambda b: (l,) + (0,) * len(shape))
    return pl.BlockSpec((None,) + shape, lambda b, i: (l,) + (0,) * len(shape))


def _proj_call(l, x, g, w1, kvn, wuk, wuv, fb, cos, sin):
    B, S, D = x.shape
    full = lambda shape: _layer_spec(l, shape, 2)
    tile = lambda n: pl.BlockSpec((1, T, n), lambda b, i: (b, i, 0))
    tile_t = lambda n: pl.BlockSpec((1, 1, n, T), lambda b, i: (b, i, 0, 0))
    shapes, specs = [], []
    for name, n, dt in PROJ_OUTS:
        if name in ROW_MAJOR:
            shapes.append(jax.ShapeDtypeStruct((B, S, n), dt))
            specs.append(tile(n))
        else:
            shapes.append(jax.ShapeDtypeStruct((B, S // T, n, T), dt))
            specs.append(tile_t(n))
    outs = pl.pallas_call(
        _proj_kernel,
        out_shape=tuple(shapes),
        grid=(B, S // T),
        in_specs=[tile(D), full((1, D)), full((D, W_COLS)), full((1, KV_LATENT)),
                  full((KV_LATENT, BR_WIDTH)), full((KV_LATENT, BR_WIDTH)), full((1, LANES)),
                  tile(2 * LANES), tile(2 * LANES)],
        out_specs=tuple(specs),
        scratch_shapes=[pltpu.VMEM((T, D), BF16), pltpu.VMEM((1, LANES), F32)],
        compiler_params=pltpu.CompilerParams(dimension_semantics=("arbitrary", "arbitrary"),
                                             vmem_limit_bytes=VMEM_LIMIT),
        name="proj",
    )(x, g, w1, kvn, wuk, wuv, fb, cos, sin)
    return dict(zip([n for n, _, _ in PROJ_OUTS], outs))


def _key_query_index(qb, j):
    key = j * T + lax.broadcasted_iota(I32, (T, T), 0)
    qry = qb * T + lax.broadcasted_iota(I32, (T, T), 1)
    return key, qry


def _pair_pads(q_t):
    row = lax.broadcasted_iota(I32, (LANES, T), 0)
    top = row < HEAD_DIM
    zero = jnp.zeros((LANES, T), q_t.dtype)
    out = []
    for pair in range(2):
        chunk = q_t[pair * LANES:(pair + 1) * LANES, :]
        out.append(jnp.where(top, chunk, zero))
        out.append(jnp.where(top, zero, chunk))
    return out


def _masked_tiles(scores, valids):
    return [[s if v is None else jnp.where(v, s, 2 * NEG) for s, v in zip(scores_h, valids)]
            for scores_h in scores]


def _tile_tops(tiles):
    return [functools.reduce(jnp.maximum, [jnp.max(s, axis=0, keepdims=True) for s in tiles_h])
            for tiles_h in tiles]


def _weighted_values(tiles, values, m):
    weights = [[jnp.exp2(s - m[h]).astype(BF16) for s in tiles_h] for h, tiles_h in enumerate(tiles)]
    parts = [[_dot(values[i][h], w) for i, w in enumerate(weights_h)] for h, weights_h in enumerate(weights)]
    return [functools.reduce(jnp.add, parts_h) for parts_h in parts]


def _softmax_update(inputs_fn, m_scr, acc_scr, lazy=False):
    if not lazy:
        _softmax_update_exact(*inputs_fn(), m_scr, acc_scr)
        return
    scores, valids, values = inputs_fn()
    m_ref = [m_scr[h] for h in range(N_HEADS)]
    tiles = _masked_tiles(scores, valids)
    updates = _weighted_values(tiles, values, m_ref)
    tops = _tile_tops(tiles)
    overshoot = functools.reduce(jnp.maximum, [jnp.max(tops[h] - m_ref[h]) for h in range(N_HEADS)])
    safe = overshoot <= LAZY_LIMIT

    @pl.when(safe)
    def _():
        for h in range(N_HEADS):
            m_new = jnp.maximum(m_ref[h], tops[h])
            acc_scr[h] = jnp.exp2(m_ref[h] - m_new) * (acc_scr[h] + updates[h])
            m_scr[h] = m_new

    @pl.when(jnp.logical_not(safe))
    def _():
        _softmax_update_exact(*inputs_fn(), m_scr, acc_scr)


def _softmax_update_exact(scores, valids, values, m_scr, acc_scr):
    m_prev = [m_scr[h] for h in range(N_HEADS)]
    tiles = _masked_tiles(scores, valids)
    m_new = [jnp.maximum(m, top) for m, top in zip(m_prev, _tile_tops(tiles))]
    updates = _weighted_values(tiles, values, m_new)
    for h in range(N_HEADS):
        acc_scr[h] = jnp.exp2(m_prev[h] - m_new[h]) * acc_scr[h] + updates[h]
        m_scr[h] = m_new[h]


def _for_tile_groups(n, group_fn, group, first_fn=None):
    def full(i, c):
        group_fn([group * i + g for g in range(group)])
        return c

    if first_fn is not None:
        @pl.when(n >= group)
        def _():
            first_fn(list(range(group)))

    lax.fori_loop(0 if first_fn is None else 1, n // group, full, 0)
    size = group // 2
    while size:
        start = (n // (2 * size)) * (2 * size)

        @pl.when((n & size) != 0)
        def _(start=start, size=size):
            group_fn([start + g for g in range(size)])

        size //= 2


INT_MIN = -2 ** 31
LOW16 = -2 ** 15
KEY_ABOVE_NEG_INF = 0x80800000 - 2 ** 32


def _key_to_float(c):
    bits = c ^ ((c >> 31) & 0x7FFFFFFF)
    return lax.bitcast_convert_type(bits, F32)


def _count_rows(ref, n_steps, step_rows, pred, cand):
    pack = SUBLANES * (4 // ref.dtype.itemsize)
    cand_b = jnp.broadcast_to(cand, (pack, T)).astype(ref.dtype)
    one, zero = jnp.ones((pack, T), ref.dtype), jnp.zeros((pack, T), ref.dtype)

    def body(c, accs):
        blk = ref[pl.ds(pl.multiple_of(c * step_rows, step_rows), step_rows), :]
        accs = list(accs)
        for r in range(step_rows // pack):
            hit = jnp.where(pred(blk[r * pack:(r + 1) * pack, :], cand_b), one, zero)
            accs[r % len(accs)] = accs[r % len(accs)] + hit
        return tuple(accs)

    accs = lax.fori_loop(0, n_steps, body, (zero,) * 4)
    total = functools.reduce(jnp.add, [a.astype(I32) if a.dtype != F32 else a for a in accs])
    return jnp.sum(total, axis=0, keepdims=True).astype(F32)


_GE = lambda a, b: a >= b
_GT = lambda a, b: a > b


def _bisect(count_ge, target, lowest, n_bits, cnt_lowest, to_cand=lambda c: c):
    cnt = count_ge(to_cand(jnp.zeros((1, T), I32)))
    take = cnt >= target
    state = (jnp.where(take, 0, lowest).astype(I32), jnp.where(take, cnt, cnt_lowest),
             jnp.where(take, 0.0, cnt))

    def step(i, state):
        val, cnt_val, cnt_next = state
        cand = val + jnp.left_shift(jnp.int32(1), n_bits - 2 - i)
        cnt = count_ge(to_cand(cand))
        take = cnt >= target
        return jnp.where(take, cand, val), jnp.where(take, cnt, cnt_val), jnp.where(take, cnt_next, cnt)

    return lax.fori_loop(0, n_bits - 1, step, state)


def _demote_ties(sc_ref, n_steps, step_rows, thr, k):
    need = float(k) - _count_rows(sc_ref, n_steps, step_rows, _GT, thr)
    s_i = lax.broadcasted_iota(I32, (LANES, LANES), 0)
    j_i = lax.broadcasted_iota(I32, (LANES, LANES), 1)
    earlier = jnp.where(j_i < s_i, 1.0, 0.0).astype(BF16)

    def body(c, seen):
        s0 = pl.multiple_of(c * LANES, LANES)
        blk = sc_ref[pl.ds(s0, LANES), :]
        eq = blk == thr
        eq_f = jnp.where(eq, 1.0, 0.0)
        rank = seen + _dot(earlier, eq_f.astype(BF16))
        sc_ref[pl.ds(s0, LANES), :] = jnp.where(eq & (rank >= need), -jnp.inf, blk)
        return seen + jnp.sum(eq_f, axis=0, keepdims=True)

    lax.fori_loop(0, n_steps * (step_rows // LANES), body, jnp.zeros((1, T), F32))


def _top_k_threshold(sc_ref, n_steps, step_rows, k, n_valid):
    kf = float(k)
    key, cnt_key, _ = _bisect(lambda c: _count_rows(sc_ref, n_steps, step_rows, _GE, c), kf, INT_MIN, 32,
                              n_valid, to_cand=_key_to_float)
    thr = _key_to_float(jnp.maximum(key, KEY_ABOVE_NEG_INF))

    @pl.when(jnp.max(cnt_key) > kf)
    def _():
        _demote_ties(sc_ref, n_steps, step_rows, thr, k)

    return thr


def _split_keys(sc):
    bits = lax.bitcast_convert_type(sc, I32)
    key = bits ^ ((bits >> 31) & 0x7FFFFFFF)
    return (key >> 16).astype(I16), ((key & 0xFFFF) + LOW16).astype(I16)


def _top_k_threshold_split(sc_ref, hi_ref, lo_ref, n_steps, step_rows, k, n_valid):
    kf = float(k)
    rows = float(step_rows) * n_steps.astype(F32)
    high, cnt_high, above = _bisect(lambda c: _count_rows(hi_ref, n_steps, step_rows, _GE, c), kf,
                                    LOW16, 16, rows)
    high_b = jnp.broadcast_to(high, (step_rows, T)).astype(I16)

    def keep_bucket(c, carry):
        rs = pl.ds(pl.multiple_of(c * step_rows, step_rows), step_rows)
        lo_ref[rs, :] = jnp.where(hi_ref[rs, :] == high_b, lo_ref[rs, :], jnp.int16(LOW16))
        return carry

    lax.fori_loop(0, n_steps, keep_bucket, 0)
    low, cnt_low, _ = _bisect(lambda c: _count_rows(lo_ref, n_steps, step_rows, _GE, c), kf - above,
                              LOW16, 16, cnt_high - above)
    key = jnp.left_shift(high, 16) + (low - LOW16)
    thr = _key_to_float(jnp.maximum(key, KEY_ABOVE_NEG_INF))

    @pl.when(jnp.max(jnp.where(n_valid > kf, above + cnt_low, 0.0)) > kf)
    def _():
        _demote_ties(sc_ref, n_steps, step_rows, thr, k)

    return thr


def _softmax_init(m_scr, acc_scr):
    m_scr[...] = jnp.full(m_scr.shape, NEG, F32)
    acc_scr[...] = jnp.zeros(acc_scr.shape, F32)


def _softmax_out(acc_scr, h):
    acc = acc_scr[h]
    return acc[:HEAD_DIM, :] / jnp.maximum(acc[HEAD_DIM:HEAD_DIM + 1, :], 1e-30)


def _store_heads(o_ref, heads_t):
    for pair in range(2):
        o_ref[0, :, pair * LANES:(pair + 1) * LANES] = jnp.concatenate(
            [t[:HEAD_DIM, :] for t in heads_t[2 * pair:2 * pair + 2]], axis=0).T


def _head_values(v_ref, j):
    return [v_ref[0, j, h * HV_ROWS:(h + 1) * HV_ROWS, :] for h in range(N_HEADS)]


def _attn_scratch(rows):
    return [pltpu.VMEM((N_HEADS, 1, T), F32), pltpu.VMEM((N_HEADS, rows, T), F32)]


def _attn_params():
    return pltpu.CompilerParams(dimension_semantics=("arbitrary", "arbitrary"),
                                vmem_limit_bytes=VMEM_LIMIT)


def _keys_spec(S, units, unit):
    return pl.BlockSpec((1, S, units * LANES), lambda b, i: (b, 0, unit // units))


def _q_spec(rows):
    return pl.BlockSpec((1, 1, rows, T), lambda b, i: (b, i, 0, 0))


def _v_spec(S, rows):
    return pl.BlockSpec((1, S // T, rows, T), lambda b, i: (b, 0, 0, 0))


_OUT_SPEC = pl.BlockSpec((1, T, BR_WIDTH), lambda b, i: (b, i, 0))


def _fox_kernel(q_ref, k_ref, v_ref, o_ref, m_scr, acc_scr):
    qb = pl.program_id(1)
    _softmax_init(m_scr, acc_scr)

    def tiles(js, diagonal=False):
        def inputs():
            scores = [[_dot(k_ref[0, pl.ds(pl.multiple_of(j * T, T), T), h * LANES:(h + 1) * LANES],
                            q_ref[0, 0, h * LANES:(h + 1) * LANES, :]) for j in js]
                      for h in range(N_HEADS)]
            valids = [None] * len(js)
            if diagonal:
                key, qry = _key_query_index(qb, js[0])
                valids = [key <= qry]
            return scores, valids, [_head_values(v_ref, j) for j in js]

        _softmax_update(inputs, m_scr, acc_scr, lazy=not diagonal)

    tiles([qb], diagonal=True)
    _for_tile_groups(qb, tiles, group=4)
    _store_heads(o_ref, [_softmax_out(acc_scr, h) for h in range(N_HEADS)])


def _fox_call(pr):
    B, S, _ = pr['keys'].shape
    return pl.pallas_call(
        _fox_kernel,
        out_shape=jax.ShapeDtypeStruct((B, S, BR_WIDTH), F32),
        grid=(B, S // T),
        in_specs=[_q_spec(N_HEADS * LANES), _keys_spec(S, 4, A_FK), _v_spec(S, N_HEADS * HV_ROWS)],
        out_specs=_OUT_SPEC,
        scratch_shapes=_attn_scratch(HV_ROWS),
        compiler_params=_attn_params(),
        name="fox",
    )(pr['fq_t'], pr['keys'], pr['fv_t'])


def _sb_kernel(q_ref, k_ref, v_ref, o_ref, r_scr, acc_scr):
    qb = pl.program_id(1)
    qp = _pair_pads(q_ref[0, 0])
    r_scr[...] = jnp.zeros(r_scr.shape, F32)
    acc_scr[...] = jnp.zeros(acc_scr.shape, F32)
    s_i = lax.broadcasted_iota(I32, (T, T), 0)
    j_i = lax.broadcasted_iota(I32, (T, T), 1)
    later_keys = jnp.where(j_i > s_i, 1.0, 0.0).astype(BF16)

    def tile(j, masked):
        s0 = pl.multiple_of(j * T, T)
        strict = None
        if masked:
            key, qry = _key_query_index(qb, j)
            strict = key < qry
        heads = range(N_HEADS)
        r_prev = [r_scr[h] for h in heads]
        z = [_dot(k_ref[0, pl.ds(s0, T), (h // 2) * LANES:(h // 2 + 1) * LANES], qp[h]) for h in heads]
        log_take = [_log_sigmoid(z[h]) for h in heads]
        log_keep = [log_take[h] - z[h] for h in heads]
        if masked:
            log_keep = [jnp.where(strict, lk, 0.0) for lk in log_keep]
        pieces = [_split3(lk) for lk in log_keep]
        within = [functools.reduce(jnp.add, [_dot(later_keys, p) for p in pieces[h]])
                  for h in heads]
        a = [jnp.exp(log_take[h] + (r_prev[h] + within[h])) for h in heads]
        if masked:
            a = [jnp.where(strict, w, 0.0) for w in a]
        updates = [_dot(v_ref[0, j, h * HEAD_DIM:(h + 1) * HEAD_DIM, :], a[h].astype(BF16)) for h in heads]
        for h in heads:
            acc_scr[h] = acc_scr[h] + updates[h]
            r_scr[h] = r_prev[h] + within[h][0:1, :] + log_keep[h][0:1, :]

    def live():
        return jnp.max(r_scr[...]) > SB_CUTOFF

    def step(state):
        i, _ = state
        tile(qb - 1 - i, False)
        return i + 1, live()

    tile(qb, True)
    lax.while_loop(lambda st: (st[0] < qb) & st[1], step, (jnp.int32(0), live()))
    _store_heads(o_ref, [acc_scr[h] for h in range(N_HEADS)])


def _sb_call(pr):
    B, S, _ = pr['keys'].shape
    return pl.pallas_call(
        _sb_kernel,
        out_shape=jax.ShapeDtypeStruct((B, S, BR_WIDTH), F32),
        grid=(B, S // T),
        in_specs=[_q_spec(BR_WIDTH), _keys_spec(S, 2, A_SK), _v_spec(S, BR_WIDTH)],
        out_specs=_OUT_SPEC,
        scratch_shapes=_attn_scratch(HEAD_DIM),
        compiler_params=_attn_params(),
        name="stick_breaking",
    )(pr['sq_t'], pr['keys'], pr['sv_t'])


def _dsa_kernel(q_ref, k_ref, v_ref, qi_ref, ki_ref, sm_ref, o_ref, sc_scr, hi_scr, lo_scr, m_scr, acc_scr,
                *, k_sel):
    qb = pl.program_id(1)
    row = lax.broadcasted_iota(I32, (LANES, T), 0)
    qi = qi_ref[0, 0]
    zero = jnp.zeros_like(qi)
    qi_pad = [jnp.where((row >= IDX_DIM * h) & (row < IDX_DIM * (h + 1)), qi, zero)
              for h in range(IDX_HEADS)]
    wi = [sm_ref[0, 0, SM_WI + h:SM_WI + h + 1, :] for h in range(IDX_HEADS)]

    def score_tiles(js, diagonal=False):
        rows = [pl.ds(pl.multiple_of(j * T, T), T) for j in js]
        heads = range(IDX_HEADS)
        raw = [[_dot(ki_ref[0, r, :], qi_pad[h]) for h in heads] for r in rows]
        rel = [[jnp.maximum(raw_t[h], 0.0) for h in heads] for raw_t in raw]
        scs = [functools.reduce(jnp.add, [wi[h] * rel_t[h] for h in heads]) for rel_t in rel]
        scs = [jnp.where(sc == 0.0, 0.0, sc) for sc in scs]
        if diagonal:
            key, qry = _key_query_index(qb, js[0])
            scs = [jnp.where(key <= qry, scs[0], -jnp.inf)]
        halves = [_split_keys(sc) for sc in scs]
        for r, sc, (hi, lo) in zip(rows, scs, halves):
            sc_scr[r, :] = sc
            hi_scr[r, :] = hi
            lo_scr[r, :] = lo

    _for_tile_groups(qb, score_tiles, group=8)
    score_tiles([qb], diagonal=True)

    @pl.when(qb % 2 == 0)
    def _():
        pad = pl.ds(pl.multiple_of((qb + 1) * T, T), T)
        sc_scr[pad, :] = jnp.full((T, T), -jnp.inf, F32)
        hi_scr[pad, :] = jnp.full((T, T), LOW16, I16)
        lo_scr[pad, :] = jnp.full((T, T), LOW16, I16)

    n_valid = (qb * T + 1 + lax.broadcasted_iota(I32, (1, T), 1)).astype(F32)
    thr = _top_k_threshold_split(sc_scr, hi_scr, lo_scr, qb // 2 + 1, 2 * T, k_sel, n_valid)

    qp = _pair_pads(q_ref[0, 0])
    _softmax_init(m_scr, acc_scr)

    def attn_tiles(js, lazy=True):
        def inputs():
            scores = [[_dot(k_ref[0, pl.ds(pl.multiple_of(j * T, T), T),
                                  (h // 2) * LANES:(h // 2 + 1) * LANES], qp[h]) for j in js]
                      for h in range(N_HEADS)]
            valids = [sc_scr[pl.ds(pl.multiple_of(j * T, T), T), :] >= thr for j in js]
            return scores, valids, [_head_values(v_ref, j) for j in js]

        _softmax_update(inputs, m_scr, acc_scr, lazy)

    attn_tiles([qb], lazy=False)
    _for_tile_groups(qb, attn_tiles, group=4, first_fn=functools.partial(attn_tiles, lazy=False))
    _store_heads(o_ref, [_softmax_out(acc_scr, h) for h in range(N_HEADS)])


def _dsa_call(pr):
    B, S, _ = pr['keys'].shape
    return pl.pallas_call(
        functools.partial(_dsa_kernel, k_sel=min(DSA_TOPK_MAX, S // 4)),
        out_shape=jax.ShapeDtypeStruct((B, S, BR_WIDTH), F32),
        grid=(B, S // T),
        in_specs=[_q_spec(BR_WIDTH), _keys_spec(S, 2, A_KA), _v_spec(S, N_HEADS * HV_ROWS),
                  _q_spec(LANES), _keys_spec(S, 1, A_KI), _q_spec(LANES)],
        out_specs=_OUT_SPEC,
        scratch_shapes=[pltpu.VMEM((S, T), F32), pltpu.VMEM((S, T), I16), pltpu.VMEM((S, T), I16)]
                       + _attn_scratch(HV_ROWS),
        compiler_params=_attn_params(),
        name="dsa",
    )(pr['dq_t'], pr['keys'], pr['va_t'], pr['qi_t'], pr['keys'], pr['sm_t'])


def _compress_kernel(xk_ref, xv_ref, pek_ref, pev_ref, w1k_ref, w1v_ref, w2k_ref, w2v_ref,
                     o_ref, ot_ref, *, nc):
    def hidden(x_ref, pe_ref, w1_ref):
        x = x_ref[0]
        first = _dot((x + pe_ref[0:1, :]).astype(BF16), w1_ref[0])
        second = _dot((x + pe_ref[1:2, :]).astype(BF16), w1_ref[1])
        pre = first + pltpu.roll(second, nc - 1, 0)
        return (pre * _sigmoid(pre)).astype(BF16)

    kvc = (_dot(hidden(xk_ref, pek_ref, w1k_ref), w2k_ref[...])
           + _dot(hidden(xv_ref, pev_ref, w1v_ref), w2v_ref[...]))
    o_ref[0] = kvc.astype(BF16)
    ot_ref[0] = kvc.T.astype(BF16)


def _compress_call(l, xk, xv, pek, pev, w1k, w1v, w2k, w2v):
    B, nc, width = xk.shape
    full = lambda shape: _layer_spec(l, shape, 1)
    tok = pl.BlockSpec((1, nc, width), lambda b: (b, 0, 0))
    return pl.pallas_call(
        functools.partial(_compress_kernel, nc=nc),
        out_shape=(jax.ShapeDtypeStruct((B, nc, LANES), BF16), jax.ShapeDtypeStruct((B, LANES, nc), BF16)),
        grid=(B,),
        in_specs=[tok, tok, full((2, width)), full((2, width)), full((2, width, CMP_HIDDEN)),
                  full((2, width, CMP_HIDDEN)), full((CMP_HIDDEN, LANES)), full((CMP_HIDDEN, LANES))],
        out_specs=(pl.BlockSpec((1, nc, LANES), lambda b: (b, 0, 0)),
                   pl.BlockSpec((1, LANES, nc), lambda b: (b, 0, 0))),
        compiler_params=pltpu.CompilerParams(dimension_semantics=("arbitrary",),
                                             vmem_limit_bytes=VMEM_LIMIT),
        name="nsa_compress",
    )(xk, xv, pek, pev, w1k, w1v, w2k, w2v)


def _nsa_kernel(q_ref, kvc_ref, kvct_ref, ksw_ref, vsw_ref, sm_ref, o_ref,
                imp_scr, sel_scr, m_scr, acc_scr, *, n_blk, n_sel, nbp):
    qb = pl.program_id(1)
    ncp = kvc_ref.shape[1]
    q_t = q_ref[0, 0]
    zero = jnp.zeros((HEAD_DIM, T), q_t.dtype)
    heads = [q_t[h * HEAD_DIM:(h + 1) * HEAD_DIM, :] for h in range(N_HEADS)]
    q_lo = [jnp.concatenate([q, zero], axis=0) for q in heads]
    q_hi = [jnp.concatenate([zero, q], axis=0) for q in heads]
    gate = lambda br, h: sm_ref[0, 0, SM_GATE + br * N_HEADS + h:SM_GATE + br * N_HEADS + h + 1, :]
    t_row = qb * T + lax.broadcasted_iota(I32, (1, T), 1)

    kvc = kvc_ref[0]
    cmp_end = lax.broadcasted_iota(I32, (ncp, T), 0) * CMP_STRIDE + (CMP_LEN - 1)
    cmp_ok = cmp_end <= t_row
    cmp_logits = [jnp.where(cmp_ok, _dot(kvc, q_lo[h]), NEG) for h in range(N_HEADS)]
    cmp_tops = [jnp.max(s, axis=0, keepdims=True) for s in cmp_logits]
    cmp_e = [jnp.where(cmp_ok, jnp.exp2(s - m), 0.0) for s, m in zip(cmp_logits, cmp_tops)]
    cmp_den = [jnp.maximum(jnp.sum(e, axis=0, keepdims=True), 1e-30) for e in cmp_e]
    pcs = [e / d for e, d in zip(cmp_e, cmp_den)]
    o_cmp = [_dot(kvct_ref[0], pc.astype(BF16))[HEAD_DIM:, :] for pc in pcs]
    p_sum = functools.reduce(jnp.add, pcs)
    n_win = WINDOW // T + 1
    j0 = jnp.maximum(qb - (n_win - 1), 0)
    heads, wins = range(N_HEADS), range(n_win)
    oks = []
    for i in wins:
        key, qry = _key_query_index(qb, j0 + i)
        dist = qry - key
        oks.append((dist >= 0) & (dist < WINDOW))
    raw = [[_dot(ksw_ref[0, pl.ds(pl.multiple_of((j0 + i) * T, T), T), :], q_hi[h]) for i in wins]
           for h in heads]
    logits = [[jnp.where(oks[i], raw[h][i], NEG) for i in wins] for h in heads]
    tops = _tile_tops(logits)
    es = [[jnp.where(oks[i], jnp.exp2(logits[h][i] - tops[h]), 0.0) for i in wins] for h in heads]
    dens = [functools.reduce(jnp.add, [jnp.sum(e, axis=0, keepdims=True) for e in es[h]]) for h in heads]
    pws = [[(es[h][i] / jnp.maximum(dens[h], 1e-30)).astype(BF16) for i in wins] for h in heads]
    o_win = [functools.reduce(jnp.add, [_dot(vsw_ref[0, j0 + i, HV_ROWS:, :], pws[h][i]) for i in wins])
             for h in heads]
    per_blk = SLC_BLOCK // CMP_STRIDE
    b_i = lax.broadcasted_iota(I32, (nbp, ncp), 0)
    c_i = lax.broadcasted_iota(I32, (nbp, ncp), 1)
    group = jnp.where(_div_pow2(c_i, per_blk) == b_i, 1.0, 0.0).astype(BF16)
    imp = _dot3x(group, p_sum)
    blk = lax.broadcasted_iota(I32, (nbp, T), 0)
    cur = _div_pow2(t_row, SLC_BLOCK)
    forced = (blk == 0) | (blk == cur) | (blk == cur - 1)
    imp = jnp.where(forced, FORCE_SCORE, jnp.where(blk <= cur, imp, -1.0))
    imp = jnp.where(blk < n_blk, imp, -2.0)
    imp_scr[...] = imp
    thr = _top_k_threshold(imp_scr, 1, nbp, n_sel, jnp.full((1, T), float(nbp), F32))
    sel_scr[...] = jnp.where(imp_scr[...] >= thr, 1.0, 0.0)

    _softmax_init(m_scr, acc_scr)
    blk_per_tile = T // SLC_BLOCK

    def slc_tiles(js, diagonal=False):
        def inputs():
            valids, scores = [], [[] for _ in range(N_HEADS)]
            for j in js:
                valid = jnp.concatenate(
                    [jnp.broadcast_to(sel_scr[pl.ds(j * blk_per_tile + b, 1), :], (SLC_BLOCK, T))
                     for b in range(blk_per_tile)], axis=0) > 0.5
                if diagonal:
                    key, qry = _key_query_index(qb, j)
                    valid = valid & (key <= qry)
                valids.append(valid)
                kt = ksw_ref[0, pl.ds(pl.multiple_of(j * T, T), T), :]
                for h in range(N_HEADS):
                    scores[h].append(_dot(kt, q_lo[h]))
            return scores, valids, [[vsw_ref[0, j, :HV_ROWS, :]] * N_HEADS for j in js]

        _softmax_update(inputs, m_scr, acc_scr, lazy=not diagonal)

    slc_tiles([qb], diagonal=True)
    _for_tile_groups(qb, slc_tiles, group=4)
    _store_heads(o_ref, [gate(0, h) * o_cmp[h] + gate(1, h) * _softmax_out(acc_scr, h)
                         + gate(2, h) * o_win[h] for h in heads])


def _nsa_call(pr, kvc, kvc_t):
    B, S, _ = pr['keys'].shape
    n_blk = S // SLC_BLOCK
    nbp = max(LANES, n_blk)
    ncp = kvc.shape[1]
    return pl.pallas_call(
        functools.partial(_nsa_kernel, n_blk=n_blk, n_sel=min(SLC_TOPN, n_blk), nbp=nbp),
        out_shape=jax.ShapeDtypeStruct((B, S, BR_WIDTH), F32),
        grid=(B, S // T),
        in_specs=[_q_spec(BR_WIDTH),
                  pl.BlockSpec((1, ncp, LANES), lambda b, i: (b, 0, 0)),
                  pl.BlockSpec((1, LANES, ncp), lambda b, i: (b, 0, 0)),
                  _keys_spec(S, 1, A_KSW), _v_spec(S, VSW_ROWS), _q_spec(LANES)],
        out_specs=_OUT_SPEC,
        scratch_shapes=[pltpu.VMEM((nbp, T), F32), pltpu.VMEM((nbp, T), F32)] + _attn_scratch(HV_ROWS),
        compiler_params=_attn_params(),
        name="nsa",
    )(pr['nq_t'], kvc, kvc_t, pr['keys'], pr['vsw_t'], pr['sm_t'])


def _mem_kv_kernel(mem_ref, g_ref, w_ref, o_ref):
    o_ref[0] = _dot(_rmsnorm(mem_ref[0], g_ref[...]).astype(BF16), w_ref[...]).astype(BF16)


def _mem_kv_call(l, mem, g, w):
    B, M, D = mem.shape
    return pl.pallas_call(
        _mem_kv_kernel,
        out_shape=jax.ShapeDtypeStruct((B, M, 2 * BR_WIDTH), BF16),
        grid=(B,),
        in_specs=[pl.BlockSpec((1, M, D), lambda b: (b, 0, 0)), _layer_spec(l, (1, D), 1),
                  _layer_spec(l, (D, 2 * BR_WIDTH), 1)],
        out_specs=pl.BlockSpec((1, M, 2 * BR_WIDTH), lambda b: (b, 0, 0)),
        name="mem_kv",
    )(mem, g, w)


def _out_kernel(x_ref, g_ref, ya_ref, yb_ref, yc_ref, yd_ref, mq_ref, mkv_ref, wz_ref, wm_ref,
                wb_ref, wo_ref, fg_ref, o_ref, h_scr, *, final):
    x = x_ref[0]
    h_scr[...] = _rmsnorm(x, g_ref[...]).astype(BF16)
    q = mq_ref[0]
    lane = lax.broadcasted_iota(I32, (T, LANES), 1)
    lo = lane < HEAD_DIM
    zero = jnp.zeros((T, LANES), q.dtype)
    heads = range(N_HEADS)
    chunks = [q[:, (h // 2) * LANES:(h // 2 + 1) * LANES] for h in heads]
    qh = [jnp.where(lo, c, zero) if h % 2 == 0 else jnp.where(lo, zero, c) for h, c in zip(heads, chunks)]
    mk = [mkv_ref[0, :, (h // 2) * LANES:(h // 2 + 1) * LANES] for h in heads]
    mv = [mkv_ref[0, :, BR_WIDTH + (h // 2) * LANES:BR_WIDTH + (h // 2 + 1) * LANES] for h in heads]
    s = [_dot_nt(qh[h], mk[h]) for h in heads]
    e = [jnp.exp(s[h] - jnp.max(s[h], axis=1, keepdims=True)) for h in heads]
    p = [(e[h] / jnp.sum(e[h], axis=1, keepdims=True)).astype(BF16) for h in heads]
    o = [_dot(p[h], mv[h]) for h in heads]
    y_e = jnp.concatenate([jnp.where(lo, o[0], o[1]), jnp.where(lo, o[2], o[3])], axis=1)
    branches = range(N_BRANCH)
    d = x.shape[1]
    ys_in = (ya_ref[0], yb_ref[0], yc_ref[0], yd_ref[0], y_e)
    z = [_dot(h_scr[...], wz_ref[n]) for n in branches]
    ys = [(ys_in[n] * (z[n] * _sigmoid(z[n]))).astype(BF16) for n in branches]
    gates = [_sigmoid(_dot(h_scr[...], wm_ref[:, n * d:(n + 1) * d])) for n in branches]
    projected = [_dot(ys[n], wb_ref[n]) for n in branches]
    merged = functools.reduce(jnp.add, [gates[n] * projected[n] for n in branches])
    out = x + _dot(merged.astype(BF16), wo_ref[...])
    o_ref[0] = _rmsnorm(out, fg_ref[...]) if final else out


def _out_call(l, x, g, ys, keys, mkv, wz, wm, wb, wo, fg, final):
    B, S, D = x.shape
    M = mkv.shape[1]
    full = lambda shape: _layer_spec(l, shape, 2)
    tile = lambda n: pl.BlockSpec((1, T, n), lambda b, i: (b, i, 0))
    return pl.pallas_call(
        functools.partial(_out_kernel, final=final),
        out_shape=jax.ShapeDtypeStruct((B, S, D), F32),
        grid=(B, S // T),
        in_specs=[tile(D), full((1, D))] + [tile(BR_WIDTH)] * 4
                 + [pl.BlockSpec((1, T, BR_WIDTH), lambda b, i: (b, i, A_MQ // 2)),
                    pl.BlockSpec((1, M, 2 * BR_WIDTH), lambda b, i: (b, 0, 0)),
                    full((N_BRANCH, D, BR_WIDTH)), full((D, N_BRANCH * D)), full((N_BRANCH, BR_WIDTH, D)),
                    full((D, D)), pl.BlockSpec((1, D), lambda b, i: (0, 0))],
        out_specs=tile(D),
        scratch_shapes=[pltpu.VMEM((T, D), BF16)],
        compiler_params=pltpu.CompilerParams(dimension_semantics=("arbitrary", "arbitrary"),
                                             vmem_limit_bytes=VMEM_LIMIT),
        name="merge_out",
    )(x, g, *ys, keys, mkv, wz, wm, wb, wo, fg)


def kernel(x, mem, positions, norm_g, w_in, kv_norm, w_uk, w_uv, fox_bias, nsa_pe_k, nsa_pe_v,
           nsa_wc1_k, nsa_wc2_k, nsa_wc1_v, nsa_wc2_v, mem_norm, w_mem_kv, w_branch, w_out, final_norm):
    B, S, D = x.shape
    depth = norm_g.shape[0]
    assert S % (2 * T) == 0 and S >= WINDOW + T and WINDOW % T == 0 and D == 1024
    cos, sin = _rope_tables(positions)
    w1, wz, wm = _proj_weight(w_in)
    fb = jnp.zeros((depth, 1, LANES), F32).at[:, 0, SM_CUM:SM_CUM + N_HEADS].set(fox_bias)
    nc = S // CMP_STRIDE
    tok_w = CMP_STRIDE * NSA_KV_DIM
    pad_k = jnp.zeros((depth, CMP_HIDDEN, LANES), F32).at[..., :NSA_KV_DIM].set(nsa_wc2_k).astype(BF16)
    pad_v = jnp.zeros((depth, CMP_HIDDEN, LANES), F32).at[..., NSA_KV_DIM:].set(nsa_wc2_v).astype(BF16)
    g, kvn, mn = norm_g[:, None, :], kv_norm[:, None, :], mem_norm[:, None, :]
    wuk, wuv, wmkv = w_uk.astype(BF16), w_uv.astype(BF16), w_mem_kv.astype(BF16)
    pek, pev = nsa_pe_k.reshape(depth, 2, tok_w), nsa_pe_v.reshape(depth, 2, tok_w)
    w1k = nsa_wc1_k.reshape(depth, 2, tok_w, CMP_HIDDEN).astype(BF16)
    w1v = nsa_wc1_v.reshape(depth, 2, tok_w, CMP_HIDDEN).astype(BF16)
    wb, wo = w_branch.astype(BF16), w_out.astype(BF16)
    for l in range(depth):
        pr = _proj_call(l, x, g, w1, kvn, wuk, wuv, fb, cos, sin)
        kvc, kvc_t = _compress_call(l, pr['kc_tok'].reshape(B, nc, tok_w), pr['vc_tok'].reshape(B, nc, tok_w),
                                    pek, pev, w1k, w1v, pad_k, pad_v)
        ys = (_dsa_call(pr), _fox_call(pr), _sb_call(pr), _nsa_call(pr, kvc, kvc_t))
        mkv = _mem_kv_call(l, mem, mn, wmkv)
        x = _out_call(l, x, g, ys, pr['keys'], mkv, wz, wm, wb, wo, final_norm[None], l == depth - 1)
    return x
```

```python
import functools

import jax
import jax.numpy as jnp
from jax import lax
from jax.experimental import pallas as pl
from jax.experimental.pallas import tpu as pltpu

F32 = jnp.float32
BF16 = jnp.bfloat16
I32 = jnp.int32
I16 = jnp.int16

LANES = 128
SUBLANES = 8
D_MODEL = 1024
N_HEADS = 4
HEAD_DIM = 64
BR_WIDTH = N_HEADS * HEAD_DIM
N_BRANCH = 5
ROPE_THETA = 10000.0
EPS = 1e-6
DSA_TOPK_MAX = 256
IDX_HEADS = 4
IDX_DIM = 32
KV_LATENT = 128
NSA_KV_DIM = 64
CMP_LEN = 32
CMP_STRIDE = 16
CMP_HIDDEN = 128
SLC_BLOCK = 64
SLC_TOPN = 16
WINDOW = 512
FORCE_SCORE = 1e4
Q_SCALE = HEAD_DIM ** -0.5
LOG2E = 1.4426950408889634
Q_SCALE2 = Q_SCALE * LOG2E
IDX_SCALE = (IDX_DIM ** -0.5) * (IDX_HEADS ** -0.5)
NEG = -1e30
SB_CUTOFF = -104.0
LAZY_LIMIT = 64.0
VMEM_LIMIT = 56 * 1024 * 1024
T = 256
ONES_ROWS = 16
HV_ROWS = HEAD_DIM + ONES_ROWS
VSW_ROWS = HV_ROWS + NSA_KV_DIM

IN_LAYOUT = (
    ('dsa_q', BR_WIDTH), ('dsa_ckv', KV_LATENT), ('idx_q', IDX_HEADS * IDX_DIM), ('idx_k', IDX_DIM),
    ('idx_w', IDX_HEADS), ('dsa_z', BR_WIDTH),
    ('fox_q', BR_WIDTH), ('fox_k', BR_WIDTH), ('fox_v', BR_WIDTH), ('fox_f', N_HEADS), ('fox_z', BR_WIDTH),
    ('sb_q', BR_WIDTH), ('sb_k', BR_WIDTH), ('sb_v', BR_WIDTH), ('sb_z', BR_WIDTH),
    ('nsa_q', BR_WIDTH), ('nsa_kc', NSA_KV_DIM), ('nsa_vc', NSA_KV_DIM), ('nsa_ks', NSA_KV_DIM),
    ('nsa_vs', NSA_KV_DIM), ('nsa_kw', NSA_KV_DIM), ('nsa_vw', NSA_KV_DIM), ('nsa_g', 3 * N_HEADS),
    ('nsa_z', BR_WIDTH), ('mem_q', BR_WIDTH), ('mem_z', BR_WIDTH), ('merge', N_BRANCH * D_MODEL),
)

A_FK, A_KA, A_SK, A_KI, A_KSW, A_MQ = 0, 4, 6, 8, 9, 10
A_COLS = 12 * LANES
SM_WI, SM_CUM, SM_GATE = 0, 4, 8
FOX_CK, FOX_ONE, FOX_END = HEAD_DIM, HEAD_DIM + 3, HEAD_DIM + 6


def _dot(a, b):
    return jnp.dot(a, b, preferred_element_type=F32)


def _dot_nt(a, b):
    return lax.dot_general(a, b, (((1,), (1,)), ((), ())), preferred_element_type=F32)


def _split3(x):
    hi = x.astype(BF16)
    r1 = x - hi.astype(F32)
    mid = r1.astype(BF16)
    lo = (r1 - mid.astype(F32)).astype(BF16)
    return hi, mid, lo


def _dot3x(m01, x):
    hi, mid, lo = _split3(x)
    return _dot(m01, hi) + _dot(m01, mid) + _dot(m01, lo)


def _div_pow2(x, n):
    assert n & (n - 1) == 0
    return x >> (n.bit_length() - 1)


def _sigmoid(x):
    return 1.0 / (1.0 + jnp.exp(-x))


def _log_sigmoid(x):
    return jnp.minimum(x, 0.0) - jnp.log(1.0 + jnp.exp(-jnp.abs(x)))


def _rmsnorm(x, g):
    return x * lax.rsqrt(jnp.mean(x * x, axis=-1, keepdims=True) + EPS) * g


def _rope128(x, cos, sin_signed, half):
    lane = lax.broadcasted_iota(I32, x.shape, 1)
    first = (lane & (2 * half - 1)) < half
    partner = jnp.where(first, pltpu.roll(x, LANES - half, 1), pltpu.roll(x, half, 1))
    return x * cos + partner * sin_signed


def _rope_table_kernel(pos_ref, inv_ref, sgn_ref, cos_ref, sin_ref):
    ang = pos_ref[0] * inv_ref[...]
    cos_ref[0] = jnp.cos(ang)
    sin_ref[0] = jnp.sin(ang) * sgn_ref[...]


def _rope_tables(positions):
    B, S = positions.shape
    lane = jnp.arange(LANES)

    def inv_row(dh):
        inv = ROPE_THETA ** (-jnp.arange(0, dh, 2, dtype=F32) / dh)
        return inv[(lane % dh) % (dh // 2)]

    def sgn_row(dh):
        return jnp.where((lane % dh) < dh // 2, -1.0, 1.0).astype(F32)

    inv = jnp.concatenate([inv_row(HEAD_DIM), inv_row(IDX_DIM)])[None, :]
    sgn = jnp.concatenate([sgn_row(HEAD_DIM), sgn_row(IDX_DIM)])[None, :]
    pos = positions.astype(F32)[..., None]
    row = pl.BlockSpec((1, 2 * LANES), lambda b, i: (0, 0))
    tab = pl.BlockSpec((1, T, 2 * LANES), lambda b, i: (b, i, 0))
    return pl.pallas_call(
        _rope_table_kernel,
        out_shape=(jax.ShapeDtypeStruct((B, S, 2 * LANES), F32),) * 2,
        grid=(B, S // T),
        in_specs=[pl.BlockSpec((1, T, 1), lambda b, i: (b, i, 0)), row, row],
        out_specs=(tab, tab),
        name="rope_tables",
    )(pos, inv, sgn)


W_DQ, W_CKV, W_QI, W_KI, W_SM, W_FQ, W_SQ, W_NQ, W_MQ, W_KVC, W_KSW, W_VSW = (
    0, 256, 384, 512, 640, 768, 1536, 2304, 2560, 2816, 2944, 3072)
W_COLS = 3200

PROJ_OUTS = (
    ('keys', A_COLS, BF16), ('kc_tok', NSA_KV_DIM, F32), ('vc_tok', NSA_KV_DIM, F32),
    ('dq_t', BR_WIDTH, BF16), ('fq_t', N_HEADS * LANES, BF16), ('sq_t', BR_WIDTH, BF16),
    ('nq_t', BR_WIDTH, BF16), ('qi_t', LANES, BF16), ('sm_t', LANES, F32),
    ('va_t', N_HEADS * HV_ROWS, BF16), ('fv_t', N_HEADS * HV_ROWS, BF16), ('sv_t', BR_WIDTH, BF16),
    ('vsw_t', VSW_ROWS, BF16),
)
ROW_MAJOR = ('keys', 'kc_tok', 'vc_tok')


def _proj_kernel(x_ref, g_ref, w_ref, kvn_ref, wuk_ref, wuv_ref, fb_ref, cos_ref, sin_ref,
                 *refs):
    out = dict(zip([n for n, _, _ in PROJ_OUTS], refs))
    h_scr, carry_scr = refs[len(PROJ_OUTS):]
    keys = out['keys']
    h_scr[...] = _rmsnorm(x_ref[0], g_ref[...]).astype(BF16)
    c64, s64 = cos_ref[0, :, :LANES], sin_ref[0, :, :LANES]
    c32, s32 = cos_ref[0, :, LANES:], sin_ref[0, :, LANES:]
    lane = lax.broadcasted_iota(I32, (T, LANES), 1)
    ones_rows = jnp.ones((ONES_ROWS, T), BF16)

    def proj(c0, n):
        return _dot(h_scr[...], w_ref[:, c0:c0 + n])

    def put(unit, val):
        keys[0, :, unit * LANES:unit * LANES + val.shape[1]] = val.astype(BF16)

    def rope64(v, j):
        return _rope128(v[:, j * LANES:(j + 1) * LANES], c64, s64, HEAD_DIM // 2)

    def put_t(ref, row0, chunk):
        ref[0, 0, row0:row0 + LANES, :] = chunk.T.astype(ref.dtype)

    def put_values_t(ref, v):
        for j in range(2):
            pair_t = v[:, j * LANES:(j + 1) * LANES].T.astype(BF16)
            for odd in range(2):
                r0 = (2 * j + odd) * HV_ROWS
                ref[0, 0, r0:r0 + HEAD_DIM, :] = pair_t[odd * HEAD_DIM:(odd + 1) * HEAD_DIM, :]
                ref[0, 0, r0 + HEAD_DIM:r0 + HV_ROWS, :] = ones_rows

    p = proj(W_DQ, BR_WIDTH)
    for j in range(2):
        put_t(out['dq_t'], j * LANES, rope64(p, j) * Q_SCALE2)
    c_kv = _rmsnorm(proj(W_CKV, KV_LATENT), kvn_ref[...]).astype(BF16)
    k_a = _dot(c_kv, wuk_ref[...])
    for j in range(2):
        put(A_KA + j, rope64(k_a, j))
    put_values_t(out['va_t'], _dot(c_kv, wuv_ref[...]))
    put_t(out['qi_t'], 0, _rope128(proj(W_QI, LANES), c32, s32, IDX_DIM // 2))
    put(A_KI, _rope128(proj(W_KI, LANES), c32, s32, IDX_DIM // 2))
    p = proj(W_SM, LANES)
    log_f = _log_sigmoid(p + fb_ref[...])
    r_i = lax.broadcasted_iota(I32, (T, T), 0)
    c_i = lax.broadcasted_iota(I32, (T, T), 1)
    tri = jnp.where(r_i >= c_i, 1.0, 0.0).astype(BF16)

    @pl.when(pl.program_id(1) == 0)
    def _():
        carry_scr[...] = jnp.zeros_like(carry_scr)

    cum = _dot3x(tri, log_f) + carry_scr[...]
    carry_scr[...] = cum[T - 1:T, :]
    small_t = jnp.where(lane < SM_CUM, p * IDX_SCALE,
                        jnp.where(lane < SM_GATE, cum, _sigmoid(p))).T
    out['sm_t'][0, 0] = small_t
    q_f = proj(W_FQ, BR_WIDTH) * Q_SCALE2
    cum2, cum2_t = cum * LOG2E, small_t * LOG2E
    k_f = proj(W_FQ + 256, BR_WIDTH)
    row = lax.broadcasted_iota(I32, (HEAD_DIM, T), 0)
    for h in range(N_HEADS):
        pair, odd = h // 2, h % 2
        k_chunk = k_f[:, pair * LANES:(pair + 1) * LANES]
        k_h = pltpu.roll(k_chunk, HEAD_DIM, 1) if odd else k_chunk
        ck = [c.astype(F32) for c in _split3(cum2[:, SM_CUM + h:SM_CUM + h + 1])]
        k_aug = jnp.where(lane < HEAD_DIM, k_h,
                          jnp.where(lane == FOX_CK, ck[0],
                                    jnp.where(lane == FOX_CK + 1, ck[1],
                                              jnp.where(lane == FOX_CK + 2, ck[2],
                                                        jnp.where(lane < FOX_END, 1.0, 0.0)))))
        put(A_FK + h, k_aug)
        q_ht = q_f[:, pair * LANES:(pair + 1) * LANES].T[odd * HEAD_DIM:(odd + 1) * HEAD_DIM, :]
        cq = [c.astype(F32) for c in _split3(cum2_t[SM_CUM + h:SM_CUM + h + 1, :])]
        aug = jnp.where(row < 3, -1.0,
                        jnp.where(row == 3, cq[0], jnp.where(row == 4, cq[1],
                                                             jnp.where(row == 5, cq[2], 0.0))))
        out['fq_t'][0, 0, h * LANES:(h + 1) * LANES, :] = jnp.concatenate(
            [q_ht, aug], axis=0).astype(BF16)
    put_values_t(out['fv_t'], proj(W_FQ + 512, BR_WIDTH))
    p = proj(W_SQ, BR_WIDTH) * Q_SCALE
    for j in range(2):
        put_t(out['sq_t'], j * LANES, p[:, j * LANES:(j + 1) * LANES])
    put(A_SK, proj(W_SQ + 256, BR_WIDTH))
    p = proj(W_SQ + 512, BR_WIDTH)
    for j in range(2):
        put_t(out['sv_t'], j * LANES, p[:, j * LANES:(j + 1) * LANES])
    put(A_MQ, proj(W_MQ, BR_WIDTH) * Q_SCALE)
    p = proj(W_NQ, BR_WIDTH)
    for j in range(2):
        put_t(out['nq_t'], j * LANES, rope64(p, j) * Q_SCALE2)
    p = proj(W_KVC, LANES)
    out['kc_tok'][0] = rope64(p, 0)[:, :NSA_KV_DIM]
    out['vc_tok'][0] = p[:, NSA_KV_DIM:]
    put(A_KSW, rope64(proj(W_KSW, LANES), 0))
    vsw_t = proj(W_VSW, LANES).T.astype(BF16)
    out['vsw_t'][0, 0, :NSA_KV_DIM, :] = vsw_t[:NSA_KV_DIM, :]
    out['vsw_t'][0, 0, NSA_KV_DIM:HV_ROWS, :] = ones_rows
    out['vsw_t'][0, 0, HV_ROWS:, :] = vsw_t[NSA_KV_DIM:, :]


def _proj_weight(w_in):
    off, o = {}, 0
    for name, n in IN_LAYOUT:
        off[name] = (o, n)
        o += n

    def col(name):
        s, n = off[name]
        return w_in[..., s:s + n]

    zeros = lambda n: jnp.zeros(w_in.shape[:-1] + (n,), w_in.dtype)
    groups = [col('dsa_q'), col('dsa_ckv'), col('idx_q'), col('idx_k'), col('idx_k'), col('idx_k'),
              col('idx_k'), col('idx_w'), col('fox_f'), col('nsa_g'), zeros(LANES - 20),
              col('fox_q'), col('fox_k'), col('fox_v'), col('sb_q'), col('sb_k'), col('sb_v'),
              col('nsa_q'), col('mem_q'), col('nsa_kc'), col('nsa_vc'), col('nsa_ks'), col('nsa_kw'),
              col('nsa_vs'), col('nsa_vw')]
    w1 = jnp.concatenate(groups, axis=-1).astype(BF16)
    wz = jnp.stack([col(n) for n in ('dsa_z', 'fox_z', 'sb_z', 'nsa_z', 'mem_z')], axis=-3).astype(BF16)
    return w1, wz, col('merge').astype(BF16)


def _layer_spec(l, shape, grid_rank):
    if grid_rank == 1:
        return pl.BlockSpec((None,) + shape, lambda b: (l,) + (0,) * len(shape))
    return pl.BlockSpec((None,) + shape, lambda b, i: (l,) + (0,) * len(shape))


def _proj_call(l, x, g, w1, kvn, wuk, wuv, fb, cos, sin):
    B, S, D = x.shape
    full = lambda shape: _layer_spec(l, shape, 2)
    tile = lambda n: pl.BlockSpec((1, T, n), lambda b, i: (b, i, 0))
    tile_t = lambda n: pl.BlockSpec((1, 1, n, T), lambda b, i: (b, i, 0, 0))
    shapes, specs = [], []
    for name, n, dt in PROJ_OUTS:
        if name in ROW_MAJOR:
            shapes.append(jax.ShapeDtypeStruct((B, S, n), dt))
            specs.append(tile(n))
        else:
            shapes.append(jax.ShapeDtypeStruct((B, S // T, n, T), dt))
            specs.append(tile_t(n))
    outs = pl.pallas_call(
        _proj_kernel,
        out_shape=tuple(shapes),
        grid=(B, S // T),
        in_specs=[tile(D), full((1, D)), full((D, W_COLS)), full((1, KV_LATENT)),
                  full((KV_LATENT, BR_WIDTH)), full((KV_LATENT, BR_WIDTH)), full((1, LANES)),
                  tile(2 * LANES), tile(2 * LANES)],
        out_specs=tuple(specs),
        scratch_shapes=[pltpu.VMEM((T, D), BF16), pltpu.VMEM((1, LANES), F32)],
        compiler_params=pltpu.CompilerParams(dimension_semantics=("arbitrary", "arbitrary"),
                                             vmem_limit_bytes=VMEM_LIMIT),
        name="proj",
    )(x, g, w1, kvn, wuk, wuv, fb, cos, sin)
    return dict(zip([n for n, _, _ in PROJ_OUTS], outs))


def _key_query_index(qb, j):
    key = j * T + lax.broadcasted_iota(I32, (T, T), 0)
    qry = qb * T + lax.broadcasted_iota(I32, (T, T), 1)
    return key, qry


def _pair_pads(q_t):
    row = lax.broadcasted_iota(I32, (LANES, T), 0)
    top = row < HEAD_DIM
    zero = jnp.zeros((LANES, T), q_t.dtype)
    out = []
    for pair in range(2):
        chunk = q_t[pair * LANES:(pair + 1) * LANES, :]
        out.append(jnp.where(top, chunk, zero))
        out.append(jnp.where(top, zero, chunk))
    return out


def _masked_tiles(scores, valids):
    return [[s if v is None else jnp.where(v, s, 2 * NEG) for s, v in zip(scores_h, valids)]
            for scores_h in scores]


def _tile_tops(tiles):
    return [functools.reduce(jnp.maximum, [jnp.max(s, axis=0, keepdims=True) for s in tiles_h])
            for tiles_h in tiles]


def _weighted_values(tiles, values, m):
    weights = [[jnp.exp2(s - m[h]).astype(BF16) for s in tiles_h] for h, tiles_h in enumerate(tiles)]
    parts = [[_dot(values[i][h], w) for i, w in enumerate(weights_h)] for h, weights_h in enumerate(weights)]
    return [functools.reduce(jnp.add, parts_h) for parts_h in parts]


def _softmax_update(inputs_fn, m_scr, acc_scr, lazy=False):
    if not lazy:
        _softmax_update_exact(*inputs_fn(), m_scr, acc_scr)
        return
    scores, valids, values = inputs_fn()
    m_ref = [m_scr[h] for h in range(N_HEADS)]
    tiles = _masked_tiles(scores, valids)
    updates = _weighted_values(tiles, values, m_ref)
    tops = _tile_tops(tiles)
    overshoot = functools.reduce(jnp.maximum, [jnp.max(tops[h] - m_ref[h]) for h in range(N_HEADS)])
    safe = overshoot <= LAZY_LIMIT

    @pl.when(safe)
    def _():
        for h in range(N_HEADS):
            m_new = jnp.maximum(m_ref[h], tops[h])
            acc_scr[h] = jnp.exp2(m_ref[h] - m_new) * (acc_scr[h] + updates[h])
            m_scr[h] = m_new

    @pl.when(jnp.logical_not(safe))
    def _():
        _softmax_update_exact(*inputs_fn(), m_scr, acc_scr)


def _softmax_update_exact(scores, valids, values, m_scr, acc_scr):
    m_prev = [m_scr[h] for h in range(N_HEADS)]
    tiles = _masked_tiles(scores, valids)
    m_new = [jnp.maximum(m, top) for m, top in zip(m_prev, _tile_tops(tiles))]
    updates = _weighted_values(tiles, values, m_new)
    for h in range(N_HEADS):
        acc_scr[h] = jnp.exp2(m_prev[h] - m_new[h]) * acc_scr[h] + updates[h]
        m_scr[h] = m_new[h]


def _for_tile_groups(n, group_fn, group, first_fn=None):
    def full(i, c):
        group_fn([group * i + g for g in range(group)])
        return c

    if first_fn is not None:
        @pl.when(n >= group)
        def _():
            first_fn(list(range(group)))

    lax.fori_loop(0 if first_fn is None else 1, n // group, full, 0)
    size = group // 2
    while size:
        start = (n // (2 * size)) * (2 * size)

        @pl.when((n & size) != 0)
        def _(start=start, size=size):
            group_fn([start + g for g in range(size)])

        size //= 2


INT_MIN = -2 ** 31
LOW16 = -2 ** 15
KEY_ABOVE_NEG_INF = 0x80800000 - 2 ** 32


def _key_to_float(c):
    bits = c ^ ((c >> 31) & 0x7FFFFFFF)
    return lax.bitcast_convert_type(bits, F32)


def _count_rows(ref, n_steps, step_rows, pred, cand):
    pack = SUBLANES * (4 // ref.dtype.itemsize)
    cand_b = jnp.broadcast_to(cand, (pack, T)).astype(ref.dtype)
    one, zero = jnp.ones((pack, T), ref.dtype), jnp.zeros((pack, T), ref.dtype)

    def body(c, accs):
        blk = ref[pl.ds(pl.multiple_of(c * step_rows, step_rows), step_rows), :]
        accs = list(accs)
        for r in range(step_rows // pack):
            hit = jnp.where(pred(blk[r * pack:(r + 1) * pack, :], cand_b), one, zero)
            accs[r % len(accs)] = accs[r % len(accs)] + hit
        return tuple(accs)

    accs = lax.fori_loop(0, n_steps, body, (zero,) * 4)
    total = functools.reduce(jnp.add, [a.astype(I32) if a.dtype != F32 else a for a in accs])
    return jnp.sum(total, axis=0, keepdims=True).astype(F32)


_GE = lambda a, b: a >= b
_GT = lambda a, b: a > b


def _bisect(count_ge, target, lowest, n_bits, cnt_lowest, to_cand=lambda c: c):
    cnt = count_ge(to_cand(jnp.zeros((1, T), I32)))
    take = cnt >= target
    state = (jnp.where(take, 0, lowest).astype(I32), jnp.where(take, cnt, cnt_lowest),
             jnp.where(take, 0.0, cnt))

    def step(i, state):
        val, cnt_val, cnt_next = state
        cand = val + jnp.left_shift(jnp.int32(1), n_bits - 2 - i)
        cnt = count_ge(to_cand(cand))
        take = cnt >= target
        return jnp.where(take, cand, val), jnp.where(take, cnt, cnt_val), jnp.where(take, cnt_next, cnt)

    return lax.fori_loop(0, n_bits - 1, step, state)


def _demote_ties(sc_ref, n_steps, step_rows, thr, k):
    need = float(k) - _count_rows(sc_ref, n_steps, step_rows, _GT, thr)
    s_i = lax.broadcasted_iota(I32, (LANES, LANES), 0)
    j_i = lax.broadcasted_iota(I32, (LANES, LANES), 1)
    earlier = jnp.where(j_i < s_i, 1.0, 0.0).astype(BF16)

    def body(c, seen):
        s0 = pl.multiple_of(c * LANES, LANES)
        blk = sc_ref[pl.ds(s0, LANES), :]
        eq = blk == thr
        eq_f = jnp.where(eq, 1.0, 0.0)
        rank = seen + _dot(earlier, eq_f.astype(BF16))
        sc_ref[pl.ds(s0, LANES), :] = jnp.where(eq & (rank >= need), -jnp.inf, blk)
        return seen + jnp.sum(eq_f, axis=0, keepdims=True)

    lax.fori_loop(0, n_steps * (step_rows // LANES), body, jnp.zeros((1, T), F32))


def _top_k_threshold(sc_ref, n_steps, step_rows, k, n_valid):
    kf = float(k)
    key, cnt_key, _ = _bisect(lambda c: _count_rows(sc_ref, n_steps, step_rows, _GE, c), kf, INT_MIN, 32,
                              n_valid, to_cand=_key_to_float)
    thr = _key_to_float(jnp.maximum(key, KEY_ABOVE_NEG_INF))

    @pl.when(jnp.max(cnt_key) > kf)
    def _():
        _demote_ties(sc_ref, n_steps, step_rows, thr, k)

    return thr


def _split_keys(sc):
    bits = lax.bitcast_convert_type(sc, I32)
    key = bits ^ ((bits >> 31) & 0x7FFFFFFF)
    return (key >> 16).astype(I16), (key ^ -LOW16).astype(I16)


def _top_k_threshold_split(sc_ref, hi_ref, lo_ref, n_steps, step_rows, k, n_valid):
    kf = float(k)
    rows = float(step_rows) * n_steps.astype(F32)
    high, cnt_high, above = _bisect(lambda c: _count_rows(hi_ref, n_steps, step_rows, _GE, c), kf,
                                    LOW16, 16, rows)
    high_b = jnp.broadcast_to(high, (step_rows, T)).astype(I16)

    def keep_bucket(c, carry):
        rs = pl.ds(pl.multiple_of(c * step_rows, step_rows), step_rows)
        lo_ref[rs, :] = jnp.where(hi_ref[rs, :] == high_b, lo_ref[rs, :], jnp.int16(LOW16))
        return carry

    lax.fori_loop(0, n_steps, keep_bucket, 0)
    low, cnt_low, _ = _bisect(lambda c: _count_rows(lo_ref, n_steps, step_rows, _GE, c), kf - above,
                              LOW16, 16, cnt_high - above)
    key = jnp.left_shift(high, 16) + (low - LOW16)
    thr = _key_to_float(jnp.maximum(key, KEY_ABOVE_NEG_INF))

    @pl.when(jnp.max(jnp.where(n_valid > kf, above + cnt_low, 0.0)) > kf)
    def _():
        _demote_ties(sc_ref, n_steps, step_rows, thr, k)

    return thr


def _softmax_init(m_scr, acc_scr):
    m_scr[...] = jnp.full(m_scr.shape, NEG, F32)
    acc_scr[...] = jnp.zeros(acc_scr.shape, F32)


def _softmax_out(acc_scr, h):
    acc = acc_scr[h]
    return acc[:HEAD_DIM, :] / jnp.maximum(acc[HEAD_DIM:HEAD_DIM + 1, :], 1e-30)


def _store_heads(o_ref, heads_t):
    for pair in range(2):
        o_ref[0, :, pair * LANES:(pair + 1) * LANES] = jnp.concatenate(
            [t[:HEAD_DIM, :] for t in heads_t[2 * pair:2 * pair + 2]], axis=0).T


def _head_values(v_ref, j):
    return [v_ref[0, j, h * HV_ROWS:(h + 1) * HV_ROWS, :] for h in range(N_HEADS)]


def _attn_scratch(rows):
    return [pltpu.VMEM((N_HEADS, 1, T), F32), pltpu.VMEM((N_HEADS, rows, T), F32)]


def _attn_params():
    return pltpu.CompilerParams(dimension_semantics=("arbitrary", "arbitrary"),
                                vmem_limit_bytes=VMEM_LIMIT)


def _keys_spec(S, units, unit):
    return pl.BlockSpec((1, S, units * LANES), lambda b, i: (b, 0, unit // units))


def _q_spec(rows):
    return pl.BlockSpec((1, 1, rows, T), lambda b, i: (b, i, 0, 0))


def _v_spec(S, rows):
    return pl.BlockSpec((1, S // T, rows, T), lambda b, i: (b, 0, 0, 0))


_OUT_SPEC = pl.BlockSpec((1, T, BR_WIDTH), lambda b, i: (b, i, 0))


def _fox_kernel(q_ref, k_ref, v_ref, o_ref, m_scr, acc_scr):
    qb = pl.program_id(1)
    _softmax_init(m_scr, acc_scr)

    def tiles(js, diagonal=False):
        def inputs():
            scores = [[_dot(k_ref[0, pl.ds(pl.multiple_of(j * T, T), T), h * LANES:(h + 1) * LANES],
                            q_ref[0, 0, h * LANES:(h + 1) * LANES, :]) for j in js]
                      for h in range(N_HEADS)]
            valids = [None] * len(js)
            if diagonal:
                key, qry = _key_query_index(qb, js[0])
                valids = [key <= qry]
            return scores, valids, [_head_values(v_ref, j) for j in js]

        _softmax_update(inputs, m_scr, acc_scr, lazy=not diagonal)

    tiles([qb], diagonal=True)
    _for_tile_groups(qb, tiles, group=8)
    _store_heads(o_ref, [_softmax_out(acc_scr, h) for h in range(N_HEADS)])


def _fox_call(pr):
    B, S, _ = pr['keys'].shape
    return pl.pallas_call(
        _fox_kernel,
        out_shape=jax.ShapeDtypeStruct((B, S, BR_WIDTH), F32),
        grid=(B, S // T),
        in_specs=[_q_spec(N_HEADS * LANES), _keys_spec(S, 4, A_FK), _v_spec(S, N_HEADS * HV_ROWS)],
        out_specs=_OUT_SPEC,
        scratch_shapes=_attn_scratch(HV_ROWS),
        compiler_params=_attn_params(),
        name="fox",
    )(pr['fq_t'], pr['keys'], pr['fv_t'])


def _sb_kernel(q_ref, k_ref, v_ref, o_ref, r_scr, acc_scr):
    qb = pl.program_id(1)
    qp = _pair_pads(q_ref[0, 0])
    r_scr[...] = jnp.zeros(r_scr.shape, F32)
    acc_scr[...] = jnp.zeros(acc_scr.shape, F32)
    s_i = lax.broadcasted_iota(I32, (T, T), 0)
    j_i = lax.broadcasted_iota(I32, (T, T), 1)
    later_keys = jnp.where(j_i > s_i, 1.0, 0.0).astype(BF16)

    def tile(j, masked):
        s0 = pl.multiple_of(j * T, T)
        strict = None
        if masked:
            key, qry = _key_query_index(qb, j)
            strict = key < qry
        heads = range(N_HEADS)
        r_prev = [r_scr[h] for h in heads]
        z = [_dot(k_ref[0, pl.ds(s0, T), (h // 2) * LANES:(h // 2 + 1) * LANES], qp[h]) for h in heads]
        log_take = [_log_sigmoid(z[h]) for h in heads]
        log_keep = [log_take[h] - z[h] for h in heads]
        if masked:
            log_keep = [jnp.where(strict, lk, 0.0) for lk in log_keep]
        pieces = [_split3(lk) for lk in log_keep]
        within = [functools.reduce(jnp.add, [_dot(later_keys, p) for p in pieces[h]])
                  for h in heads]
        a = [jnp.exp(log_take[h] + (r_prev[h] + within[h])) for h in heads]
        if masked:
            a = [jnp.where(strict, w, 0.0) for w in a]
        updates = [_dot(v_ref[0, j, h * HEAD_DIM:(h + 1) * HEAD_DIM, :], a[h].astype(BF16)) for h in heads]
        for h in heads:
            acc_scr[h] = acc_scr[h] + updates[h]
            r_scr[h] = r_prev[h] + within[h][0:1, :] + log_keep[h][0:1, :]

    def live():
        return jnp.max(r_scr[...]) > SB_CUTOFF

    def step(state):
        i, _ = state
        tile(qb - 1 - i, False)
        return i + 1, live()

    tile(qb, True)
    lax.while_loop(lambda st: (st[0] < qb) & st[1], step, (jnp.int32(0), live()))
    _store_heads(o_ref, [acc_scr[h] for h in range(N_HEADS)])


def _sb_call(pr):
    B, S, _ = pr['keys'].shape
    return pl.pallas_call(
        _sb_kernel,
        out_shape=jax.ShapeDtypeStruct((B, S, BR_WIDTH), F32),
        grid=(B, S // T),
        in_specs=[_q_spec(BR_WIDTH), _keys_spec(S, 2, A_SK), _v_spec(S, BR_WIDTH)],
        out_specs=_OUT_SPEC,
        scratch_shapes=_attn_scratch(HEAD_DIM),
        compiler_params=_attn_params(),
        name="stick_breaking",
    )(pr['sq_t'], pr['keys'], pr['sv_t'])


def _dsa_kernel(q_ref, k_ref, v_ref, qi_ref, ki_ref, sm_ref, o_ref, sc_scr, hi_scr, lo_scr, m_scr, acc_scr,
                *, k_sel):
    qb = pl.program_id(1)
    row = lax.broadcasted_iota(I32, (LANES, T), 0)
    qi = qi_ref[0, 0]
    zero = jnp.zeros_like(qi)
    qi_pad = [jnp.where((row >= IDX_DIM * h) & (row < IDX_DIM * (h + 1)), qi, zero)
              for h in range(IDX_HEADS)]
    wi = [sm_ref[0, 0, SM_WI + h:SM_WI + h + 1, :] for h in range(IDX_HEADS)]

    def score_tiles(js, diagonal=False):
        rows = [pl.ds(pl.multiple_of(j * T, T), T) for j in js]
        heads = range(IDX_HEADS)
        raw = [[_dot(ki_ref[0, r, :], qi_pad[h]) for h in heads] for r in rows]
        rel = [[jnp.maximum(raw_t[h], 0.0) for h in heads] for raw_t in raw]
        scs = [functools.reduce(jnp.add, [wi[h] * rel_t[h] for h in heads]) for rel_t in rel]
        scs = [jnp.where(sc == 0.0, 0.0, sc) for sc in scs]
        if diagonal:
            key, qry = _key_query_index(qb, js[0])
            scs = [jnp.where(key <= qry, scs[0], -jnp.inf)]
        halves = [_split_keys(sc) for sc in scs]
        for r, sc, (hi, lo) in zip(rows, scs, halves):
            sc_scr[r, :] = sc
            hi_scr[r, :] = hi
            lo_scr[r, :] = lo

    _for_tile_groups(qb, score_tiles, group=8)
    score_tiles([qb], diagonal=True)

    @pl.when(qb % 2 == 0)
    def _():
        pad = pl.ds(pl.multiple_of((qb + 1) * T, T), T)
        sc_scr[pad, :] = jnp.full((T, T), -jnp.inf, F32)
        hi_scr[pad, :] = jnp.full((T, T), LOW16, I16)
        lo_scr[pad, :] = jnp.full((T, T), LOW16, I16)

    n_valid = (qb * T + 1 + lax.broadcasted_iota(I32, (1, T), 1)).astype(F32)
    thr = _top_k_threshold_split(sc_scr, hi_scr, lo_scr, qb // 2 + 1, 2 * T, k_sel, n_valid)

    qp = _pair_pads(q_ref[0, 0])
    _softmax_init(m_scr, acc_scr)

    def attn_tiles(js, lazy=True):
        def inputs():
            scores = [[_dot(k_ref[0, pl.ds(pl.multiple_of(j * T, T), T),
                                  (h // 2) * LANES:(h // 2 + 1) * LANES], qp[h]) for j in js]
                      for h in range(N_HEADS)]
            valids = [sc_scr[pl.ds(pl.multiple_of(j * T, T), T), :] >= thr for j in js]
            return scores, valids, [_head_values(v_ref, j) for j in js]

        _softmax_update(inputs, m_scr, acc_scr, lazy)

    attn_tiles([qb], lazy=False)
    _for_tile_groups(qb, attn_tiles, group=4, first_fn=functools.partial(attn_tiles, lazy=False))
    _store_heads(o_ref, [_softmax_out(acc_scr, h) for h in range(N_HEADS)])


def _dsa_call(pr):
    B, S, _ = pr['keys'].shape
    return pl.pallas_call(
        functools.partial(_dsa_kernel, k_sel=min(DSA_TOPK_MAX, S // 4)),
        out_shape=jax.ShapeDtypeStruct((B, S, BR_WIDTH), F32),
        grid=(B, S // T),
        in_specs=[_q_spec(BR_WIDTH), _keys_spec(S, 2, A_KA), _v_spec(S, N_HEADS * HV_ROWS),
                  _q_spec(LANES), _keys_spec(S, 1, A_KI), _q_spec(LANES)],
        out_specs=_OUT_SPEC,
        scratch_shapes=[pltpu.VMEM((S, T), F32), pltpu.VMEM((S, T), I16), pltpu.VMEM((S, T), I16)]
                       + _attn_scratch(HV_ROWS),
        compiler_params=_attn_params(),
        name="dsa",
    )(pr['dq_t'], pr['keys'], pr['va_t'], pr['qi_t'], pr['keys'], pr['sm_t'])


def _compress_kernel(xk_ref, xv_ref, pek_ref, pev_ref, w1k_ref, w1v_ref, w2k_ref, w2v_ref,
                     o_ref, ot_ref, *, nc):
    def hidden(x_ref, pe_ref, w1_ref):
        x = x_ref[0]
        first = _dot((x + pe_ref[0:1, :]).astype(BF16), w1_ref[0])
        second = _dot((x + pe_ref[1:2, :]).astype(BF16), w1_ref[1])
        pre = first + pltpu.roll(second, nc - 1, 0)
        return (pre * _sigmoid(pre)).astype(BF16)

    kvc = (_dot(hidden(xk_ref, pek_ref, w1k_ref), w2k_ref[...])
           + _dot(hidden(xv_ref, pev_ref, w1v_ref), w2v_ref[...]))
    o_ref[0] = kvc.astype(BF16)
    ot_ref[0] = kvc.T.astype(BF16)


def _compress_call(l, xk, xv, pek, pev, w1k, w1v, w2k, w2v):
    B, nc, width = xk.shape
    full = lambda shape: _layer_spec(l, shape, 1)
    tok = pl.BlockSpec((1, nc, width), lambda b: (b, 0, 0))
    return pl.pallas_call(
        functools.partial(_compress_kernel, nc=nc),
        out_shape=(jax.ShapeDtypeStruct((B, nc, LANES), BF16), jax.ShapeDtypeStruct((B, LANES, nc), BF16)),
        grid=(B,),
        in_specs=[tok, tok, full((2, width)), full((2, width)), full((2, width, CMP_HIDDEN)),
                  full((2, width, CMP_HIDDEN)), full((CMP_HIDDEN, LANES)), full((CMP_HIDDEN, LANES))],
        out_specs=(pl.BlockSpec((1, nc, LANES), lambda b: (b, 0, 0)),
                   pl.BlockSpec((1, LANES, nc), lambda b: (b, 0, 0))),
        compiler_params=pltpu.CompilerParams(dimension_semantics=("arbitrary",),
                                             vmem_limit_bytes=VMEM_LIMIT),
        name="nsa_compress",
    )(xk, xv, pek, pev, w1k, w1v, w2k, w2v)


def _nsa_kernel(q_ref, kvc_ref, kvct_ref, ksw_ref, vsw_ref, sm_ref, o_ref,
                imp_scr, sel_scr, m_scr, acc_scr, *, n_blk, n_sel, nbp):
    qb = pl.program_id(1)
    ncp = kvc_ref.shape[1]
    q_t = q_ref[0, 0]
    zero = jnp.zeros((HEAD_DIM, T), q_t.dtype)
    heads = [q_t[h * HEAD_DIM:(h + 1) * HEAD_DIM, :] for h in range(N_HEADS)]
    q_lo = [jnp.concatenate([q, zero], axis=0) for q in heads]
    q_hi = [jnp.concatenate([zero, q], axis=0) for q in heads]
    gate = lambda br, h: sm_ref[0, 0, SM_GATE + br * N_HEADS + h:SM_GATE + br * N_HEADS + h + 1, :]
    t_row = qb * T + lax.broadcasted_iota(I32, (1, T), 1)

    kvc = kvc_ref[0]
    cmp_end = lax.broadcasted_iota(I32, (ncp, T), 0) * CMP_STRIDE + (CMP_LEN - 1)
    cmp_ok = cmp_end <= t_row
    cmp_logits = [jnp.where(cmp_ok, _dot(kvc, q_lo[h]), NEG) for h in range(N_HEADS)]
    cmp_tops = [jnp.max(s, axis=0, keepdims=True) for s in cmp_logits]
    cmp_e = [jnp.where(cmp_ok, jnp.exp2(s - m), 0.0) for s, m in zip(cmp_logits, cmp_tops)]
    cmp_den = [jnp.maximum(jnp.sum(e, axis=0, keepdims=True), 1e-30) for e in cmp_e]
    pcs = [e / d for e, d in zip(cmp_e, cmp_den)]
    o_cmp = [_dot(kvct_ref[0], pc.astype(BF16))[HEAD_DIM:, :] for pc in pcs]
    p_sum = functools.reduce(jnp.add, pcs)
    n_win = WINDOW // T + 1
    j0 = jnp.maximum(qb - (n_win - 1), 0)
    heads, wins = range(N_HEADS), range(n_win)
    oks = []
    for i in wins:
        key, qry = _key_query_index(qb, j0 + i)
        dist = qry - key
        oks.append((dist >= 0) & (dist < WINDOW))
    raw = [[_dot(ksw_ref[0, pl.ds(pl.multiple_of((j0 + i) * T, T), T), :], q_hi[h]) for i in wins]
           for h in heads]
    logits = [[jnp.where(oks[i], raw[h][i], NEG) for i in wins] for h in heads]
    tops = _tile_tops(logits)
    es = [[jnp.where(oks[i], jnp.exp2(logits[h][i] - tops[h]), 0.0) for i in wins] for h in heads]
    dens = [functools.reduce(jnp.add, [jnp.sum(e, axis=0, keepdims=True) for e in es[h]]) for h in heads]
    pws = [[(es[h][i] / jnp.maximum(dens[h], 1e-30)).astype(BF16) for i in wins] for h in heads]
    o_win = [functools.reduce(jnp.add, [_dot(vsw_ref[0, j0 + i, HV_ROWS:, :], pws[h][i]) for i in wins])
             for h in heads]
    per_blk = SLC_BLOCK // CMP_STRIDE
    b_i = lax.broadcasted_iota(I32, (nbp, ncp), 0)
    c_i = lax.broadcasted_iota(I32, (nbp, ncp), 1)
    group = jnp.where(_div_pow2(c_i, per_blk) == b_i, 1.0, 0.0).astype(BF16)
    imp = _dot3x(group, p_sum)
    blk = lax.broadcasted_iota(I32, (nbp, T), 0)
    cur = _div_pow2(t_row, SLC_BLOCK)
    forced = (blk == 0) | (blk == cur) | (blk == cur - 1)
    imp = jnp.where(forced, FORCE_SCORE, jnp.where(blk <= cur, imp, -1.0))
    imp = jnp.where(blk < n_blk, imp, -2.0)
    imp_scr[...] = imp
    thr = _top_k_threshold(imp_scr, 1, nbp, n_sel, jnp.full((1, T), float(nbp), F32))
    sel_scr[...] = jnp.where(imp_scr[...] >= thr, 1.0, 0.0)

    _softmax_init(m_scr, acc_scr)
    blk_per_tile = T // SLC_BLOCK

    def slc_tiles(js, diagonal=False):
        def inputs():
            valids, scores = [], [[] for _ in range(N_HEADS)]
            for j in js:
                valid = jnp.concatenate(
                    [jnp.broadcast_to(sel_scr[pl.ds(j * blk_per_tile + b, 1), :], (SLC_BLOCK, T))
                     for b in range(blk_per_tile)], axis=0) > 0.5
                if diagonal:
                    key, qry = _key_query_index(qb, j)
                    valid = valid & (key <= qry)
                valids.append(valid)
                kt = ksw_ref[0, pl.ds(pl.multiple_of(j * T, T), T), :]
                for h in range(N_HEADS):
                    scores[h].append(_dot(kt, q_lo[h]))
            return scores, valids, [[vsw_ref[0, j, :HV_ROWS, :]] * N_HEADS for j in js]

        _softmax_update(inputs, m_scr, acc_scr, lazy=not diagonal)

    slc_tiles([qb], diagonal=True)
    _for_tile_groups(qb, slc_tiles, group=4)
    _store_heads(o_ref, [gate(0, h) * o_cmp[h] + gate(1, h) * _softmax_out(acc_scr, h)
                         + gate(2, h) * o_win[h] for h in heads])


def _nsa_call(pr, kvc, kvc_t):
    B, S, _ = pr['keys'].shape
    n_blk = S // SLC_BLOCK
    nbp = max(LANES, n_blk)
    ncp = kvc.shape[1]
    return pl.pallas_call(
        functools.partial(_nsa_kernel, n_blk=n_blk, n_sel=min(SLC_TOPN, n_blk), nbp=nbp),
        out_shape=jax.ShapeDtypeStruct((B, S, BR_WIDTH), F32),
        grid=(B, S // T),
        in_specs=[_q_spec(BR_WIDTH),
                  pl.BlockSpec((1, ncp, LANES), lambda b, i: (b, 0, 0)),
                  pl.BlockSpec((1, LANES, ncp), lambda b, i: (b, 0, 0)),
                  _keys_spec(S, 1, A_KSW), _v_spec(S, VSW_ROWS), _q_spec(LANES)],
        out_specs=_OUT_SPEC,
        scratch_shapes=[pltpu.VMEM((nbp, T), F32), pltpu.VMEM((nbp, T), F32)] + _attn_scratch(HV_ROWS),
        compiler_params=_attn_params(),
        name="nsa",
    )(pr['nq_t'], kvc, kvc_t, pr['keys'], pr['vsw_t'], pr['sm_t'])


def _mem_kv_kernel(mem_ref, g_ref, w_ref, o_ref):
    o_ref[0] = _dot(_rmsnorm(mem_ref[0], g_ref[...]).astype(BF16), w_ref[...]).astype(BF16)


def _mem_kv_call(l, mem, g, w):
    B, M, D = mem.shape
    return pl.pallas_call(
        _mem_kv_kernel,
        out_shape=jax.ShapeDtypeStruct((B, M, 2 * BR_WIDTH), BF16),
        grid=(B,),
        in_specs=[pl.BlockSpec((1, M, D), lambda b: (b, 0, 0)), _layer_spec(l, (1, D), 1),
                  _layer_spec(l, (D, 2 * BR_WIDTH), 1)],
        out_specs=pl.BlockSpec((1, M, 2 * BR_WIDTH), lambda b: (b, 0, 0)),
        name="mem_kv",
    )(mem, g, w)


def _out_kernel(x_ref, g_ref, ya_ref, yb_ref, yc_ref, yd_ref, mq_ref, mkv_ref, wz_ref, wm_ref,
                wb_ref, wo_ref, fg_ref, o_ref, h_scr, *, final):
    x = x_ref[0]
    h_scr[...] = _rmsnorm(x, g_ref[...]).astype(BF16)
    q = mq_ref[0]
    lane = lax.broadcasted_iota(I32, (T, LANES), 1)
    lo = lane < HEAD_DIM
    zero = jnp.zeros((T, LANES), q.dtype)
    heads = range(N_HEADS)
    chunks = [q[:, (h // 2) * LANES:(h // 2 + 1) * LANES] for h in heads]
    qh = [jnp.where(lo, c, zero) if h % 2 == 0 else jnp.where(lo, zero, c) for h, c in zip(heads, chunks)]
    mk = [mkv_ref[0, :, (h // 2) * LANES:(h // 2 + 1) * LANES] for h in heads]
    mv = [mkv_ref[0, :, BR_WIDTH + (h // 2) * LANES:BR_WIDTH + (h // 2 + 1) * LANES] for h in heads]
    s = [_dot_nt(qh[h], mk[h]) for h in heads]
    e = [jnp.exp(s[h] - jnp.max(s[h], axis=1, keepdims=True)) for h in heads]
    p = [(e[h] / jnp.sum(e[h], axis=1, keepdims=True)).astype(BF16) for h in heads]
    o = [_dot(p[h], mv[h]) for h in heads]
    y_e = jnp.concatenate([jnp.where(lo, o[0], o[1]), jnp.where(lo, o[2], o[3])], axis=1)
    branches = range(N_BRANCH)
    d = x.shape[1]
    ys_in = (ya_ref[0], yb_ref[0], yc_ref[0], yd_ref[0], y_e)
    z = [_dot(h_scr[...], wz_ref[n]) for n in branches]
    ys = [(ys_in[n] * (z[n] * _sigmoid(z[n]))).astype(BF16) for n in branches]
    gates = [_sigmoid(_dot(h_scr[...], wm_ref[:, n * d:(n + 1) * d])) for n in branches]
    projected = [_dot(ys[n], wb_ref[n]) for n in branches]
    merged = functools.reduce(jnp.add, [gates[n] * projected[n] for n in branches])
    out = x + _dot(merged.astype(BF16), wo_ref[...])
    o_ref[0] = _rmsnorm(out, fg_ref[...]) if final else out


def _out_call(l, x, g, ys, keys, mkv, wz, wm, wb, wo, fg, final):
    B, S, D = x.shape
    M = mkv.shape[1]
    full = lambda shape: _layer_spec(l, shape, 2)
    tile = lambda n: pl.BlockSpec((1, T, n), lambda b, i: (b, i, 0))
    return pl.pallas_call(
        functools.partial(_out_kernel, final=final),
        out_shape=jax.ShapeDtypeStruct((B, S, D), F32),
        grid=(B, S // T),
        in_specs=[tile(D), full((1, D))] + [tile(BR_WIDTH)] * 4
                 + [pl.BlockSpec((1, T, BR_WIDTH), lambda b, i: (b, i, A_MQ // 2)),
                    pl.BlockSpec((1, M, 2 * BR_WIDTH), lambda b, i: (b, 0, 0)),
                    full((N_BRANCH, D, BR_WIDTH)), full((D, N_BRANCH * D)), full((N_BRANCH, BR_WIDTH, D)),
                    full((D, D)), pl.BlockSpec((1, D), lambda b, i: (0, 0))],
        out_specs=tile(D),
        scratch_shapes=[pltpu.VMEM((T, D), BF16)],
        compiler_params=pltpu.CompilerParams(dimension_semantics=("arbitrary", "arbitrary"),
                                             vmem_limit_bytes=VMEM_LIMIT),
        name="merge_out",
    )(x, g, *ys, keys, mkv, wz, wm, wb, wo, fg)


def kernel(x, mem, positions, norm_g, w_in, kv_norm, w_uk, w_uv, fox_bias, nsa_pe_k, nsa_pe_v,
           nsa_wc1_k, nsa_wc2_k, nsa_wc1_v, nsa_wc2_v, mem_norm, w_mem_kv, w_branch, w_out, final_norm):
    B, S, D = x.shape
    depth = norm_g.shape[0]
    assert S % (2 * T) == 0 and S >= WINDOW + T and WINDOW % T == 0 and D == D_MODEL
    cos, sin = _rope_tables(positions)
    w1, wz, wm = _proj_weight(w_in)
    fb = jnp.zeros((depth, 1, LANES), F32).at[:, 0, SM_CUM:SM_CUM + N_HEADS].set(fox_bias)
    nc = S // CMP_STRIDE
    tok_w = CMP_STRIDE * NSA_KV_DIM
    pad_k = jnp.zeros((depth, CMP_HIDDEN, LANES), F32).at[..., :NSA_KV_DIM].set(nsa_wc2_k).astype(BF16)
    pad_v = jnp.zeros((depth, CMP_HIDDEN, LANES), F32).at[..., NSA_KV_DIM:].set(nsa_wc2_v).astype(BF16)
    g, kvn, mn = norm_g[:, None, :], kv_norm[:, None, :], mem_norm[:, None, :]
    wuk, wuv, wmkv = w_uk.astype(BF16), w_uv.astype(BF16), w_mem_kv.astype(BF16)
    pek, pev = nsa_pe_k.reshape(depth, 2, tok_w), nsa_pe_v.reshape(depth, 2, tok_w)
    w1k = nsa_wc1_k.reshape(depth, 2, tok_w, CMP_HIDDEN).astype(BF16)
    w1v = nsa_wc1_v.reshape(depth, 2, tok_w, CMP_HIDDEN).astype(BF16)
    wb, wo = w_branch.astype(BF16), w_out.astype(BF16)
    for l in range(depth):
        pr = _proj_call(l, x, g, w1, kvn, wuk, wuv, fb, cos, sin)
        kvc, kvc_t = _compress_call(l, pr['kc_tok'].reshape(B, nc, tok_w), pr['vc_tok'].reshape(B, nc, tok_w),
                                    pek, pev, w1k, w1v, pad_k, pad_v)
        ys = (_dsa_call(pr), _fox_call(pr), _sb_call(pr), _nsa_call(pr, kvc, kvc_t))
        mkv = _mem_kv_call(l, mem, mn, wmkv)
        x = _out_call(l, x, g, ys, pr['keys'], mkv, wz, wm, wb, wo, final_norm[None], l == depth - 1)
    return x
```

```python
import functools

import jax
import jax.numpy as jnp
from jax import lax
from jax.experimental import pallas as pl
from jax.experimental.pallas import tpu as pltpu

F32 = jnp.float32
BF16 = jnp.bfloat16
I32 = jnp.int32
I16 = jnp.int16

LANES = 128
SUBLANES = 8
D_MODEL = 1024
N_HEADS = 4
HEAD_DIM = 64
BR_WIDTH = N_HEADS * HEAD_DIM
N_BRANCH = 5
ROPE_THETA = 10000.0
EPS = 1e-6
DSA_TOPK_MAX = 256
IDX_HEADS = 4
IDX_DIM = 32
KV_LATENT = 128
NSA_KV_DIM = 64
CMP_LEN = 32
CMP_STRIDE = 16
CMP_HIDDEN = 128
SLC_BLOCK = 64
SLC_TOPN = 16
WINDOW = 512
FORCE_SCORE = 1e4
Q_SCALE = HEAD_DIM ** -0.5
LOG2E = 1.4426950408889634
Q_SCALE2 = Q_SCALE * LOG2E
IDX_SCALE = (IDX_DIM ** -0.5) * (IDX_HEADS ** -0.5)
NEG = -1e30
SB_CUTOFF = -104.0
LAZY_LIMIT = 64.0
VMEM_LIMIT = 56 * 1024 * 1024
T = 256
ONES_ROWS = 16
HV_ROWS = HEAD_DIM + ONES_ROWS
VSW_ROWS = HV_ROWS + NSA_KV_DIM

IN_LAYOUT = (
    ('dsa_q', BR_WIDTH), ('dsa_ckv', KV_LATENT), ('idx_q', IDX_HEADS * IDX_DIM), ('idx_k', IDX_DIM),
    ('idx_w', IDX_HEADS), ('dsa_z', BR_WIDTH),
    ('fox_q', BR_WIDTH), ('fox_k', BR_WIDTH), ('fox_v', BR_WIDTH), ('fox_f', N_HEADS), ('fox_z', BR_WIDTH),
    ('sb_q', BR_WIDTH), ('sb_k', BR_WIDTH), ('sb_v', BR_WIDTH), ('sb_z', BR_WIDTH),
    ('nsa_q', BR_WIDTH), ('nsa_kc', NSA_KV_DIM), ('nsa_vc', NSA_KV_DIM), ('nsa_ks', NSA_KV_DIM),
    ('nsa_vs', NSA_KV_DIM), ('nsa_kw', NSA_KV_DIM), ('nsa_vw', NSA_KV_DIM), ('nsa_g', 3 * N_HEADS),
    ('nsa_z', BR_WIDTH), ('mem_q', BR_WIDTH), ('mem_z', BR_WIDTH), ('merge', N_BRANCH * D_MODEL),
)

A_FK, A_KA, A_SK, A_KI, A_KSW, A_MQ = 0, 4, 6, 8, 9, 10
A_COLS = 12 * LANES
SM_WI, SM_CUM, SM_GATE = 0, 4, 8
FOX_CK, FOX_ONE, FOX_END = HEAD_DIM, HEAD_DIM + 3, HEAD_DIM + 6


def _dot(a, b):
    return jnp.dot(a, b, preferred_element_type=F32)


def _dot_nt(a, b):
    return lax.dot_general(a, b, (((1,), (1,)), ((), ())), preferred_element_type=F32)


def _split3(x):
    hi = x.astype(BF16)
    r1 = x - hi.astype(F32)
    mid = r1.astype(BF16)
    lo = (r1 - mid.astype(F32)).astype(BF16)
    return hi, mid, lo


def _dot3x(m01, x):
    hi, mid, lo = _split3(x)
    return _dot(m01, hi) + _dot(m01, mid) + _dot(m01, lo)


def _div_pow2(x, n):
    assert n & (n - 1) == 0
    return x >> (n.bit_length() - 1)


def _sigmoid(x):
    return 1.0 / (1.0 + jnp.exp(-x))


def _log_sigmoid(x):
    return jnp.minimum(x, 0.0) - jnp.log(1.0 + jnp.exp(-jnp.abs(x)))


def _rmsnorm(x, g):
    return x * lax.rsqrt(jnp.mean(x * x, axis=-1, keepdims=True) + EPS) * g


def _rope128(x, cos, sin_signed, half):
    lane = lax.broadcasted_iota(I32, x.shape, 1)
    first = (lane & (2 * half - 1)) < half
    partner = jnp.where(first, pltpu.roll(x, LANES - half, 1), pltpu.roll(x, half, 1))
    return x * cos + partner * sin_signed


def _rope_table_kernel(pos_ref, inv_ref, sgn_ref, cos_ref, sin_ref):
    ang = pos_ref[0] * inv_ref[...]
    cos_ref[0] = jnp.cos(ang)
    sin_ref[0] = jnp.sin(ang) * sgn_ref[...]


def _rope_tables(positions):
    B, S = positions.shape
    lane = jnp.arange(LANES)

    def inv_row(dh):
        inv = ROPE_THETA ** (-jnp.arange(0, dh, 2, dtype=F32) / dh)
        return inv[(lane % dh) % (dh // 2)]

    def sgn_row(dh):
        return jnp.where((lane % dh) < dh // 2, -1.0, 1.0).astype(F32)

    inv = jnp.concatenate([inv_row(HEAD_DIM), inv_row(IDX_DIM)])[None, :]
    sgn = jnp.concatenate([sgn_row(HEAD_DIM), sgn_row(IDX_DIM)])[None, :]
    pos = positions.astype(F32)[..., None]
    row = pl.BlockSpec((1, 2 * LANES), lambda b, i: (0, 0))
    tab = pl.BlockSpec((1, T, 2 * LANES), lambda b, i: (b, i, 0))
    return pl.pallas_call(
        _rope_table_kernel,
        out_shape=(jax.ShapeDtypeStruct((B, S, 2 * LANES), F32),) * 2,
        grid=(B, S // T),
        in_specs=[pl.BlockSpec((1, T, 1), lambda b, i: (b, i, 0)), row, row],
        out_specs=(tab, tab),
        name="rope_tables",
    )(pos, inv, sgn)


W_DQ, W_CKV, W_QI, W_KI, W_SM, W_FQ, W_SQ, W_NQ, W_MQ, W_KVC, W_KSW, W_VSW = (
    0, 256, 384, 512, 640, 768, 1536, 2304, 2560, 2816, 2944, 3072)
W_COLS = 3200

PROJ_OUTS = (
    ('keys', A_COLS, BF16), ('kc_tok', NSA_KV_DIM, F32), ('vc_tok', NSA_KV_DIM, F32),
    ('dq_t', BR_WIDTH, BF16), ('fq_t', N_HEADS * LANES, BF16), ('sq_t', BR_WIDTH, BF16),
    ('nq_t', BR_WIDTH, BF16), ('qi_t', LANES, BF16), ('sm_t', LANES, F32),
    ('va_t', N_HEADS * HV_ROWS, BF16), ('fv_t', N_HEADS * HV_ROWS, BF16), ('sv_t', BR_WIDTH, BF16),
    ('vsw_t', VSW_ROWS, BF16),
)
ROW_MAJOR = ('keys', 'kc_tok', 'vc_tok')


def _proj_kernel(x_ref, g_ref, w_ref, kvn_ref, wuk_ref, wuv_ref, fb_ref, cos_ref, sin_ref,
                 *refs):
    out = dict(zip([n for n, _, _ in PROJ_OUTS], refs))
    h_scr, carry_scr = refs[len(PROJ_OUTS):]
    keys = out['keys']
    h_scr[...] = _rmsnorm(x_ref[0], g_ref[...]).astype(BF16)
    c64, s64 = cos_ref[0, :, :LANES], sin_ref[0, :, :LANES]
    c32, s32 = cos_ref[0, :, LANES:], sin_ref[0, :, LANES:]
    lane = lax.broadcasted_iota(I32, (T, LANES), 1)
    ones_rows = jnp.ones((ONES_ROWS, T), BF16)

    def proj(c0, n):
        return _dot(h_scr[...], w_ref[:, c0:c0 + n])

    def put(unit, val):
        keys[0, :, unit * LANES:unit * LANES + val.shape[1]] = val.astype(BF16)

    def rope64(v, j):
        return _rope128(v[:, j * LANES:(j + 1) * LANES], c64, s64, HEAD_DIM // 2)

    def put_t(ref, row0, chunk):
        ref[0, 0, row0:row0 + LANES, :] = chunk.T.astype(ref.dtype)

    def put_values_t(ref, v):
        for j in range(2):
            pair_t = v[:, j * LANES:(j + 1) * LANES].T.astype(BF16)
            for odd in range(2):
                r0 = (2 * j + odd) * HV_ROWS
                ref[0, 0, r0:r0 + HEAD_DIM, :] = pair_t[odd * HEAD_DIM:(odd + 1) * HEAD_DIM, :]
                ref[0, 0, r0 + HEAD_DIM:r0 + HV_ROWS, :] = ones_rows

    p = proj(W_DQ, BR_WIDTH)
    for j in range(2):
        put_t(out['dq_t'], j * LANES, rope64(p, j) * Q_SCALE2)
    c_kv = _rmsnorm(proj(W_CKV, KV_LATENT), kvn_ref[...]).astype(BF16)
    k_a = _dot(c_kv, wuk_ref[...])
    for j in range(2):
        put(A_KA + j, rope64(k_a, j))
    put_values_t(out['va_t'], _dot(c_kv, wuv_ref[...]))
    put_t(out['qi_t'], 0, _rope128(proj(W_QI, LANES), c32, s32, IDX_DIM // 2))
    put(A_KI, _rope128(proj(W_KI, LANES), c32, s32, IDX_DIM // 2))
    p = proj(W_SM, LANES)
    log_f = _log_sigmoid(p + fb_ref[...])
    r_i = lax.broadcasted_iota(I32, (T, T), 0)
    c_i = lax.broadcasted_iota(I32, (T, T), 1)
    tri = jnp.where(r_i >= c_i, 1.0, 0.0).astype(BF16)

    @pl.when(pl.program_id(1) == 0)
    def _():
        carry_scr[...] = jnp.zeros_like(carry_scr)

    cum = _dot3x(tri, log_f) + carry_scr[...]
    carry_scr[...] = cum[T - 1:T, :]
    small_t = jnp.where(lane < SM_CUM, p * IDX_SCALE,
                        jnp.where(lane < SM_GATE, cum, _sigmoid(p))).T
    out['sm_t'][0, 0] = small_t
    q_f = proj(W_FQ, BR_WIDTH) * Q_SCALE2
    cum2, cum2_t = cum * LOG2E, small_t * LOG2E
    k_f = proj(W_FQ + 256, BR_WIDTH)
    row = lax.broadcasted_iota(I32, (HEAD_DIM, T), 0)
    for h in range(N_HEADS):
        pair, odd = h // 2, h % 2
        k_chunk = k_f[:, pair * LANES:(pair + 1) * LANES]
        k_h = pltpu.roll(k_chunk, HEAD_DIM, 1) if odd else k_chunk
        ck = [c.astype(F32) for c in _split3(cum2[:, SM_CUM + h:SM_CUM + h + 1])]
        k_aug = jnp.where(lane < HEAD_DIM, k_h,
                          jnp.where(lane == FOX_CK, ck[0],
                                    jnp.where(lane == FOX_CK + 1, ck[1],
                                              jnp.where(lane == FOX_CK + 2, ck[2],
                                                        jnp.where(lane < FOX_END, 1.0, 0.0)))))
        put(A_FK + h, k_aug)
        q_ht = q_f[:, pair * LANES:(pair + 1) * LANES].T[odd * HEAD_DIM:(odd + 1) * HEAD_DIM, :]
        cq = [c.astype(F32) for c in _split3(cum2_t[SM_CUM + h:SM_CUM + h + 1, :])]
        aug = jnp.where(row < 3, -1.0,
                        jnp.where(row == 3, cq[0], jnp.where(row == 4, cq[1],
                                                             jnp.where(row == 5, cq[2], 0.0))))
        out['fq_t'][0, 0, h * LANES:(h + 1) * LANES, :] = jnp.concatenate(
            [q_ht, aug], axis=0).astype(BF16)
    put_values_t(out['fv_t'], proj(W_FQ + 512, BR_WIDTH))
    p = proj(W_SQ, BR_WIDTH) * Q_SCALE
    for j in range(2):
        put_t(out['sq_t'], j * LANES, p[:, j * LANES:(j + 1) * LANES])
    put(A_SK, proj(W_SQ + 256, BR_WIDTH))
    p = proj(W_SQ + 512, BR_WIDTH)
    for j in range(2):
        put_t(out['sv_t'], j * LANES, p[:, j * LANES:(j + 1) * LANES])
    put(A_MQ, proj(W_MQ, BR_WIDTH) * Q_SCALE)
    p = proj(W_NQ, BR_WIDTH)
    for j in range(2):
        put_t(out['nq_t'], j * LANES, rope64(p, j) * Q_SCALE2)
    p = proj(W_KVC, LANES)
    out['kc_tok'][0] = rope64(p, 0)[:, :NSA_KV_DIM]
    out['vc_tok'][0] = p[:, NSA_KV_DIM:]
    put(A_KSW, rope64(proj(W_KSW, LANES), 0))
    vsw_t = proj(W_VSW, LANES).T.astype(BF16)
    out['vsw_t'][0, 0, :NSA_KV_DIM, :] = vsw_t[:NSA_KV_DIM, :]
    out['vsw_t'][0, 0, NSA_KV_DIM:HV_ROWS, :] = ones_rows
    out['vsw_t'][0, 0, HV_ROWS:, :] = vsw_t[NSA_KV_DIM:, :]


def _proj_weight(w_in):
    off, o = {}, 0
    for name, n in IN_LAYOUT:
        off[name] = (o, n)
        o += n

    def col(name):
        s, n = off[name]
        return w_in[..., s:s + n]

    zeros = lambda n: jnp.zeros(w_in.shape[:-1] + (n,), w_in.dtype)
    groups = [col('dsa_q'), col('dsa_ckv'), col('idx_q'), col('idx_k'), col('idx_k'), col('idx_k'),
              col('idx_k'), col('idx_w'), col('fox_f'), col('nsa_g'), zeros(LANES - 20),
              col('fox_q'), col('fox_k'), col('fox_v'), col('sb_q'), col('sb_k'), col('sb_v'),
              col('nsa_q'), col('mem_q'), col('nsa_kc'), col('nsa_vc'), col('nsa_ks'), col('nsa_kw'),
              col('nsa_vs'), col('nsa_vw')]
    w1 = jnp.concatenate(groups, axis=-1).astype(BF16)
    wz = jnp.stack([col(n) for n in ('dsa_z', 'fox_z', 'sb_z', 'nsa_z', 'mem_z')], axis=-3).astype(BF16)
    return w1, wz, col('merge').astype(BF16)


def _layer_spec(l, shape, grid_rank):
    if grid_rank == 1:
        return pl.BlockSpec((None,) + shape, lambda b: (l,) + (0,) * len(shape))
    return pl.BlockSpec((None,) + shape, lambda b, i: (l,) + (0,) * len(shape))


def _proj_call(l, x, g, w1, kvn, wuk, wuv, fb, cos, sin):
    B, S, D = x.shape
    full = lambda shape: _layer_spec(l, shape, 2)
    tile = lambda n: pl.BlockSpec((1, T, n), lambda b, i: (b, i, 0))
    tile_t = lambda n: pl.BlockSpec((1, 1, n, T), lambda b, i: (b, i, 0, 0))
    shapes, specs = [], []
    for name, n, dt in PROJ_OUTS:
        if name in ROW_MAJOR:
            shapes.append(jax.ShapeDtypeStruct((B, S, n), dt))
            specs.append(tile(n))
        else:
            shapes.append(jax.ShapeDtypeStruct((B, S // T, n, T), dt))
            specs.append(tile_t(n))
    outs = pl.pallas_call(
        _proj_kernel,
        out_shape=tuple(shapes),
        grid=(B, S // T),
        in_specs=[tile(D), full((1, D)), full((D, W_COLS)), full((1, KV_LATENT)),
                  full((KV_LATENT, BR_WIDTH)), full((KV_LATENT, BR_WIDTH)), full((1, LANES)),
                  tile(2 * LANES), tile(2 * LANES)],
        out_specs=tuple(specs),
        scratch_shapes=[pltpu.VMEM((T, D), BF16), pltpu.VMEM((1, LANES), F32)],
        compiler_params=pltpu.CompilerParams(dimension_semantics=("arbitrary", "arbitrary"),
                                             vmem_limit_bytes=VMEM_LIMIT),
        name="proj",
    )(x, g, w1, kvn, wuk, wuv, fb, cos, sin)
    return dict(zip([n for n, _, _ in PROJ_OUTS], outs))


def _key_query_index(qb, j):
    key = j * T + lax.broadcasted_iota(I32, (T, T), 0)
    qry = qb * T + lax.broadcasted_iota(I32, (T, T), 1)
    return key, qry


def _pair_pads(q_t):
    row = lax.broadcasted_iota(I32, (LANES, T), 0)
    top = row < HEAD_DIM
    zero = jnp.zeros((LANES, T), q_t.dtype)
    out = []
    for pair in range(2):
        chunk = q_t[pair * LANES:(pair + 1) * LANES, :]
        out.append(jnp.where(top, chunk, zero))
        out.append(jnp.where(top, zero, chunk))
    return out


def _masked_tiles(scores, valids):
    return [[s if v is None else jnp.where(v, s, 2 * NEG) for s, v in zip(scores_h, valids)]
            for scores_h in scores]


def _tile_tops(tiles):
    return [functools.reduce(jnp.maximum, [jnp.max(s, axis=0, keepdims=True) for s in tiles_h])
            for tiles_h in tiles]


def _weighted_values(tiles, values, m):
    weights = [[jnp.exp2(s - m[h]).astype(BF16) for s in tiles_h] for h, tiles_h in enumerate(tiles)]
    parts = [[_dot(values[i][h], w) for i, w in enumerate(weights_h)] for h, weights_h in enumerate(weights)]
    return [functools.reduce(jnp.add, parts_h) for parts_h in parts]


def _softmax_update(inputs_fn, m_scr, acc_scr, lazy=False):
    if not lazy:
        _softmax_update_exact(*inputs_fn(), m_scr, acc_scr)
        return
    scores, valids, values = inputs_fn()
    m_ref = [m_scr[h] for h in range(N_HEADS)]
    tiles = _masked_tiles(scores, valids)
    updates = _weighted_values(tiles, values, m_ref)
    tops = _tile_tops(tiles)
    overshoot = functools.reduce(jnp.maximum, [jnp.max(tops[h] - m_ref[h]) for h in range(N_HEADS)])
    safe = overshoot <= LAZY_LIMIT

    @pl.when(safe)
    def _():
        for h in range(N_HEADS):
            m_new = jnp.maximum(m_ref[h], tops[h])
            acc_scr[h] = jnp.exp2(m_ref[h] - m_new) * (acc_scr[h] + updates[h])
            m_scr[h] = m_new

    @pl.when(jnp.logical_not(safe))
    def _():
        _softmax_update_exact(*inputs_fn(), m_scr, acc_scr)


def _softmax_update_exact(scores, valids, values, m_scr, acc_scr):
    m_prev = [m_scr[h] for h in range(N_HEADS)]
    tiles = _masked_tiles(scores, valids)
    m_new = [jnp.maximum(m, top) for m, top in zip(m_prev, _tile_tops(tiles))]
    updates = _weighted_values(tiles, values, m_new)
    for h in range(N_HEADS):
        acc_scr[h] = jnp.exp2(m_prev[h] - m_new[h]) * acc_scr[h] + updates[h]
        m_scr[h] = m_new[h]


def _for_tile_groups(n, group_fn, group, first_fn=None):
    def full(i, c):
        group_fn([group * i + g for g in range(group)])
        return c

    if first_fn is not None:
        @pl.when(n >= group)
        def _():
            first_fn(list(range(group)))

    lax.fori_loop(0 if first_fn is None else 1, n // group, full, 0)
    size = group // 2
    while size:
        start = (n // (2 * size)) * (2 * size)

        @pl.when((n & size) != 0)
        def _(start=start, size=size):
            group_fn([start + g for g in range(size)])

        size //= 2


INT_MIN = -2 ** 31
LOW16 = -2 ** 15
KEY_ABOVE_NEG_INF = 0x80800000 - 2 ** 32


def _key_to_float(c):
    bits = c ^ ((c >> 31) & 0x7FFFFFFF)
    return lax.bitcast_convert_type(bits, F32)


def _count_rows(ref, n_steps, step_rows, pred, cand):
    pack = SUBLANES * (4 // ref.dtype.itemsize)
    cand_b = jnp.broadcast_to(cand, (pack, T)).astype(ref.dtype)
    one, zero = jnp.ones((pack, T), ref.dtype), jnp.zeros((pack, T), ref.dtype)

    def body(c, accs):
        blk = ref[pl.ds(pl.multiple_of(c * step_rows, step_rows), step_rows), :]
        accs = list(accs)
        for r in range(step_rows // pack):
            hit = jnp.where(pred(blk[r * pack:(r + 1) * pack, :], cand_b), one, zero)
            accs[r % len(accs)] = accs[r % len(accs)] + hit
        return tuple(accs)

    accs = lax.fori_loop(0, n_steps, body, (zero,) * 4)
    total = functools.reduce(jnp.add, [a.astype(I32) if a.dtype != F32 else a for a in accs])
    return jnp.sum(total, axis=0, keepdims=True).astype(F32)


_GE = lambda a, b: a >= b
_GT = lambda a, b: a > b


def _bisect(count_ge, target, lowest, n_bits, cnt_lowest, to_cand=lambda c: c):
    cnt = count_ge(to_cand(jnp.zeros((1, T), I32)))
    take = cnt >= target
    state = (jnp.where(take, 0, lowest).astype(I32), jnp.where(take, cnt, cnt_lowest),
             jnp.where(take, 0.0, cnt))

    def step(i, state):
        val, cnt_val, cnt_next = state
        cand = val + jnp.left_shift(jnp.int32(1), n_bits - 2 - i)
        cnt = count_ge(to_cand(cand))
        take = cnt >= target
        return jnp.where(take, cand, val), jnp.where(take, cnt, cnt_val), jnp.where(take, cnt_next, cnt)

    return lax.fori_loop(0, n_bits - 1, step, state)


def _demote_ties(sc_ref, n_steps, step_rows, thr, k):
    need = float(k) - _count_rows(sc_ref, n_steps, step_rows, _GT, thr)
    s_i = lax.broadcasted_iota(I32, (LANES, LANES), 0)
    j_i = lax.broadcasted_iota(I32, (LANES, LANES), 1)
    earlier = jnp.where(j_i < s_i, 1.0, 0.0).astype(BF16)

    def body(c, seen):
        s0 = pl.multiple_of(c * LANES, LANES)
        blk = sc_ref[pl.ds(s0, LANES), :]
        eq = blk == thr
        eq_f = jnp.where(eq, 1.0, 0.0)
        rank = seen + _dot(earlier, eq_f.astype(BF16))
        sc_ref[pl.ds(s0, LANES), :] = jnp.where(eq & (rank >= need), -jnp.inf, blk)
        return seen + jnp.sum(eq_f, axis=0, keepdims=True)

    lax.fori_loop(0, n_steps * (step_rows // LANES), body, jnp.zeros((1, T), F32))


def _top_k_threshold(sc_ref, n_steps, step_rows, k, n_valid):
    kf = float(k)
    key, cnt_key, _ = _bisect(lambda c: _count_rows(sc_ref, n_steps, step_rows, _GE, c), kf, INT_MIN, 32,
                              n_valid, to_cand=_key_to_float)
    thr = _key_to_float(jnp.maximum(key, KEY_ABOVE_NEG_INF))

    @pl.when(jnp.max(cnt_key) > kf)
    def _():
        _demote_ties(sc_ref, n_steps, step_rows, thr, k)

    return thr


def _split_keys(sc):
    bits = lax.bitcast_convert_type(sc, I32)
    key = bits ^ ((bits >> 31) & 0x7FFFFFFF)
    return (key >> 16).astype(I16), (key ^ -LOW16).astype(I16)


def _top_k_threshold_split(sc_ref, hi_ref, lo_ref, n_steps, step_rows, k, n_valid):
    kf = float(k)
    rows = float(step_rows) * n_steps.astype(F32)
    high, cnt_high, above = _bisect(lambda c: _count_rows(hi_ref, n_steps, step_rows, _GE, c), kf,
                                    LOW16, 16, rows)
    high_b = jnp.broadcast_to(high, (step_rows, T)).astype(I16)

    def keep_bucket(c, carry):
        rs = pl.ds(pl.multiple_of(c * step_rows, step_rows), step_rows)
        lo_ref[rs, :] = jnp.where(hi_ref[rs, :] == high_b, lo_ref[rs, :], jnp.int16(LOW16))
        return carry

    lax.fori_loop(0, n_steps, keep_bucket, 0)
    low, cnt_low, _ = _bisect(lambda c: _count_rows(lo_ref, n_steps, step_rows, _GE, c), kf - above,
                              LOW16, 16, cnt_high - above)
    key = jnp.left_shift(high, 16) + (low - LOW16)
    thr = _key_to_float(jnp.maximum(key, KEY_ABOVE_NEG_INF))

    @pl.when(jnp.max(jnp.where(n_valid > kf, above + cnt_low, 0.0)) > kf)
    def _():
        _demote_ties(sc_ref, n_steps, step_rows, thr, k)

    return thr


def _softmax_init(m_scr, acc_scr):
    m_scr[...] = jnp.full(m_scr.shape, NEG, F32)
    acc_scr[...] = jnp.zeros(acc_scr.shape, F32)


def _softmax_out(acc_scr, h):
    acc = acc_scr[h]
    return acc[:HEAD_DIM, :] / jnp.maximum(acc[HEAD_DIM:HEAD_DIM + 1, :], 1e-30)


def _store_heads(o_ref, heads_t):
    for pair in range(2):
        o_ref[0, :, pair * LANES:(pair + 1) * LANES] = jnp.concatenate(
            [t[:HEAD_DIM, :] for t in heads_t[2 * pair:2 * pair + 2]], axis=0).T


def _head_values(v_ref, j):
    return [v_ref[0, j, h * HV_ROWS:(h + 1) * HV_ROWS, :] for h in range(N_HEADS)]


def _attn_scratch(rows):
    return [pltpu.VMEM((N_HEADS, 1, T), F32), pltpu.VMEM((N_HEADS, rows, T), F32)]


def _attn_params():
    return pltpu.CompilerParams(dimension_semantics=("arbitrary", "arbitrary"),
                                vmem_limit_bytes=VMEM_LIMIT)


def _keys_spec(S, units, unit):
    return pl.BlockSpec((1, S, units * LANES), lambda b, i: (b, 0, unit // units))


def _q_spec(rows):
    return pl.BlockSpec((1, 1, rows, T), lambda b, i: (b, i, 0, 0))


def _v_spec(S, rows):
    return pl.BlockSpec((1, S // T, rows, T), lambda b, i: (b, 0, 0, 0))


_OUT_SPEC = pl.BlockSpec((1, T, BR_WIDTH), lambda b, i: (b, i, 0))


def _fox_kernel(q_ref, k_ref, v_ref, o_ref, m_scr, acc_scr):
    qb = pl.program_id(1)
    _softmax_init(m_scr, acc_scr)

    def tiles(js, diagonal=False):
        def inputs():
            scores = [[_dot(k_ref[0, pl.ds(pl.multiple_of(j * T, T), T), h * LANES:(h + 1) * LANES],
                            q_ref[0, 0, h * LANES:(h + 1) * LANES, :]) for j in js]
                      for h in range(N_HEADS)]
            valids = [None] * len(js)
            if diagonal:
                key, qry = _key_query_index(qb, js[0])
                valids = [key <= qry]
            return scores, valids, [_head_values(v_ref, j) for j in js]

        _softmax_update(inputs, m_scr, acc_scr, lazy=not diagonal)

    tiles([qb], diagonal=True)
    _for_tile_groups(qb, tiles, group=8)
    _store_heads(o_ref, [_softmax_out(acc_scr, h) for h in range(N_HEADS)])


def _fox_call(pr):
    B, S, _ = pr['keys'].shape
    return pl.pallas_call(
        _fox_kernel,
        out_shape=jax.ShapeDtypeStruct((B, S, BR_WIDTH), F32),
        grid=(B, S // T),
        in_specs=[_q_spec(N_HEADS * LANES), _keys_spec(S, 4, A_FK), _v_spec(S, N_HEADS * HV_ROWS)],
        out_specs=_OUT_SPEC,
        scratch_shapes=_attn_scratch(HV_ROWS),
        compiler_params=_attn_params(),
        name="fox",
    )(pr['fq_t'], pr['keys'], pr['fv_t'])


def _sb_kernel(q_ref, k_ref, v_ref, o_ref, r_scr, acc_scr):
    qb = pl.program_id(1)
    qp = _pair_pads(q_ref[0, 0])
    r_scr[...] = jnp.zeros(r_scr.shape, F32)
    acc_scr[...] = jnp.zeros(acc_scr.shape, F32)
    s_i = lax.broadcasted_iota(I32, (T, T), 0)
    j_i = lax.broadcasted_iota(I32, (T, T), 1)
    later_keys = jnp.where(j_i > s_i, 1.0, 0.0).astype(BF16)

    def tile(j, masked):
        s0 = pl.multiple_of(j * T, T)
        strict = None
        if masked:
            key, qry = _key_query_index(qb, j)
            strict = key < qry
        heads = range(N_HEADS)
        r_prev = [r_scr[h] for h in heads]
        z = [_dot(k_ref[0, pl.ds(s0, T), (h // 2) * LANES:(h // 2 + 1) * LANES], qp[h]) for h in heads]
        log_take = [_log_sigmoid(z[h]) for h in heads]
        log_keep = [log_take[h] - z[h] for h in heads]
        if masked:
            log_keep = [jnp.where(strict, lk, 0.0) for lk in log_keep]
        pieces = [_split3(lk) for lk in log_keep]
        within = [functools.reduce(jnp.add, [_dot(later_keys, p) for p in pieces[h]])
                  for h in heads]
        a = [jnp.exp(log_take[h] + (r_prev[h] + within[h])) for h in heads]
        if masked:
            a = [jnp.where(strict, w, 0.0) for w in a]
        updates = [_dot(v_ref[0, j, h * HEAD_DIM:(h + 1) * HEAD_DIM, :], a[h].astype(BF16)) for h in heads]
        for h in heads:
            acc_scr[h] = acc_scr[h] + updates[h]
            r_scr[h] = r_prev[h] + within[h][0:1, :] + log_keep[h][0:1, :]

    def live():
        return jnp.max(r_scr[...]) > SB_CUTOFF

    def step(state):
        i, _ = state
        tile(qb - 1 - i, False)
        return i + 1, live()

    tile(qb, True)
    lax.while_loop(lambda st: (st[0] < qb) & st[1], step, (jnp.int32(0), live()))
    _store_heads(o_ref, [acc_scr[h] for h in range(N_HEADS)])


def _sb_call(pr):
    B, S, _ = pr['keys'].shape
    return pl.pallas_call(
        _sb_kernel,
        out_shape=jax.ShapeDtypeStruct((B, S, BR_WIDTH), F32),
        grid=(B, S // T),
        in_specs=[_q_spec(BR_WIDTH), _keys_spec(S, 2, A_SK), _v_spec(S, BR_WIDTH)],
        out_specs=_OUT_SPEC,
        scratch_shapes=_attn_scratch(HEAD_DIM),
        compiler_params=_attn_params(),
        name="stick_breaking",
    )(pr['sq_t'], pr['keys'], pr['sv_t'])


def _dsa_kernel(q_ref, k_ref, v_ref, qi_ref, ki_ref, sm_ref, o_ref, sc_scr, hi_scr, lo_scr, m_scr, acc_scr,
                *, k_sel):
    qb = pl.program_id(1)
    row = lax.broadcasted_iota(I32, (LANES, T), 0)
    qi = qi_ref[0, 0]
    zero = jnp.zeros_like(qi)
    qi_pad = [jnp.where((row >= IDX_DIM * h) & (row < IDX_DIM * (h + 1)), qi, zero)
              for h in range(IDX_HEADS)]
    wi = [sm_ref[0, 0, SM_WI + h:SM_WI + h + 1, :] for h in range(IDX_HEADS)]

    def score_tiles(js, diagonal=False):
        rows = [pl.ds(pl.multiple_of(j * T, T), T) for j in js]
        heads = range(IDX_HEADS)
        raw = [[_dot(ki_ref[0, r, :], qi_pad[h]) for h in heads] for r in rows]
        rel = [[jnp.maximum(raw_t[h], 0.0) for h in heads] for raw_t in raw]
        scs = [functools.reduce(jnp.add, [wi[h] * rel_t[h] for h in heads]) for rel_t in rel]
        scs = [jnp.where(sc == 0.0, 0.0, sc) for sc in scs]
        if diagonal:
            key, qry = _key_query_index(qb, js[0])
            scs = [jnp.where(key <= qry, scs[0], -jnp.inf)]
        halves = [_split_keys(sc) for sc in scs]
        for r, sc, (hi, lo) in zip(rows, scs, halves):
            sc_scr[r, :] = sc
            hi_scr[r, :] = hi
            lo_scr[r, :] = lo

    _for_tile_groups(qb, score_tiles, group=8)
    score_tiles([qb], diagonal=True)

    @pl.when(qb % 2 == 0)
    def _():
        pad = pl.ds(pl.multiple_of((qb + 1) * T, T), T)
        sc_scr[pad, :] = jnp.full((T, T), -jnp.inf, F32)
        hi_scr[pad, :] = jnp.full((T, T), LOW16, I16)
        lo_scr[pad, :] = jnp.full((T, T), LOW16, I16)

    n_valid = (qb * T + 1 + lax.broadcasted_iota(I32, (1, T), 1)).astype(F32)
    thr = _top_k_threshold_split(sc_scr, hi_scr, lo_scr, qb // 2 + 1, 2 * T, k_sel, n_valid)

    qp = _pair_pads(q_ref[0, 0])
    _softmax_init(m_scr, acc_scr)

    def attn_tiles(js, lazy=True):
        def inputs():
            scores = [[_dot(k_ref[0, pl.ds(pl.multiple_of(j * T, T), T),
                                  (h // 2) * LANES:(h // 2 + 1) * LANES], qp[h]) for j in js]
                      for h in range(N_HEADS)]
            valids = [sc_scr[pl.ds(pl.multiple_of(j * T, T), T), :] >= thr for j in js]
            return scores, valids, [_head_values(v_ref, j) for j in js]

        _softmax_update(inputs, m_scr, acc_scr, lazy)

    attn_tiles([qb], lazy=False)
    _for_tile_groups(qb, attn_tiles, group=4, first_fn=functools.partial(attn_tiles, lazy=False))
    _store_heads(o_ref, [_softmax_out(acc_scr, h) for h in range(N_HEADS)])


def _dsa_call(pr):
    B, S, _ = pr['keys'].shape
    return pl.pallas_call(
        functools.partial(_dsa_kernel, k_sel=min(DSA_TOPK_MAX, S // 4)),
        out_shape=jax.ShapeDtypeStruct((B, S, BR_WIDTH), F32),
        grid=(B, S // T),
        in_specs=[_q_spec(BR_WIDTH), _keys_spec(S, 2, A_KA), _v_spec(S, N_HEADS * HV_ROWS),
                  _q_spec(LANES), _keys_spec(S, 1, A_KI), _q_spec(LANES)],
        out_specs=_OUT_SPEC,
        scratch_shapes=[pltpu.VMEM((S, T), F32), pltpu.VMEM((S, T), I16), pltpu.VMEM((S, T), I16)]
                       + _attn_scratch(HV_ROWS),
        compiler_params=_attn_params(),
        name="dsa",
    )(pr['dq_t'], pr['keys'], pr['va_t'], pr['qi_t'], pr['keys'], pr['sm_t'])


def _compress_kernel(xk_ref, xv_ref, pek_ref, pev_ref, w1k_ref, w1v_ref, w2k_ref, w2v_ref,
                     o_ref, ot_ref, *, nc):
    def hidden(x_ref, pe_ref, w1_ref):
        x = x_ref[0]
        first = _dot((x + pe_ref[0:1, :]).astype(BF16), w1_ref[0])
        second = _dot((x + pe_ref[1:2, :]).astype(BF16), w1_ref[1])
        pre = first + pltpu.roll(second, nc - 1, 0)
        return (pre * _sigmoid(pre)).astype(BF16)

    kvc = (_dot(hidden(xk_ref, pek_ref, w1k_ref), w2k_ref[...])
           + _dot(hidden(xv_ref, pev_ref, w1v_ref), w2v_ref[...]))
    o_ref[0] = kvc.astype(BF16)
    ot_ref[0] = kvc.T.astype(BF16)


def _compress_call(l, xk, xv, pek, pev, w1k, w1v, w2k, w2v):
    B, nc, width = xk.shape
    full = lambda shape: _layer_spec(l, shape, 1)
    tok = pl.BlockSpec((1, nc, width), lambda b: (b, 0, 0))
    return pl.pallas_call(
        functools.partial(_compress_kernel, nc=nc),
        out_shape=(jax.ShapeDtypeStruct((B, nc, LANES), BF16), jax.ShapeDtypeStruct((B, LANES, nc), BF16)),
        grid=(B,),
        in_specs=[tok, tok, full((2, width)), full((2, width)), full((2, width, CMP_HIDDEN)),
                  full((2, width, CMP_HIDDEN)), full((CMP_HIDDEN, LANES)), full((CMP_HIDDEN, LANES))],
        out_specs=(pl.BlockSpec((1, nc, LANES), lambda b: (b, 0, 0)),
                   pl.BlockSpec((1, LANES, nc), lambda b: (b, 0, 0))),
        compiler_params=pltpu.CompilerParams(dimension_semantics=("arbitrary",),
                                             vmem_limit_bytes=VMEM_LIMIT),
        name="nsa_compress",
    )(xk, xv, pek, pev, w1k, w1v, w2k, w2v)


def _nsa_kernel(q_ref, kvc_ref, kvct_ref, ksw_ref, vsw_ref, sm_ref, o_ref,
                imp_scr, sel_scr, ocmp_scr, m_scr, acc_scr, *, n_blk, n_sel, nbp):
    qb = pl.program_id(1)
    ncp = kvc_ref.shape[1]
    q_t = q_ref[0, 0]
    zero = jnp.zeros((HEAD_DIM, T), q_t.dtype)
    heads = [q_t[h * HEAD_DIM:(h + 1) * HEAD_DIM, :] for h in range(N_HEADS)]
    q_lo = [jnp.concatenate([q, zero], axis=0) for q in heads]
    q_hi = [jnp.concatenate([zero, q], axis=0) for q in heads]
    gate = lambda br, h: sm_ref[0, 0, SM_GATE + br * N_HEADS + h:SM_GATE + br * N_HEADS + h + 1, :]
    t_row = qb * T + lax.broadcasted_iota(I32, (1, T), 1)

    per_blk = SLC_BLOCK // CMP_STRIDE

    def compressed(rows):
        kvc = kvc_ref[0, :rows, :]
        cmp_end = lax.broadcasted_iota(I32, (rows, T), 0) * CMP_STRIDE + (CMP_LEN - 1)
        cmp_ok = cmp_end <= t_row
        cmp_logits = [jnp.where(cmp_ok, _dot(kvc, q_lo[h]), NEG) for h in range(N_HEADS)]
        cmp_tops = [jnp.max(s, axis=0, keepdims=True) for s in cmp_logits]
        cmp_e = [jnp.where(cmp_ok, jnp.exp2(s - m), 0.0) for s, m in zip(cmp_logits, cmp_tops)]
        cmp_den = [jnp.maximum(jnp.sum(e, axis=0, keepdims=True), 1e-30) for e in cmp_e]
        pcs = [e / d for e, d in zip(cmp_e, cmp_den)]
        outs = [_dot(kvct_ref[0, :, :rows], pc.astype(BF16))[HEAD_DIM:, :] for pc in pcs]
        b_i = lax.broadcasted_iota(I32, (nbp, rows), 0)
        c_i = lax.broadcasted_iota(I32, (nbp, rows), 1)
        group = jnp.where(_div_pow2(c_i, per_blk) == b_i, 1.0, 0.0).astype(BF16)
        imp_scr[...] = _dot3x(group, functools.reduce(jnp.add, pcs))
        for h in range(N_HEADS):
            ocmp_scr[h] = outs[h]

    visible = (qb + 1) * (T // CMP_STRIDE)
    covered = 0
    for rows in sorted({r for r in (ncp // 4, ncp // 2, ncp) if r % LANES == 0}):
        @pl.when((visible > covered) & ((visible <= rows) | (rows == ncp)))
        def _(rows=rows):
            compressed(rows)

        covered = rows
    o_cmp = [ocmp_scr[h] for h in range(N_HEADS)]
    n_win = WINDOW // T + 1
    j0 = jnp.maximum(qb - (n_win - 1), 0)
    heads, wins = range(N_HEADS), range(n_win)
    oks = []
    for i in wins:
        key, qry = _key_query_index(qb, j0 + i)
        dist = qry - key
        oks.append((dist >= 0) & (dist < WINDOW))
    raw = [[_dot(ksw_ref[0, pl.ds(pl.multiple_of((j0 + i) * T, T), T), :], q_hi[h]) for i in wins]
           for h in heads]
    logits = [[jnp.where(oks[i], raw[h][i], NEG) for i in wins] for h in heads]
    tops = _tile_tops(logits)
    es = [[jnp.where(oks[i], jnp.exp2(logits[h][i] - tops[h]), 0.0) for i in wins] for h in heads]
    dens = [functools.reduce(jnp.add, [jnp.sum(e, axis=0, keepdims=True) for e in es[h]]) for h in heads]
    pws = [[(es[h][i] / jnp.maximum(dens[h], 1e-30)).astype(BF16) for i in wins] for h in heads]
    o_win = [functools.reduce(jnp.add, [_dot(vsw_ref[0, j0 + i, HV_ROWS:, :], pws[h][i]) for i in wins])
             for h in heads]
    imp = imp_scr[...]
    blk = lax.broadcasted_iota(I32, (nbp, T), 0)
    cur = _div_pow2(t_row, SLC_BLOCK)
    forced = (blk == 0) | (blk == cur) | (blk == cur - 1)
    imp = jnp.where(forced, FORCE_SCORE, jnp.where(blk <= cur, imp, -1.0))
    imp = jnp.where(blk < n_blk, imp, -2.0)
    imp_scr[...] = imp
    thr = _top_k_threshold(imp_scr, 1, nbp, n_sel, jnp.full((1, T), float(nbp), F32))
    sel_scr[...] = jnp.where(imp_scr[...] >= thr, 1.0, 0.0)

    _softmax_init(m_scr, acc_scr)
    blk_per_tile = T // SLC_BLOCK

    def slc_tiles(js, diagonal=False):
        def inputs():
            valids, scores = [], [[] for _ in range(N_HEADS)]
            for j in js:
                valid = jnp.concatenate(
                    [jnp.broadcast_to(sel_scr[pl.ds(j * blk_per_tile + b, 1), :], (SLC_BLOCK, T))
                     for b in range(blk_per_tile)], axis=0) > 0.5
                if diagonal:
                    key, qry = _key_query_index(qb, j)
                    valid = valid & (key <= qry)
                valids.append(valid)
                kt = ksw_ref[0, pl.ds(pl.multiple_of(j * T, T), T), :]
                for h in range(N_HEADS):
                    scores[h].append(_dot(kt, q_lo[h]))
            return scores, valids, [[vsw_ref[0, j, :HV_ROWS, :]] * N_HEADS for j in js]

        _softmax_update(inputs, m_scr, acc_scr, lazy=not diagonal)

    slc_tiles([qb], diagonal=True)
    _for_tile_groups(qb, slc_tiles, group=4)
    _store_heads(o_ref, [gate(0, h) * o_cmp[h] + gate(1, h) * _softmax_out(acc_scr, h)
                         + gate(2, h) * o_win[h] for h in heads])


def _nsa_call(pr, kvc, kvc_t):
    B, S, _ = pr['keys'].shape
    n_blk = S // SLC_BLOCK
    nbp = max(LANES, n_blk)
    ncp = kvc.shape[1]
    return pl.pallas_call(
        functools.partial(_nsa_kernel, n_blk=n_blk, n_sel=min(SLC_TOPN, n_blk), nbp=nbp),
        out_shape=jax.ShapeDtypeStruct((B, S, BR_WIDTH), F32),
        grid=(B, S // T),
        in_specs=[_q_spec(BR_WIDTH),
                  pl.BlockSpec((1, ncp, LANES), lambda b, i: (b, 0, 0)),
                  pl.BlockSpec((1, LANES, ncp), lambda b, i: (b, 0, 0)),
                  _keys_spec(S, 1, A_KSW), _v_spec(S, VSW_ROWS), _q_spec(LANES)],
        out_specs=_OUT_SPEC,
        scratch_shapes=[pltpu.VMEM((nbp, T), F32), pltpu.VMEM((nbp, T), F32),
                        pltpu.VMEM((N_HEADS, HEAD_DIM, T), F32)] + _attn_scratch(HV_ROWS),
        compiler_params=_attn_params(),
        name="nsa",
    )(pr['nq_t'], kvc, kvc_t, pr['keys'], pr['vsw_t'], pr['sm_t'])


def _mem_kv_kernel(mem_ref, g_ref, w_ref, o_ref):
    o_ref[0] = _dot(_rmsnorm(mem_ref[0], g_ref[...]).astype(BF16), w_ref[...]).astype(BF16)


def _mem_kv_call(l, mem, g, w):
    B, M, D = mem.shape
    return pl.pallas_call(
        _mem_kv_kernel,
        out_shape=jax.ShapeDtypeStruct((B, M, 2 * BR_WIDTH), BF16),
        grid=(B,),
        in_specs=[pl.BlockSpec((1, M, D), lambda b: (b, 0, 0)), _layer_spec(l, (1, D), 1),
                  _layer_spec(l, (D, 2 * BR_WIDTH), 1)],
        out_specs=pl.BlockSpec((1, M, 2 * BR_WIDTH), lambda b: (b, 0, 0)),
        name="mem_kv",
    )(mem, g, w)


def _out_kernel(x_ref, g_ref, ya_ref, yb_ref, yc_ref, yd_ref, mq_ref, mkv_ref, wz_ref, wm_ref,
                wb_ref, wo_ref, fg_ref, o_ref, h_scr, *, final):
    x = x_ref[0]
    h_scr[...] = _rmsnorm(x, g_ref[...]).astype(BF16)
    q = mq_ref[0]
    lane = lax.broadcasted_iota(I32, (T, LANES), 1)
    lo = lane < HEAD_DIM
    zero = jnp.zeros((T, LANES), q.dtype)
    heads = range(N_HEADS)
    chunks = [q[:, (h // 2) * LANES:(h // 2 + 1) * LANES] for h in heads]
    qh = [jnp.where(lo, c, zero) if h % 2 == 0 else jnp.where(lo, zero, c) for h, c in zip(heads, chunks)]
    mk = [mkv_ref[0, :, (h // 2) * LANES:(h // 2 + 1) * LANES] for h in heads]
    mv = [mkv_ref[0, :, BR_WIDTH + (h // 2) * LANES:BR_WIDTH + (h // 2 + 1) * LANES] for h in heads]
    s = [_dot_nt(qh[h], mk[h]) for h in heads]
    e = [jnp.exp(s[h] - jnp.max(s[h], axis=1, keepdims=True)) for h in heads]
    p = [(e[h] / jnp.sum(e[h], axis=1, keepdims=True)).astype(BF16) for h in heads]
    o = [_dot(p[h], mv[h]) for h in heads]
    y_e = jnp.concatenate([jnp.where(lo, o[0], o[1]), jnp.where(lo, o[2], o[3])], axis=1)
    branches = range(N_BRANCH)
    d = x.shape[1]
    ys_in = (ya_ref[0], yb_ref[0], yc_ref[0], yd_ref[0], y_e)
    z = [_dot(h_scr[...], wz_ref[n]) for n in branches]
    ys = [(ys_in[n] * (z[n] * _sigmoid(z[n]))).astype(BF16) for n in branches]
    gates = [_sigmoid(_dot(h_scr[...], wm_ref[:, n * d:(n + 1) * d])) for n in branches]
    projected = [_dot(ys[n], wb_ref[n]) for n in branches]
    merged = functools.reduce(jnp.add, [gates[n] * projected[n] for n in branches])
    out = x + _dot(merged.astype(BF16), wo_ref[...])
    o_ref[0] = _rmsnorm(out, fg_ref[...]) if final else out


def _out_call(l, x, g, ys, keys, mkv, wz, wm, wb, wo, fg, final):
    B, S, D = x.shape
    M = mkv.shape[1]
    full = lambda shape: _layer_spec(l, shape, 2)
    tile = lambda n: pl.BlockSpec((1, T, n), lambda b, i: (b, i, 0))
    return pl.pallas_call(
        functools.partial(_out_kernel, final=final),
        out_shape=jax.ShapeDtypeStruct((B, S, D), F32),
        grid=(B, S // T),
        in_specs=[tile(D), full((1, D))] + [tile(BR_WIDTH)] * 4
                 + [pl.BlockSpec((1, T, BR_WIDTH), lambda b, i: (b, i, A_MQ // 2)),
                    pl.BlockSpec((1, M, 2 * BR_WIDTH), lambda b, i: (b, 0, 0)),
                    full((N_BRANCH, D, BR_WIDTH)), full((D, N_BRANCH * D)), full((N_BRANCH, BR_WIDTH, D)),
                    full((D, D)), pl.BlockSpec((1, D), lambda b, i: (0, 0))],
        out_specs=tile(D),
        scratch_shapes=[pltpu.VMEM((T, D), BF16)],
        compiler_params=pltpu.CompilerParams(dimension_semantics=("arbitrary", "arbitrary"),
                                             vmem_limit_bytes=VMEM_LIMIT),
        name="merge_out",
    )(x, g, *ys, keys, mkv, wz, wm, wb, wo, fg)


def kernel(x, mem, positions, norm_g, w_in, kv_norm, w_uk, w_uv, fox_bias, nsa_pe_k, nsa_pe_v,
           nsa_wc1_k, nsa_wc2_k, nsa_wc1_v, nsa_wc2_v, mem_norm, w_mem_kv, w_branch, w_out, final_norm):
    B, S, D = x.shape
    depth = norm_g.shape[0]
    assert S % (2 * T) == 0 and S >= WINDOW + T and WINDOW % T == 0 and D == D_MODEL
    cos, sin = _rope_tables(positions)
    w1, wz, wm = _proj_weight(w_in)
    fb = jnp.zeros((depth, 1, LANES), F32).at[:, 0, SM_CUM:SM_CUM + N_HEADS].set(fox_bias)
    nc = S // CMP_STRIDE
    tok_w = CMP_STRIDE * NSA_KV_DIM
    pad_k = jnp.zeros((depth, CMP_HIDDEN, LANES), F32).at[..., :NSA_KV_DIM].set(nsa_wc2_k).astype(BF16)
    pad_v = jnp.zeros((depth, CMP_HIDDEN, LANES), F32).at[..., NSA_KV_DIM:].set(nsa_wc2_v).astype(BF16)
    g, kvn, mn = norm_g[:, None, :], kv_norm[:, None, :], mem_norm[:, None, :]
    wuk, wuv, wmkv = w_uk.astype(BF16), w_uv.astype(BF16), w_mem_kv.astype(BF16)
    pek, pev = nsa_pe_k.reshape(depth, 2, tok_w), nsa_pe_v.reshape(depth, 2, tok_w)
    w1k = nsa_wc1_k.reshape(depth, 2, tok_w, CMP_HIDDEN).astype(BF16)
    w1v = nsa_wc1_v.reshape(depth, 2, tok_w, CMP_HIDDEN).astype(BF16)
    wb, wo = w_branch.astype(BF16), w_out.astype(BF16)
    for l in range(depth):
        pr = _proj_call(l, x, g, w1, kvn, wuk, wuv, fb, cos, sin)
        kvc, kvc_t = _compress_call(l, pr['kc_tok'].reshape(B, nc, tok_w), pr['vc_tok'].reshape(B, nc, tok_w),
                                    pek, pev, w1k, w1v, pad_k, pad_v)
        ys = (_dsa_call(pr), _fox_call(pr), _sb_call(pr), _nsa_call(pr, kvc, kvc_t))
        mkv = _mem_kv_call(l, mem, mn, wmkv)
        x = _out_call(l, x, g, ys, pr['keys'], mkv, wz, wm, wb, wo, final_norm[None], l == depth - 1)
    return x
```

```python
import functools

import jax
import jax.numpy as jnp
from jax import lax
from jax.experimental import pallas as pl
from jax.experimental.pallas import tpu as pltpu

F32 = jnp.float32
BF16 = jnp.bfloat16
I32 = jnp.int32
I16 = jnp.int16

LANES = 128
SUBLANES = 8
D_MODEL = 1024
N_HEADS = 4
HEAD_DIM = 64
BR_WIDTH = N_HEADS * HEAD_DIM
N_BRANCH = 5
ROPE_THETA = 10000.0
EPS = 1e-6
DSA_TOPK_MAX = 256
IDX_HEADS = 4
IDX_DIM = 32
KV_LATENT = 128
NSA_KV_DIM = 64
CMP_LEN = 32
CMP_STRIDE = 16
CMP_HIDDEN = 128
SLC_BLOCK = 64
SLC_TOPN = 16
WINDOW = 512
FORCE_SCORE = 1e4
Q_SCALE = HEAD_DIM ** -0.5
LOG2E = 1.4426950408889634
Q_SCALE2 = Q_SCALE * LOG2E
IDX_SCALE = (IDX_DIM ** -0.5) * (IDX_HEADS ** -0.5)
NEG = -1e30
SB_CUTOFF = -104.0
LAZY_LIMIT = 64.0
VMEM_LIMIT = 56 * 1024 * 1024
T = 256
ONES_ROWS = 16
HV_ROWS = HEAD_DIM + ONES_ROWS
VSW_ROWS = HV_ROWS + NSA_KV_DIM

IN_LAYOUT = (
    ('dsa_q', BR_WIDTH), ('dsa_ckv', KV_LATENT), ('idx_q', IDX_HEADS * IDX_DIM), ('idx_k', IDX_DIM),
    ('idx_w', IDX_HEADS), ('dsa_z', BR_WIDTH),
    ('fox_q', BR_WIDTH), ('fox_k', BR_WIDTH), ('fox_v', BR_WIDTH), ('fox_f', N_HEADS), ('fox_z', BR_WIDTH),
    ('sb_q', BR_WIDTH), ('sb_k', BR_WIDTH), ('sb_v', BR_WIDTH), ('sb_z', BR_WIDTH),
    ('nsa_q', BR_WIDTH), ('nsa_kc', NSA_KV_DIM), ('nsa_vc', NSA_KV_DIM), ('nsa_ks', NSA_KV_DIM),
    ('nsa_vs', NSA_KV_DIM), ('nsa_kw', NSA_KV_DIM), ('nsa_vw', NSA_KV_DIM), ('nsa_g', 3 * N_HEADS),
    ('nsa_z', BR_WIDTH), ('mem_q', BR_WIDTH), ('mem_z', BR_WIDTH), ('merge', N_BRANCH * D_MODEL),
)

A_FK, A_KA, A_SK, A_KI, A_KSW, A_MQ = 0, 4, 6, 8, 9, 10
A_COLS = 12 * LANES
SM_WI, SM_CUM, SM_GATE = 0, 4, 8
FOX_CK, FOX_ONE, FOX_END = HEAD_DIM, HEAD_DIM + 3, HEAD_DIM + 6


def _dot(a, b):
    return jnp.dot(a, b, preferred_element_type=F32)


def _dot_nt(a, b):
    return lax.dot_general(a, b, (((1,), (1,)), ((), ())), preferred_element_type=F32)


def _split3(x):
    hi = x.astype(BF16)
    r1 = x - hi.astype(F32)
    mid = r1.astype(BF16)
    lo = (r1 - mid.astype(F32)).astype(BF16)
    return hi, mid, lo


def _dot3x(m01, x):
    hi, mid, lo = _split3(x)
    return _dot(m01, hi) + _dot(m01, mid) + _dot(m01, lo)


def _div_pow2(x, n):
    assert n & (n - 1) == 0
    return x >> (n.bit_length() - 1)


def _sigmoid(x):
    return 1.0 / (1.0 + jnp.exp(-x))


def _log_sigmoid(x):
    return jnp.minimum(x, 0.0) - jnp.log(1.0 + jnp.exp(-jnp.abs(x)))


def _rmsnorm(x, g):
    return x * lax.rsqrt(jnp.mean(x * x, axis=-1, keepdims=True) + EPS) * g


def _rope128(x, cos, sin_signed, half):
    lane = lax.broadcasted_iota(I32, x.shape, 1)
    first = (lane & (2 * half - 1)) < half
    partner = jnp.where(first, pltpu.roll(x, LANES - half, 1), pltpu.roll(x, half, 1))
    return x * cos + partner * sin_signed


def _rope_table_kernel(pos_ref, inv_ref, sgn_ref, cos_ref, sin_ref):
    ang = pos_ref[0] * inv_ref[...]
    cos_ref[0] = jnp.cos(ang)
    sin_ref[0] = jnp.sin(ang) * sgn_ref[...]


def _rope_tables(positions):
    B, S = positions.shape
    lane = jnp.arange(LANES)

    def inv_row(dh):
        inv = ROPE_THETA ** (-jnp.arange(0, dh, 2, dtype=F32) / dh)
        return inv[(lane % dh) % (dh // 2)]

    def sgn_row(dh):
        return jnp.where((lane % dh) < dh // 2, -1.0, 1.0).astype(F32)

    inv = jnp.concatenate([inv_row(HEAD_DIM), inv_row(IDX_DIM)])[None, :]
    sgn = jnp.concatenate([sgn_row(HEAD_DIM), sgn_row(IDX_DIM)])[None, :]
    pos = positions.astype(F32)[..., None]
    row = pl.BlockSpec((1, 2 * LANES), lambda b, i: (0, 0))
    tab = pl.BlockSpec((1, T, 2 * LANES), lambda b, i: (b, i, 0))
    return pl.pallas_call(
        _rope_table_kernel,
        out_shape=(jax.ShapeDtypeStruct((B, S, 2 * LANES), F32),) * 2,
        grid=(B, S // T),
        in_specs=[pl.BlockSpec((1, T, 1), lambda b, i: (b, i, 0)), row, row],
        out_specs=(tab, tab),
        name="rope_tables",
    )(pos, inv, sgn)


W_DQ, W_CKV, W_QI, W_KI, W_SM, W_FQ, W_SQ, W_NQ, W_MQ, W_KVC, W_KSW, W_VSW = (
    0, 256, 384, 512, 640, 768, 1536, 2304, 2560, 2816, 2944, 3072)
W_COLS = 3200

PROJ_OUTS = (
    ('keys', A_COLS, BF16), ('kc_tok', NSA_KV_DIM, F32), ('vc_tok', NSA_KV_DIM, F32),
    ('dq_t', BR_WIDTH, BF16), ('fq_t', N_HEADS * LANES, BF16), ('sq_t', BR_WIDTH, BF16),
    ('nq_t', BR_WIDTH, BF16), ('qi_t', LANES, BF16), ('sm_t', LANES, F32),
    ('va_t', N_HEADS * HV_ROWS, BF16), ('fv_t', N_HEADS * HV_ROWS, BF16), ('sv_t', BR_WIDTH, BF16),
    ('vsw_t', VSW_ROWS, BF16),
)
ROW_MAJOR = ('keys', 'kc_tok', 'vc_tok')


def _proj_kernel(x_ref, g_ref, w_ref, kvn_ref, wuk_ref, wuv_ref, fb_ref, cos_ref, sin_ref,
                 *refs):
    out = dict(zip([n for n, _, _ in PROJ_OUTS], refs))
    h_scr, carry_scr = refs[len(PROJ_OUTS):]
    keys = out['keys']
    h_scr[...] = _rmsnorm(x_ref[0], g_ref[...]).astype(BF16)
    c64, s64 = cos_ref[0, :, :LANES], sin_ref[0, :, :LANES]
    c32, s32 = cos_ref[0, :, LANES:], sin_ref[0, :, LANES:]
    lane = lax.broadcasted_iota(I32, (T, LANES), 1)
    ones_rows = jnp.ones((ONES_ROWS, T), BF16)

    def proj(c0, n):
        return _dot(h_scr[...], w_ref[:, c0:c0 + n])

    def put(unit, val):
        keys[0, :, unit * LANES:unit * LANES + val.shape[1]] = val.astype(BF16)

    def rope64(v, j):
        return _rope128(v[:, j * LANES:(j + 1) * LANES], c64, s64, HEAD_DIM // 2)

    def put_t(ref, row0, chunk):
        ref[0, 0, row0:row0 + LANES, :] = chunk.T.astype(ref.dtype)

    def put_values_t(ref, v):
        for j in range(2):
            pair_t = v[:, j * LANES:(j + 1) * LANES].T.astype(BF16)
            for odd in range(2):
                r0 = (2 * j + odd) * HV_ROWS
                ref[0, 0, r0:r0 + HEAD_DIM, :] = pair_t[odd * HEAD_DIM:(odd + 1) * HEAD_DIM, :]
                ref[0, 0, r0 + HEAD_DIM:r0 + HV_ROWS, :] = ones_rows

    p = proj(W_DQ, BR_WIDTH)
    for j in range(2):
        put_t(out['dq_t'], j * LANES, rope64(p, j) * Q_SCALE2)
    c_kv = _rmsnorm(proj(W_CKV, KV_LATENT), kvn_ref[...]).astype(BF16)
    k_a = _dot(c_kv, wuk_ref[...])
    for j in range(2):
        put(A_KA + j, rope64(k_a, j))
    put_values_t(out['va_t'], _dot(c_kv, wuv_ref[...]))
    put_t(out['qi_t'], 0, _rope128(proj(W_QI, LANES), c32, s32, IDX_DIM // 2))
    put(A_KI, _rope128(proj(W_KI, LANES), c32, s32, IDX_DIM // 2))
    p = proj(W_SM, LANES)
    log_f = _log_sigmoid(p + fb_ref[...])
    r_i = lax.broadcasted_iota(I32, (T, T), 0)
    c_i = lax.broadcasted_iota(I32, (T, T), 1)
    tri = jnp.where(r_i >= c_i, 1.0, 0.0).astype(BF16)

    @pl.when(pl.program_id(1) == 0)
    def _():
        carry_scr[...] = jnp.zeros_like(carry_scr)

    cum = _dot3x(tri, log_f) + carry_scr[...]
    carry_scr[...] = cum[T - 1:T, :]
    small_t = jnp.where(lane < SM_CUM, p * IDX_SCALE,
                        jnp.where(lane < SM_GATE, cum, _sigmoid(p))).T
    out['sm_t'][0, 0] = small_t
    q_f = proj(W_FQ, BR_WIDTH) * Q_SCALE2
    cum2, cum2_t = cum * LOG2E, small_t * LOG2E
    k_f = proj(W_FQ + 256, BR_WIDTH)
    row = lax.broadcasted_iota(I32, (HEAD_DIM, T), 0)
    for h in range(N_HEADS):
        pair, odd = h // 2, h % 2
        k_chunk = k_f[:, pair * LANES:(pair + 1) * LANES]
        k_h = pltpu.roll(k_chunk, HEAD_DIM, 1) if odd else k_chunk
        ck = [c.astype(F32) for c in _split3(cum2[:, SM_CUM + h:SM_CUM + h + 1])]
        k_aug = jnp.where(lane < HEAD_DIM, k_h,
                          jnp.where(lane == FOX_CK, ck[0],
                                    jnp.where(lane == FOX_CK + 1, ck[1],
                                              jnp.where(lane == FOX_CK + 2, ck[2],
                                                        jnp.where(lane < FOX_END, 1.0, 0.0)))))
        put(A_FK + h, k_aug)
        q_ht = q_f[:, pair * LANES:(pair + 1) * LANES].T[odd * HEAD_DIM:(odd + 1) * HEAD_DIM, :]
        cq = [c.astype(F32) for c in _split3(cum2_t[SM_CUM + h:SM_CUM + h + 1, :])]
        aug = jnp.where(row < 3, -1.0,
                        jnp.where(row == 3, cq[0], jnp.where(row == 4, cq[1],
                                                             jnp.where(row == 5, cq[2], 0.0))))
        out['fq_t'][0, 0, h * LANES:(h + 1) * LANES, :] = jnp.concatenate(
            [q_ht, aug], axis=0).astype(BF16)
    put_values_t(out['fv_t'], proj(W_FQ + 512, BR_WIDTH))
    p = proj(W_SQ, BR_WIDTH) * Q_SCALE
    for j in range(2):
        put_t(out['sq_t'], j * LANES, p[:, j * LANES:(j + 1) * LANES])
    put(A_SK, proj(W_SQ + 256, BR_WIDTH))
    p = proj(W_SQ + 512, BR_WIDTH)
    for j in range(2):
        put_t(out['sv_t'], j * LANES, p[:, j * LANES:(j + 1) * LANES])
    put(A_MQ, proj(W_MQ, BR_WIDTH) * Q_SCALE)
    p = proj(W_NQ, BR_WIDTH)
    for j in range(2):
        put_t(out['nq_t'], j * LANES, rope64(p, j) * Q_SCALE2)
    p = proj(W_KVC, LANES)
    out['kc_tok'][0] = rope64(p, 0)[:, :NSA_KV_DIM]
    out['vc_tok'][0] = p[:, NSA_KV_DIM:]
    put(A_KSW, rope64(proj(W_KSW, LANES), 0))
    vsw_t = proj(W_VSW, LANES).T.astype(BF16)
    out['vsw_t'][0, 0, :NSA_KV_DIM, :] = vsw_t[:NSA_KV_DIM, :]
    out['vsw_t'][0, 0, NSA_KV_DIM:HV_ROWS, :] = ones_rows
    out['vsw_t'][0, 0, HV_ROWS:, :] = vsw_t[NSA_KV_DIM:, :]


def _proj_weight(w_in):
    off, o = {}, 0
    for name, n in IN_LAYOUT:
        off[name] = (o, n)
        o += n

    def col(name):
        s, n = off[name]
        return w_in[..., s:s + n]

    zeros = lambda n: jnp.zeros(w_in.shape[:-1] + (n,), w_in.dtype)
    groups = [col('dsa_q'), col('dsa_ckv'), col('idx_q'), col('idx_k'), col('idx_k'), col('idx_k'),
              col('idx_k'), col('idx_w'), col('fox_f'), col('nsa_g'), zeros(LANES - 20),
              col('fox_q'), col('fox_k'), col('fox_v'), col('sb_q'), col('sb_k'), col('sb_v'),
              col('nsa_q'), col('mem_q'), col('nsa_kc'), col('nsa_vc'), col('nsa_ks'), col('nsa_kw'),
              col('nsa_vs'), col('nsa_vw')]
    w1 = jnp.concatenate(groups, axis=-1).astype(BF16)
    wz = jnp.stack([col(n) for n in ('dsa_z', 'fox_z', 'sb_z', 'nsa_z', 'mem_z')], axis=-3).astype(BF16)
    return w1, wz, col('merge').astype(BF16)


def _layer_spec(l, shape, grid_rank):
    if grid_rank == 1:
        return pl.BlockSpec((None,) + shape, lambda b: (l,) + (0,) * len(shape))
    return pl.BlockSpec((None,) + shape, lambda b, i: (l,) + (0,) * len(shape))


def _proj_call(l, x, g, w1, kvn, wuk, wuv, fb, cos, sin):
    B, S, D = x.shape
    full = lambda shape: _layer_spec(l, shape, 2)
    tile = lambda n: pl.BlockSpec((1, T, n), lambda b, i: (b, i, 0))
    tile_t = lambda n: pl.BlockSpec((1, 1, n, T), lambda b, i: (b, i, 0, 0))
    shapes, specs = [], []
    for name, n, dt in PROJ_OUTS:
        if name in ROW_MAJOR:
            shapes.append(jax.ShapeDtypeStruct((B, S, n), dt))
            specs.append(tile(n))
        else:
            shapes.append(jax.ShapeDtypeStruct((B, S // T, n, T), dt))
            specs.append(tile_t(n))
    outs = pl.pallas_call(
        _proj_kernel,
        out_shape=tuple(shapes),
        grid=(B, S // T),
        in_specs=[tile(D), full((1, D)), full((D, W_COLS)), full((1, KV_LATENT)),
                  full((KV_LATENT, BR_WIDTH)), full((KV_LATENT, BR_WIDTH)), full((1, LANES)),
                  tile(2 * LANES), tile(2 * LANES)],
        out_specs=tuple(specs),
        scratch_shapes=[pltpu.VMEM((T, D), BF16), pltpu.VMEM((1, LANES), F32)],
        compiler_params=pltpu.CompilerParams(dimension_semantics=("arbitrary", "arbitrary"),
                                             vmem_limit_bytes=VMEM_LIMIT),
        name="proj",
    )(x, g, w1, kvn, wuk, wuv, fb, cos, sin)
    return dict(zip([n for n, _, _ in PROJ_OUTS], outs))


def _key_query_index(qb, j):
    key = j * T + lax.broadcasted_iota(I32, (T, T), 0)
    qry = qb * T + lax.broadcasted_iota(I32, (T, T), 1)
    return key, qry


def _pair_pads(q_t):
    row = lax.broadcasted_iota(I32, (LANES, T), 0)
    top = row < HEAD_DIM
    zero = jnp.zeros((LANES, T), q_t.dtype)
    out = []
    for pair in range(2):
        chunk = q_t[pair * LANES:(pair + 1) * LANES, :]
        out.append(jnp.where(top, chunk, zero))
        out.append(jnp.where(top, zero, chunk))
    return out


def _masked_tiles(scores, valids):
    return [[s if v is None else jnp.where(v, s, 2 * NEG) for s, v in zip(scores_h, valids)]
            for scores_h in scores]


def _tile_tops(tiles):
    return [functools.reduce(jnp.maximum, [jnp.max(s, axis=0, keepdims=True) for s in tiles_h])
            for tiles_h in tiles]


def _weighted_values(tiles, values, m):
    weights = [[jnp.exp2(s - m[h]).astype(BF16) for s in tiles_h] for h, tiles_h in enumerate(tiles)]
    parts = [[_dot(values[i][h], w) for i, w in enumerate(weights_h)] for h, weights_h in enumerate(weights)]
    return [functools.reduce(jnp.add, parts_h) for parts_h in parts]


def _softmax_update(inputs_fn, m_scr, acc_scr, lazy=False):
    if not lazy:
        _softmax_update_exact(*inputs_fn(), m_scr, acc_scr)
        return
    scores, valids, values = inputs_fn()
    m_ref = [m_scr[h] for h in range(N_HEADS)]
    tiles = _masked_tiles(scores, valids)
    updates = _weighted_values(tiles, values, m_ref)
    tops = _tile_tops(tiles)
    overshoot = functools.reduce(jnp.maximum, [jnp.max(tops[h] - m_ref[h]) for h in range(N_HEADS)])
    safe = overshoot <= LAZY_LIMIT

    @pl.when(safe)
    def _():
        for h in range(N_HEADS):
            m_new = jnp.maximum(m_ref[h], tops[h])
            acc_scr[h] = jnp.exp2(m_ref[h] - m_new) * (acc_scr[h] + updates[h])
            m_scr[h] = m_new

    @pl.when(jnp.logical_not(safe))
    def _():
        _softmax_update_exact(*inputs_fn(), m_scr, acc_scr)


def _softmax_update_exact(scores, valids, values, m_scr, acc_scr):
    m_prev = [m_scr[h] for h in range(N_HEADS)]
    tiles = _masked_tiles(scores, valids)
    m_new = [jnp.maximum(m, top) for m, top in zip(m_prev, _tile_tops(tiles))]
    updates = _weighted_values(tiles, values, m_new)
    for h in range(N_HEADS):
        acc_scr[h] = jnp.exp2(m_prev[h] - m_new[h]) * acc_scr[h] + updates[h]
        m_scr[h] = m_new[h]


def _for_tile_groups(n, group_fn, group, first_fn=None):
    def full(i, c):
        group_fn([group * i + g for g in range(group)])
        return c

    if first_fn is not None:
        @pl.when(n >= group)
        def _():
            first_fn(list(range(group)))

    lax.fori_loop(0 if first_fn is None else 1, n // group, full, 0)
    size = group // 2
    while size:
        start = (n // (2 * size)) * (2 * size)

        @pl.when((n & size) != 0)
        def _(start=start, size=size):
            group_fn([start + g for g in range(size)])

        size //= 2


INT_MIN = -2 ** 31
LOW16 = -2 ** 15
KEY_ABOVE_NEG_INF = 0x80800000 - 2 ** 32


def _key_to_float(c):
    bits = c ^ ((c >> 31) & 0x7FFFFFFF)
    return lax.bitcast_convert_type(bits, F32)


def _count_rows(ref, n_steps, step_rows, pred, cand):
    pack = SUBLANES * (4 // ref.dtype.itemsize)
    cand_b = jnp.broadcast_to(cand, (pack, T)).astype(ref.dtype)
    one, zero = jnp.ones((pack, T), ref.dtype), jnp.zeros((pack, T), ref.dtype)

    def body(c, accs):
        blk = ref[pl.ds(pl.multiple_of(c * step_rows, step_rows), step_rows), :]
        accs = list(accs)
        for r in range(step_rows // pack):
            hit = jnp.where(pred(blk[r * pack:(r + 1) * pack, :], cand_b), one, zero)
            accs[r % len(accs)] = accs[r % len(accs)] + hit
        return tuple(accs)

    accs = lax.fori_loop(0, n_steps, body, (zero,) * 4)
    total = functools.reduce(jnp.add, [a.astype(I32) if a.dtype != F32 else a for a in accs])
    return jnp.sum(total, axis=0, keepdims=True).astype(F32)


_GE = lambda a, b: a >= b
_GT = lambda a, b: a > b


def _bisect(count_ge, target, lowest, n_bits, cnt_lowest, to_cand=lambda c: c):
    cnt = count_ge(to_cand(jnp.zeros((1, T), I32)))
    take = cnt >= target
    state = (jnp.where(take, 0, lowest).astype(I32), jnp.where(take, cnt, cnt_lowest),
             jnp.where(take, 0.0, cnt))

    def step(i, state):
        val, cnt_val, cnt_next = state
        cand = val + jnp.left_shift(jnp.int32(1), n_bits - 2 - i)
        cnt = count_ge(to_cand(cand))
        take = cnt >= target
        return jnp.where(take, cand, val), jnp.where(take, cnt, cnt_val), jnp.where(take, cnt_next, cnt)

    return lax.fori_loop(0, n_bits - 1, step, state)


def _demote_ties(sc_ref, n_steps, step_rows, thr, k):
    need = float(k) - _count_rows(sc_ref, n_steps, step_rows, _GT, thr)
    s_i = lax.broadcasted_iota(I32, (LANES, LANES), 0)
    j_i = lax.broadcasted_iota(I32, (LANES, LANES), 1)
    earlier = jnp.where(j_i < s_i, 1.0, 0.0).astype(BF16)

    def body(c, seen):
        s0 = pl.multiple_of(c * LANES, LANES)
        blk = sc_ref[pl.ds(s0, LANES), :]
        eq = blk == thr
        eq_f = jnp.where(eq, 1.0, 0.0)
        rank = seen + _dot(earlier, eq_f.astype(BF16))
        sc_ref[pl.ds(s0, LANES), :] = jnp.where(eq & (rank >= need), -jnp.inf, blk)
        return seen + jnp.sum(eq_f, axis=0, keepdims=True)

    lax.fori_loop(0, n_steps * (step_rows // LANES), body, jnp.zeros((1, T), F32))


def _top_k_threshold(sc_ref, n_steps, step_rows, k, n_valid):
    kf = float(k)
    key, cnt_key, _ = _bisect(lambda c: _count_rows(sc_ref, n_steps, step_rows, _GE, c), kf, INT_MIN, 32,
                              n_valid, to_cand=_key_to_float)
    thr = _key_to_float(jnp.maximum(key, KEY_ABOVE_NEG_INF))

    @pl.when(jnp.max(cnt_key) > kf)
    def _():
        _demote_ties(sc_ref, n_steps, step_rows, thr, k)

    return thr


def _split_keys(sc):
    bits = lax.bitcast_convert_type(sc, I32)
    key = bits ^ ((bits >> 31) & 0x7FFFFFFF)
    return (key >> 16).astype(I16), (key ^ -LOW16).astype(I16)


def _top_k_threshold_split(sc_ref, hi_ref, lo_ref, n_steps, step_rows, k, n_valid):
    kf = float(k)
    rows = float(step_rows) * n_steps.astype(F32)
    high, cnt_high, above = _bisect(lambda c: _count_rows(hi_ref, n_steps, step_rows, _GE, c), kf,
                                    LOW16, 16, rows)
    high_b = jnp.broadcast_to(high, (step_rows, T)).astype(I16)

    def keep_bucket(c, carry):
        rs = pl.ds(pl.multiple_of(c * step_rows, step_rows), step_rows)
        lo_ref[rs, :] = jnp.where(hi_ref[rs, :] == high_b, lo_ref[rs, :], jnp.int16(LOW16))
        return carry

    lax.fori_loop(0, n_steps, keep_bucket, 0)
    low, cnt_low, _ = _bisect(lambda c: _count_rows(lo_ref, n_steps, step_rows, _GE, c), kf - above,
                              LOW16, 16, cnt_high - above)
    key = jnp.left_shift(high, 16) + (low - LOW16)
    thr = _key_to_float(jnp.maximum(key, KEY_ABOVE_NEG_INF))

    @pl.when(jnp.max(jnp.where(n_valid > kf, above + cnt_low, 0.0)) > kf)
    def _():
        _demote_ties(sc_ref, n_steps, step_rows, thr, k)

    return thr


def _softmax_init(m_scr, acc_scr):
    m_scr[...] = jnp.full(m_scr.shape, NEG, F32)
    acc_scr[...] = jnp.zeros(acc_scr.shape, F32)


def _softmax_out(acc_scr, h):
    acc = acc_scr[h]
    return acc[:HEAD_DIM, :] / jnp.maximum(acc[HEAD_DIM:HEAD_DIM + 1, :], 1e-30)


def _store_heads(o_ref, heads_t):
    for pair in range(2):
        o_ref[0, :, pair * LANES:(pair + 1) * LANES] = jnp.concatenate(
            [t[:HEAD_DIM, :] for t in heads_t[2 * pair:2 * pair + 2]], axis=0).T


def _head_values(v_ref, j):
    return [v_ref[0, j, h * HV_ROWS:(h + 1) * HV_ROWS, :] for h in range(N_HEADS)]


def _attn_scratch(rows):
    return [pltpu.VMEM((N_HEADS, 1, T), F32), pltpu.VMEM((N_HEADS, rows, T), F32)]


def _attn_params():
    return pltpu.CompilerParams(dimension_semantics=("arbitrary", "arbitrary"),
                                vmem_limit_bytes=VMEM_LIMIT)


def _keys_spec(S, units, unit):
    return pl.BlockSpec((1, S, units * LANES), lambda b, i: (b, 0, unit // units),
                        pipeline_mode=pl.Buffered(1))


def _q_spec(rows):
    return pl.BlockSpec((1, 1, rows, T), lambda b, i: (b, i, 0, 0))


def _v_spec(S, rows):
    return pl.BlockSpec((1, S // T, rows, T), lambda b, i: (b, 0, 0, 0), pipeline_mode=pl.Buffered(1))


_OUT_SPEC = pl.BlockSpec((1, T, BR_WIDTH), lambda b, i: (b, i, 0))


def _fox_kernel(q_ref, k_ref, v_ref, o_ref, m_scr, acc_scr):
    qb = pl.program_id(1)
    _softmax_init(m_scr, acc_scr)

    def tiles(js, diagonal=False):
        def inputs():
            scores = [[_dot(k_ref[0, pl.ds(pl.multiple_of(j * T, T), T), h * LANES:(h + 1) * LANES],
                            q_ref[0, 0, h * LANES:(h + 1) * LANES, :]) for j in js]
                      for h in range(N_HEADS)]
            valids = [None] * len(js)
            if diagonal:
                key, qry = _key_query_index(qb, js[0])
                valids = [key <= qry]
            return scores, valids, [_head_values(v_ref, j) for j in js]

        _softmax_update(inputs, m_scr, acc_scr, lazy=not diagonal)

    tiles([qb], diagonal=True)
    _for_tile_groups(qb, tiles, group=8)
    _store_heads(o_ref, [_softmax_out(acc_scr, h) for h in range(N_HEADS)])


def _fox_call(pr):
    B, S, _ = pr['keys'].shape
    return pl.pallas_call(
        _fox_kernel,
        out_shape=jax.ShapeDtypeStruct((B, S, BR_WIDTH), F32),
        grid=(B, S // T),
        in_specs=[_q_spec(N_HEADS * LANES), _keys_spec(S, 4, A_FK), _v_spec(S, N_HEADS * HV_ROWS)],
        out_specs=_OUT_SPEC,
        scratch_shapes=_attn_scratch(HV_ROWS),
        compiler_params=_attn_params(),
        name="fox",
    )(pr['fq_t'], pr['keys'], pr['fv_t'])


def _sb_kernel(q_ref, k_ref, v_ref, o_ref, r_scr, acc_scr):
    qb = pl.program_id(1)
    qp = _pair_pads(q_ref[0, 0])
    r_scr[...] = jnp.zeros(r_scr.shape, F32)
    acc_scr[...] = jnp.zeros(acc_scr.shape, F32)
    s_i = lax.broadcasted_iota(I32, (T, T), 0)
    j_i = lax.broadcasted_iota(I32, (T, T), 1)
    later_keys = jnp.where(j_i > s_i, 1.0, 0.0).astype(BF16)

    def tile(j, masked):
        s0 = pl.multiple_of(j * T, T)
        strict = None
        if masked:
            key, qry = _key_query_index(qb, j)
            strict = key < qry
        heads = range(N_HEADS)
        r_prev = [r_scr[h] for h in heads]
        z = [_dot(k_ref[0, pl.ds(s0, T), (h // 2) * LANES:(h // 2 + 1) * LANES], qp[h]) for h in heads]
        log_take = [_log_sigmoid(z[h]) for h in heads]
        log_keep = [log_take[h] - z[h] for h in heads]
        if masked:
            log_keep = [jnp.where(strict, lk, 0.0) for lk in log_keep]
        pieces = [_split3(lk) for lk in log_keep]
        within = [functools.reduce(jnp.add, [_dot(later_keys, p) for p in pieces[h]])
                  for h in heads]
        a = [jnp.exp(log_take[h] + (r_prev[h] + within[h])) for h in heads]
        if masked:
            a = [jnp.where(strict, w, 0.0) for w in a]
        updates = [_dot(v_ref[0, j, h * HEAD_DIM:(h + 1) * HEAD_DIM, :], a[h].astype(BF16)) for h in heads]
        for h in heads:
            acc_scr[h] = acc_scr[h] + updates[h]
            r_scr[h] = r_prev[h] + within[h][0:1, :] + log_keep[h][0:1, :]

    def live():
        return jnp.max(r_scr[...]) > SB_CUTOFF

    def step(state):
        i, _ = state
        tile(qb - 1 - i, False)
        return i + 1, live()

    tile(qb, True)
    lax.while_loop(lambda st: (st[0] < qb) & st[1], step, (jnp.int32(0), live()))
    _store_heads(o_ref, [acc_scr[h] for h in range(N_HEADS)])


def _sb_call(pr):
    B, S, _ = pr['keys'].shape
    return pl.pallas_call(
        _sb_kernel,
        out_shape=jax.ShapeDtypeStruct((B, S, BR_WIDTH), F32),
        grid=(B, S // T),
        in_specs=[_q_spec(BR_WIDTH), _keys_spec(S, 2, A_SK), _v_spec(S, BR_WIDTH)],
        out_specs=_OUT_SPEC,
        scratch_shapes=_attn_scratch(HEAD_DIM),
        compiler_params=_attn_params(),
        name="stick_breaking",
    )(pr['sq_t'], pr['keys'], pr['sv_t'])


def _dsa_kernel(q_ref, k_ref, v_ref, qi_ref, ki_ref, sm_ref, o_ref, sc_scr, hi_scr, lo_scr, m_scr, acc_scr,
                *, k_sel):
    qb = pl.program_id(1)
    row = lax.broadcasted_iota(I32, (LANES, T), 0)
    qi = qi_ref[0, 0]
    zero = jnp.zeros_like(qi)
    qi_pad = [jnp.where((row >= IDX_DIM * h) & (row < IDX_DIM * (h + 1)), qi, zero)
              for h in range(IDX_HEADS)]
    wi = [sm_ref[0, 0, SM_WI + h:SM_WI + h + 1, :] for h in range(IDX_HEADS)]

    def score_tiles(js, diagonal=False):
        rows = [pl.ds(pl.multiple_of(j * T, T), T) for j in js]
        heads = range(IDX_HEADS)
        raw = [[_dot(ki_ref[0, r, :], qi_pad[h]) for h in heads] for r in rows]
        rel = [[jnp.maximum(raw_t[h], 0.0) for h in heads] for raw_t in raw]
        scs = [functools.reduce(jnp.add, [wi[h] * rel_t[h] for h in heads]) for rel_t in rel]
        scs = [jnp.where(sc == 0.0, 0.0, sc) for sc in scs]
        if diagonal:
            key, qry = _key_query_index(qb, js[0])
            scs = [jnp.where(key <= qry, scs[0], -jnp.inf)]
        halves = [_split_keys(sc) for sc in scs]
        for r, sc, (hi, lo) in zip(rows, scs, halves):
            sc_scr[r, :] = sc
            hi_scr[r, :] = hi
            lo_scr[r, :] = lo

    _for_tile_groups(qb, score_tiles, group=8)
    score_tiles([qb], diagonal=True)

    @pl.when(qb % 2 == 0)
    def _():
        pad = pl.ds(pl.multiple_of((qb + 1) * T, T), T)
        sc_scr[pad, :] = jnp.full((T, T), -jnp.inf, F32)
        hi_scr[pad, :] = jnp.full((T, T), LOW16, I16)
        lo_scr[pad, :] = jnp.full((T, T), LOW16, I16)

    n_valid = (qb * T + 1 + lax.broadcasted_iota(I32, (1, T), 1)).astype(F32)
    thr = _top_k_threshold_split(sc_scr, hi_scr, lo_scr, qb // 2 + 1, 2 * T, k_sel, n_valid)

    qp = _pair_pads(q_ref[0, 0])
    _softmax_init(m_scr, acc_scr)

    def attn_tiles(js, lazy=True):
        def inputs():
            scores = [[_dot(k_ref[0, pl.ds(pl.multiple_of(j * T, T), T),
                                  (h // 2) * LANES:(h // 2 + 1) * LANES], qp[h]) for j in js]
                      for h in range(N_HEADS)]
            valids = [sc_scr[pl.ds(pl.multiple_of(j * T, T), T), :] >= thr for j in js]
            return scores, valids, [_head_values(v_ref, j) for j in js]

        _softmax_update(inputs, m_scr, acc_scr, lazy)

    attn_tiles([qb], lazy=False)
    _for_tile_groups(qb, attn_tiles, group=4, first_fn=functools.partial(attn_tiles, lazy=False))
    _store_heads(o_ref, [_softmax_out(acc_scr, h) for h in range(N_HEADS)])


def _dsa_call(pr):
    B, S, _ = pr['keys'].shape
    return pl.pallas_call(
        functools.partial(_dsa_kernel, k_sel=min(DSA_TOPK_MAX, S // 4)),
        out_shape=jax.ShapeDtypeStruct((B, S, BR_WIDTH), F32),
        grid=(B, S // T),
        in_specs=[_q_spec(BR_WIDTH), _keys_spec(S, 2, A_KA), _v_spec(S, N_HEADS * HV_ROWS),
                  _q_spec(LANES), _keys_spec(S, 1, A_KI), _q_spec(LANES)],
        out_specs=_OUT_SPEC,
        scratch_shapes=[pltpu.VMEM((S, T), F32), pltpu.VMEM((S, T), I16), pltpu.VMEM((S, T), I16)]
                       + _attn_scratch(HV_ROWS),
        compiler_params=_attn_params(),
        name="dsa",
    )(pr['dq_t'], pr['keys'], pr['va_t'], pr['qi_t'], pr['keys'], pr['sm_t'])


def _compress_kernel(xk_ref, xv_ref, pek_ref, pev_ref, w1k_ref, w1v_ref, w2k_ref, w2v_ref,
                     o_ref, ot_ref, *, nc):
    def hidden(x_ref, pe_ref, w1_ref):
        x = x_ref[0]
        first = _dot((x + pe_ref[0:1, :]).astype(BF16), w1_ref[0])
        second = _dot((x + pe_ref[1:2, :]).astype(BF16), w1_ref[1])
        pre = first + pltpu.roll(second, nc - 1, 0)
        return (pre * _sigmoid(pre)).astype(BF16)

    kvc = (_dot(hidden(xk_ref, pek_ref, w1k_ref), w2k_ref[...])
           + _dot(hidden(xv_ref, pev_ref, w1v_ref), w2v_ref[...]))
    o_ref[0] = kvc.astype(BF16)
    ot_ref[0] = kvc.T.astype(BF16)


def _compress_call(l, xk, xv, pek, pev, w1k, w1v, w2k, w2v):
    B, nc, width = xk.shape
    full = lambda shape: _layer_spec(l, shape, 1)
    tok = pl.BlockSpec((1, nc, width), lambda b: (b, 0, 0))
    return pl.pallas_call(
        functools.partial(_compress_kernel, nc=nc),
        out_shape=(jax.ShapeDtypeStruct((B, nc, LANES), BF16), jax.ShapeDtypeStruct((B, LANES, nc), BF16)),
        grid=(B,),
        in_specs=[tok, tok, full((2, width)), full((2, width)), full((2, width, CMP_HIDDEN)),
                  full((2, width, CMP_HIDDEN)), full((CMP_HIDDEN, LANES)), full((CMP_HIDDEN, LANES))],
        out_specs=(pl.BlockSpec((1, nc, LANES), lambda b: (b, 0, 0)),
                   pl.BlockSpec((1, LANES, nc), lambda b: (b, 0, 0))),
        compiler_params=pltpu.CompilerParams(dimension_semantics=("arbitrary",),
                                             vmem_limit_bytes=VMEM_LIMIT),
        name="nsa_compress",
    )(xk, xv, pek, pev, w1k, w1v, w2k, w2v)


def _nsa_kernel(q_ref, kvc_ref, kvct_ref, ksw_ref, vsw_ref, sm_ref, o_ref,
                imp_scr, sel_scr, m_scr, acc_scr, *, n_blk, n_sel, nbp):
    qb = pl.program_id(1)
    ncp = kvc_ref.shape[1]
    q_t = q_ref[0, 0]
    zero = jnp.zeros((HEAD_DIM, T), q_t.dtype)
    heads = [q_t[h * HEAD_DIM:(h + 1) * HEAD_DIM, :] for h in range(N_HEADS)]
    q_lo = [jnp.concatenate([q, zero], axis=0) for q in heads]
    q_hi = [jnp.concatenate([zero, q], axis=0) for q in heads]
    gate = lambda br, h: sm_ref[0, 0, SM_GATE + br * N_HEADS + h:SM_GATE + br * N_HEADS + h + 1, :]
    t_row = qb * T + lax.broadcasted_iota(I32, (1, T), 1)

    kvc = kvc_ref[0]
    cmp_end = lax.broadcasted_iota(I32, (ncp, T), 0) * CMP_STRIDE + (CMP_LEN - 1)
    cmp_ok = cmp_end <= t_row
    cmp_logits = [jnp.where(cmp_ok, _dot(kvc, q_lo[h]), NEG) for h in range(N_HEADS)]
    cmp_tops = [jnp.max(s, axis=0, keepdims=True) for s in cmp_logits]
    cmp_e = [jnp.where(cmp_ok, jnp.exp2(s - m), 0.0) for s, m in zip(cmp_logits, cmp_tops)]
    cmp_den = [jnp.maximum(jnp.sum(e, axis=0, keepdims=True), 1e-30) for e in cmp_e]
    pcs = [e / d for e, d in zip(cmp_e, cmp_den)]
    o_cmp = [_dot(kvct_ref[0], pc.astype(BF16))[HEAD_DIM:, :] for pc in pcs]
    p_sum = functools.reduce(jnp.add, pcs)
    n_win = WINDOW // T + 1
    j0 = jnp.maximum(qb - (n_win - 1), 0)
    heads, wins = range(N_HEADS), range(n_win)
    oks = []
    for i in wins:
        key, qry = _key_query_index(qb, j0 + i)
        dist = qry - key
        oks.append((dist >= 0) & (dist < WINDOW))
    raw = [[_dot(ksw_ref[0, pl.ds(pl.multiple_of((j0 + i) * T, T), T), :], q_hi[h]) for i in wins]
           for h in heads]
    logits = [[jnp.where(oks[i], raw[h][i], NEG) for i in wins] for h in heads]
    tops = _tile_tops(logits)
    es = [[jnp.where(oks[i], jnp.exp2(logits[h][i] - tops[h]), 0.0) for i in wins] for h in heads]
    dens = [functools.reduce(jnp.add, [jnp.sum(e, axis=0, keepdims=True) for e in es[h]]) for h in heads]
    pws = [[(es[h][i] / jnp.maximum(dens[h], 1e-30)).astype(BF16) for i in wins] for h in heads]
    o_win = [functools.reduce(jnp.add, [_dot(vsw_ref[0, j0 + i, HV_ROWS:, :], pws[h][i]) for i in wins])
             for h in heads]
    per_blk = SLC_BLOCK // CMP_STRIDE
    b_i = lax.broadcasted_iota(I32, (nbp, ncp), 0)
    c_i = lax.broadcasted_iota(I32, (nbp, ncp), 1)
    group = jnp.where(_div_pow2(c_i, per_blk) == b_i, 1.0, 0.0).astype(BF16)
    imp = _dot3x(group, p_sum)
    blk = lax.broadcasted_iota(I32, (nbp, T), 0)
    cur = _div_pow2(t_row, SLC_BLOCK)
    forced = (blk == 0) | (blk == cur) | (blk == cur - 1)
    imp = jnp.where(forced, FORCE_SCORE, jnp.where(blk <= cur, imp, -1.0))
    imp = jnp.where(blk < n_blk, imp, -2.0)
    imp_scr[...] = imp
    thr = _top_k_threshold(imp_scr, 1, nbp, n_sel, jnp.full((1, T), float(nbp), F32))
    sel_scr[...] = jnp.where(imp_scr[...] >= thr, 1.0, 0.0)

    _softmax_init(m_scr, acc_scr)
    blk_per_tile = T // SLC_BLOCK

    def slc_tiles(js, diagonal=False):
        def inputs():
            valids, scores = [], [[] for _ in range(N_HEADS)]
            for j in js:
                valid = jnp.concatenate(
                    [jnp.broadcast_to(sel_scr[pl.ds(j * blk_per_tile + b, 1), :], (SLC_BLOCK, T))
                     for b in range(blk_per_tile)], axis=0) > 0.5
                if diagonal:
                    key, qry = _key_query_index(qb, j)
                    valid = valid & (key <= qry)
                valids.append(valid)
                kt = ksw_ref[0, pl.ds(pl.multiple_of(j * T, T), T), :]
                for h in range(N_HEADS):
                    scores[h].append(_dot(kt, q_lo[h]))
            return scores, valids, [[vsw_ref[0, j, :HV_ROWS, :]] * N_HEADS for j in js]

        _softmax_update(inputs, m_scr, acc_scr, lazy=not diagonal)

    slc_tiles([qb], diagonal=True)
    _for_tile_groups(qb, slc_tiles, group=4)
    _store_heads(o_ref, [gate(0, h) * o_cmp[h] + gate(1, h) * _softmax_out(acc_scr, h)
                         + gate(2, h) * o_win[h] for h in heads])


def _nsa_call(pr, kvc, kvc_t):
    B, S, _ = pr['keys'].shape
    n_blk = S // SLC_BLOCK
    nbp = max(LANES, n_blk)
    ncp = kvc.shape[1]
    return pl.pallas_call(
        functools.partial(_nsa_kernel, n_blk=n_blk, n_sel=min(SLC_TOPN, n_blk), nbp=nbp),
        out_shape=jax.ShapeDtypeStruct((B, S, BR_WIDTH), F32),
        grid=(B, S // T),
        in_specs=[_q_spec(BR_WIDTH),
                  pl.BlockSpec((1, ncp, LANES), lambda b, i: (b, 0, 0)),
                  pl.BlockSpec((1, LANES, ncp), lambda b, i: (b, 0, 0)),
                  _keys_spec(S, 1, A_KSW), _v_spec(S, VSW_ROWS), _q_spec(LANES)],
        out_specs=_OUT_SPEC,
        scratch_shapes=[pltpu.VMEM((nbp, T), F32), pltpu.VMEM((nbp, T), F32)] + _attn_scratch(HV_ROWS),
        compiler_params=_attn_params(),
        name="nsa",
    )(pr['nq_t'], kvc, kvc_t, pr['keys'], pr['vsw_t'], pr['sm_t'])


def _mem_kv_kernel(mem_ref, g_ref, w_ref, o_ref):
    o_ref[0] = _dot(_rmsnorm(mem_ref[0], g_ref[...]).astype(BF16), w_ref[...]).astype(BF16)


def _mem_kv_call(l, mem, g, w):
    B, M, D = mem.shape
    return pl.pallas_call(
        _mem_kv_kernel,
        out_shape=jax.ShapeDtypeStruct((B, M, 2 * BR_WIDTH), BF16),
        grid=(B,),
        in_specs=[pl.BlockSpec((1, M, D), lambda b: (b, 0, 0)), _layer_spec(l, (1, D), 1),
                  _layer_spec(l, (D, 2 * BR_WIDTH), 1)],
        out_specs=pl.BlockSpec((1, M, 2 * BR_WIDTH), lambda b: (b, 0, 0)),
        name="mem_kv",
    )(mem, g, w)


def _out_kernel(x_ref, g_ref, ya_ref, yb_ref, yc_ref, yd_ref, mq_ref, mkv_ref, wz_ref, wm_ref,
                wb_ref, wo_ref, fg_ref, o_ref, h_scr, *, final):
    x = x_ref[0]
    h_scr[...] = _rmsnorm(x, g_ref[...]).astype(BF16)
    q = mq_ref[0]
    lane = lax.broadcasted_iota(I32, (T, LANES), 1)
    lo = lane < HEAD_DIM
    zero = jnp.zeros((T, LANES), q.dtype)
    heads = range(N_HEADS)
    chunks = [q[:, (h // 2) * LANES:(h // 2 + 1) * LANES] for h in heads]
    qh = [jnp.where(lo, c, zero) if h % 2 == 0 else jnp.where(lo, zero, c) for h, c in zip(heads, chunks)]
    mk = [mkv_ref[0, :, (h // 2) * LANES:(h // 2 + 1) * LANES] for h in heads]
    mv = [mkv_ref[0, :, BR_WIDTH + (h // 2) * LANES:BR_WIDTH + (h // 2 + 1) * LANES] for h in heads]
    s = [_dot_nt(qh[h], mk[h]) for h in heads]
    e = [jnp.exp(s[h] - jnp.max(s[h], axis=1, keepdims=True)) for h in heads]
    p = [(e[h] / jnp.sum(e[h], axis=1, keepdims=True)).astype(BF16) for h in heads]
    o = [_dot(p[h], mv[h]) for h in heads]
    y_e = jnp.concatenate([jnp.where(lo, o[0], o[1]), jnp.where(lo, o[2], o[3])], axis=1)
    branches = range(N_BRANCH)
    d = x.shape[1]
    ys_in = (ya_ref[0], yb_ref[0], yc_ref[0], yd_ref[0], y_e)
    z = [_dot(h_scr[...], wz_ref[n]) for n in branches]
    ys = [(ys_in[n] * (z[n] * _sigmoid(z[n]))).astype(BF16) for n in branches]
    gates = [_sigmoid(_dot(h_scr[...], wm_ref[:, n * d:(n + 1) * d])) for n in branches]
    projected = [_dot(ys[n], wb_ref[n]) for n in branches]
    merged = functools.reduce(jnp.add, [gates[n] * projected[n] for n in branches])
    out = x + _dot(merged.astype(BF16), wo_ref[...])
    o_ref[0] = _rmsnorm(out, fg_ref[...]) if final else out


def _out_call(l, x, g, ys, keys, mkv, wz, wm, wb, wo, fg, final):
    B, S, D = x.shape
    M = mkv.shape[1]
    full = lambda shape: _layer_spec(l, shape, 2)
    tile = lambda n: pl.BlockSpec((1, T, n), lambda b, i: (b, i, 0))
    return pl.pallas_call(
        functools.partial(_out_kernel, final=final),
        out_shape=jax.ShapeDtypeStruct((B, S, D), F32),
        grid=(B, S // T),
        in_specs=[tile(D), full((1, D))] + [tile(BR_WIDTH)] * 4
                 + [pl.BlockSpec((1, T, BR_WIDTH), lambda b, i: (b, i, A_MQ // 2)),
                    pl.BlockSpec((1, M, 2 * BR_WIDTH), lambda b, i: (b, 0, 0)),
                    full((N_BRANCH, D, BR_WIDTH)), full((D, N_BRANCH * D)), full((N_BRANCH, BR_WIDTH, D)),
                    full((D, D)), pl.BlockSpec((1, D), lambda b, i: (0, 0))],
        out_specs=tile(D),
        scratch_shapes=[pltpu.VMEM((T, D), BF16)],
        compiler_params=pltpu.CompilerParams(dimension_semantics=("arbitrary", "arbitrary"),
                                             vmem_limit_bytes=VMEM_LIMIT),
        name="merge_out",
    )(x, g, *ys, keys, mkv, wz, wm, wb, wo, fg)


def kernel(x, mem, positions, norm_g, w_in, kv_norm, w_uk, w_uv, fox_bias, nsa_pe_k, nsa_pe_v,
           nsa_wc1_k, nsa_wc2_k, nsa_wc1_v, nsa_wc2_v, mem_norm, w_mem_kv, w_branch, w_out, final_norm):
    B, S, D = x.shape
    depth = norm_g.shape[0]
    assert S % (2 * T) == 0 and S >= WINDOW + T and WINDOW % T == 0 and D == D_MODEL
    cos, sin = _rope_tables(positions)
    w1, wz, wm = _proj_weight(w_in)
    fb = jnp.zeros((depth, 1, LANES), F32).at[:, 0, SM_CUM:SM_CUM + N_HEADS].set(fox_bias)
    nc = S // CMP_STRIDE
    tok_w = CMP_STRIDE * NSA_KV_DIM
    pad_k = jnp.zeros((depth, CMP_HIDDEN, LANES), F32).at[..., :NSA_KV_DIM].set(nsa_wc2_k).astype(BF16)
    pad_v = jnp.zeros((depth, CMP_HIDDEN, LANES), F32).at[..., NSA_KV_DIM:].set(nsa_wc2_v).astype(BF16)
    g, kvn, mn = norm_g[:, None, :], kv_norm[:, None, :], mem_norm[:, None, :]
    wuk, wuv, wmkv = w_uk.astype(BF16), w_uv.astype(BF16), w_mem_kv.astype(BF16)
    pek, pev = nsa_pe_k.reshape(depth, 2, tok_w), nsa_pe_v.reshape(depth, 2, tok_w)
    w1k = nsa_wc1_k.reshape(depth, 2, tok_w, CMP_HIDDEN).astype(BF16)
    w1v = nsa_wc1_v.reshape(depth, 2, tok_w, CMP_HIDDEN).astype(BF16)
    wb, wo = w_branch.astype(BF16), w_out.astype(BF16)
    for l in range(depth):
        pr = _proj_call(l, x, g, w1, kvn, wuk, wuv, fb, cos, sin)
        kvc, kvc_t = _compress_call(l, pr['kc_tok'].reshape(B, nc, tok_w), pr['vc_tok'].reshape(B, nc, tok_w),
                                    pek, pev, w1k, w1v, pad_k, pad_v)
        ys = (_dsa_call(pr), _fox_call(pr), _sb_call(pr), _nsa_call(pr, kvc, kvc_t))
        mkv = _mem_kv_call(l, mem, mn, wmkv)
        x = _out_call(l, x, g, ys, pr['keys'], mkv, wz, wm, wb, wo, final_norm[None], l == depth - 1)
    return x
```
